```python
import jax, jax.numpy as jnp
from jax import lax
import numpy as np

D_MODEL = 1024
BATCH = 4
SEQ = 4096
DEPTH = 1

HEAD_DIM = 64
MIX_WIDTH = D_MODEL
NSA_WIDTH = MIX_WIDTH // 2
DSA_WIDTH = MIX_WIDTH - NSA_WIDTH
NSA_HEADS = NSA_WIDTH // HEAD_DIM
NSA_KV_HEADS = 2
NSA_KV_W = NSA_KV_HEADS * HEAD_DIM
DSA_HEADS = DSA_WIDTH // HEAD_DIM
DSA_KV_HEADS = 2
DSA_KV_W = DSA_KV_HEADS * HEAD_DIM
IDX_HEADS = 4
IDX_DIM = 64
CMP_BLOCK = 32
CMP_STRIDE = 16
CMP_HIDDEN = 256
SEL_BLOCK = 64
SEL_TOPN = 16
WINDOW = 512
DSA_TOPK_MAX = 256
ROPE_THETA = 10000.0
EPS = 1e-6
Q_BLOCK = 128
SEL_Q_BLOCK = 64
NEG = -1e30
FORCE = 1e6
ATTN_SCALE = HEAD_DIM ** -0.5

IN_WIDTHS = (NSA_WIDTH, NSA_KV_W, NSA_KV_W, NSA_KV_W, NSA_KV_W, NSA_KV_W, NSA_KV_W,
             3 * NSA_HEADS, NSA_WIDTH,
             DSA_WIDTH, DSA_KV_W, DSA_KV_W, IDX_HEADS * IDX_DIM, IDX_DIM, IDX_HEADS, DSA_WIDTH)
IN_COLS = sum(IN_WIDTHS)

kernel_name = "hymba_nsa_dsa_hybrid_layer"


def rms_norm(x, gain):
    xf = x.astype(jnp.float32)
    y = xf * lax.rsqrt(jnp.mean(xf * xf, axis=-1, keepdims=True) + EPS)
    return (y * gain.astype(jnp.float32)).astype(x.dtype)


def rope(x, pos):
    half = x.shape[-1] // 2
    inv_freq = ROPE_THETA ** (-jnp.arange(half, dtype=jnp.float32) / half)
    ang = pos[:, None] * inv_freq[None, :]
    cos = jnp.cos(ang)[:, None, :]
    sin = jnp.sin(ang)[:, None, :]
    xf = x.astype(jnp.float32)
    x1, x2 = xf[..., :half], xf[..., half:]
    return jnp.concatenate([x1 * cos - x2 * sin, x2 * cos + x1 * sin], axis=-1).astype(x.dtype)


def masked_softmax(s, mask):
    s = jnp.where(mask, s.astype(jnp.float32), NEG)
    p = jax.nn.softmax(s, axis=-1)
    return jnp.where(mask, p, 0.0)


def split_columns(proj, widths):
    offsets = np.cumsum(np.array(widths))[:-1]
    return jnp.split(proj, [int(o) for o in offsets], axis=-1)


def compress(k_raw, pe, w1, b1, w2):
    B, S, G, Dh = k_raw.shape
    n_cmp = (S - CMP_BLOCK) // CMP_STRIDE + 1
    idx = jnp.arange(n_cmp)[:, None] * CMP_STRIDE + jnp.arange(CMP_BLOCK)[None, :]
    blocks = k_raw[:, idx] + pe[None, None, :, None, :]
    flat = blocks.transpose(0, 1, 3, 2, 4).reshape(B, n_cmp, G, CMP_BLOCK * Dh)
    hid = jax.nn.silu(flat @ w1 + b1)
    return hid @ w2


def compressed_branch(q, kc, vc):
    S = q.shape[1]
    n_cmp = kc.shape[1]
    cmp_end = jnp.arange(n_cmp) * CMP_STRIDE + CMP_BLOCK - 1
    t = jnp.arange(S)
    s = jnp.einsum('bsgrd,bcgd->bgrsc', q, kc) * ATTN_SCALE
    p = masked_softmax(s, cmp_end[None, :] <= t[:, None])
    o = jnp.einsum('bgrsc,bcgd->bsgrd', p.astype(vc.dtype), vc)
    return o, p


def selection_indices(p_cmp):
    S, n_cmp = p_cmp.shape[-2], p_cmp.shape[-1]
    n_sel = S // SEL_BLOCK
    c_start = jnp.arange(n_cmp) * CMP_STRIDE
    j = jnp.arange(n_sel)
    j_start = j * SEL_BLOCK
    overlap = jnp.clip(jnp.minimum(c_start[:, None] + CMP_BLOCK, j_start[None, :] + SEL_BLOCK)
                       - jnp.maximum(c_start[:, None], j_start[None, :]), 0, None)
    overlap = overlap.astype(jnp.float32) / CMP_BLOCK
    imp = jnp.einsum('bgrsc,cj->bgsj', p_cmp, overlap)
    t = jnp.arange(S)
    cur = t // SEL_BLOCK
    forced = (j[None, :] == 0) | (j[None, :] == cur[:, None]) | (j[None, :] == cur[:, None] - 1)
    imp = jnp.where(forced, FORCE, imp)
    imp = jnp.where(j_start[None, :] <= t[:, None], imp, NEG)
    _, idx = lax.top_k(imp, min(SEL_TOPN, n_sel))
    return idx.transpose(0, 2, 1, 3)


def selected_branch(q, ks, vs, sel_idx):
    B, S, G, R, Dh = q.shape
    n_sel = S // SEL_BLOCK
    n = sel_idx.shape[-1]
    k_blocks = ks.reshape(B, n_sel, SEL_BLOCK, G, Dh).transpose(0, 3, 1, 2, 4)
    v_blocks = vs.reshape(B, n_sel, SEL_BLOCK, G, Dh).transpose(0, 3, 1, 2, 4)
    b_ix = jnp.arange(B)[:, None, None, None]
    g_ix = jnp.arange(G)[None, None, :, None]

    def one_block(i):
        t0 = i * SEL_Q_BLOCK
        q_b = lax.dynamic_slice_in_dim(q, t0, SEL_Q_BLOCK, axis=1)
        idx_b = lax.dynamic_slice_in_dim(sel_idx, t0, SEL_Q_BLOCK, axis=1)
        k_sel = k_blocks[b_ix, g_ix, idx_b].reshape(B, SEL_Q_BLOCK, G, n * SEL_BLOCK, Dh)
        v_sel = v_blocks[b_ix, g_ix, idx_b].reshape(B, SEL_Q_BLOCK, G, n * SEL_BLOCK, Dh)
        kpos = (idx_b[..., None] * SEL_BLOCK + jnp.arange(SEL_BLOCK)).reshape(B, SEL_Q_BLOCK, G, n * SEL_BLOCK)
        t = t0 + jnp.arange(SEL_Q_BLOCK)
        mask = (kpos <= t[None, :, None, None])[:, :, :, None, :]
        s = jnp.einsum('btgrd,btgkd->btgrk', q_b, k_sel) * ATTN_SCALE
        p = masked_softmax(s, mask)
        return jnp.einsum('btgrk,btgkd->btgrd', p.astype(v_sel.dtype), v_sel)

    out = lax.map(one_block, jnp.arange(S // SEL_Q_BLOCK))
    return out.transpose(1, 0, 2, 3, 4, 5).reshape(B, S, G, R, Dh)


def window_branch(q, kw, vw):
    B, S, G, R, Dh = q.shape
    pad = ((0, 0), (WINDOW, 0), (0, 0), (0, 0))
    kp = jnp.pad(kw, pad)
    vp = jnp.pad(vw, pad)
    span = Q_BLOCK + WINDOW

    def one_block(i):
        t0 = i * Q_BLOCK
        q_b = lax.dynamic_slice_in_dim(q, t0, Q_BLOCK, axis=1)
        k_b = lax.dynamic_slice_in_dim(kp, t0, span, axis=1)
        v_b = lax.dynamic_slice_in_dim(vp, t0, span, axis=1)
        kpos = t0 - WINDOW + jnp.arange(span)
        t = t0 + jnp.arange(Q_BLOCK)
        diff = t[:, None] - kpos[None, :]
        mask = (diff >= 0) & (diff < WINDOW) & (kpos[None, :] >= 0)
        s = jnp.einsum('btgrd,bkgd->bgrtk', q_b, k_b) * ATTN_SCALE
        p = masked_softmax(s, mask)
        return jnp.einsum('bgrtk,bkgd->btgrd', p.astype(v_b.dtype), v_b)

    out = lax.map(one_block, jnp.arange(S // Q_BLOCK))
    return out.transpose(1, 0, 2, 3, 4, 5).reshape(B, S, G, R, Dh)


def dsa_branch(q, k, v, qi, ki, wi):
    B, S, G, R, Dh = q.shape
    topk = min(DSA_TOPK_MAX, S // 4)
    b_ix = jnp.arange(B)[:, None, None]
    key_pos = jnp.arange(S)
    wi = wi * (IDX_HEADS ** -0.5)

    def one_block(i):
        t0 = i * Q_BLOCK
        q_b = lax.dynamic_slice_in_dim(q, t0, Q_BLOCK, axis=1)
        qi_b = lax.dynamic_slice_in_dim(qi, t0, Q_BLOCK, axis=1)
        wi_b = lax.dynamic_slice_in_dim(wi, t0, Q_BLOCK, axis=1)
        t = t0 + jnp.arange(Q_BLOCK)
        logits = jnp.einsum('bthd,bsd->bths', qi_b, ki) * (IDX_DIM ** -0.5)
        score = jnp.einsum('bths,bth->bts', jax.nn.relu(logits), wi_b).astype(jnp.float32)
        score = jnp.where(key_pos[None, None, :] <= t[None, :, None], score, NEG)
        _, idx = lax.top_k(score, topk)
        k_sel = k[b_ix, idx]
        v_sel = v[b_ix, idx]
        mask = (idx <= t[None, :, None])[:, :, None, None, :]
        s = jnp.einsum('btgrd,btkgd->btgrk', q_b, k_sel) * ATTN_SCALE
        p = masked_softmax(s, mask)
        return jnp.einsum('btgrk,btkgd->btgrd', p.astype(v_sel.dtype), v_sel)

    out = lax.map(one_block, jnp.arange(S // Q_BLOCK))
    return out.transpose(1, 0, 2, 3, 4, 5).reshape(B, S, G, R, Dh)


def hybrid_layer(x, norm_gain, w_in, nsa_q_gain, nsa_kc_gain, nsa_ks_gain, nsa_kw_gain,
                 cmp_pe_k, cmp_k_w1, cmp_k_b1, cmp_k_w2, cmp_pe_v, cmp_v_w1, cmp_v_b1, cmp_v_w2,
                 dsa_q_gain, dsa_k_gain, w_out):
    B, S, _ = x.shape
    pos = jnp.arange(S, dtype=jnp.float32)
    h = rms_norm(x, norm_gain)
    proj = jnp.einsum('bsd,dc->bsc', h, w_in)
    (q_n, kc, vc, ks, vs, kw, vw, gate_logits, z_n,
     q_d, k_d, v_d, qi, ki, wi, z_d) = split_columns(proj, IN_WIDTHS)

    def heads(a, n):
        return a.reshape(B, S, n, HEAD_DIM)

    G, R = NSA_KV_HEADS, NSA_HEADS // NSA_KV_HEADS
    qn = rope(rms_norm(heads(q_n, NSA_HEADS), nsa_q_gain), pos).reshape(B, S, G, R, HEAD_DIM)
    kc_cmp = compress(heads(kc, G), cmp_pe_k, cmp_k_w1, cmp_k_b1, cmp_k_w2)
    vc_cmp = compress(heads(vc, G), cmp_pe_v, cmp_v_w1, cmp_v_b1, cmp_v_w2)
    cmp_pos = (jnp.arange(kc_cmp.shape[1]) * CMP_STRIDE + CMP_BLOCK - 1).astype(jnp.float32)
    kc_cmp = rope(rms_norm(kc_cmp, nsa_kc_gain), cmp_pos)
    o_cmp, p_cmp = compressed_branch(qn, kc_cmp, vc_cmp)
    sel_idx = selection_indices(p_cmp)
    ks_r = rope(rms_norm(heads(ks, G), nsa_ks_gain), pos)
    o_sel = selected_branch(qn, ks_r, heads(vs, G), sel_idx)
    kw_r = rope(rms_norm(heads(kw, G), nsa_kw_gain), pos)
    o_win = window_branch(qn, kw_r, heads(vw, G))
    g = jax.nn.sigmoid(gate_logits.astype(jnp.float32)).reshape(B, S, G, R, 3).astype(o_cmp.dtype)
    o_nsa = g[..., 0:1] * o_cmp + g[..., 1:2] * o_sel + g[..., 2:3] * o_win
    o_nsa = o_nsa.reshape(B, S, NSA_WIDTH) * jax.nn.silu(z_n)

    Gd, Rd = DSA_KV_HEADS, DSA_HEADS // DSA_KV_HEADS
    qd = rope(rms_norm(heads(q_d, DSA_HEADS), dsa_q_gain), pos).reshape(B, S, Gd, Rd, HEAD_DIM)
    kd = rope(rms_norm(heads(k_d, Gd), dsa_k_gain), pos)
    qi_r = rope(qi.reshape(B, S, IDX_HEADS, IDX_DIM), pos)
    ki_r = rope(ki.reshape(B, S, 1, IDX_DIM), pos).reshape(B, S, IDX_DIM)
    o_dsa = dsa_branch(qd, kd, heads(v_d, Gd), qi_r, ki_r, wi)
    o_dsa = o_dsa.reshape(B, S, DSA_WIDTH) * jax.nn.silu(z_d)

    mixed = jnp.concatenate([o_nsa, o_dsa], axis=-1)
    return x + jnp.einsum('bsc,cd->bsd', mixed, w_out)


def setup_inputs(seed: int = 0) -> dict:
    key = jax.random.key(seed)
    k = jax.random.split(key, 18)
    L = DEPTH
    Dh = HEAD_DIM

    def nrm(kk, shape, scale):
        return jax.random.normal(kk, shape, jnp.float32) * scale

    def gain(kk, n):
        return 1.0 + 0.01 * jax.random.normal(kk, (L, n), jnp.float32)

    return {
        "x": nrm(k[0], (BATCH, SEQ, D_MODEL), 1.0),
        "norm_gain": gain(k[1], D_MODEL),
        "w_in": nrm(k[2], (L, D_MODEL, IN_COLS), D_MODEL ** -0.5),
        "nsa_q_gain": gain(k[3], Dh),
        "nsa_kc_gain": gain(k[4], Dh),
        "nsa_ks_gain": gain(k[5], Dh),
        "nsa_kw_gain": gain(k[6], Dh),
        "cmp_pe_k": nrm(k[7], (L, CMP_BLOCK, Dh), 0.1),
        "cmp_k_w1": nrm(k[8], (L, CMP_BLOCK * Dh, CMP_HIDDEN), (CMP_BLOCK * Dh) ** -0.5),
        "cmp_k_b1": nrm(k[9], (L, CMP_HIDDEN), 0.01),
        "cmp_k_w2": nrm(k[10], (L, CMP_HIDDEN, Dh), CMP_HIDDEN ** -0.5),
        "cmp_pe_v": nrm(k[11], (L, CMP_BLOCK, Dh), 0.1),
        "cmp_v_w1": nrm(k[12], (L, CMP_BLOCK * Dh, CMP_HIDDEN), (CMP_BLOCK * Dh) ** -0.5),
        "cmp_v_b1": nrm(k[13], (L, CMP_HIDDEN), 0.01),
        "cmp_v_w2": nrm(k[14], (L, CMP_HIDDEN, Dh), CMP_HIDDEN ** -0.5),
        "dsa_q_gain": gain(k[15], Dh),
        "dsa_k_gain": gain(k[16], Dh),
        "w_out": nrm(k[17], (L, MIX_WIDTH, D_MODEL), MIX_WIDTH ** -0.5),
    }


def reference(x, norm_gain, w_in, nsa_q_gain, nsa_kc_gain, nsa_ks_gain, nsa_kw_gain,
              cmp_pe_k, cmp_k_w1, cmp_k_b1, cmp_k_w2, cmp_pe_v, cmp_v_w1, cmp_v_b1, cmp_v_w2,
              dsa_q_gain, dsa_k_gain, w_out):
    for l in range(DEPTH):
        x = hybrid_layer(x, norm_gain[l], w_in[l], nsa_q_gain[l], nsa_kc_gain[l], nsa_ks_gain[l],
                         nsa_kw_gain[l], cmp_pe_k[l], cmp_k_w1[l], cmp_k_b1[l], cmp_k_w2[l],
                         cmp_pe_v[l], cmp_v_w1[l], cmp_v_b1[l], cmp_v_w2[l],
                         dsa_q_gain[l], dsa_k_gain[l], w_out[l])
    return x
```

```python
import functools

import numpy as np
import jax
import jax.numpy as jnp
from jax import lax
from jax.experimental import pallas as pl
from jax.experimental.pallas import tpu as pltpu

D_MODEL = 1024
HEAD_DIM = 64
NSA_HEADS = 8
DSA_HEADS = 8
KV_GROUPS = 2
HEADS_PER_GROUP = 4
IDX_HEADS = 4
CMP_BLOCK = 32
CMP_STRIDE = 16
CMP_HIDDEN = 256
SEL_BLOCK = 64
SEL_TOPN = 16
WINDOW = 512
DSA_TOPK_MAX = 256
ROPE_THETA = 10000.0
EPS = 1e-6
NEG = -1e30
FORCE = 1e6
ATTN_SCALE = HEAD_DIM ** -0.5
IDX_SCALE = HEAD_DIM ** -0.5
WI_SCALE = IDX_HEADS ** -0.5

LANES = 128
M_FLOOR = -5e29
INT_MIN = -2 ** 31
VMEM_LIMIT = 56 * 1024 * 1024

IN_WIDTHS = (512, 128, 128, 128, 128, 128, 128, 24, 512, 512, 128, 128, 256, 64, 4, 512)
IN_NAMES = ("q_n", "kc", "vc", "ks", "vs", "kw", "vw", "gate", "z_n",
            "q_d", "k_d", "v_d", "qi", "ki", "wi", "z_d")
IN_COLS = sum(IN_WIDTHS)

CH_QN, CH_KS, CH_KW, CH_QD, CH_KD, CH_QI, CH_KI = 0, 4, 5, 6, 10, 11, 13
CH_KC, CH_VC, CH_VS, CH_VW, CH_VD, CH_ZN, CH_ZD, CH_MISC = 14, 15, 16, 17, 18, 19, 23, 27
N_CHUNKS = 28
MISC_WI = 24

TM_PROJ = 512
TQ = 128
KC = 512

f32 = jnp.float32
bf16 = jnp.bfloat16


def _column_permutation():
    off = dict(zip(IN_NAMES, np.cumsum((0,) + IN_WIDTHS[:-1])))
    zero = IN_COLS

    def rng(name, n):
        return np.arange(off[name], off[name] + n)

    parts = [rng("q_n", 512), rng("ks", 128), rng("kw", 128), rng("q_d", 512), rng("k_d", 128),
             rng("qi", 256), rng("ki", 64), rng("ki", 64),
             rng("kc", 128), rng("vc", 128), rng("vs", 128), rng("vw", 128), rng("v_d", 128),
             rng("z_n", 512), rng("z_d", 512),
             rng("gate", 24), rng("wi", 4), np.full(LANES - 28, zero)]
    cols = np.concatenate(parts)
    assert cols.shape[0] == N_CHUNKS * LANES
    return cols


def _nt_dot(a, b):
    return lax.dot_general(a, b, (((1,), (1,)), ((), ())), preferred_element_type=f32)


def _dot(a, b):
    return jnp.dot(a, b, preferred_element_type=f32)


def _split_bf16(v):
    hi = v.astype(bf16)
    lo = (v - hi.astype(f32)).astype(bf16)
    return hi, lo


def _head_rms_norm(y, bd, gain_row):
    hi, lo = _split_bf16(y * y)
    ssq = _dot(hi, bd) + _dot(lo, bd)
    return y * lax.rsqrt(ssq * (1.0 / HEAD_DIM) + EPS) * gain_row


def _rope(y, cos, sin_signed, lo32):
    partner = jnp.where(lo32, pltpu.roll(y, LANES - 32, 1), pltpu.roll(y, 32, 1))
    return y * cos + partner * sin_signed


def _dup_halves(y, lo64):
    r = pltpu.roll(y, 64, 1)
    return jnp.where(lo64, y, r), jnp.where(lo64, r, y)


def _inproj_kernel(x_ref, ng_ref, w_ref, bd_ref, cos_ref, sin_ref, gains_ref,
                   qn_ref, qd_ref, qi_ref, ki_ref, ks_ref, kw_ref, kd_ref,
                   vs_ref, vw_ref, vd_ref, kc_ref, vc_ref, zn_ref, zd_ref, misc_ref):
    x = x_ref[0]
    ms = jnp.mean(x * x, axis=-1, keepdims=True)
    h = (x * lax.rsqrt(ms + EPS) * ng_ref[...]).astype(bf16)
    cos = cos_ref[...]
    sin = sin_ref[...]
    bd = bd_ref[...]
    lane = lax.broadcasted_iota(jnp.int32, (1, LANES), 1)
    lo32 = (lane % 64) < 32
    lo64 = lane < 64

    def proj(c0, n):
        return _dot(h, w_ref[:, c0 * LANES:(c0 + n) * LANES])

    def chunk(y, i):
        return y[:, i * LANES:(i + 1) * LANES]

    def normed_rope(y, gain_idx):
        return _rope(_head_rms_norm(y, bd, gains_ref[gain_idx:gain_idx + 1, :]), cos, sin, lo32)

    y = proj(CH_QN, 4)
    for i in range(4):
        qn_ref[0, :, i * LANES:(i + 1) * LANES] = (normed_rope(chunk(y, i), 0) * ATTN_SCALE).astype(bf16)
    y = proj(CH_QD, 4)
    for i in range(4):
        qd_ref[0, :, i * LANES:(i + 1) * LANES] = (normed_rope(chunk(y, i), 3) * ATTN_SCALE).astype(bf16)
    y = proj(CH_QI, 2)
    for i in range(2):
        qi_ref[0, :, i * LANES:(i + 1) * LANES] = (_rope(chunk(y, i), cos, sin, lo32) * IDX_SCALE).astype(bf16)
    ki_ref[0] = _rope(proj(CH_KI, 1), cos, sin, lo32).astype(bf16)

    for c0, gain_idx, ref in ((CH_KS, 1, ks_ref), (CH_KW, 2, kw_ref), (CH_KD, 4, kd_ref)):
        a, b = _dup_halves(normed_rope(proj(c0, 1), gain_idx), lo64)
        ref[0, 0] = a.astype(bf16)
        ref[0, 1] = b.astype(bf16)
    for c0, ref in ((CH_VS, vs_ref), (CH_VW, vw_ref), (CH_VD, vd_ref)):
        a, b = _dup_halves(proj(c0, 1), lo64)
        ref[0, 0] = a.astype(bf16)
        ref[0, 1] = b.astype(bf16)

    kc_ref[0] = proj(CH_KC, 1)
    vc_ref[0] = proj(CH_VC, 1)
    z = proj(CH_ZN, 4)
    zn_ref[0] = (z * jax.nn.sigmoid(z)).astype(bf16)
    z = proj(CH_ZD, 4)
    zd_ref[0] = (z * jax.nn.sigmoid(z)).astype(bf16)
    m = proj(CH_MISC, 1)
    misc_ref[0] = jnp.where(lane < MISC_WI, jax.nn.sigmoid(m), m * WI_SCALE)


def _in_projection(x, norm_gain, w_perm, bd, cos_t, sin_t, gains):
    B, S, _ = x.shape
    tm = min(TM_PROJ, S)
    grid = (B, S // tm)
    row = lambda c: pl.BlockSpec((1, tm, c), lambda b, i: (b, i, 0))
    dup = pl.BlockSpec((1, KV_GROUPS, tm, LANES), lambda b, i: (b, 0, i, 0))
    const = lambda shape: pl.BlockSpec(shape, lambda b, i: tuple(0 for _ in shape))
    tab = pl.BlockSpec((tm, LANES), lambda b, i: (i, 0))
    sds = jax.ShapeDtypeStruct
    out_shape = (
        sds((B, S, 512), bf16), sds((B, S, 512), bf16), sds((B, S, 256), bf16), sds((B, S, LANES), bf16),
        sds((B, KV_GROUPS, S, LANES), bf16), sds((B, KV_GROUPS, S, LANES), bf16),
        sds((B, KV_GROUPS, S, LANES), bf16), sds((B, KV_GROUPS, S, LANES), bf16),
        sds((B, KV_GROUPS, S, LANES), bf16), sds((B, KV_GROUPS, S, LANES), bf16),
        sds((B, S, LANES), f32), sds((B, S, LANES), f32),
        sds((B, S, 512), bf16), sds((B, S, 512), bf16), sds((B, S, LANES), f32),
    )
    out_specs = (row(512), row(512), row(256), row(LANES), dup, dup, dup, dup, dup, dup,
                 row(LANES), row(LANES), row(512), row(512), row(LANES))
    return pl.pallas_call(
        _inproj_kernel,
        grid=grid,
        in_specs=[row(D_MODEL), const((1, D_MODEL)), const((D_MODEL, N_CHUNKS * LANES)),
                  const((LANES, LANES)), tab, tab, const((8, LANES))],
        out_specs=out_specs,
        out_shape=out_shape,
        compiler_params=pltpu.CompilerParams(
            dimension_semantics=("arbitrary", "arbitrary"), vmem_limit_bytes=VMEM_LIMIT),
        name="in_projection",
    )(x, norm_gain, w_perm, bd, cos_t, sin_t, gains)


def _compress_kernel(src_ref, pe_ref, w1_ref, b1_ref, w2_ref, bd_ref, gain_ref, cos_ref, sin_ref,
                     out_ref, *, n_rows, is_key):
    half = CMP_BLOCK // 2
    acc_a = jnp.zeros((n_rows, 2 * CMP_HIDDEN), f32)
    acc_b = jnp.zeros((n_rows, 2 * CMP_HIDDEN), f32)
    for l in range(half):
        rows = src_ref[0, pl.ds(l, n_rows, stride=CMP_STRIDE), :]
        acc_a = acc_a + _dot((rows + pe_ref[l:l + 1, :]).astype(bf16), w1_ref[l])
        acc_b = acc_b + _dot((rows + pe_ref[l + half:l + half + 1, :]).astype(bf16), w1_ref[l + half])
    pre = acc_a + pltpu.roll(acc_b, n_rows - 1, 0) + b1_ref[...]
    hid = pre * jax.nn.sigmoid(pre)
    out = _dot(hid.astype(bf16), w2_ref[...])
    lane = lax.broadcasted_iota(jnp.int32, (1, LANES), 1)
    if is_key:
        out = _head_rms_norm(out, bd_ref[...], gain_ref[...])
        out = _rope(out, cos_ref[...], sin_ref[...], (lane % 64) < 32)
    a, b = _dup_halves(out, lane < 64)
    out_ref[0, 0] = a.astype(bf16)
    out_ref[0, 1] = b.astype(bf16)


def _compress(src, pe_dup, w1_bd, b1_dup, w2_bd, bd, gain_dup, cos_c, sin_c, *, is_key):
    B, S, _ = src.shape
    n_rows = S // CMP_STRIDE
    const = lambda shape: pl.BlockSpec(shape, lambda b: tuple(0 for _ in shape))
    return pl.pallas_call(
        functools.partial(_compress_kernel, n_rows=n_rows, is_key=is_key),
        grid=(B,),
        in_specs=[pl.BlockSpec((1, S, LANES), lambda b: (b, 0, 0)),
                  const((CMP_BLOCK, LANES)), const((CMP_BLOCK, LANES, 2 * CMP_HIDDEN)),
                  const((1, 2 * CMP_HIDDEN)), const((2 * CMP_HIDDEN, LANES)), const((LANES, LANES)),
                  const((1, LANES)), const((n_rows, LANES)), const((n_rows, LANES))],
        out_specs=pl.BlockSpec((1, KV_GROUPS, n_rows, LANES), lambda b: (b, 0, 0, 0)),
        out_shape=jax.ShapeDtypeStruct((B, KV_GROUPS, n_rows, LANES), bf16),
        compiler_params=pltpu.CompilerParams(
            dimension_semantics=("arbitrary",), vmem_limit_bytes=VMEM_LIMIT),
        name="compress_k" if is_key else "compress_v",
    )(src, pe_dup, w1_bd, b1_dup, w2_bd, bd, gain_dup, cos_c, sin_c)


def _stack_group_queries(q, g, lo64):
    slabs = []
    for r in range(HEADS_PER_GROUP):
        h = g * HEADS_PER_GROUP + r
        pair = q[:, (h // 2) * LANES:(h // 2 + 1) * LANES]
        keep = lo64 if h % 2 == 0 else jnp.logical_not(lo64)
        slabs.append(jnp.where(keep, pair, jnp.zeros_like(pair)))
    return jnp.concatenate(slabs, axis=0)


def _masked_flash(q4, k_ref, v_ref, g, n_chunks, bias_fn):
    rows = q4.shape[0]

    def body(c, carry):
        m, l, acc = carry
        k0 = pl.multiple_of(c * KC, KC)
        s = _nt_dot(q4, k_ref[0, g, pl.ds(k0, KC), :])
        s = (s.reshape(HEADS_PER_GROUP, TQ, KC) + bias_fn(k0)[None]).reshape(rows, KC)
        m_new = jnp.maximum(m, jnp.max(s, axis=-1, keepdims=True))
        alpha = jnp.exp(m - m_new)
        p = jnp.exp(s - m_new)
        l = alpha * l + jnp.sum(p, axis=-1, keepdims=True)
        acc = alpha * acc + _dot(p.astype(bf16), v_ref[0, g, pl.ds(k0, KC), :])
        return m_new, l, acc

    m0 = jnp.full((rows, 1), M_FLOOR, f32)
    l0 = jnp.zeros((rows, 1), f32)
    a0 = jnp.zeros((rows, LANES), f32)
    _, l, acc = lax.fori_loop(0, n_chunks, body, (m0, l0, a0))
    return acc / l


def _group_output(og, lo64):
    pairs = []
    for p in range(2):
        even = og[(2 * p) * TQ:(2 * p + 1) * TQ]
        odd = og[(2 * p + 1) * TQ:(2 * p + 2) * TQ]
        pairs.append(jnp.where(lo64, even, odd))
    return jnp.concatenate(pairs, axis=1)


def _nsa_kernel(qn_ref, kcmp_ref, vcmp_ref, ks_ref, vs_ref, kw_ref, vw_ref, misc_ref, zn_ref, ovt_ref,
                o_ref, *, seq, n_cmp_pad, n_sel, top_n):
    t0 = pl.program_id(1) * TQ
    lane = lax.broadcasted_iota(jnp.int32, (1, LANES), 1)
    lo64 = lane < 64
    q = qn_ref[0]
    misc = misc_ref[0]
    t_col = t0 + lax.broadcasted_iota(jnp.int32, (TQ, 1), 0)
    t4 = jnp.concatenate([t_col] * HEADS_PER_GROUP, axis=0)
    n_chunks = (t0 + TQ + KC - 1) // KC
    win_span = WINDOW + TQ
    win_start = pl.multiple_of(jnp.maximum(t0 - WINDOW, 0), TQ)

    group_outs = []
    for g in range(KV_GROUPS):
        q4 = _stack_group_queries(q, g, lo64)

        s = _nt_dot(q4, kcmp_ref[0, g])
        cmp_end = lax.broadcasted_iota(jnp.int32, (1, n_cmp_pad), 1) * CMP_STRIDE + (CMP_BLOCK - 1)
        vis = cmp_end <= t4
        s = jnp.where(vis, s, NEG)
        m = jnp.max(s, axis=-1, keepdims=True)
        e = jnp.where(vis, jnp.exp(s - m), 0.0)
        l = jnp.sum(e, axis=-1, keepdims=True)
        p = e / jnp.maximum(l, 1e-30)
        o_cmp = _dot(p.astype(bf16), vcmp_ref[0, g])

        psum = p[0:TQ] + p[TQ:2 * TQ] + p[2 * TQ:3 * TQ] + p[3 * TQ:4 * TQ]
        hi, lo = _split_bf16(psum)
        ovt = ovt_ref[...]
        imp = _nt_dot(ovt, hi) + _nt_dot(ovt, lo)
        jrow = lax.broadcasted_iota(jnp.int32, (n_sel, TQ), 0)
        tl = t0 + lax.broadcasted_iota(jnp.int32, (n_sel, TQ), 1)
        cur = lax.shift_right_logical(tl, 6)
        forced = (jrow == 0) | (jrow == cur) | (jrow == cur - 1)
        imp = jnp.where(forced, FORCE, imp)
        imp = jnp.where(jrow * SEL_BLOCK <= tl, imp, NEG)
        rank = jnp.zeros((n_sel, TQ), f32)
        for i in range(n_sel):
            row = imp[i:i + 1, :]
            tie = jnp.where(jrow > i, 1.0, 0.0)
            rank = rank + jnp.where(row > imp, 1.0, jnp.where(row == imp, tie, 0.0))
        sel = jnp.where(rank < top_n, 1.0, 0.0).T.astype(bf16)

        def sel_bias(k0):
            jj = lax.broadcasted_iota(jnp.int32, (n_sel, KC), 0)
            kk = lax.shift_right_logical(k0 + lax.broadcasted_iota(jnp.int32, (n_sel, KC), 1), 6)
            expand = jnp.where(jj == kk, 1.0, 0.0).astype(bf16)
            chosen = _dot(sel, expand)
            kpos = k0 + lax.broadcasted_iota(jnp.int32, (1, KC), 1)
            return jnp.where(kpos <= t_col, jnp.where(chosen > 0.5, 0.0, NEG), NEG)

        o_sel = _masked_flash(q4, ks_ref, vs_ref, g, n_chunks, sel_bias)

        s = _nt_dot(q4, kw_ref[0, g, pl.ds(win_start, win_span), :])
        diff = t4 - (win_start + lax.broadcasted_iota(jnp.int32, (1, win_span), 1))
        ok = (diff >= 0) & (diff < WINDOW)
        s = jnp.where(ok, s, NEG)
        m = jnp.max(s, axis=-1, keepdims=True)
        e = jnp.exp(s - m)
        pw = e / jnp.sum(e, axis=-1, keepdims=True)
        o_win = _dot(pw.astype(bf16), vw_ref[0, g, pl.ds(win_start, win_span), :])

        def gate(branch):
            cols = [misc[:, (g * HEADS_PER_GROUP + r) * 3 + branch:(g * HEADS_PER_GROUP + r) * 3 + branch + 1]
                    for r in range(HEADS_PER_GROUP)]
            return jnp.concatenate(cols, axis=0)

        og = gate(0) * o_cmp + gate(1) * o_sel + gate(2) * o_win
        group_outs.append(_group_output(og, lo64))

    o = jnp.concatenate(group_outs, axis=1) * zn_ref[0].astype(f32)
    o_ref[0] = o.astype(bf16)


def _nsa_attention(qn, kcmp, vcmp, ks, vs, kw, vw, misc, zn, ovt):
    B, S, _ = qn.shape
    n_cmp_pad = kcmp.shape[2]
    n_sel = S // SEL_BLOCK
    row = lambda c: pl.BlockSpec((1, TQ, c), lambda b, i: (b, i, 0))
    whole = lambda n: pl.BlockSpec((1, KV_GROUPS, n, LANES), lambda b, i: (b, 0, 0, 0))
    kern = functools.partial(_nsa_kernel, seq=S, n_cmp_pad=n_cmp_pad, n_sel=n_sel,
                             top_n=min(SEL_TOPN, n_sel))
    return pl.pallas_call(
        kern,
        grid=(B, S // TQ),
        in_specs=[row(512), whole(n_cmp_pad), whole(n_cmp_pad), whole(S), whole(S), whole(S), whole(S),
                  row(LANES), row(512), pl.BlockSpec((n_sel, n_cmp_pad), lambda b, i: (0, 0))],
        out_specs=row(512),
        out_shape=jax.ShapeDtypeStruct((B, S, 512), bf16),
        compiler_params=pltpu.CompilerParams(
            dimension_semantics=("arbitrary", "arbitrary"), vmem_limit_bytes=VMEM_LIMIT),
        name="nsa_attention",
    )(qn, kcmp, vcmp, ks, vs, kw, vw, misc, zn, ovt)


def _dsa_kernel(qd_ref, qi_ref, misc_ref, ki_ref, kd_ref, vd_ref, zd_ref, tri_ref,
                o_ref, key_scr, bias_scr, *, top_k):
    t0 = pl.program_id(1) * TQ
    lane = lax.broadcasted_iota(jnp.int32, (1, LANES), 1)
    lo64 = lane < 64
    t_col = t0 + lax.broadcasted_iota(jnp.int32, (TQ, 1), 0)
    n_chunks = (t0 + TQ + KC - 1) // KC
    misc = misc_ref[0]
    qi = qi_ref[0]
    qi_heads = []
    for h in range(IDX_HEADS):
        pair = qi[:, (h // 2) * LANES:(h // 2 + 1) * LANES]
        keep = lo64 if h % 2 == 0 else jnp.logical_not(lo64)
        qi_heads.append(jnp.where(keep, pair, jnp.zeros_like(pair)))
    wi = [misc[:, MISC_WI + h:MISC_WI + h + 1] for h in range(IDX_HEADS)]

    def kpos_row(k0):
        return k0 + lax.broadcasted_iota(jnp.int32, (1, KC), 1)

    def score_body(c, carry):
        k0 = pl.multiple_of(c * KC, KC)
        kib = ki_ref[0, pl.ds(k0, KC), :]
        sc = jnp.zeros((TQ, KC), f32)
        for h in range(IDX_HEADS):
            sc = sc + jnp.maximum(_nt_dot(qi_heads[h], kib), 0.0) * wi[h]
        sc = jnp.where(sc == 0.0, 0.0, sc)
        bits = pltpu.bitcast(sc, jnp.int32)
        key = bits ^ (jnp.right_shift(bits, 31) & jnp.int32(0x7FFFFFFF))
        key_scr[:, pl.ds(k0, KC)] = jnp.where(kpos_row(k0) <= t_col, key, jnp.int32(INT_MIN))
        return carry

    lax.fori_loop(0, n_chunks, score_body, 0)

    def count(pred):
        def body(c, acc):
            k0 = pl.multiple_of(c * KC, KC)
            blk = key_scr[:, pl.ds(k0, KC)]
            for u in range(KC // LANES):
                acc = acc + jnp.where(pred(blk[:, u * LANES:(u + 1) * LANES]), 1.0, 0.0)
            return acc
        acc = lax.fori_loop(0, n_chunks, body, jnp.zeros((TQ, LANES), f32))
        return jnp.sum(acc, axis=-1, keepdims=True)

    kf = float(top_k)
    thr = jnp.where(count(lambda b: b >= 0) >= kf, jnp.int32(0), jnp.int32(INT_MIN))

    def bit_body(i, thr):
        cand = thr + jnp.left_shift(jnp.int32(1), 30 - i)
        return jnp.where(count(lambda b: b >= cand) >= kf, cand, thr)

    thr = lax.fori_loop(0, 31, bit_body, thr)
    need = kf - count(lambda b: b > thr)

    def bias_body(c, ties_before):
        k0 = pl.multiple_of(c * KC, KC)
        blk = key_scr[:, pl.ds(k0, KC)]
        eq = jnp.where(blk == thr, 1.0, 0.0)
        prefix = ties_before + _dot(eq.astype(bf16), tri_ref[...])
        tie_ok = jnp.where(prefix <= need, eq, 0.0)
        chosen = jnp.where(blk > thr, 1.0, tie_ok)
        bias_scr[:, pl.ds(k0, KC)] = jnp.where(kpos_row(k0) <= t_col, jnp.where(chosen > 0.5, 0.0, NEG), NEG)
        return prefix[:, KC - 1:KC]

    lax.fori_loop(0, n_chunks, bias_body, jnp.zeros((TQ, 1), f32))

    def bias_fn(k0):
        return bias_scr[:, pl.ds(k0, KC)]

    qd = qd_ref[0]
    group_outs = []
    for g in range(KV_GROUPS):
        q4 = _stack_group_queries(qd, g, lo64)
        group_outs.append(_group_output(_masked_flash(q4, kd_ref, vd_ref, g, n_chunks, bias_fn), lo64))
    o = jnp.concatenate(group_outs, axis=1) * zd_ref[0].astype(f32)
    o_ref[0] = o.astype(bf16)


def _dsa_attention(qd, qi, misc, ki, kd, vd, zd, tri):
    B, S, _ = qd.shape
    row = lambda c: pl.BlockSpec((1, TQ, c), lambda b, i: (b, i, 0))
    whole = pl.BlockSpec((1, KV_GROUPS, S, LANES), lambda b, i: (b, 0, 0, 0))
    return pl.pallas_call(
        functools.partial(_dsa_kernel, top_k=min(DSA_TOPK_MAX, S // 4)),
        grid=(B, S // TQ),
        in_specs=[row(512), row(256), row(LANES), pl.BlockSpec((1, S, LANES), lambda b, i: (b, 0, 0)),
                  whole, whole, row(512), pl.BlockSpec((KC, KC), lambda b, i: (0, 0))],
        out_specs=row(512),
        out_shape=jax.ShapeDtypeStruct((B, S, 512), bf16),
        scratch_shapes=[pltpu.VMEM((TQ, S), jnp.int32), pltpu.VMEM((TQ, S), f32)],
        compiler_params=pltpu.CompilerParams(
            dimension_semantics=("arbitrary", "arbitrary"), vmem_limit_bytes=VMEM_LIMIT),
        name="dsa_attention",
    )(qd, qi, misc, ki, kd, vd, zd, tri)


def _outproj_kernel(x_ref, on_ref, od_ref, w_ref, o_ref):
    half = w_ref.shape[0] // 2
    o_ref[0] = x_ref[0] + _dot(on_ref[0], w_ref[0:half, :]) + _dot(od_ref[0], w_ref[half:, :])


def _out_projection(x, o_nsa, o_dsa, w_out):
    B, S, D = x.shape
    tm = min(TM_PROJ, S)
    row = lambda c: pl.BlockSpec((1, tm, c), lambda b, i: (b, i, 0))
    return pl.pallas_call(
        _outproj_kernel,
        grid=(B, S // tm),
        in_specs=[row(D), row(512), row(512), pl.BlockSpec(w_out.shape, lambda b, i: (0, 0))],
        out_specs=row(D),
        out_shape=jax.ShapeDtypeStruct((B, S, D), f32),
        compiler_params=pltpu.CompilerParams(
            dimension_semantics=("arbitrary", "arbitrary"), vmem_limit_bytes=VMEM_LIMIT),
        name="out_projection",
    )(x, o_nsa, o_dsa, w_out)


def _rope_tables(pos):
    half = HEAD_DIM // 2
    inv_freq = ROPE_THETA ** (-jnp.arange(half, dtype=f32) / half)
    ang = pos[:, None] * inv_freq[None, :]
    cos, sin = jnp.cos(ang), jnp.sin(ang)
    cos_t = jnp.tile(cos, (1, LANES // half))
    sin_t = jnp.tile(jnp.concatenate([-sin, sin], axis=1), (1, LANES // HEAD_DIM))
    return cos_t, sin_t


def _overlap_t(seq, n_cmp_pad):
    n_cmp = (seq - CMP_BLOCK) // CMP_STRIDE + 1
    n_sel = seq // SEL_BLOCK
    c_start = np.arange(n_cmp) * CMP_STRIDE
    j_start = np.arange(n_sel) * SEL_BLOCK
    ov = np.clip(np.minimum(c_start[:, None] + CMP_BLOCK, j_start[None, :] + SEL_BLOCK)
                 - np.maximum(c_start[:, None], j_start[None, :]), 0, None).astype(np.float32) / CMP_BLOCK
    out = np.zeros((n_sel, n_cmp_pad), np.float32)
    out[:, :n_cmp] = ov.T
    return out


def _block_diag2(w):
    z = jnp.zeros_like(w)
    return jnp.concatenate([jnp.concatenate([w, z], axis=-1), jnp.concatenate([z, w], axis=-1)], axis=-2)


def _layer(x, norm_gain, w_in, nsa_q_gain, nsa_kc_gain, nsa_ks_gain, nsa_kw_gain,
           cmp_pe_k, cmp_k_w1, cmp_k_b1, cmp_k_w2, cmp_pe_v, cmp_v_w1, cmp_v_b1, cmp_v_w2,
           dsa_q_gain, dsa_k_gain, w_out):
    B, S, _ = x.shape
    assert S % KC == 0 and S >= WINDOW + TQ
    n_cmp_pad = S // CMP_STRIDE

    cols = _column_permutation()
    w_ext = jnp.concatenate([w_in, jnp.zeros((D_MODEL, 1), w_in.dtype)], axis=1)
    w_perm = jnp.take(w_ext, cols, axis=1).astype(bf16)
    head_of_lane = np.arange(LANES) // HEAD_DIM
    bd = jnp.asarray(head_of_lane[:, None] == head_of_lane[None, :], bf16)
    dup = lambda v: jnp.tile(v.reshape(1, -1), (1, 2))
    gains = jnp.concatenate([dup(nsa_q_gain), dup(nsa_ks_gain), dup(nsa_kw_gain), dup(dsa_q_gain),
                             dup(dsa_k_gain), jnp.ones((3, LANES), f32)], axis=0)
    cos_t, sin_t = _rope_tables(jnp.arange(S, dtype=f32))

    (qn, qd, qi, ki, ks, kw, kd, vs, vw, vd, kc, vc, zn, zd, misc) = _in_projection(
        x, norm_gain.reshape(1, -1), w_perm, bd, cos_t, sin_t, gains)

    cmp_pos = (jnp.arange(n_cmp_pad) * CMP_STRIDE + CMP_BLOCK - 1).astype(f32)
    cos_c, sin_c = _rope_tables(cmp_pos)

    def cmp_weights(pe, w1, b1, w2):
        w1_bd = _block_diag2(w1.reshape(CMP_BLOCK, HEAD_DIM, CMP_HIDDEN)).astype(bf16)
        return jnp.tile(pe, (1, 2)), w1_bd, dup(b1), _block_diag2(w2).astype(bf16)

    kcmp = _compress(kc, *cmp_weights(cmp_pe_k, cmp_k_w1, cmp_k_b1, cmp_k_w2), bd, dup(nsa_kc_gain),
                     cos_c, sin_c, is_key=True)
    vcmp = _compress(vc, *cmp_weights(cmp_pe_v, cmp_v_w1, cmp_v_b1, cmp_v_w2), bd, dup(nsa_kc_gain),
                     cos_c, sin_c, is_key=False)

    ovt = jnp.asarray(_overlap_t(S, n_cmp_pad), bf16)
    o_nsa = _nsa_attention(qn, kcmp, vcmp, ks, vs, kw, vw, misc, zn, ovt)

    tri = jnp.asarray(np.triu(np.ones((KC, KC), np.float32)), bf16)
    o_dsa = _dsa_attention(qd, qi, misc, ki, kd, vd, zd, tri)

    return _out_projection(x, o_nsa, o_dsa, w_out.astype(bf16))


def kernel(x, norm_gain, w_in, nsa_q_gain, nsa_kc_gain, nsa_ks_gain, nsa_kw_gain, cmp_pe_k, cmp_k_w1,
           cmp_k_b1, cmp_k_w2, cmp_pe_v, cmp_v_w1, cmp_v_b1, cmp_v_w2, dsa_q_gain, dsa_k_gain, w_out):
    for l in range(norm_gain.shape[0]):
        x = _layer(x, norm_gain[l], w_in[l], nsa_q_gain[l], nsa_kc_gain[l], nsa_ks_gain[l], nsa_kw_gain[l],
                   cmp_pe_k[l], cmp_k_w1[l], cmp_k_b1[l], cmp_k_w2[l], cmp_pe_v[l], cmp_v_w1[l], cmp_v_b1[l],
                   cmp_v_w2[l], dsa_q_gain[l], dsa_k_gain[l], w_out[l])
    return x
```

```python
import functools

import numpy as np
import jax
import jax.numpy as jnp
from jax import lax
from jax.experimental import pallas as pl
from jax.experimental.pallas import tpu as pltpu

D_MODEL = 1024
HEAD_DIM = 64
NSA_HEADS = 8
DSA_HEADS = 8
KV_GROUPS = 2
HEADS_PER_GROUP = 4
IDX_HEADS = 4
CMP_BLOCK = 32
CMP_STRIDE = 16
CMP_HIDDEN = 256
SEL_BLOCK = 64
SEL_TOPN = 16
WINDOW = 512
DSA_TOPK_MAX = 256
ROPE_THETA = 10000.0
EPS = 1e-6
NEG = -1e30
FORCE = 1e6
ATTN_SCALE = HEAD_DIM ** -0.5
IDX_SCALE = HEAD_DIM ** -0.5
WI_SCALE = IDX_HEADS ** -0.5
LOG2E = 1.4426950408889634
Q_SCALE = ATTN_SCALE * LOG2E

LANES = 128
SUBLANES = 8
M_FLOOR = -5e29
F32_MIN_NORMAL = 1.1754943508222875e-38
NON_CAUSAL_MARK = -3.0e38
VMEM_LIMIT = 56 * 1024 * 1024

IN_WIDTHS = (512, 128, 128, 128, 128, 128, 128, 24, 512, 512, 128, 128, 256, 64, 4, 512)
IN_NAMES = ("q_n", "kc", "vc", "ks", "vs", "kw", "vw", "gate", "z_n",
            "q_d", "k_d", "v_d", "qi", "ki", "wi", "z_d")
IN_COLS = sum(IN_WIDTHS)

CH_QN, CH_KS, CH_KW, CH_QD, CH_KD, CH_QI, CH_KI = 0, 4, 5, 6, 10, 11, 13
CH_KC, CH_VC, CH_VS, CH_VW, CH_VD, CH_ZN, CH_ZD, CH_MISC = 14, 15, 16, 17, 18, 19, 23, 27
N_CHUNKS = 28
MISC_WI = 24

TM_PROJ = 512
TQ_NSA = 128
TQ_DSA = 256
KC = 512

f32 = jnp.float32
bf16 = jnp.bfloat16


def _column_permutation():
    off = dict(zip(IN_NAMES, np.cumsum((0,) + IN_WIDTHS[:-1])))
    zero = IN_COLS

    def rng(name, n):
        return np.arange(off[name], off[name] + n)

    parts = [rng("q_n", 512), rng("ks", 128), rng("kw", 128), rng("q_d", 512), rng("k_d", 128),
             rng("qi", 256), rng("ki", 64), rng("ki", 64),
             rng("kc", 128), rng("vc", 128), rng("vs", 128), rng("vw", 128), rng("v_d", 128),
             rng("z_n", 512), rng("z_d", 512),
             rng("gate", 24), rng("wi", 4), np.full(LANES - 28, zero)]
    cols = np.concatenate(parts)
    assert cols.shape[0] == N_CHUNKS * LANES
    return cols


def _nt_dot(a, b):
    return lax.dot_general(a, b, (((1,), (1,)), ((), ())), preferred_element_type=f32)


def _dot(a, b):
    return jnp.dot(a, b, preferred_element_type=f32)


def _split_bf16(v):
    hi = v.astype(bf16)
    lo = (v - hi.astype(f32)).astype(bf16)
    return hi, lo


def _head_rms_norm(y, bd, gain_row):
    hi, lo = _split_bf16(y * y)
    ssq = _dot(hi, bd) + _dot(lo, bd)
    return y * lax.rsqrt(ssq * (1.0 / HEAD_DIM) + EPS) * gain_row


def _rope(y, cos, sin_signed, lo32):
    partner = jnp.where(lo32, pltpu.roll(y, LANES - 32, 1), pltpu.roll(y, 32, 1))
    return y * cos + partner * sin_signed


def _dup_halves(y, lo64):
    r = pltpu.roll(y, 64, 1)
    return jnp.where(lo64, y, r), jnp.where(lo64, r, y)


def _value_with_ones(y, lo64):
    return jnp.where(lo64, y, 1.0), jnp.where(lo64, pltpu.roll(y, 64, 1), 1.0)


def _inproj_kernel(x_ref, ng_ref, w_ref, bd_ref, cos_ref, sin_ref, gains_ref,
                   qn_ref, qd_ref, qi_ref, ki_ref, ks_ref, kw_ref, kd_ref,
                   vs_ref, vw_ref, vd_ref, kc_ref, vc_ref, zn_ref, zd_ref, misc_ref):
    x = x_ref[0]
    ms = jnp.mean(x * x, axis=-1, keepdims=True)
    h = (x * lax.rsqrt(ms + EPS) * ng_ref[...]).astype(bf16)
    cos = cos_ref[...]
    sin = sin_ref[...]
    bd = bd_ref[...]
    lane = lax.broadcasted_iota(jnp.int32, (1, LANES), 1)
    lo32 = (lane % 64) < 32
    lo64 = lane < 64

    def proj(c0, n):
        return _dot(h, w_ref[:, c0 * LANES:(c0 + n) * LANES])

    def chunk(y, i):
        return y[:, i * LANES:(i + 1) * LANES]

    def normed_rope(y, gain_idx):
        return _rope(_head_rms_norm(y, bd, gains_ref[gain_idx:gain_idx + 1, :]), cos, sin, lo32)

    y = proj(CH_QN, 4)
    for i in range(4):
        qn_ref[0, :, i * LANES:(i + 1) * LANES] = (normed_rope(chunk(y, i), 0) * Q_SCALE).astype(bf16)
    y = proj(CH_QD, 4)
    for i in range(4):
        qd_ref[0, :, i * LANES:(i + 1) * LANES] = (normed_rope(chunk(y, i), 3) * Q_SCALE).astype(bf16)
    y = proj(CH_QI, 2)
    for i in range(2):
        qi_ref[0, :, i * LANES:(i + 1) * LANES] = (_rope(chunk(y, i), cos, sin, lo32) * IDX_SCALE).astype(bf16)
    ki_ref[0] = _rope(proj(CH_KI, 1), cos, sin, lo32).astype(bf16)

    for c0, gain_idx, ref in ((CH_KS, 1, ks_ref), (CH_KW, 2, kw_ref), (CH_KD, 4, kd_ref)):
        a, b = _dup_halves(normed_rope(proj(c0, 1), gain_idx), lo64)
        ref[0, 0] = a.astype(bf16)
        ref[0, 1] = b.astype(bf16)
    for c0, ref in ((CH_VS, vs_ref), (CH_VW, vw_ref), (CH_VD, vd_ref)):
        a, b = _value_with_ones(proj(c0, 1), lo64)
        ref[0, 0] = a.astype(bf16)
        ref[0, 1] = b.astype(bf16)

    kc_ref[0] = proj(CH_KC, 1)
    vc_ref[0] = proj(CH_VC, 1)
    z = proj(CH_ZN, 4)
    zn_ref[0] = (z * jax.nn.sigmoid(z)).astype(bf16)
    z = proj(CH_ZD, 4)
    zd_ref[0] = (z * jax.nn.sigmoid(z)).astype(bf16)
    m = proj(CH_MISC, 1)
    misc_ref[0] = jnp.where(lane < MISC_WI, jax.nn.sigmoid(m), m * WI_SCALE)


def _in_projection(x, norm_gain, w_perm, bd, cos_t, sin_t, gains):
    B, S, _ = x.shape
    tm = min(TM_PROJ, S)
    grid = (B, S // tm)
    row = lambda c: pl.BlockSpec((1, tm, c), lambda b, i: (b, i, 0))
    dup = pl.BlockSpec((1, KV_GROUPS, tm, LANES), lambda b, i: (b, 0, i, 0))
    const = lambda shape: pl.BlockSpec(shape, lambda b, i: tuple(0 for _ in shape))
    tab = pl.BlockSpec((tm, LANES), lambda b, i: (i, 0))
    sds = jax.ShapeDtypeStruct
    out_shape = (
        sds((B, S, 512), bf16), sds((B, S, 512), bf16), sds((B, S, 256), bf16), sds((B, S, LANES), bf16),
        sds((B, KV_GROUPS, S, LANES), bf16), sds((B, KV_GROUPS, S, LANES), bf16),
        sds((B, KV_GROUPS, S, LANES), bf16), sds((B, KV_GROUPS, S, LANES), bf16),
        sds((B, KV_GROUPS, S, LANES), bf16), sds((B, KV_GROUPS, S, LANES), bf16),
        sds((B, S, LANES), f32), sds((B, S, LANES), f32),
        sds((B, S, 512), bf16), sds((B, S, 512), bf16), sds((B, S, LANES), f32),
    )
    out_specs = (row(512), row(512), row(256), row(LANES), dup, dup, dup, dup, dup, dup,
                 row(LANES), row(LANES), row(512), row(512), row(LANES))
    return pl.pallas_call(
        _inproj_kernel,
        grid=grid,
        in_specs=[row(D_MODEL), const((1, D_MODEL)), const((D_MODEL, N_CHUNKS * LANES)),
                  const((LANES, LANES)), tab, tab, const((8, LANES))],
        out_specs=out_specs,
        out_shape=out_shape,
        compiler_params=pltpu.CompilerParams(
            dimension_semantics=("arbitrary", "arbitrary"), vmem_limit_bytes=VMEM_LIMIT),
        name="in_projection",
    )(x, norm_gain, w_perm, bd, cos_t, sin_t, gains)


def _compress_kernel(src_ref, pe_ref, w1_ref, b1_ref, w2_ref, bd_ref, gain_ref, cos_ref, sin_ref,
                     out_ref, *, n_rows, is_key):
    half = CMP_BLOCK // 2
    acc_a = jnp.zeros((n_rows, 2 * CMP_HIDDEN), f32)
    acc_b = jnp.zeros((n_rows, 2 * CMP_HIDDEN), f32)
    for l in range(half):
        rows = src_ref[0, pl.ds(l, n_rows, stride=CMP_STRIDE), :]
        acc_a = acc_a + _dot((rows + pe_ref[l:l + 1, :]).astype(bf16), w1_ref[l])
        acc_b = acc_b + _dot((rows + pe_ref[l + half:l + half + 1, :]).astype(bf16), w1_ref[l + half])
    pre = acc_a + pltpu.roll(acc_b, n_rows - 1, 0) + b1_ref[...]
    hid = pre * jax.nn.sigmoid(pre)
    out = _dot(hid.astype(bf16), w2_ref[...])
    lane = lax.broadcasted_iota(jnp.int32, (1, LANES), 1)
    if is_key:
        out = _head_rms_norm(out, bd_ref[...], gain_ref[...])
        out = _rope(out, cos_ref[...], sin_ref[...], (lane % 64) < 32)
        a, b = _dup_halves(out, lane < 64)
    else:
        a, b = _value_with_ones(out, lane < 64)
    out_ref[0, 0] = a.astype(bf16)
    out_ref[0, 1] = b.astype(bf16)


def _compress(src, pe_dup, w1_bd, b1_dup, w2_bd, bd, gain_dup, cos_c, sin_c, *, is_key):
    B, S, _ = src.shape
    n_rows = S // CMP_STRIDE
    const = lambda shape: pl.BlockSpec(shape, lambda b: tuple(0 for _ in shape))
    return pl.pallas_call(
        functools.partial(_compress_kernel, n_rows=n_rows, is_key=is_key),
        grid=(B,),
        in_specs=[pl.BlockSpec((1, S, LANES), lambda b: (b, 0, 0)),
                  const((CMP_BLOCK, LANES)), const((CMP_BLOCK, LANES, 2 * CMP_HIDDEN)),
                  const((1, 2 * CMP_HIDDEN)), const((2 * CMP_HIDDEN, LANES)), const((LANES, LANES)),
                  const((1, LANES)), const((n_rows, LANES)), const((n_rows, LANES))],
        out_specs=pl.BlockSpec((1, KV_GROUPS, n_rows, LANES), lambda b: (b, 0, 0, 0)),
        out_shape=jax.ShapeDtypeStruct((B, KV_GROUPS, n_rows, LANES), bf16),
        compiler_params=pltpu.CompilerParams(
            dimension_semantics=("arbitrary",), vmem_limit_bytes=VMEM_LIMIT),
        name="compress_k" if is_key else "compress_v",
    )(src, pe_dup, w1_bd, b1_dup, w2_bd, bd, gain_dup, cos_c, sin_c)


def _half_masked(pair, h, lo64):
    keep = lo64 if h % 2 == 0 else jnp.logical_not(lo64)
    return jnp.where(keep, pair, jnp.zeros_like(pair))


def _stack_group_queries(q, g, lo64):
    slabs = []
    for r in range(HEADS_PER_GROUP):
        h = g * HEADS_PER_GROUP + r
        slabs.append(_half_masked(q[:, (h // 2) * LANES:(h // 2 + 1) * LANES], h, lo64))
    return jnp.concatenate(slabs, axis=0)


def _masked_flash(q4s, k_ref, v_ref, n_chunks, bias_fns, tq):
    rows = HEADS_PER_GROUP * tq

    def body(c, carry):
        k0 = pl.multiple_of(c * KC, KC)
        new = []
        for g in range(KV_GROUPS):
            m, acc = carry[g]
            s = _nt_dot(q4s[g], k_ref[0, g, pl.ds(k0, KC), :])
            s = (s.reshape(HEADS_PER_GROUP, tq, KC) + bias_fns[g](k0)[None]).reshape(rows, KC)
            m_new = jnp.maximum(m, jnp.max(s, axis=-1, keepdims=True))
            p = jnp.exp2(s - m_new).astype(bf16)
            acc = jnp.exp2(m - m_new) * acc + _dot(p, v_ref[0, g, pl.ds(k0, KC), :])
            new.append((m_new, acc))
        return tuple(new)

    init = tuple((jnp.full((rows, 1), M_FLOOR, f32), jnp.zeros((rows, LANES), f32)) for _ in range(KV_GROUPS))
    out = lax.fori_loop(0, n_chunks, body, init)
    return [acc / acc[:, HEAD_DIM:HEAD_DIM + 1] for _, acc in out]


def _group_output(og, lo64, tq):
    pairs = []
    for p in range(2):
        even = og[(2 * p) * tq:(2 * p + 1) * tq]
        odd = og[(2 * p + 1) * tq:(2 * p + 2) * tq]
        pairs.append(jnp.where(lo64, even, pltpu.roll(odd, HEAD_DIM, 1)))
    return jnp.concatenate(pairs, axis=1)


def _top_n_blocks(imp, n_sel, top_n):
    groups = [imp[SUBLANES * a:SUBLANES * (a + 1)] for a in range(n_sel // SUBLANES)]
    jsub = lax.broadcasted_iota(jnp.int32, groups[0].shape, 0)
    ranks = [jnp.zeros(groups[0].shape, f32) for _ in groups]
    for i in range(n_sel):
        row = imp[i:i + 1, :]
        for a, grp in enumerate(groups):
            if SUBLANES * a > i:
                beats = row >= grp
            elif SUBLANES * (a + 1) - 1 < i:
                beats = row > grp
            else:
                tie = jnp.where(jsub + SUBLANES * a > i, 1.0, 0.0)
                ranks[a] = ranks[a] + jnp.where(row > grp, 1.0, jnp.where(row == grp, tie, 0.0))
                continue
            ranks[a] = ranks[a] + jnp.where(beats, 1.0, 0.0)
    rank = jnp.concatenate(ranks, axis=0)
    return jnp.where(rank < top_n, 1.0, 0.0)


def _nsa_kernel(qn_ref, kcmp_ref, vcmp_ref, ks_ref, vs_ref, kw_ref, vw_ref, misc_ref, zn_ref, ovt_ref,
                o_ref, *, n_cmp_pad, n_sel, top_n):
    tq = TQ_NSA
    t0 = pl.program_id(1) * tq
    lane = lax.broadcasted_iota(jnp.int32, (1, LANES), 1)
    lo64 = lane < 64
    q = qn_ref[0]
    misc = misc_ref[0]
    t_col = t0 + lax.broadcasted_iota(jnp.int32, (tq, 1), 0)
    t4 = jnp.concatenate([t_col] * HEADS_PER_GROUP, axis=0)
    n_chunks = (t0 + tq + KC - 1) // KC
    win_span = WINDOW + tq
    win_start = pl.multiple_of(jnp.maximum(t0 - WINDOW, 0), tq)
    ovt = ovt_ref[...]

    q4s, o_cmp, o_win, sels = [], [], [], []
    for g in range(KV_GROUPS):
        q4 = _stack_group_queries(q, g, lo64)
        q4s.append(q4)

        s = _nt_dot(q4, kcmp_ref[0, g])
        cmp_end = lax.broadcasted_iota(jnp.int32, (1, n_cmp_pad), 1) * CMP_STRIDE + (CMP_BLOCK - 1)
        vis = cmp_end <= t4
        s = jnp.where(vis, s, NEG)
        m = jnp.max(s, axis=-1, keepdims=True)
        e = jnp.where(vis, jnp.exp2(s - m), 0.0)
        l = jnp.sum(e, axis=-1, keepdims=True)
        p = e / jnp.maximum(l, 1e-30)
        o_cmp.append(_dot(p.astype(bf16), vcmp_ref[0, g]))

        psum = p[0:tq] + p[tq:2 * tq] + p[2 * tq:3 * tq] + p[3 * tq:4 * tq]
        hi, lo = _split_bf16(psum)
        imp = _nt_dot(ovt, hi) + _nt_dot(ovt, lo)
        jrow = lax.broadcasted_iota(jnp.int32, (n_sel, tq), 0)
        tl = t0 + lax.broadcasted_iota(jnp.int32, (n_sel, tq), 1)
        cur = lax.shift_right_logical(tl, 6)
        forced = (jrow == 0) | (jrow == cur) | (jrow == cur - 1)
        imp = jnp.where(forced, FORCE, imp)
        imp = jnp.where(jrow * SEL_BLOCK <= tl, imp, NEG)
        sels.append(_top_n_blocks(imp, n_sel, top_n).T.astype(bf16))

        s = _nt_dot(q4, kw_ref[0, g, pl.ds(win_start, win_span), :])
        diff = t4 - (win_start + lax.broadcasted_iota(jnp.int32, (1, win_span), 1))
        s = jnp.where((diff >= 0) & (diff < WINDOW), s, NEG)
        m = jnp.max(s, axis=-1, keepdims=True)
        acc = _dot(jnp.exp2(s - m).astype(bf16), vw_ref[0, g, pl.ds(win_start, win_span), :])
        o_win.append(acc / acc[:, HEAD_DIM:HEAD_DIM + 1])

    def sel_bias(g):
        def fn(k0):
            jj = lax.broadcasted_iota(jnp.int32, (n_sel, KC), 0)
            kk = lax.shift_right_logical(k0 + lax.broadcasted_iota(jnp.int32, (n_sel, KC), 1), 6)
            expand = jnp.where(jj == kk, 1.0, 0.0).astype(bf16)
            chosen = _dot(sels[g], expand)
            kpos = k0 + lax.broadcasted_iota(jnp.int32, (1, KC), 1)
            return jnp.where(kpos <= t_col, jnp.where(chosen > 0.5, 0.0, NEG), NEG)
        return fn

    o_sel = _masked_flash(q4s, ks_ref, vs_ref, n_chunks, [sel_bias(g) for g in range(KV_GROUPS)], tq)

    group_outs = []
    for g in range(KV_GROUPS):
        def gate(branch):
            cols = [misc[:, (g * HEADS_PER_GROUP + r) * 3 + branch:(g * HEADS_PER_GROUP + r) * 3 + branch + 1]
                    for r in range(HEADS_PER_GROUP)]
            return jnp.concatenate(cols, axis=0)

        og = gate(0) * o_cmp[g] + gate(1) * o_sel[g] + gate(2) * o_win[g]
        group_outs.append(_group_output(og, lo64, tq))

    o = jnp.concatenate(group_outs, axis=1) * zn_ref[0].astype(f32)
    o_ref[0] = o.astype(bf16)


def _nsa_attention(qn, kcmp, vcmp, ks, vs, kw, vw, misc, zn, ovt):
    B, S, _ = qn.shape
    tq = TQ_NSA
    n_cmp_pad = kcmp.shape[2]
    n_sel = S // SEL_BLOCK
    row = lambda c: pl.BlockSpec((1, tq, c), lambda b, i: (b, i, 0))
    whole = lambda n: pl.BlockSpec((1, KV_GROUPS, n, LANES), lambda b, i: (b, 0, 0, 0))
    kern = functools.partial(_nsa_kernel, n_cmp_pad=n_cmp_pad, n_sel=n_sel, top_n=min(SEL_TOPN, n_sel))
    return pl.pallas_call(
        kern,
        grid=(B, S // tq),
        in_specs=[row(512), whole(n_cmp_pad), whole(n_cmp_pad), whole(S), whole(S), whole(S), whole(S),
                  row(LANES), row(512), pl.BlockSpec((n_sel, n_cmp_pad), lambda b, i: (0, 0))],
        out_specs=row(512),
        out_shape=jax.ShapeDtypeStruct((B, S, 512), bf16),
        compiler_params=pltpu.CompilerParams(
            dimension_semantics=("arbitrary", "arbitrary"), vmem_limit_bytes=VMEM_LIMIT),
        name="nsa_attention",
    )(qn, kcmp, vcmp, ks, vs, kw, vw, misc, zn, ovt)


def _dsa_kernel(qd_ref, qi_ref, misc_ref, ki_ref, kd_ref, vd_ref, zd_ref, tri_ref,
                o_ref, hi_scr, mid_scr, low_scr, bias_scr, *, top_k):
    tq = TQ_DSA
    t0 = pl.program_id(1) * tq
    lane = lax.broadcasted_iota(jnp.int32, (1, LANES), 1)
    lo64 = lane < 64
    t_col = t0 + lax.broadcasted_iota(jnp.int32, (tq, 1), 0)
    n_chunks = (t0 + tq + KC - 1) // KC
    misc = misc_ref[0]
    qi = qi_ref[0]
    qi_heads = [_half_masked(qi[:, (h // 2) * LANES:(h // 2 + 1) * LANES], h, lo64) for h in range(IDX_HEADS)]
    wi = [misc[:, MISC_WI + h:MISC_WI + h + 1] for h in range(IDX_HEADS)]

    def causal(k0):
        return (k0 + lax.broadcasted_iota(jnp.int32, (1, KC), 1)) <= t_col

    def score_body(c, carry):
        k0 = pl.multiple_of(c * KC, KC)
        kib = ki_ref[0, pl.ds(k0, KC), :]
        sc = jnp.zeros((tq, KC), f32)
        for h in range(IDX_HEADS):
            sc = sc + jnp.maximum(_nt_dot(qi_heads[h], kib), 0.0) * wi[h]
        sc = jnp.where(jnp.abs(sc) < F32_MIN_NORMAL, 0.0, sc)
        bits = pltpu.bitcast(sc, jnp.int32)
        key = bits ^ (jnp.right_shift(bits, 31) & jnp.int32(0x7FFFFFFF))
        ok = causal(k0)
        hi = pltpu.bitcast(bits & jnp.int32(-65536), f32)
        hi_scr[:, pl.ds(k0, KC)] = jnp.where(ok, hi, NON_CAUSAL_MARK).astype(bf16)
        mid_scr[:, pl.ds(k0, KC)] = (jnp.right_shift(key, 8) & 0xFF).astype(f32).astype(bf16)
        low_scr[:, pl.ds(k0, KC)] = (key & 0xFF).astype(f32).astype(bf16)
        return carry

    lax.fori_loop(0, n_chunks, score_body, 0)

    one = jnp.ones((), bf16)
    zero = jnp.zeros((), bf16)

    def count(ref, pred):
        def body(c, acc):
            k0 = pl.multiple_of(c * KC, KC)
            blk = ref[:, pl.ds(k0, KC)]
            for u in range(KC // LANES):
                acc = acc + jnp.where(pred(blk[:, u * LANES:(u + 1) * LANES]), one, zero)
            return acc
        acc = lax.fori_loop(0, n_chunks, body, jnp.zeros((tq, LANES), bf16))
        return jnp.sum(acc.astype(f32), axis=-1, keepdims=True)

    def lanes_bf16(col):
        return jnp.broadcast_to(col, (tq, LANES)).astype(bf16)

    kf = float(top_k)

    def key16_to_float(k16):
        sign_mag = k16 ^ (jnp.right_shift(k16, 15) & 0x7FFF)
        return pltpu.bitcast(jnp.left_shift(sign_mag, 16), f32)

    def count_hi_ge(k16):
        cand = lanes_bf16(key16_to_float(k16))
        return count(hi_scr, lambda b: b >= cand)

    k16 = jnp.where(count_hi_ge(jnp.zeros((tq, 1), jnp.int32)) >= kf, jnp.int32(0), jnp.int32(-32768))

    def hi_bit(i, k16):
        cand = k16 + jnp.left_shift(jnp.int32(1), 14 - i)
        return jnp.where(count_hi_ge(cand) >= kf, cand, k16)

    k16 = lax.fori_loop(0, 15, hi_bit, k16)
    t1 = lanes_bf16(key16_to_float(k16))

    def refine(prev_ref, prev_thr, digit_ref):
        thr = jnp.concatenate([prev_thr] * (KC // LANES), axis=1)

        def body(c, carry):
            sl = pl.ds(pl.multiple_of(c * KC, KC), KC)
            prev = prev_ref[:, sl]
            digit_ref[:, sl] = jnp.where(prev > thr, jnp.asarray(256.0, bf16),
                                         jnp.where(prev == thr, digit_ref[:, sl], jnp.asarray(-1.0, bf16)))
            return carry
        lax.fori_loop(0, n_chunks, body, 0)

    def byte_threshold(ref):
        def bit(i, v):
            cand = v + jnp.left_shift(jnp.int32(1), 7 - i).astype(f32)
            cand_b = lanes_bf16(cand)
            return jnp.where(count(ref, lambda b: b >= cand_b) >= kf, cand, v)
        return lanes_bf16(lax.fori_loop(0, 8, bit, jnp.zeros((tq, 1), f32)))

    refine(hi_scr, t1, mid_scr)
    t2 = byte_threshold(mid_scr)
    refine(mid_scr, t2, low_scr)
    t3 = byte_threshold(low_scr)
    need = kf - count(low_scr, lambda b: b > t3)
    t3f = t3.astype(f32)[:, 0:1]

    def bias_body(c, ties_before):
        k0 = pl.multiple_of(c * KC, KC)
        blk = low_scr[:, pl.ds(k0, KC)].astype(f32)
        eq = jnp.where(blk == t3f, 1.0, 0.0)
        prefix = ties_before + _dot(eq.astype(bf16), tri_ref[...])
        chosen = jnp.where(blk > t3f, 1.0, jnp.where(prefix <= need, eq, 0.0))
        bias_scr[:, pl.ds(k0, KC)] = jnp.where(causal(k0), jnp.where(chosen > 0.5, 0.0, NEG), NEG)
        return prefix[:, KC - 1:KC]

    lax.fori_loop(0, n_chunks, bias_body, jnp.zeros((tq, 1), f32))

    def bias_fn(k0):
        return bias_scr[:, pl.ds(k0, KC)]

    qd = qd_ref[0]
    q4s = [_stack_group_queries(qd, g, lo64) for g in range(KV_GROUPS)]
    outs = _masked_flash(q4s, kd_ref, vd_ref, n_chunks, [bias_fn] * KV_GROUPS, tq)
    o = jnp.concatenate([_group_output(og, lo64, tq) for og in outs], axis=1) * zd_ref[0].astype(f32)
    o_ref[0] = o.astype(bf16)


def _dsa_attention(qd, qi, misc, ki, kd, vd, zd, tri):
    B, S, _ = qd.shape
    tq = TQ_DSA
    assert S // LANES <= 256
    row = lambda c: pl.BlockSpec((1, tq, c), lambda b, i: (b, i, 0))
    whole = pl.BlockSpec((1, KV_GROUPS, S, LANES), lambda b, i: (b, 0, 0, 0))
    return pl.pallas_call(
        functools.partial(_dsa_kernel, top_k=min(DSA_TOPK_MAX, S // 4)),
        grid=(B, S // tq),
        in_specs=[row(512), row(256), row(LANES), pl.BlockSpec((1, S, LANES), lambda b, i: (b, 0, 0)),
                  whole, whole, row(512), pl.BlockSpec((KC, KC), lambda b, i: (0, 0))],
        out_specs=row(512),
        out_shape=jax.ShapeDtypeStruct((B, S, 512), bf16),
        scratch_shapes=[pltpu.VMEM((tq, S), bf16), pltpu.VMEM((tq, S), bf16), pltpu.VMEM((tq, S), bf16),
                        pltpu.VMEM((tq, S), f32)],
        compiler_params=pltpu.CompilerParams(
            dimension_semantics=("arbitrary", "arbitrary"), vmem_limit_bytes=VMEM_LIMIT),
        name="dsa_attention",
    )(qd, qi, misc, ki, kd, vd, zd, tri)


def _outproj_kernel(x_ref, on_ref, od_ref, w_ref, o_ref):
    half = w_ref.shape[0] // 2
    o_ref[0] = x_ref[0] + _dot(on_ref[0], w_ref[0:half, :]) + _dot(od_ref[0], w_ref[half:, :])


def _out_projection(x, o_nsa, o_dsa, w_out):
    B, S, D = x.shape
    tm = min(TM_PROJ, S)
    row = lambda c: pl.BlockSpec((1, tm, c), lambda b, i: (b, i, 0))
    return pl.pallas_call(
        _outproj_kernel,
        grid=(B, S // tm),
        in_specs=[row(D), row(512), row(512), pl.BlockSpec(w_out.shape, lambda b, i: (0, 0))],
        out_specs=row(D),
        out_shape=jax.ShapeDtypeStruct((B, S, D), f32),
        compiler_params=pltpu.CompilerParams(
            dimension_semantics=("arbitrary", "arbitrary"), vmem_limit_bytes=VMEM_LIMIT),
        name="out_projection",
    )(x, o_nsa, o_dsa, w_out)


def _rope_tables(pos):
    half = HEAD_DIM // 2
    inv_freq = ROPE_THETA ** (-jnp.arange(half, dtype=f32) / half)
    ang = pos[:, None] * inv_freq[None, :]
    cos, sin = jnp.cos(ang), jnp.sin(ang)
    cos_t = jnp.tile(cos, (1, LANES // half))
    sin_t = jnp.tile(jnp.concatenate([-sin, sin], axis=1), (1, LANES // HEAD_DIM))
    return cos_t, sin_t


def _overlap_t(seq, n_cmp_pad):
    n_cmp = (seq - CMP_BLOCK) // CMP_STRIDE + 1
    n_sel = seq // SEL_BLOCK
    c_start = np.arange(n_cmp) * CMP_STRIDE
    j_start = np.arange(n_sel) * SEL_BLOCK
    ov = np.clip(np.minimum(c_start[:, None] + CMP_BLOCK, j_start[None, :] + SEL_BLOCK)
                 - np.maximum(c_start[:, None], j_start[None, :]), 0, None).astype(np.float32) / CMP_BLOCK
    out = np.zeros((n_sel, n_cmp_pad), np.float32)
    out[:, :n_cmp] = ov.T
    return out


def _block_diag2(w):
    z = jnp.zeros_like(w)
    return jnp.concatenate([jnp.concatenate([w, z], axis=-1), jnp.concatenate([z, w], axis=-1)], axis=-2)


def _layer(x, norm_gain, w_in, nsa_q_gain, nsa_kc_gain, nsa_ks_gain, nsa_kw_gain,
           cmp_pe_k, cmp_k_w1, cmp_k_b1, cmp_k_w2, cmp_pe_v, cmp_v_w1, cmp_v_b1, cmp_v_w2,
           dsa_q_gain, dsa_k_gain, w_out):
    B, S, _ = x.shape
    assert S % KC == 0 and S >= WINDOW + TQ_NSA
    n_cmp_pad = S // CMP_STRIDE

    cols = _column_permutation()
    w_ext = jnp.concatenate([w_in, jnp.zeros((D_MODEL, 1), w_in.dtype)], axis=1)
    w_perm = jnp.take(w_ext, cols, axis=1).astype(bf16)
    head_of_lane = np.arange(LANES) // HEAD_DIM
    bd = jnp.asarray(head_of_lane[:, None] == head_of_lane[None, :], bf16)
    dup = lambda v: jnp.tile(v.reshape(1, -1), (1, 2))
    gains = jnp.concatenate([dup(nsa_q_gain), dup(nsa_ks_gain), dup(nsa_kw_gain), dup(dsa_q_gain),
                             dup(dsa_k_gain), jnp.ones((3, LANES), f32)], axis=0)
    cos_t, sin_t = _rope_tables(jnp.arange(S, dtype=f32))

    (qn, qd, qi, ki, ks, kw, kd, vs, vw, vd, kc, vc, zn, zd, misc) = _in_projection(
        x, norm_gain.reshape(1, -1), w_perm, bd, cos_t, sin_t, gains)

    cmp_pos = (jnp.arange(n_cmp_pad) * CMP_STRIDE + CMP_BLOCK - 1).astype(f32)
    cos_c, sin_c = _rope_tables(cmp_pos)

    def cmp_weights(pe, w1, b1, w2):
        w1_bd = _block_diag2(w1.reshape(CMP_BLOCK, HEAD_DIM, CMP_HIDDEN)).astype(bf16)
        return jnp.tile(pe, (1, 2)), w1_bd, dup(b1), _block_diag2(w2).astype(bf16)

    kcmp = _compress(kc, *cmp_weights(cmp_pe_k, cmp_k_w1, cmp_k_b1, cmp_k_w2), bd, dup(nsa_kc_gain),
                     cos_c, sin_c, is_key=True)
    vcmp = _compress(vc, *cmp_weights(cmp_pe_v, cmp_v_w1, cmp_v_b1, cmp_v_w2), bd, dup(nsa_kc_gain),
                     cos_c, sin_c, is_key=False)

    ovt = jnp.asarray(_overlap_t(S, n_cmp_pad), bf16)
    o_nsa = _nsa_attention(qn, kcmp, vcmp, ks, vs, kw, vw, misc, zn, ovt)

    tri = jnp.asarray(np.triu(np.ones((KC, KC), np.float32)), bf16)
    o_dsa = _dsa_attention(qd, qi, misc, ki, kd, vd, zd, tri)

    return _out_projection(x, o_nsa, o_dsa, w_out.astype(bf16))


def kernel(x, norm_gain, w_in, nsa_q_gain, nsa_kc_gain, nsa_ks_gain, nsa_kw_gain, cmp_pe_k, cmp_k_w1,
           cmp_k_b1, cmp_k_w2, cmp_pe_v, cmp_v_w1, cmp_v_b1, cmp_v_w2, dsa_q_gain, dsa_k_gain, w_out):
    for l in range(norm_gain.shape[0]):
        x = _layer(x, norm_gain[l], w_in[l], nsa_q_gain[l], nsa_kc_gain[l], nsa_ks_gain[l], nsa_kw_gain[l],
                   cmp_pe_k[l], cmp_k_w1[l], cmp_k_b1[l], cmp_k_w2[l], cmp_pe_v[l], cmp_v_w1[l], cmp_v_b1[l],
                   cmp_v_w2[l], dsa_q_gain[l], dsa_k_gain[l], w_out[l])
    return x
```

```python
import functools

import numpy as np
import jax
import jax.numpy as jnp
from jax import lax
from jax.experimental import pallas as pl
from jax.experimental.pallas import tpu as pltpu

D_MODEL = 1024
HEAD_DIM = 64
NSA_HEADS = 8
DSA_HEADS = 8
KV_GROUPS = 2
HEADS_PER_GROUP = 4
IDX_HEADS = 4
CMP_BLOCK = 32
CMP_STRIDE = 16
CMP_HIDDEN = 256
SEL_BLOCK = 64
SEL_TOPN = 16
WINDOW = 512
DSA_TOPK_MAX = 256
ROPE_THETA = 10000.0
EPS = 1e-6
NEG = -1e30
FORCE = 1e6
ATTN_SCALE = HEAD_DIM ** -0.5
IDX_SCALE = HEAD_DIM ** -0.5
WI_SCALE = IDX_HEADS ** -0.5
LOG2E = 1.4426950408889634
Q_SCALE = ATTN_SCALE * LOG2E

LANES = 128
SUBLANES = 8
M_FLOOR = -5e29
NON_CAUSAL_MARK = -3.0e38
VMEM_LIMIT = 56 * 1024 * 1024

IN_WIDTHS = (512, 128, 128, 128, 128, 128, 128, 24, 512, 512, 128, 128, 256, 64, 4, 512)
IN_NAMES = ("q_n", "kc", "vc", "ks", "vs", "kw", "vw", "gate", "z_n",
            "q_d", "k_d", "v_d", "qi", "ki", "wi", "z_d")
IN_COLS = sum(IN_WIDTHS)

CH_QN, CH_KS, CH_KW, CH_QD, CH_KD, CH_QI, CH_KI = 0, 4, 5, 6, 10, 11, 13
CH_KC, CH_VC, CH_VS, CH_VW, CH_VD, CH_ZN, CH_ZD, CH_MISC = 14, 15, 16, 17, 18, 19, 23, 27
N_CHUNKS = 28
MISC_WI = 24

TM_PROJ = 512
TQ_NSA = 128
TQ_DSA = 256
KC = 512

f32 = jnp.float32
bf16 = jnp.bfloat16


def _column_permutation():
    off = dict(zip(IN_NAMES, np.cumsum((0,) + IN_WIDTHS[:-1])))
    zero = IN_COLS

    def rng(name, n):
        return np.arange(off[name], off[name] + n)

    parts = [rng("q_n", 512), rng("ks", 128), rng("kw", 128), rng("q_d", 512), rng("k_d", 128),
             rng("qi", 256), rng("ki", 64), rng("ki", 64),
             rng("kc", 128), rng("vc", 128), rng("vs", 128), rng("vw", 128), rng("v_d", 128),
             rng("z_n", 512), rng("z_d", 512),
             rng("gate", 24), rng("wi", 4), np.full(LANES - 28, zero)]
    cols = np.concatenate(parts)
    assert cols.shape[0] == N_CHUNKS * LANES
    return cols


def _nt_dot(a, b):
    return lax.dot_general(a, b, (((1,), (1,)), ((), ())), preferred_element_type=f32)


def _dot(a, b):
    return jnp.dot(a, b, preferred_element_type=f32)


def _split_bf16(v):
    hi = v.astype(bf16)
    lo = (v - hi.astype(f32)).astype(bf16)
    return hi, lo


def _head_rms_norm(y, bd, gain_row):
    hi, lo = _split_bf16(y * y)
    ssq = _dot(hi, bd) + _dot(lo, bd)
    return y * lax.rsqrt(ssq * (1.0 / HEAD_DIM) + EPS) * gain_row


def _rope(y, cos, sin_signed, lo32):
    partner = jnp.where(lo32, pltpu.roll(y, LANES - 32, 1), pltpu.roll(y, 32, 1))
    return y * cos + partner * sin_signed


def _dup_halves(y, lo64):
    r = pltpu.roll(y, 64, 1)
    return jnp.where(lo64, y, r), jnp.where(lo64, r, y)


def _value_with_ones(y, lo64):
    return jnp.where(lo64, y, 1.0), jnp.where(lo64, pltpu.roll(y, 64, 1), 1.0)


def _inproj_kernel(x_ref, ng_ref, w_ref, bd_ref, cos_ref, sin_ref, gains_ref,
                   qn_ref, qd_ref, qi_ref, ki_ref, ks_ref, kw_ref, kd_ref,
                   vs_ref, vw_ref, vd_ref, kc_ref, vc_ref, zn_ref, zd_ref, misc_ref):
    x = x_ref[0]
    ms = jnp.mean(x * x, axis=-1, keepdims=True)
    h = (x * lax.rsqrt(ms + EPS) * ng_ref[...]).astype(bf16)
    cos = cos_ref[...]
    sin = sin_ref[...]
    bd = bd_ref[...]
    lane = lax.broadcasted_iota(jnp.int32, (1, LANES), 1)
    lo32 = (lane % 64) < 32
    lo64 = lane < 64

    def proj(c0, n):
        return _dot(h, w_ref[:, c0 * LANES:(c0 + n) * LANES])

    def chunk(y, i):
        return y[:, i * LANES:(i + 1) * LANES]

    def normed_rope(y, gain_idx):
        return _rope(_head_rms_norm(y, bd, gains_ref[gain_idx:gain_idx + 1, :]), cos, sin, lo32)

    y = proj(CH_QN, 4)
    for i in range(4):
        qn_ref[0, :, i * LANES:(i + 1) * LANES] = (normed_rope(chunk(y, i), 0) * Q_SCALE).astype(bf16)
    y = proj(CH_QD, 4)
    for i in range(4):
        qd_ref[0, :, i * LANES:(i + 1) * LANES] = (normed_rope(chunk(y, i), 3) * Q_SCALE).astype(bf16)
    y = proj(CH_QI, 2)
    for i in range(2):
        qi_ref[0, :, i * LANES:(i + 1) * LANES] = (_rope(chunk(y, i), cos, sin, lo32) * IDX_SCALE).astype(bf16)
    ki_ref[0] = _rope(proj(CH_KI, 1), cos, sin, lo32).astype(bf16)

    for c0, gain_idx, ref in ((CH_KS, 1, ks_ref), (CH_KW, 2, kw_ref), (CH_KD, 4, kd_ref)):
        a, b = _dup_halves(normed_rope(proj(c0, 1), gain_idx), lo64)
        ref[0, 0] = a.astype(bf16)
        ref[0, 1] = b.astype(bf16)
    for c0, ref in ((CH_VS, vs_ref), (CH_VW, vw_ref), (CH_VD, vd_ref)):
        a, b = _value_with_ones(proj(c0, 1), lo64)
        ref[0, 0] = a.astype(bf16)
        ref[0, 1] = b.astype(bf16)

    kc_ref[0] = proj(CH_KC, 1)
    vc_ref[0] = proj(CH_VC, 1)
    z = proj(CH_ZN, 4)
    zn_ref[0] = (z * jax.nn.sigmoid(z)).astype(bf16)
    z = proj(CH_ZD, 4)
    zd_ref[0] = (z * jax.nn.sigmoid(z)).astype(bf16)
    m = proj(CH_MISC, 1)
    misc_ref[0] = jnp.where(lane < MISC_WI, jax.nn.sigmoid(m), m * WI_SCALE)


def _in_projection(x, norm_gain, w_perm, bd, cos_t, sin_t, gains):
    B, S, _ = x.shape
    tm = min(TM_PROJ, S)
    grid = (B, S // tm)
    row = lambda c: pl.BlockSpec((1, tm, c), lambda b, i: (b, i, 0))
    dup = pl.BlockSpec((1, KV_GROUPS, tm, LANES), lambda b, i: (b, 0, i, 0))
    const = lambda shape: pl.BlockSpec(shape, lambda b, i: tuple(0 for _ in shape))
    tab = pl.BlockSpec((tm, LANES), lambda b, i: (i, 0))
    sds = jax.ShapeDtypeStruct
    out_shape = (
        sds((B, S, 512), bf16), sds((B, S, 512), bf16), sds((B, S, 256), bf16), sds((B, S, LANES), bf16),
        sds((B, KV_GROUPS, S, LANES), bf16), sds((B, KV_GROUPS, S, LANES), bf16),
        sds((B, KV_GROUPS, S, LANES), bf16), sds((B, KV_GROUPS, S, LANES), bf16),
        sds((B, KV_GROUPS, S, LANES), bf16), sds((B, KV_GROUPS, S, LANES), bf16),
        sds((B, S, LANES), f32), sds((B, S, LANES), f32),
        sds((B, S, 512), bf16), sds((B, S, 512), bf16), sds((B, S, LANES), f32),
    )
    out_specs = (row(512), row(512), row(256), row(LANES), dup, dup, dup, dup, dup, dup,
                 row(LANES), row(LANES), row(512), row(512), row(LANES))
    return pl.pallas_call(
        _inproj_kernel,
        grid=grid,
        in_specs=[row(D_MODEL), const((1, D_MODEL)), const((D_MODEL, N_CHUNKS * LANES)),
                  const((LANES, LANES)), tab, tab, const((8, LANES))],
        out_specs=out_specs,
        out_shape=out_shape,
        compiler_params=pltpu.CompilerParams(
            dimension_semantics=("arbitrary", "arbitrary"), vmem_limit_bytes=VMEM_LIMIT),
        name="in_projection",
    )(x, norm_gain, w_perm, bd, cos_t, sin_t, gains)


def _compress_kernel(src_ref, pe_ref, w1_ref, b1_ref, w2_ref, bd_ref, gain_ref, cos_ref, sin_ref,
                     out_ref, *, n_rows, is_key):
    half = CMP_BLOCK // 2
    acc_a = jnp.zeros((n_rows, 2 * CMP_HIDDEN), f32)
    acc_b = jnp.zeros((n_rows, 2 * CMP_HIDDEN), f32)
    for l in range(half):
        rows = src_ref[0, pl.ds(l, n_rows, stride=CMP_STRIDE), :]
        acc_a = acc_a + _dot((rows + pe_ref[l:l + 1, :]).astype(bf16), w1_ref[l])
        acc_b = acc_b + _dot((rows + pe_ref[l + half:l + half + 1, :]).astype(bf16), w1_ref[l + half])
    pre = acc_a + pltpu.roll(acc_b, n_rows - 1, 0) + b1_ref[...]
    hid = pre * jax.nn.sigmoid(pre)
    out = _dot(hid.astype(bf16), w2_ref[...])
    lane = lax.broadcasted_iota(jnp.int32, (1, LANES), 1)
    if is_key:
        out = _head_rms_norm(out, bd_ref[...], gain_ref[...])
        out = _rope(out, cos_ref[...], sin_ref[...], (lane % 64) < 32)
        a, b = _dup_halves(out, lane < 64)
    else:
        a, b = _value_with_ones(out, lane < 64)
    out_ref[0, 0] = a.astype(bf16)
    out_ref[0, 1] = b.astype(bf16)


def _compress(src, pe_dup, w1_bd, b1_dup, w2_bd, bd, gain_dup, cos_c, sin_c, *, is_key):
    B, S, _ = src.shape
    n_rows = S // CMP_STRIDE
    const = lambda shape: pl.BlockSpec(shape, lambda b: tuple(0 for _ in shape))
    return pl.pallas_call(
        functools.partial(_compress_kernel, n_rows=n_rows, is_key=is_key),
        grid=(B,),
        in_specs=[pl.BlockSpec((1, S, LANES), lambda b: (b, 0, 0)),
                  const((CMP_BLOCK, LANES)), const((CMP_BLOCK, LANES, 2 * CMP_HIDDEN)),
                  const((1, 2 * CMP_HIDDEN)), const((2 * CMP_HIDDEN, LANES)), const((LANES, LANES)),
                  const((1, LANES)), const((n_rows, LANES)), const((n_rows, LANES))],
        out_specs=pl.BlockSpec((1, KV_GROUPS, n_rows, LANES), lambda b: (b, 0, 0, 0)),
        out_shape=jax.ShapeDtypeStruct((B, KV_GROUPS, n_rows, LANES), bf16),
        compiler_params=pltpu.CompilerParams(
            dimension_semantics=("arbitrary",), vmem_limit_bytes=VMEM_LIMIT),
        name="compress_k" if is_key else "compress_v",
    )(src, pe_dup, w1_bd, b1_dup, w2_bd, bd, gain_dup, cos_c, sin_c)


def _half_masked(pair, h, lo64):
    keep = lo64 if h % 2 == 0 else jnp.logical_not(lo64)
    return jnp.where(keep, pair, jnp.zeros_like(pair))


def _stack_group_queries(q, g, lo64):
    slabs = []
    for r in range(HEADS_PER_GROUP):
        h = g * HEADS_PER_GROUP + r
        slabs.append(_half_masked(q[:, (h // 2) * LANES:(h // 2 + 1) * LANES], h, lo64))
    return jnp.concatenate(slabs, axis=0)


def _masked_flash(q4s, k_ref, v_ref, n_chunks, bias_fns, tq):
    rows = HEADS_PER_GROUP * tq

    def body(c, carry):
        k0 = pl.multiple_of(c * KC, KC)
        new = []
        for g in range(KV_GROUPS):
            m, acc = carry[g]
            s = _nt_dot(q4s[g], k_ref[0, g, pl.ds(k0, KC), :])
            s = (s.reshape(HEADS_PER_GROUP, tq, KC) + bias_fns[g](k0)[None]).reshape(rows, KC)
            m_new = jnp.maximum(m, jnp.max(s, axis=-1, keepdims=True))
            p = jnp.exp2(s - m_new).astype(bf16)
            acc = jnp.exp2(m - m_new) * acc + _dot(p, v_ref[0, g, pl.ds(k0, KC), :])
            new.append((m_new, acc))
        return tuple(new)

    init = tuple((jnp.full((rows, 1), M_FLOOR, f32), jnp.zeros((rows, LANES), f32)) for _ in range(KV_GROUPS))
    out = lax.fori_loop(0, n_chunks, body, init)
    return [acc / acc[:, HEAD_DIM:HEAD_DIM + 1] for _, acc in out]


def _group_output(og, lo64, tq):
    pairs = []
    for p in range(2):
        even = og[(2 * p) * tq:(2 * p + 1) * tq]
        odd = og[(2 * p + 1) * tq:(2 * p + 2) * tq]
        pairs.append(jnp.where(lo64, even, pltpu.roll(odd, HEAD_DIM, 1)))
    return jnp.concatenate(pairs, axis=1)


def _top_n_blocks(imp, n_sel, top_n):
    groups = [imp[SUBLANES * a:SUBLANES * (a + 1)] for a in range(n_sel // SUBLANES)]
    jsub = lax.broadcasted_iota(jnp.int32, groups[0].shape, 0)
    ranks = [jnp.zeros(groups[0].shape, f32) for _ in groups]
    for i in range(n_sel):
        row = imp[i:i + 1, :]
        for a, grp in enumerate(groups):
            if SUBLANES * a > i:
                beats = row >= grp
            elif SUBLANES * (a + 1) - 1 < i:
                beats = row > grp
            else:
                tie = jnp.where(jsub + SUBLANES * a > i, 1.0, 0.0)
                ranks[a] = ranks[a] + jnp.where(row > grp, 1.0, jnp.where(row == grp, tie, 0.0))
                continue
            ranks[a] = ranks[a] + jnp.where(beats, 1.0, 0.0)
    rank = jnp.concatenate(ranks, axis=0)
    return jnp.where(rank < top_n, 1.0, 0.0)


def _nsa_kernel(qn_ref, kcmp_ref, vcmp_ref, ks_ref, vs_ref, kw_ref, vw_ref, misc_ref, zn_ref, ovt_ref,
                o_ref, *, n_cmp_pad, n_sel, top_n):
    tq = TQ_NSA
    t0 = pl.program_id(1) * tq
    lane = lax.broadcasted_iota(jnp.int32, (1, LANES), 1)
    lo64 = lane < 64
    q = qn_ref[0]
    misc = misc_ref[0]
    t_col = t0 + lax.broadcasted_iota(jnp.int32, (tq, 1), 0)
    t4 = jnp.concatenate([t_col] * HEADS_PER_GROUP, axis=0)
    n_chunks = (t0 + tq + KC - 1) // KC
    win_span = WINDOW + tq
    win_start = pl.multiple_of(jnp.maximum(t0 - WINDOW, 0), tq)
    ovt = ovt_ref[...]

    q4s, o_cmp, o_win, sels = [], [], [], []
    for g in range(KV_GROUPS):
        q4 = _stack_group_queries(q, g, lo64)
        q4s.append(q4)

        s = _nt_dot(q4, kcmp_ref[0, g])
        cmp_end = lax.broadcasted_iota(jnp.int32, (1, n_cmp_pad), 1) * CMP_STRIDE + (CMP_BLOCK - 1)
        vis = cmp_end <= t4
        s = jnp.where(vis, s, NEG)
        m = jnp.max(s, axis=-1, keepdims=True)
        e = jnp.where(vis, jnp.exp2(s - m), 0.0)
        l = jnp.sum(e, axis=-1, keepdims=True)
        p = e / jnp.maximum(l, 1e-30)
        o_cmp.append(_dot(p.astype(bf16), vcmp_ref[0, g]))

        psum = p[0:tq] + p[tq:2 * tq] + p[2 * tq:3 * tq] + p[3 * tq:4 * tq]
        hi, lo = _split_bf16(psum)
        imp = _nt_dot(ovt, hi) + _nt_dot(ovt, lo)
        jrow = lax.broadcasted_iota(jnp.int32, (n_sel, tq), 0)
        tl = t0 + lax.broadcasted_iota(jnp.int32, (n_sel, tq), 1)
        cur = lax.shift_right_logical(tl, 6)
        forced = (jrow == 0) | (jrow == cur) | (jrow == cur - 1)
        imp = jnp.where(forced, FORCE, imp)
        imp = jnp.where(jrow * SEL_BLOCK <= tl, imp, NEG)
        sels.append(_top_n_blocks(imp, n_sel, top_n).T.astype(bf16))

        s = _nt_dot(q4, kw_ref[0, g, pl.ds(win_start, win_span), :])
        diff = t4 - (win_start + lax.broadcasted_iota(jnp.int32, (1, win_span), 1))
        s = jnp.where((diff >= 0) & (diff < WINDOW), s, NEG)
        m = jnp.max(s, axis=-1, keepdims=True)
        acc = _dot(jnp.exp2(s - m).astype(bf16), vw_ref[0, g, pl.ds(win_start, win_span), :])
        o_win.append(acc / acc[:, HEAD_DIM:HEAD_DIM + 1])

    def sel_bias(g):
        def fn(k0):
            jj = lax.broadcasted_iota(jnp.int32, (n_sel, KC), 0)
            kk = lax.shift_right_logical(k0 + lax.broadcasted_iota(jnp.int32, (n_sel, KC), 1), 6)
            expand = jnp.where(jj == kk, 1.0, 0.0).astype(bf16)
            chosen = _dot(sels[g], expand)
            kpos = k0 + lax.broadcasted_iota(jnp.int32, (1, KC), 1)
            return jnp.where(kpos <= t_col, jnp.where(chosen > 0.5, 0.0, NEG), NEG)
        return fn

    o_sel = _masked_flash(q4s, ks_ref, vs_ref, n_chunks, [sel_bias(g) for g in range(KV_GROUPS)], tq)

    group_outs = []
    for g in range(KV_GROUPS):
        def gate(branch):
            cols = [misc[:, (g * HEADS_PER_GROUP + r) * 3 + branch:(g * HEADS_PER_GROUP + r) * 3 + branch + 1]
                    for r in range(HEADS_PER_GROUP)]
            return jnp.concatenate(cols, axis=0)

        og = gate(0) * o_cmp[g] + gate(1) * o_sel[g] + gate(2) * o_win[g]
        group_outs.append(_group_output(og, lo64, tq))

    o = jnp.concatenate(group_outs, axis=1) * zn_ref[0].astype(f32)
    o_ref[0] = o.astype(bf16)


def _nsa_attention(qn, kcmp, vcmp, ks, vs, kw, vw, misc, zn, ovt):
    B, S, _ = qn.shape
    tq = TQ_NSA
    n_cmp_pad = kcmp.shape[2]
    n_sel = S // SEL_BLOCK
    row = lambda c: pl.BlockSpec((1, tq, c), lambda b, i: (b, i, 0))
    whole = lambda n: pl.BlockSpec((1, KV_GROUPS, n, LANES), lambda b, i: (b, 0, 0, 0))
    kern = functools.partial(_nsa_kernel, n_cmp_pad=n_cmp_pad, n_sel=n_sel, top_n=min(SEL_TOPN, n_sel))
    return pl.pallas_call(
        kern,
        grid=(B, S // tq),
        in_specs=[row(512), whole(n_cmp_pad), whole(n_cmp_pad), whole(S), whole(S), whole(S), whole(S),
                  row(LANES), row(512), pl.BlockSpec((n_sel, n_cmp_pad), lambda b, i: (0, 0))],
        out_specs=row(512),
        out_shape=jax.ShapeDtypeStruct((B, S, 512), bf16),
        compiler_params=pltpu.CompilerParams(
            dimension_semantics=("arbitrary", "arbitrary"), vmem_limit_bytes=VMEM_LIMIT),
        name="nsa_attention",
    )(qn, kcmp, vcmp, ks, vs, kw, vw, misc, zn, ovt)


N_SLABS = 32
_TRANSPOSE_STAGES = ((16, 0x0000FFFF), (8, 0x00FF00FF), (4, 0x0F0F0F0F), (2, 0x33333333), (1, 0x55555555))


def _ordered_bits(v):
    return v ^ (jnp.right_shift(v, 31) & jnp.int32(0x7FFFFFFF))


def _bit_planes(score_ref, plane_ref):
    for stage, (j, mask) in enumerate(_TRANSPOSE_STAGES):
        if stage == 0:
            load = lambda k: _ordered_bits(pltpu.bitcast(score_ref[k], jnp.int32))
        else:
            load = lambda k: plane_ref[k]

        def body(p, carry, j=j, mask=mask, load=load):
            k = jnp.left_shift(p & ~(j - 1), 1) | (p & (j - 1))
            lo = load(k)
            hi = load(k + j)
            t = (lo ^ lax.shift_right_logical(hi, j)) & mask
            plane_ref[k] = lo ^ t
            plane_ref[k + j] = hi ^ jnp.left_shift(t, j)
            return carry

        lax.fori_loop(0, N_SLABS // 2, body, 0)


def _dsa_kernel(qd_ref, qi_ref, misc_ref, ki_ref, kd_ref, vd_ref, zd_ref, tri_ref,
                o_ref, score_scr, plane_scr, bias_scr, *, top_k, tq):
    t0 = pl.program_id(1) * tq
    lane = lax.broadcasted_iota(jnp.int32, (1, LANES), 1)
    lo64 = lane < 64
    t_col = t0 + lax.broadcasted_iota(jnp.int32, (tq, 1), 0)
    t_row = t0 + lax.broadcasted_iota(jnp.int32, (1, tq), 1)
    n_chunks = (t0 + tq + KC - 1) // KC
    slabs_per_chunk = KC // LANES
    int_min = jnp.int32(-2 ** 31)

    def chunk_slabs(c):
        return pl.ds(c * slabs_per_chunk, slabs_per_chunk)

    def to_col(row):
        return jnp.broadcast_to(row, (SUBLANES, tq)).T[:, 0:1]
    qi = qi_ref[0]
    qi_heads = [_half_masked(qi[:, (h // 2) * LANES:(h // 2 + 1) * LANES], h, lo64) for h in range(IDX_HEADS)]
    misc_t = misc_ref[0].T
    wi = [misc_t[MISC_WI + h:MISC_WI + h + 1, :] for h in range(IDX_HEADS)]

    def score_body(c, carry):
        k0 = pl.multiple_of(c * KC, KC)
        kib = ki_ref[0, pl.ds(k0, KC), :]
        sc = jnp.zeros((KC, tq), f32)
        for h in range(IDX_HEADS):
            sc = sc + jnp.maximum(_nt_dot(kib, qi_heads[h]), 0.0) * wi[h]
        sc = jnp.where(sc == 0.0, 0.0, sc)
        kpos = k0 + lax.broadcasted_iota(jnp.int32, (KC, 1), 0)
        sc = jnp.where(kpos <= t_row, sc, NON_CAUSAL_MARK)
        score_scr[chunk_slabs(c)] = sc.reshape(slabs_per_chunk, LANES, tq)
        return carry

    lax.fori_loop(0, n_chunks, score_body, 0)

    def fill_body(c, carry):
        score_scr[chunk_slabs(c)] = jnp.full((slabs_per_chunk, LANES, tq), NON_CAUSAL_MARK, f32)
        return carry

    lax.fori_loop(n_chunks, N_SLABS // slabs_per_chunk, fill_body, 0)

    _bit_planes(score_scr, plane_scr)
    kf = float(top_k)

    def select_bit(i, carry):
        alive, above, thr = carry
        plane = plane_scr[i] ^ jnp.where(i == 0, jnp.int32(-1), jnp.int32(0))
        ones = alive & plane
        c1 = jnp.sum(lax.population_count(ones).astype(f32), axis=0, keepdims=True)
        take = (above + c1) >= kf
        alive = jnp.where(take, ones, alive ^ ones)
        above = jnp.where(take, above, above + c1)
        thr = jnp.where(take, thr | jnp.left_shift(jnp.int32(1), 31 - i), thr)
        return alive, above, thr

    init = (jnp.full((LANES, tq), -1, jnp.int32), jnp.zeros((1, tq), f32), jnp.zeros((1, tq), jnp.int32))
    _, above, thr_u = lax.fori_loop(0, 32, select_bit, init)

    def key_to_score(key_row):
        return pltpu.bitcast(_ordered_bits(key_row), f32)

    def write_bias(thr_row, need_row):
        thr = to_col(thr_row)
        need = to_col(need_row)

        def body(c, carry):
            ties_before, n_above = carry
            k0 = pl.multiple_of(c * KC, KC)
            blk = score_scr[chunk_slabs(c)].reshape(KC, tq).T
            gt = jnp.where(blk > thr, 1.0, 0.0)
            eq = jnp.where(blk == thr, 1.0, 0.0)
            prefix = ties_before + _dot(eq.astype(bf16), tri_ref[...])
            chosen = gt + jnp.where(prefix <= need, eq, 0.0)
            kpos = k0 + lax.broadcasted_iota(jnp.int32, (1, KC), 1)
            bias_scr[:, pl.ds(k0, KC)] = jnp.where(kpos <= t_col, jnp.where(chosen > 0.5, 0.0, NEG), NEG)
            return prefix[:, KC - 1:KC], n_above + jnp.sum(gt, axis=-1, keepdims=True)

        zero = jnp.zeros((tq, 1), f32)
        return lax.fori_loop(0, n_chunks, body, (zero, zero))

    ties, n_above = write_bias(key_to_score(thr_u ^ int_min), kf - above)
    verified = (n_above == to_col(above)) & (n_above < kf) & (n_above + ties >= kf)

    @pl.when(jnp.sum(jnp.where(verified, 0.0, 1.0)) > 0.0)
    def _():
        def count(pred):
            def body(c, acc):
                blk = score_scr[chunk_slabs(c)].reshape(KC, tq)
                return acc + jnp.sum(jnp.where(pred(blk), 1.0, 0.0), axis=0, keepdims=True)
            return lax.fori_loop(0, N_SLABS // slabs_per_chunk, body, jnp.zeros((1, tq), f32))

        key = jnp.where(count(lambda b: b >= 0.0) >= kf, jnp.int32(0), int_min)

        def bit(i, key):
            cand = key + jnp.left_shift(jnp.int32(1), 30 - i)
            cand_f = key_to_score(cand)
            return jnp.where(count(lambda b: b >= cand_f) >= kf, cand, key)

        thr_f = key_to_score(lax.fori_loop(0, 31, bit, key))
        write_bias(thr_f, kf - count(lambda b: b > thr_f))

    def bias_fn(k0):
        return bias_scr[:, pl.ds(k0, KC)]

    qd = qd_ref[0]
    q4s = [_stack_group_queries(qd, g, lo64) for g in range(KV_GROUPS)]
    outs = _masked_flash(q4s, kd_ref, vd_ref, n_chunks, [bias_fn] * KV_GROUPS, tq)
    o = jnp.concatenate([_group_output(og, lo64, tq) for og in outs], axis=1) * zd_ref[0].astype(f32)
    o_ref[0] = o.astype(bf16)


def _dsa_attention(qd, qi, misc, ki, kd, vd, zd, tri):
    B, S, _ = qd.shape
    tq = TQ_DSA
    assert S <= N_SLABS * LANES
    row = lambda c: pl.BlockSpec((1, tq, c), lambda b, i: (b, i, 0))
    whole = pl.BlockSpec((1, KV_GROUPS, S, LANES), lambda b, i: (b, 0, 0, 0))
    return pl.pallas_call(
        functools.partial(_dsa_kernel, top_k=min(DSA_TOPK_MAX, S // 4), tq=tq),
        grid=(B, S // tq),
        in_specs=[row(512), row(256), row(LANES), pl.BlockSpec((1, S, LANES), lambda b, i: (b, 0, 0)),
                  whole, whole, row(512), pl.BlockSpec((KC, KC), lambda b, i: (0, 0))],
        out_specs=row(512),
        out_shape=jax.ShapeDtypeStruct((B, S, 512), bf16),
        scratch_shapes=[pltpu.VMEM((N_SLABS, LANES, tq), f32), pltpu.VMEM((N_SLABS, LANES, tq), jnp.int32),
                        pltpu.VMEM((tq, S), f32)],
        compiler_params=pltpu.CompilerParams(
            dimension_semantics=("arbitrary", "arbitrary"), vmem_limit_bytes=VMEM_LIMIT),
        name="dsa_attention",
    )(qd, qi, misc, ki, kd, vd, zd, tri)


def _outproj_kernel(x_ref, on_ref, od_ref, w_ref, o_ref):
    half = w_ref.shape[0] // 2
    o_ref[0] = x_ref[0] + _dot(on_ref[0], w_ref[0:half, :]) + _dot(od_ref[0], w_ref[half:, :])


def _out_projection(x, o_nsa, o_dsa, w_out):
    B, S, D = x.shape
    tm = min(TM_PROJ, S)
    row = lambda c: pl.BlockSpec((1, tm, c), lambda b, i: (b, i, 0))
    return pl.pallas_call(
        _outproj_kernel,
        grid=(B, S // tm),
        in_specs=[row(D), row(512), row(512), pl.BlockSpec(w_out.shape, lambda b, i: (0, 0))],
        out_specs=row(D),
        out_shape=jax.ShapeDtypeStruct((B, S, D), f32),
        compiler_params=pltpu.CompilerParams(
            dimension_semantics=("arbitrary", "arbitrary"), vmem_limit_bytes=VMEM_LIMIT),
        name="out_projection",
    )(x, o_nsa, o_dsa, w_out)


def _rope_tables(pos):
    half = HEAD_DIM // 2
    inv_freq = ROPE_THETA ** (-jnp.arange(half, dtype=f32) / half)
    ang = pos[:, None] * inv_freq[None, :]
    cos, sin = jnp.cos(ang), jnp.sin(ang)
    cos_t = jnp.tile(cos, (1, LANES // half))
    sin_t = jnp.tile(jnp.concatenate([-sin, sin], axis=1), (1, LANES // HEAD_DIM))
    return cos_t, sin_t


def _overlap_t(seq, n_cmp_pad):
    n_cmp = (seq - CMP_BLOCK) // CMP_STRIDE + 1
    n_sel = seq // SEL_BLOCK
    c_start = np.arange(n_cmp) * CMP_STRIDE
    j_start = np.arange(n_sel) * SEL_BLOCK
    ov = np.clip(np.minimum(c_start[:, None] + CMP_BLOCK, j_start[None, :] + SEL_BLOCK)
                 - np.maximum(c_start[:, None], j_start[None, :]), 0, None).astype(np.float32) / CMP_BLOCK
    out = np.zeros((n_sel, n_cmp_pad), np.float32)
    out[:, :n_cmp] = ov.T
    return out


def _block_diag2(w):
    z = jnp.zeros_like(w)
    return jnp.concatenate([jnp.concatenate([w, z], axis=-1), jnp.concatenate([z, w], axis=-1)], axis=-2)


def _layer(x, norm_gain, w_in, nsa_q_gain, nsa_kc_gain, nsa_ks_gain, nsa_kw_gain,
           cmp_pe_k, cmp_k_w1, cmp_k_b1, cmp_k_w2, cmp_pe_v, cmp_v_w1, cmp_v_b1, cmp_v_w2,
           dsa_q_gain, dsa_k_gain, w_out):
    B, S, _ = x.shape
    assert S % KC == 0 and S >= WINDOW + TQ_NSA
    n_cmp_pad = S // CMP_STRIDE

    cols = _column_permutation()
    w_ext = jnp.concatenate([w_in, jnp.zeros((D_MODEL, 1), w_in.dtype)], axis=1)
    w_perm = jnp.take(w_ext, cols, axis=1).astype(bf16)
    head_of_lane = np.arange(LANES) // HEAD_DIM
    bd = jnp.asarray(head_of_lane[:, None] == head_of_lane[None, :], bf16)
    dup = lambda v: jnp.tile(v.reshape(1, -1), (1, 2))
    gains = jnp.concatenate([dup(nsa_q_gain), dup(nsa_ks_gain), dup(nsa_kw_gain), dup(dsa_q_gain),
                             dup(dsa_k_gain), jnp.ones((3, LANES), f32)], axis=0)
    cos_t, sin_t = _rope_tables(jnp.arange(S, dtype=f32))

    (qn, qd, qi, ki, ks, kw, kd, vs, vw, vd, kc, vc, zn, zd, misc) = _in_projection(
        x, norm_gain.reshape(1, -1), w_perm, bd, cos_t, sin_t, gains)

    cmp_pos = (jnp.arange(n_cmp_pad) * CMP_STRIDE + CMP_BLOCK - 1).astype(f32)
    cos_c, sin_c = _rope_tables(cmp_pos)

    def cmp_weights(pe, w1, b1, w2):
        w1_bd = _block_diag2(w1.reshape(CMP_BLOCK, HEAD_DIM, CMP_HIDDEN)).astype(bf16)
        return jnp.tile(pe, (1, 2)), w1_bd, dup(b1), _block_diag2(w2).astype(bf16)

    kcmp = _compress(kc, *cmp_weights(cmp_pe_k, cmp_k_w1, cmp_k_b1, cmp_k_w2), bd, dup(nsa_kc_gain),
                     cos_c, sin_c, is_key=True)
    vcmp = _compress(vc, *cmp_weights(cmp_pe_v, cmp_v_w1, cmp_v_b1, cmp_v_w2), bd, dup(nsa_kc_gain),
                     cos_c, sin_c, is_key=False)

    ovt = jnp.asarray(_overlap_t(S, n_cmp_pad), bf16)
    o_nsa = _nsa_attention(qn, kcmp, vcmp, ks, vs, kw, vw, misc, zn, ovt)

    tri = jnp.asarray(np.triu(np.ones((KC, KC), np.float32)), bf16)
    o_dsa = _dsa_attention(qd, qi, misc, ki, kd, vd, zd, tri)

    return _out_projection(x, o_nsa, o_dsa, w_out.astype(bf16))


def kernel(x, norm_gain, w_in, nsa_q_gain, nsa_kc_gain, nsa_ks_gain, nsa_kw_gain, cmp_pe_k, cmp_k_w1,
           cmp_k_b1, cmp_k_w2, cmp_pe_v, cmp_v_w1, cmp_v_b1, cmp_v_w2, dsa_q_gain, dsa_k_gain, w_out):
    for l in range(norm_gain.shape[0]):
        x = _layer(x, norm_gain[l], w_in[l], nsa_q_gain[l], nsa_kc_gain[l], nsa_ks_gain[l], nsa_kw_gain[l],
                   cmp_pe_k[l], cmp_k_w1[l], cmp_k_b1[l], cmp_k_w2[l], cmp_pe_v[l], cmp_v_w1[l], cmp_v_b1[l],
                   cmp_v_w2[l], dsa_q_gain[l], dsa_k_gain[l], w_out[l])
    return x
```

```python
import functools

import numpy as np
import jax
import jax.numpy as jnp
from jax import lax
from jax.experimental import pallas as pl
from jax.experimental.pallas import tpu as pltpu

D_MODEL = 1024
HEAD_DIM = 64
NSA_HEADS = 8
DSA_HEADS = 8
KV_GROUPS = 2
HEADS_PER_GROUP = 4
IDX_HEADS = 4
CMP_BLOCK = 32
CMP_STRIDE = 16
CMP_HIDDEN = 256
SEL_BLOCK = 64
SEL_TOPN = 16
WINDOW = 512
DSA_TOPK_MAX = 256
ROPE_THETA = 10000.0
EPS = 1e-6
NEG = -1e30
FORCE = 1e6
ATTN_SCALE = HEAD_DIM ** -0.5
IDX_SCALE = HEAD_DIM ** -0.5
WI_SCALE = IDX_HEADS ** -0.5
LOG2E = 1.4426950408889634
Q_SCALE = ATTN_SCALE * LOG2E

LANES = 128
SUBLANES = 8
M_FLOOR = -5e29
NON_CAUSAL_MARK = -3.0e38
VMEM_LIMIT = 56 * 1024 * 1024

IN_WIDTHS = (512, 128, 128, 128, 128, 128, 128, 24, 512, 512, 128, 128, 256, 64, 4, 512)
IN_NAMES = ("q_n", "kc", "vc", "ks", "vs", "kw", "vw", "gate", "z_n",
            "q_d", "k_d", "v_d", "qi", "ki", "wi", "z_d")
IN_COLS = sum(IN_WIDTHS)

CH_QN, CH_KS, CH_KW, CH_QD, CH_KD, CH_QI, CH_KI = 0, 4, 5, 6, 10, 11, 13
CH_KC, CH_VC, CH_VS, CH_VW, CH_VD, CH_ZN, CH_ZD, CH_MISC = 14, 15, 16, 17, 18, 19, 23, 27
N_CHUNKS = 28
MISC_WI = 24

TM_PROJ = 512
TQ_NSA = 256
TQ_DSA = 256
KC = 512

f32 = jnp.float32
bf16 = jnp.bfloat16


def _column_permutation():
    off = dict(zip(IN_NAMES, np.cumsum((0,) + IN_WIDTHS[:-1])))
    zero = IN_COLS

    def rng(name, n):
        return np.arange(off[name], off[name] + n)

    parts = [rng("q_n", 512), rng("ks", 128), rng("kw", 128), rng("q_d", 512), rng("k_d", 128),
             rng("qi", 256), rng("ki", 64), rng("ki", 64),
             rng("kc", 128), rng("vc", 128), rng("vs", 128), rng("vw", 128), rng("v_d", 128),
             rng("z_n", 512), rng("z_d", 512),
             rng("gate", 24), rng("wi", 4), np.full(LANES - 28, zero)]
    cols = np.concatenate(parts)
    assert cols.shape[0] == N_CHUNKS * LANES
    return cols


def _nt_dot(a, b):
    return lax.dot_general(a, b, (((1,), (1,)), ((), ())), preferred_element_type=f32)


def _dot(a, b):
    return jnp.dot(a, b, preferred_element_type=f32)


def _split_bf16(v):
    hi = v.astype(bf16)
    lo = (v - hi.astype(f32)).astype(bf16)
    return hi, lo


def _head_rms_norm(y, bd, gain_row):
    hi, lo = _split_bf16(y * y)
    ssq = _dot(hi, bd) + _dot(lo, bd)
    return y * lax.rsqrt(ssq * (1.0 / HEAD_DIM) + EPS) * gain_row


def _rope(y, cos, sin_signed, lo32):
    partner = jnp.where(lo32, pltpu.roll(y, LANES - 32, 1), pltpu.roll(y, 32, 1))
    return y * cos + partner * sin_signed


def _dup_halves(y, lo64):
    r = pltpu.roll(y, 64, 1)
    return jnp.where(lo64, y, r), jnp.where(lo64, r, y)


def _value_with_ones(y, lo64):
    return jnp.where(lo64, y, 1.0), jnp.where(lo64, pltpu.roll(y, 64, 1), 1.0)


def _inproj_kernel(x_ref, ng_ref, w_ref, bd_ref, cos_ref, sin_ref, gains_ref,
                   qn_ref, qd_ref, qi_ref, ki_ref, ks_ref, kw_ref, kd_ref,
                   vs_ref, vw_ref, vd_ref, kc_ref, vc_ref, zn_ref, zd_ref, misc_ref):
    x = x_ref[0]
    ms = jnp.mean(x * x, axis=-1, keepdims=True)
    h = (x * lax.rsqrt(ms + EPS) * ng_ref[...]).astype(bf16)
    cos = cos_ref[...]
    sin = sin_ref[...]
    bd = bd_ref[...]
    lane = lax.broadcasted_iota(jnp.int32, (1, LANES), 1)
    lo32 = (lane % 64) < 32
    lo64 = lane < 64

    def proj(c0, n):
        return _dot(h, w_ref[:, c0 * LANES:(c0 + n) * LANES])

    def chunk(y, i):
        return y[:, i * LANES:(i + 1) * LANES]

    def normed_rope(y, gain_idx):
        return _rope(_head_rms_norm(y, bd, gains_ref[gain_idx:gain_idx + 1, :]), cos, sin, lo32)

    y = proj(CH_QN, 4)
    for i in range(4):
        qn_ref[0, :, i * LANES:(i + 1) * LANES] = (normed_rope(chunk(y, i), 0) * Q_SCALE).astype(bf16)
    y = proj(CH_QD, 4)
    for i in range(4):
        qd_ref[0, :, i * LANES:(i + 1) * LANES] = (normed_rope(chunk(y, i), 3) * Q_SCALE).astype(bf16)
    y = proj(CH_QI, 2)
    for i in range(2):
        qi_ref[0, :, i * LANES:(i + 1) * LANES] = (_rope(chunk(y, i), cos, sin, lo32) * IDX_SCALE).astype(bf16)
    ki_ref[0] = _rope(proj(CH_KI, 1), cos, sin, lo32).astype(bf16)

    for c0, gain_idx, ref in ((CH_KS, 1, ks_ref), (CH_KW, 2, kw_ref), (CH_KD, 4, kd_ref)):
        a, b = _dup_halves(normed_rope(proj(c0, 1), gain_idx), lo64)
        ref[0, 0] = a.astype(bf16)
        ref[0, 1] = b.astype(bf16)
    for c0, ref in ((CH_VS, vs_ref), (CH_VW, vw_ref), (CH_VD, vd_ref)):
        a, b = _value_with_ones(proj(c0, 1), lo64)
        ref[0, 0] = a.astype(bf16)
        ref[0, 1] = b.astype(bf16)

    kc_ref[0] = proj(CH_KC, 1)
    vc_ref[0] = proj(CH_VC, 1)
    z = proj(CH_ZN, 4)
    zn_ref[0] = (z * jax.nn.sigmoid(z)).astype(bf16)
    z = proj(CH_ZD, 4)
    zd_ref[0] = (z * jax.nn.sigmoid(z)).astype(bf16)
    m = proj(CH_MISC, 1)
    misc_ref[0] = jnp.where(lane < MISC_WI, jax.nn.sigmoid(m), m * WI_SCALE)


def _in_projection(x, norm_gain, w_perm, bd, cos_t, sin_t, gains):
    B, S, _ = x.shape
    tm = min(TM_PROJ, S)
    grid = (B, S // tm)
    row = lambda c: pl.BlockSpec((1, tm, c), lambda b, i: (b, i, 0))
    dup = pl.BlockSpec((1, KV_GROUPS, tm, LANES), lambda b, i: (b, 0, i, 0))
    const = lambda shape: pl.BlockSpec(shape, lambda b, i: tuple(0 for _ in shape))
    tab = pl.BlockSpec((tm, LANES), lambda b, i: (i, 0))
    sds = jax.ShapeDtypeStruct
    out_shape = (
        sds((B, S, 512), bf16), sds((B, S, 512), bf16), sds((B, S, 256), bf16), sds((B, S, LANES), bf16),
        sds((B, KV_GROUPS, S, LANES), bf16), sds((B, KV_GROUPS, S, LANES), bf16),
        sds((B, KV_GROUPS, S, LANES), bf16), sds((B, KV_GROUPS, S, LANES), bf16),
        sds((B, KV_GROUPS, S, LANES), bf16), sds((B, KV_GROUPS, S, LANES), bf16),
        sds((B, S, LANES), f32), sds((B, S, LANES), f32),
        sds((B, S, 512), bf16), sds((B, S, 512), bf16), sds((B, S, LANES), f32),
    )
    out_specs = (row(512), row(512), row(256), row(LANES), dup, dup, dup, dup, dup, dup,
                 row(LANES), row(LANES), row(512), row(512), row(LANES))
    return pl.pallas_call(
        _inproj_kernel,
        grid=grid,
        in_specs=[row(D_MODEL), const((1, D_MODEL)), const((D_MODEL, N_CHUNKS * LANES)),
                  const((LANES, LANES)), tab, tab, const((8, LANES))],
        out_specs=out_specs,
        out_shape=out_shape,
        compiler_params=pltpu.CompilerParams(
            dimension_semantics=("arbitrary", "arbitrary"), vmem_limit_bytes=VMEM_LIMIT),
        name="in_projection",
    )(x, norm_gain, w_perm, bd, cos_t, sin_t, gains)


def _compress_kernel(src_ref, pe_ref, w1_ref, b1_ref, w2_ref, bd_ref, gain_ref, cos_ref, sin_ref,
                     out_ref, *, n_rows, is_key):
    half = CMP_BLOCK // 2
    acc_a = jnp.zeros((n_rows, 2 * CMP_HIDDEN), f32)
    acc_b = jnp.zeros((n_rows, 2 * CMP_HIDDEN), f32)
    for l in range(half):
        rows = src_ref[0, pl.ds(l, n_rows, stride=CMP_STRIDE), :]
        acc_a = acc_a + _dot((rows + pe_ref[l:l + 1, :]).astype(bf16), w1_ref[l])
        acc_b = acc_b + _dot((rows + pe_ref[l + half:l + half + 1, :]).astype(bf16), w1_ref[l + half])
    pre = acc_a + pltpu.roll(acc_b, n_rows - 1, 0) + b1_ref[...]
    hid = pre * jax.nn.sigmoid(pre)
    out = _dot(hid.astype(bf16), w2_ref[...])
    lane = lax.broadcasted_iota(jnp.int32, (1, LANES), 1)
    if is_key:
        out = _head_rms_norm(out, bd_ref[...], gain_ref[...])
        out = _rope(out, cos_ref[...], sin_ref[...], (lane % 64) < 32)
        a, b = _dup_halves(out, lane < 64)
    else:
        a, b = _value_with_ones(out, lane < 64)
    out_ref[0, 0] = a.astype(bf16)
    out_ref[0, 1] = b.astype(bf16)


def _compress(src, pe_dup, w1_bd, b1_dup, w2_bd, bd, gain_dup, cos_c, sin_c, *, is_key):
    B, S, _ = src.shape
    n_rows = S // CMP_STRIDE
    const = lambda shape: pl.BlockSpec(shape, lambda b: tuple(0 for _ in shape))
    return pl.pallas_call(
        functools.partial(_compress_kernel, n_rows=n_rows, is_key=is_key),
        grid=(B,),
        in_specs=[pl.BlockSpec((1, S, LANES), lambda b: (b, 0, 0)),
                  const((CMP_BLOCK, LANES)), const((CMP_BLOCK, LANES, 2 * CMP_HIDDEN)),
                  const((1, 2 * CMP_HIDDEN)), const((2 * CMP_HIDDEN, LANES)), const((LANES, LANES)),
                  const((1, LANES)), const((n_rows, LANES)), const((n_rows, LANES))],
        out_specs=pl.BlockSpec((1, KV_GROUPS, n_rows, LANES), lambda b: (b, 0, 0, 0)),
        out_shape=jax.ShapeDtypeStruct((B, KV_GROUPS, n_rows, LANES), bf16),
        compiler_params=pltpu.CompilerParams(
            dimension_semantics=("arbitrary",), vmem_limit_bytes=VMEM_LIMIT),
        name="compress_k" if is_key else "compress_v",
    )(src, pe_dup, w1_bd, b1_dup, w2_bd, bd, gain_dup, cos_c, sin_c)


def _half_masked(pair, h, lo64):
    keep = lo64 if h % 2 == 0 else jnp.logical_not(lo64)
    return jnp.where(keep, pair, jnp.zeros_like(pair))


def _stack_group_queries(q, g, lo64):
    slabs = []
    for r in range(HEADS_PER_GROUP):
        h = g * HEADS_PER_GROUP + r
        slabs.append(_half_masked(q[:, (h // 2) * LANES:(h // 2 + 1) * LANES], h, lo64))
    return jnp.concatenate(slabs, axis=0)


def _masked_flash(q4s, k_ref, v_ref, n_chunks, bias_fns, tq):
    rows = HEADS_PER_GROUP * tq

    def body(c, carry):
        k0 = pl.multiple_of(c * KC, KC)
        new = []
        for g in range(KV_GROUPS):
            m, acc = carry[g]
            s = _nt_dot(q4s[g], k_ref[0, g, pl.ds(k0, KC), :])
            s = (s.reshape(HEADS_PER_GROUP, tq, KC) + bias_fns[g](k0)[None]).reshape(rows, KC)
            m_new = jnp.maximum(m, jnp.max(s, axis=-1, keepdims=True))
            p = jnp.exp2(s - m_new).astype(bf16)
            acc = jnp.exp2(m - m_new) * acc + _dot(p, v_ref[0, g, pl.ds(k0, KC), :])
            new.append((m_new, acc))
        return tuple(new)

    init = tuple((jnp.full((rows, 1), M_FLOOR, f32), jnp.zeros((rows, LANES), f32)) for _ in range(KV_GROUPS))
    out = lax.fori_loop(0, n_chunks, body, init)
    return [acc / acc[:, HEAD_DIM:HEAD_DIM + 1] for _, acc in out]


def _group_output(og, lo64, tq):
    pairs = []
    for p in range(2):
        even = og[(2 * p) * tq:(2 * p + 1) * tq]
        odd = og[(2 * p + 1) * tq:(2 * p + 2) * tq]
        pairs.append(jnp.where(lo64, even, pltpu.roll(odd, HEAD_DIM, 1)))
    return jnp.concatenate(pairs, axis=1)


def _top_n_blocks(imp, n_sel, top_n):
    groups = [imp[SUBLANES * a:SUBLANES * (a + 1)] for a in range(n_sel // SUBLANES)]
    jsub = lax.broadcasted_iota(jnp.int32, groups[0].shape, 0)
    ranks = [jnp.zeros(groups[0].shape, f32) for _ in groups]
    for i in range(n_sel):
        row = imp[i:i + 1, :]
        for a, grp in enumerate(groups):
            if SUBLANES * a > i:
                beats = row >= grp
            elif SUBLANES * (a + 1) - 1 < i:
                beats = row > grp
            else:
                tie = jnp.where(jsub + SUBLANES * a > i, 1.0, 0.0)
                ranks[a] = ranks[a] + jnp.where(row > grp, 1.0, jnp.where(row == grp, tie, 0.0))
                continue
            ranks[a] = ranks[a] + jnp.where(beats, 1.0, 0.0)
    rank = jnp.concatenate(ranks, axis=0)
    return jnp.where(rank < top_n, 1.0, 0.0)


def _nsa_kernel(qn_ref, kcmp_ref, vcmp_ref, ks_ref, vs_ref, kw_ref, vw_ref, misc_ref, zn_ref, ovt_ref,
                o_ref, *, n_cmp_pad, n_sel, top_n):
    tq = TQ_NSA
    t0 = pl.program_id(1) * tq
    lane = lax.broadcasted_iota(jnp.int32, (1, LANES), 1)
    lo64 = lane < 64
    q = qn_ref[0]
    misc = misc_ref[0]
    t_col = t0 + lax.broadcasted_iota(jnp.int32, (tq, 1), 0)
    n_chunks = (t0 + tq + KC - 1) // KC
    win_span = WINDOW + tq
    win_start = pl.multiple_of(jnp.maximum(t0 - WINDOW, 0), tq)
    ovt = ovt_ref[...]
    rows = HEADS_PER_GROUP * tq

    def add_bias(s, bias):
        return (s.reshape(HEADS_PER_GROUP, tq, s.shape[-1]) + bias[None]).reshape(rows, s.shape[-1])

    cmp_end = lax.broadcasted_iota(jnp.int32, (1, n_cmp_pad), 1) * CMP_STRIDE + (CMP_BLOCK - 1)
    cmp_bias = jnp.where(cmp_end <= t_col, 0.0, NEG)
    win_diff = t_col - (win_start + lax.broadcasted_iota(jnp.int32, (1, win_span), 1))
    win_bias = jnp.where((win_diff >= 0) & (win_diff < WINDOW), 0.0, NEG)

    q4s, o_cmp, o_win, sels = [], [], [], []
    for g in range(KV_GROUPS):
        q4 = _stack_group_queries(q, g, lo64)
        q4s.append(q4)

        s = add_bias(_nt_dot(q4, kcmp_ref[0, g]), cmp_bias)
        m = jnp.maximum(jnp.max(s, axis=-1, keepdims=True), M_FLOOR)
        e = jnp.exp2(s - m)
        l = jnp.sum(e, axis=-1, keepdims=True)
        p = e * (1.0 / jnp.maximum(l, 1e-30))
        o_cmp.append(_dot(p.astype(bf16), vcmp_ref[0, g]))

        psum = p[0:tq] + p[tq:2 * tq] + p[2 * tq:3 * tq] + p[3 * tq:4 * tq]
        hi, lo = _split_bf16(psum)
        imp = _nt_dot(ovt, hi) + _nt_dot(ovt, lo)
        jrow = lax.broadcasted_iota(jnp.int32, (n_sel, tq), 0)
        tl = t0 + lax.broadcasted_iota(jnp.int32, (n_sel, tq), 1)
        cur = lax.shift_right_logical(tl, 6)
        forced = (jrow == 0) | (jrow == cur) | (jrow == cur - 1)
        imp = jnp.where(forced, FORCE, imp)
        imp = jnp.where(jrow * SEL_BLOCK <= tl, imp, NEG)
        sels.append(_top_n_blocks(imp, n_sel, top_n).T.astype(bf16))

        s = add_bias(_nt_dot(q4, kw_ref[0, g, pl.ds(win_start, win_span), :]), win_bias)
        m = jnp.max(s, axis=-1, keepdims=True)
        acc = _dot(jnp.exp2(s - m).astype(bf16), vw_ref[0, g, pl.ds(win_start, win_span), :])
        o_win.append(acc / acc[:, HEAD_DIM:HEAD_DIM + 1])

    def sel_bias(g):
        def fn(k0):
            jj = lax.broadcasted_iota(jnp.int32, (n_sel, KC), 0)
            kk = lax.shift_right_logical(k0 + lax.broadcasted_iota(jnp.int32, (n_sel, KC), 1), 6)
            expand = jnp.where(jj == kk, 1.0, 0.0).astype(bf16)
            chosen = _dot(sels[g], expand)
            kpos = k0 + lax.broadcasted_iota(jnp.int32, (1, KC), 1)
            return jnp.where(kpos <= t_col, jnp.where(chosen > 0.5, 0.0, NEG), NEG)
        return fn

    o_sel = _masked_flash(q4s, ks_ref, vs_ref, n_chunks, [sel_bias(g) for g in range(KV_GROUPS)], tq)

    group_outs = []
    for g in range(KV_GROUPS):
        def gate(branch):
            cols = [misc[:, (g * HEADS_PER_GROUP + r) * 3 + branch:(g * HEADS_PER_GROUP + r) * 3 + branch + 1]
                    for r in range(HEADS_PER_GROUP)]
            return jnp.concatenate(cols, axis=0)

        og = gate(0) * o_cmp[g] + gate(1) * o_sel[g] + gate(2) * o_win[g]
        group_outs.append(_group_output(og, lo64, tq))

    o = jnp.concatenate(group_outs, axis=1) * zn_ref[0].astype(f32)
    o_ref[0] = o.astype(bf16)


def _nsa_attention(qn, kcmp, vcmp, ks, vs, kw, vw, misc, zn, ovt):
    B, S, _ = qn.shape
    tq = TQ_NSA
    n_cmp_pad = kcmp.shape[2]
    n_sel = S // SEL_BLOCK
    row = lambda c: pl.BlockSpec((1, tq, c), lambda b, i: (b, i, 0))
    whole = lambda n: pl.BlockSpec((1, KV_GROUPS, n, LANES), lambda b, i: (b, 0, 0, 0))
    kern = functools.partial(_nsa_kernel, n_cmp_pad=n_cmp_pad, n_sel=n_sel, top_n=min(SEL_TOPN, n_sel))
    return pl.pallas_call(
        kern,
        grid=(B, S // tq),
        in_specs=[row(512), whole(n_cmp_pad), whole(n_cmp_pad), whole(S), whole(S), whole(S), whole(S),
                  row(LANES), row(512), pl.BlockSpec((n_sel, n_cmp_pad), lambda b, i: (0, 0))],
        out_specs=row(512),
        out_shape=jax.ShapeDtypeStruct((B, S, 512), bf16),
        compiler_params=pltpu.CompilerParams(
            dimension_semantics=("arbitrary", "arbitrary"), vmem_limit_bytes=VMEM_LIMIT),
        name="nsa_attention",
    )(qn, kcmp, vcmp, ks, vs, kw, vw, misc, zn, ovt)


N_SLABS = 32
_TRANSPOSE_STAGES = ((16, 0x0000FFFF), (8, 0x00FF00FF), (4, 0x0F0F0F0F), (2, 0x33333333), (1, 0x55555555))


def _ordered_bits(v):
    return v ^ (jnp.right_shift(v, 31) & jnp.int32(0x7FFFFFFF))


def _bit_planes(score_ref, plane_ref):
    for stage, (j, mask) in enumerate(_TRANSPOSE_STAGES):
        if stage == 0:
            load = lambda k: _ordered_bits(pltpu.bitcast(score_ref[k], jnp.int32))
        else:
            load = lambda k: plane_ref[k]

        def body(p, carry, j=j, mask=mask, load=load):
            k = jnp.left_shift(p & ~(j - 1), 1) | (p & (j - 1))
            lo = load(k)
            hi = load(k + j)
            t = (lo ^ lax.shift_right_logical(hi, j)) & mask
            plane_ref[k] = lo ^ t
            plane_ref[k + j] = hi ^ jnp.left_shift(t, j)
            return carry

        lax.fori_loop(0, N_SLABS // 2, body, 0)


def _dsa_kernel(qd_ref, qi_ref, misc_ref, ki_ref, kd_ref, vd_ref, zd_ref, tri_ref,
                o_ref, score_scr, plane_scr, bias_scr, *, top_k, tq):
    t0 = pl.program_id(1) * tq
    lane = lax.broadcasted_iota(jnp.int32, (1, LANES), 1)
    lo64 = lane < 64
    t_col = t0 + lax.broadcasted_iota(jnp.int32, (tq, 1), 0)
    t_row = t0 + lax.broadcasted_iota(jnp.int32, (1, tq), 1)
    n_chunks = (t0 + tq + KC - 1) // KC
    slabs_per_chunk = KC // LANES
    int_min = jnp.int32(-2 ** 31)

    def chunk_slabs(c):
        return pl.ds(c * slabs_per_chunk, slabs_per_chunk)

    def to_col(row):
        return jnp.broadcast_to(row, (SUBLANES, tq)).T[:, 0:1]
    qi = qi_ref[0]
    qi_heads = [_half_masked(qi[:, (h // 2) * LANES:(h // 2 + 1) * LANES], h, lo64) for h in range(IDX_HEADS)]
    misc_t = misc_ref[0].T
    wi = [misc_t[MISC_WI + h:MISC_WI + h + 1, :] for h in range(IDX_HEADS)]

    def score_body(c, carry):
        k0 = pl.multiple_of(c * KC, KC)
        kib = ki_ref[0, pl.ds(k0, KC), :]
        sc = jnp.zeros((KC, tq), f32)
        for h in range(IDX_HEADS):
            sc = sc + jnp.maximum(_nt_dot(kib, qi_heads[h]), 0.0) * wi[h]
        sc = jnp.where(sc == 0.0, 0.0, sc)
        kpos = k0 + lax.broadcasted_iota(jnp.int32, (KC, 1), 0)
        sc = jnp.where(kpos <= t_row, sc, NON_CAUSAL_MARK)
        score_scr[chunk_slabs(c)] = sc.reshape(slabs_per_chunk, LANES, tq)
        return carry

    lax.fori_loop(0, n_chunks, score_body, 0)

    def fill_body(c, carry):
        score_scr[chunk_slabs(c)] = jnp.full((slabs_per_chunk, LANES, tq), NON_CAUSAL_MARK, f32)
        return carry

    lax.fori_loop(n_chunks, N_SLABS // slabs_per_chunk, fill_body, 0)

    _bit_planes(score_scr, plane_scr)
    kf = float(top_k)

    def select_bit(i, carry):
        alive, above, thr = carry
        plane = plane_scr[i] ^ jnp.where(i == 0, jnp.int32(-1), jnp.int32(0))
        ones = alive & plane
        c1 = jnp.sum(lax.population_count(ones).astype(f32), axis=0, keepdims=True)
        take = (above + c1) >= kf
        alive = jnp.where(take, ones, alive ^ ones)
        above = jnp.where(take, above, above + c1)
        thr = jnp.where(take, thr | jnp.left_shift(jnp.int32(1), 31 - i), thr)
        return alive, above, thr

    init = (jnp.full((LANES, tq), -1, jnp.int32), jnp.zeros((1, tq), f32), jnp.zeros((1, tq), jnp.int32))
    _, above, thr_u = lax.fori_loop(0, 32, select_bit, init)

    def key_to_score(key_row):
        return pltpu.bitcast(_ordered_bits(key_row), f32)

    def write_bias(thr_row, need_row):
        thr = to_col(thr_row)
        need = to_col(need_row)

        def body(c, carry):
            ties_before, n_above = carry
            k0 = pl.multiple_of(c * KC, KC)
            blk = score_scr[chunk_slabs(c)].reshape(KC, tq).T
            gt = jnp.where(blk > thr, 1.0, 0.0)
            eq = jnp.where(blk == thr, 1.0, 0.0)
            prefix = ties_before + _dot(eq.astype(bf16), tri_ref[...])
            chosen = gt + jnp.where(prefix <= need, eq, 0.0)
            kpos = k0 + lax.broadcasted_iota(jnp.int32, (1, KC), 1)
            bias_scr[:, pl.ds(k0, KC)] = jnp.where(kpos <= t_col, jnp.where(chosen > 0.5, 0.0, NEG), NEG)
            return prefix[:, KC - 1:KC], n_above + jnp.sum(gt, axis=-1, keepdims=True)

        zero = jnp.zeros((tq, 1), f32)
        return lax.fori_loop(0, n_chunks, body, (zero, zero))

    ties, n_above = write_bias(key_to_score(thr_u ^ int_min), kf - above)
    verified = (n_above == to_col(above)) & (n_above < kf) & (n_above + ties >= kf)

    @pl.when(jnp.sum(jnp.where(verified, 0.0, 1.0)) > 0.0)
    def _():
        def count(pred):
            def body(c, acc):
                blk = score_scr[chunk_slabs(c)].reshape(KC, tq)
                return acc + jnp.sum(jnp.where(pred(blk), 1.0, 0.0), axis=0, keepdims=True)
            return lax.fori_loop(0, N_SLABS // slabs_per_chunk, body, jnp.zeros((1, tq), f32))

        key = jnp.where(count(lambda b: b >= 0.0) >= kf, jnp.int32(0), int_min)

        def bit(i, key):
            cand = key + jnp.left_shift(jnp.int32(1), 30 - i)
            cand_f = key_to_score(cand)
            return jnp.where(count(lambda b: b >= cand_f) >= kf, cand, key)

        thr_f = key_to_score(lax.fori_loop(0, 31, bit, key))
        write_bias(thr_f, kf - count(lambda b: b > thr_f))

    def bias_fn(k0):
        return bias_scr[:, pl.ds(k0, KC)]

    qd = qd_ref[0]
    q4s = [_stack_group_queries(qd, g, lo64) for g in range(KV_GROUPS)]
    outs = _masked_flash(q4s, kd_ref, vd_ref, n_chunks, [bias_fn] * KV_GROUPS, tq)
    o = jnp.concatenate([_group_output(og, lo64, tq) for og in outs], axis=1) * zd_ref[0].astype(f32)
    o_ref[0] = o.astype(bf16)


def _dsa_attention(qd, qi, misc, ki, kd, vd, zd, tri):
    B, S, _ = qd.shape
    tq = TQ_DSA
    assert S <= N_SLABS * LANES
    row = lambda c: pl.BlockSpec((1, tq, c), lambda b, i: (b, i, 0))
    whole = pl.BlockSpec((1, KV_GROUPS, S, LANES), lambda b, i: (b, 0, 0, 0))
    return pl.pallas_call(
        functools.partial(_dsa_kernel, top_k=min(DSA_TOPK_MAX, S // 4), tq=tq),
        grid=(B, S // tq),
        in_specs=[row(512), row(256), row(LANES), pl.BlockSpec((1, S, LANES), lambda b, i: (b, 0, 0)),
                  whole, whole, row(512), pl.BlockSpec((KC, KC), lambda b, i: (0, 0))],
        out_specs=row(512),
        out_shape=jax.ShapeDtypeStruct((B, S, 512), bf16),
        scratch_shapes=[pltpu.VMEM((N_SLABS, LANES, tq), f32), pltpu.VMEM((N_SLABS, LANES, tq), jnp.int32),
                        pltpu.VMEM((tq, S), f32)],
        compiler_params=pltpu.CompilerParams(
            dimension_semantics=("arbitrary", "arbitrary"), vmem_limit_bytes=VMEM_LIMIT),
        name="dsa_attention",
    )(qd, qi, misc, ki, kd, vd, zd, tri)


def _outproj_kernel(x_ref, on_ref, od_ref, w_ref, o_ref):
    half = w_ref.shape[0] // 2
    o_ref[0] = x_ref[0] + _dot(on_ref[0], w_ref[0:half, :]) + _dot(od_ref[0], w_ref[half:, :])


def _out_projection(x, o_nsa, o_dsa, w_out):
    B, S, D = x.shape
    tm = min(TM_PROJ, S)
    row = lambda c: pl.BlockSpec((1, tm, c), lambda b, i: (b, i, 0))
    return pl.pallas_call(
        _outproj_kernel,
        grid=(B, S // tm),
        in_specs=[row(D), row(512), row(512), pl.BlockSpec(w_out.shape, lambda b, i: (0, 0))],
        out_specs=row(D),
        out_shape=jax.ShapeDtypeStruct((B, S, D), f32),
        compiler_params=pltpu.CompilerParams(
            dimension_semantics=("arbitrary", "arbitrary"), vmem_limit_bytes=VMEM_LIMIT),
        name="out_projection",
    )(x, o_nsa, o_dsa, w_out)


def _rope_tables(pos):
    half = HEAD_DIM // 2
    inv_freq = ROPE_THETA ** (-jnp.arange(half, dtype=f32) / half)
    ang = pos[:, None] * inv_freq[None, :]
    cos, sin = jnp.cos(ang), jnp.sin(ang)
    cos_t = jnp.tile(cos, (1, LANES // half))
    sin_t = jnp.tile(jnp.concatenate([-sin, sin], axis=1), (1, LANES // HEAD_DIM))
    return cos_t, sin_t


def _overlap_t(seq, n_cmp_pad):
    n_cmp = (seq - CMP_BLOCK) // CMP_STRIDE + 1
    n_sel = seq // SEL_BLOCK
    c_start = np.arange(n_cmp) * CMP_STRIDE
    j_start = np.arange(n_sel) * SEL_BLOCK
    ov = np.clip(np.minimum(c_start[:, None] + CMP_BLOCK, j_start[None, :] + SEL_BLOCK)
                 - np.maximum(c_start[:, None], j_start[None, :]), 0, None).astype(np.float32) / CMP_BLOCK
    out = np.zeros((n_sel, n_cmp_pad), np.float32)
    out[:, :n_cmp] = ov.T
    return out


def _block_diag2(w):
    z = jnp.zeros_like(w)
    return jnp.concatenate([jnp.concatenate([w, z], axis=-1), jnp.concatenate([z, w], axis=-1)], axis=-2)


def _layer(x, norm_gain, w_in, nsa_q_gain, nsa_kc_gain, nsa_ks_gain, nsa_kw_gain,
           cmp_pe_k, cmp_k_w1, cmp_k_b1, cmp_k_w2, cmp_pe_v, cmp_v_w1, cmp_v_b1, cmp_v_w2,
           dsa_q_gain, dsa_k_gain, w_out):
    B, S, _ = x.shape
    assert S % KC == 0 and S >= WINDOW + TQ_NSA
    n_cmp_pad = S // CMP_STRIDE

    cols = _column_permutation()
    w_ext = jnp.concatenate([w_in, jnp.zeros((D_MODEL, 1), w_in.dtype)], axis=1)
    w_perm = jnp.take(w_ext, cols, axis=1).astype(bf16)
    head_of_lane = np.arange(LANES) // HEAD_DIM
    bd = jnp.asarray(head_of_lane[:, None] == head_of_lane[None, :], bf16)
    dup = lambda v: jnp.tile(v.reshape(1, -1), (1, 2))
    gains = jnp.concatenate([dup(nsa_q_gain), dup(nsa_ks_gain), dup(nsa_kw_gain), dup(dsa_q_gain),
                             dup(dsa_k_gain), jnp.ones((3, LANES), f32)], axis=0)
    cos_t, sin_t = _rope_tables(jnp.arange(S, dtype=f32))

    (qn, qd, qi, ki, ks, kw, kd, vs, vw, vd, kc, vc, zn, zd, misc) = _in_projection(
        x, norm_gain.reshape(1, -1), w_perm, bd, cos_t, sin_t, gains)

    cmp_pos = (jnp.arange(n_cmp_pad) * CMP_STRIDE + CMP_BLOCK - 1).astype(f32)
    cos_c, sin_c = _rope_tables(cmp_pos)

    def cmp_weights(pe, w1, b1, w2):
        w1_bd = _block_diag2(w1.reshape(CMP_BLOCK, HEAD_DIM, CMP_HIDDEN)).astype(bf16)
        return jnp.tile(pe, (1, 2)), w1_bd, dup(b1), _block_diag2(w2).astype(bf16)

    kcmp = _compress(kc, *cmp_weights(cmp_pe_k, cmp_k_w1, cmp_k_b1, cmp_k_w2), bd, dup(nsa_kc_gain),
                     cos_c, sin_c, is_key=True)
    vcmp = _compress(vc, *cmp_weights(cmp_pe_v, cmp_v_w1, cmp_v_b1, cmp_v_w2), bd, dup(nsa_kc_gain),
                     cos_c, sin_c, is_key=False)

    ovt = jnp.asarray(_overlap_t(S, n_cmp_pad), bf16)
    o_nsa = _nsa_attention(qn, kcmp, vcmp, ks, vs, kw, vw, misc, zn, ovt)

    tri = jnp.asarray(np.triu(np.ones((KC, KC), np.float32)), bf16)
    o_dsa = _dsa_attention(qd, qi, misc, ki, kd, vd, zd, tri)

    return _out_projection(x, o_nsa, o_dsa, w_out.astype(bf16))


def kernel(x, norm_gain, w_in, nsa_q_gain, nsa_kc_gain, nsa_ks_gain, nsa_kw_gain, cmp_pe_k, cmp_k_w1,
           cmp_k_b1, cmp_k_w2, cmp_pe_v, cmp_v_w1, cmp_v_b1, cmp_v_w2, dsa_q_gain, dsa_k_gain, w_out):
    for l in range(norm_gain.shape[0]):
        x = _layer(x, norm_gain[l], w_in[l], nsa_q_gain[l], nsa_kc_gain[l], nsa_ks_gain[l], nsa_kw_gain[l],
                   cmp_pe_k[l], cmp_k_w1[l], cmp_k_b1[l], cmp_k_w2[l], cmp_pe_v[l], cmp_v_w1[l], cmp_v_b1[l],
                   cmp_v_w2[l], dsa_q_gain[l], dsa_k_gain[l], w_out[l])
    return x
```

```python
import functools

import numpy as np
import jax
import jax.numpy as jnp
from jax import lax
from jax.experimental import pallas as pl
from jax.experimental.pallas import tpu as pltpu

D_MODEL = 1024
HEAD_DIM = 64
NSA_HEADS = 8
DSA_HEADS = 8
KV_GROUPS = 2
HEADS_PER_GROUP = 4
IDX_HEADS = 4
CMP_BLOCK = 32
CMP_STRIDE = 16
CMP_HIDDEN = 256
SEL_BLOCK = 64
SEL_TOPN = 16
WINDOW = 512
DSA_TOPK_MAX = 256
ROPE_THETA = 10000.0
EPS = 1e-6
NEG = -1e30
FORCE = 1e6
ATTN_SCALE = HEAD_DIM ** -0.5
IDX_SCALE = HEAD_DIM ** -0.5
WI_SCALE = IDX_HEADS ** -0.5
LOG2E = 1.4426950408889634
Q_SCALE = ATTN_SCALE * LOG2E

LANES = 128
SUBLANES = 8
M_FLOOR = -5e29
NON_CAUSAL_MARK = -3.0e38
VMEM_LIMIT = 56 * 1024 * 1024

IN_WIDTHS = (512, 128, 128, 128, 128, 128, 128, 24, 512, 512, 128, 128, 256, 64, 4, 512)
IN_NAMES = ("q_n", "kc", "vc", "ks", "vs", "kw", "vw", "gate", "z_n",
            "q_d", "k_d", "v_d", "qi", "ki", "wi", "z_d")
IN_COLS = sum(IN_WIDTHS)

CH_QN, CH_KS, CH_KW, CH_QD, CH_KD, CH_QI, CH_KI = 0, 4, 5, 6, 10, 11, 13
CH_KC, CH_VC, CH_VS, CH_VW, CH_VD, CH_ZN, CH_ZD, CH_MISC = 14, 15, 16, 17, 18, 19, 23, 27
N_CHUNKS = 28
MISC_WI = 24

TM_PROJ = 512
TQ_NSA = 256
TQ_DSA = 256
KC = 512

f32 = jnp.float32
bf16 = jnp.bfloat16


def _permuted_weight(w_in):
    off = dict(zip(IN_NAMES, np.cumsum((0,) + IN_WIDTHS[:-1])))
    width = dict(zip(IN_NAMES, IN_WIDTHS))
    order = ("q_n", "ks", "kw", "q_d", "k_d", "qi", "ki", "ki", "kc", "vc", "vs", "vw", "v_d", "z_n", "z_d",
             "gate", "wi")
    parts = [w_in[:, off[n]:off[n] + width[n]] for n in order]
    used = sum(width[n] for n in order)
    parts.append(jnp.zeros((D_MODEL, N_CHUNKS * LANES - used), w_in.dtype))
    return jnp.concatenate(parts, axis=1).astype(bf16)


def _nt_dot(a, b):
    return lax.dot_general(a, b, (((1,), (1,)), ((), ())), preferred_element_type=f32)


def _dot(a, b):
    return jnp.dot(a, b, preferred_element_type=f32)


def _split_bf16(v):
    hi = v.astype(bf16)
    lo = (v - hi.astype(f32)).astype(bf16)
    return hi, lo


def _head_rms_norm(y, bd, gain_row):
    ssq = _dot((y * y).astype(bf16), bd)
    return y * lax.rsqrt(ssq * (1.0 / HEAD_DIM) + EPS) * gain_row


def _rope(y, cos, sin_signed, lo32):
    partner = jnp.where(lo32, pltpu.roll(y, LANES - 32, 1), pltpu.roll(y, 32, 1))
    return y * cos + partner * sin_signed


def _dup_halves(y, lo64):
    r = pltpu.roll(y, 64, 1)
    return jnp.where(lo64, y, r), jnp.where(lo64, r, y)


def _value_with_ones(y, lo64):
    return jnp.where(lo64, y, 1.0), jnp.where(lo64, pltpu.roll(y, 64, 1), 1.0)


def _inproj_kernel(x_ref, ng_ref, w_ref, bd_ref, cos_ref, sin_ref, gains_ref,
                   qn_ref, qd_ref, qi_ref, ki_ref, ks_ref, kw_ref, kd_ref,
                   vs_ref, vw_ref, vd_ref, kc_ref, vc_ref, zn_ref, zd_ref, misc_ref):
    x = x_ref[0]
    ms = jnp.mean(x * x, axis=-1, keepdims=True)
    h = (x * lax.rsqrt(ms + EPS) * ng_ref[...]).astype(bf16)
    cos = cos_ref[...]
    sin = sin_ref[...]
    bd = bd_ref[...]
    lane = lax.broadcasted_iota(jnp.int32, (1, LANES), 1)
    lo32 = (lane % 64) < 32
    lo64 = lane < 64

    def proj(c0, n):
        return _dot(h, w_ref[:, c0 * LANES:(c0 + n) * LANES])

    def chunk(y, i):
        return y[:, i * LANES:(i + 1) * LANES]

    def normed_rope(y, gain_idx):
        w = y.shape[1]
        yn = _head_rms_norm(y, bd[:w, :w], gains_ref[gain_idx:gain_idx + 1, 0:w])
        return [_rope(chunk(yn, i), cos, sin, lo32) for i in range(w // LANES)]

    for c0, gain_idx, ref in ((CH_QN, 0, qn_ref), (CH_QD, 2, qd_ref)):
        y = proj(c0, 4)
        for i in range(2):
            for j, r in enumerate(normed_rope(y[:, 2 * i * LANES:(2 * i + 2) * LANES], gain_idx)):
                ref[0, :, (2 * i + j) * LANES:(2 * i + j + 1) * LANES] = (r * Q_SCALE).astype(bf16)
    y = proj(CH_QI, 2)
    for i in range(2):
        qi_ref[0, :, i * LANES:(i + 1) * LANES] = (_rope(chunk(y, i), cos, sin, lo32) * IDX_SCALE).astype(bf16)
    ki_ref[0] = _rope(proj(CH_KI, 1), cos, sin, lo32).astype(bf16)

    k_pairs = normed_rope(proj(CH_KS, 2), 1) + normed_rope(proj(CH_KD, 1), 3)
    for r, ref in zip(k_pairs, (ks_ref, kw_ref, kd_ref)):
        a, b = _dup_halves(r, lo64)
        ref[0, 0] = a.astype(bf16)
        ref[0, 1] = b.astype(bf16)
    for c0, ref in ((CH_VS, vs_ref), (CH_VW, vw_ref), (CH_VD, vd_ref)):
        a, b = _value_with_ones(proj(c0, 1), lo64)
        ref[0, 0] = a.astype(bf16)
        ref[0, 1] = b.astype(bf16)

    kc_ref[0] = proj(CH_KC, 1)
    vc_ref[0] = proj(CH_VC, 1)
    z = proj(CH_ZN, 4)
    zn_ref[0] = (z * jax.nn.sigmoid(z)).astype(bf16)
    z = proj(CH_ZD, 4)
    zd_ref[0] = (z * jax.nn.sigmoid(z)).astype(bf16)
    m = proj(CH_MISC, 1)
    misc_ref[0] = jnp.where(lane < MISC_WI, jax.nn.sigmoid(m), m * WI_SCALE)


def _in_projection(x, norm_gain, w_perm, bd, cos_t, sin_t, gains):
    B, S, _ = x.shape
    tm = min(TM_PROJ, S)
    grid = (B, S // tm)
    row = lambda c: pl.BlockSpec((1, tm, c), lambda b, i: (b, i, 0))
    dup = pl.BlockSpec((1, KV_GROUPS, tm, LANES), lambda b, i: (b, 0, i, 0))
    const = lambda shape: pl.BlockSpec(shape, lambda b, i: tuple(0 for _ in shape))
    tab = pl.BlockSpec((tm, LANES), lambda b, i: (i, 0))
    sds = jax.ShapeDtypeStruct
    out_shape = (
        sds((B, S, 512), bf16), sds((B, S, 512), bf16), sds((B, S, 256), bf16), sds((B, S, LANES), bf16),
        sds((B, KV_GROUPS, S, LANES), bf16), sds((B, KV_GROUPS, S, LANES), bf16),
        sds((B, KV_GROUPS, S, LANES), bf16), sds((B, KV_GROUPS, S, LANES), bf16),
        sds((B, KV_GROUPS, S, LANES), bf16), sds((B, KV_GROUPS, S, LANES), bf16),
        sds((B, S, LANES), f32), sds((B, S, LANES), f32),
        sds((B, S, 512), bf16), sds((B, S, 512), bf16), sds((B, S, LANES), f32),
    )
    out_specs = (row(512), row(512), row(256), row(LANES), dup, dup, dup, dup, dup, dup,
                 row(LANES), row(LANES), row(512), row(512), row(LANES))
    return pl.pallas_call(
        _inproj_kernel,
        grid=grid,
        in_specs=[row(D_MODEL), const((1, D_MODEL)), const((D_MODEL, N_CHUNKS * LANES)),
                  const((2 * LANES, 2 * LANES)), tab, tab, const((8, 2 * LANES))],
        out_specs=out_specs,
        out_shape=out_shape,
        compiler_params=pltpu.CompilerParams(
            dimension_semantics=("arbitrary", "arbitrary"), vmem_limit_bytes=VMEM_LIMIT),
        name="in_projection",
    )(x, norm_gain, w_perm, bd, cos_t, sin_t, gains)


def _compress_kernel(src_ref, pe_ref, w1_ref, b1_ref, w2_ref, bd_ref, gain_ref, cos_ref, sin_ref,
                     out_ref, *, n_rows, is_key):
    half = CMP_BLOCK // 2
    acc_a = jnp.zeros((n_rows, 2 * CMP_HIDDEN), f32)
    acc_b = jnp.zeros((n_rows, 2 * CMP_HIDDEN), f32)
    for l in range(half):
        rows = src_ref[0, pl.ds(l, n_rows, stride=CMP_STRIDE), :]
        acc_a = acc_a + _dot((rows + pe_ref[l:l + 1, :]).astype(bf16), w1_ref[l])
        acc_b = acc_b + _dot((rows + pe_ref[l + half:l + half + 1, :]).astype(bf16), w1_ref[l + half])
    pre = acc_a + pltpu.roll(acc_b, n_rows - 1, 0) + b1_ref[...]
    hid = pre * jax.nn.sigmoid(pre)
    out = _dot(hid.astype(bf16), w2_ref[...])
    lane = lax.broadcasted_iota(jnp.int32, (1, LANES), 1)
    if is_key:
        out = _head_rms_norm(out, bd_ref[...], gain_ref[...])
        out = _rope(out, cos_ref[...], sin_ref[...], (lane % 64) < 32)
        a, b = _dup_halves(out, lane < 64)
    else:
        a, b = _value_with_ones(out, lane < 64)
    out_ref[0, 0] = a.astype(bf16)
    out_ref[0, 1] = b.astype(bf16)


def _compress(src, pe_dup, w1_bd, b1_dup, w2_bd, bd, gain_dup, cos_c, sin_c, *, is_key):
    B, S, _ = src.shape
    n_rows = S // CMP_STRIDE
    const = lambda shape: pl.BlockSpec(shape, lambda b: tuple(0 for _ in shape))
    return pl.pallas_call(
        functools.partial(_compress_kernel, n_rows=n_rows, is_key=is_key),
        grid=(B,),
        in_specs=[pl.BlockSpec((1, S, LANES), lambda b: (b, 0, 0)),
                  const((CMP_BLOCK, LANES)), const((CMP_BLOCK, LANES, 2 * CMP_HIDDEN)),
                  const((1, 2 * CMP_HIDDEN)), const((2 * CMP_HIDDEN, LANES)), const((LANES, LANES)),
                  const((1, LANES)), const((n_rows, LANES)), const((n_rows, LANES))],
        out_specs=pl.BlockSpec((1, KV_GROUPS, n_rows, LANES), lambda b: (b, 0, 0, 0)),
        out_shape=jax.ShapeDtypeStruct((B, KV_GROUPS, n_rows, LANES), bf16),
        compiler_params=pltpu.CompilerParams(
            dimension_semantics=("arbitrary",), vmem_limit_bytes=VMEM_LIMIT),
        name="compress_k" if is_key else "compress_v",
    )(src, pe_dup, w1_bd, b1_dup, w2_bd, bd, gain_dup, cos_c, sin_c)


def _half_masked(pair, h, lo64):
    keep = lo64 if h % 2 == 0 else jnp.logical_not(lo64)
    return jnp.where(keep, pair, jnp.zeros_like(pair))


def _stack_group_queries(q, g, lo64):
    slabs = []
    for r in range(HEADS_PER_GROUP):
        h = g * HEADS_PER_GROUP + r
        slabs.append(_half_masked(q[:, (h // 2) * LANES:(h // 2 + 1) * LANES], h, lo64))
    return jnp.concatenate(slabs, axis=0)


def _masked_flash(q4s, k_ref, v_ref, n_chunks, bias_fns, tq):
    rows = HEADS_PER_GROUP * tq

    def body(c, carry):
        k0 = pl.multiple_of(c * KC, KC)
        new = []
        for g in range(KV_GROUPS):
            m, acc = carry[g]
            s = _nt_dot(q4s[g], k_ref[0, g, pl.ds(k0, KC), :])
            s = (s.reshape(HEADS_PER_GROUP, tq, KC) + bias_fns[g](k0)[None]).reshape(rows, KC)
            m_new = jnp.maximum(m, jnp.max(s, axis=-1, keepdims=True))
            p = jnp.exp2(s - m_new).astype(bf16)
            acc = jnp.exp2(m - m_new) * acc + _dot(p, v_ref[0, g, pl.ds(k0, KC), :])
            new.append((m_new, acc))
        return tuple(new)

    init = tuple((jnp.full((rows, 1), M_FLOOR, f32), jnp.zeros((rows, LANES), f32)) for _ in range(KV_GROUPS))
    out = lax.fori_loop(0, n_chunks, body, init)
    return [acc / acc[:, HEAD_DIM:HEAD_DIM + 1] for _, acc in out]


def _group_output(og, lo64, tq):
    pairs = []
    for p in range(2):
        even = og[(2 * p) * tq:(2 * p + 1) * tq]
        odd = og[(2 * p + 1) * tq:(2 * p + 2) * tq]
        pairs.append(jnp.where(lo64, even, pltpu.roll(odd, HEAD_DIM, 1)))
    return jnp.concatenate(pairs, axis=1)


def _top_n_blocks(imp, n_sel, top_n):
    groups = [imp[SUBLANES * a:SUBLANES * (a + 1)] for a in range(n_sel // SUBLANES)]
    jsub = lax.broadcasted_iota(jnp.int32, groups[0].shape, 0)
    ranks = [jnp.zeros(groups[0].shape, f32) for _ in groups]
    for i in range(n_sel):
        row = imp[i:i + 1, :]
        for a, grp in enumerate(groups):
            if SUBLANES * a > i:
                beats = row >= grp
            elif SUBLANES * (a + 1) - 1 < i:
                beats = row > grp
            else:
                tie = jnp.where(jsub + SUBLANES * a > i, 1.0, 0.0)
                ranks[a] = ranks[a] + jnp.where(row > grp, 1.0, jnp.where(row == grp, tie, 0.0))
                continue
            ranks[a] = ranks[a] + jnp.where(beats, 1.0, 0.0)
    rank = jnp.concatenate(ranks, axis=0)
    return jnp.where(rank < top_n, 1.0, 0.0)


def _nsa_kernel(qn_ref, kcmp_ref, vcmp_ref, ks_ref, vs_ref, kw_ref, vw_ref, misc_ref, zn_ref, ovt_ref,
                o_ref, *, n_cmp_pad, n_sel, top_n):
    tq = TQ_NSA
    t0 = pl.program_id(1) * tq
    lane = lax.broadcasted_iota(jnp.int32, (1, LANES), 1)
    lo64 = lane < 64
    q = qn_ref[0]
    misc = misc_ref[0]
    t_col = t0 + lax.broadcasted_iota(jnp.int32, (tq, 1), 0)
    n_chunks = (t0 + tq + KC - 1) // KC
    win_span = WINDOW + tq
    win_start = pl.multiple_of(jnp.maximum(t0 - WINDOW, 0), tq)
    ovt = ovt_ref[...]
    rows = HEADS_PER_GROUP * tq

    def add_bias(s, bias):
        return (s.reshape(HEADS_PER_GROUP, tq, s.shape[-1]) + bias[None]).reshape(rows, s.shape[-1])

    cmp_end = lax.broadcasted_iota(jnp.int32, (1, n_cmp_pad), 1) * CMP_STRIDE + (CMP_BLOCK - 1)
    cmp_bias = jnp.where(cmp_end <= t_col, 0.0, NEG)
    win_diff = t_col - (win_start + lax.broadcasted_iota(jnp.int32, (1, win_span), 1))
    win_bias = jnp.where((win_diff >= 0) & (win_diff < WINDOW), 0.0, NEG)

    q4s, o_cmp, o_win, sels = [], [], [], []
    for g in range(KV_GROUPS):
        q4 = _stack_group_queries(q, g, lo64)
        q4s.append(q4)

        s = add_bias(_nt_dot(q4, kcmp_ref[0, g]), cmp_bias)
        m = jnp.maximum(jnp.max(s, axis=-1, keepdims=True), M_FLOOR)
        e = jnp.exp2(s - m)
        l = jnp.sum(e, axis=-1, keepdims=True)
        p = e * (1.0 / jnp.maximum(l, 1e-30))
        o_cmp.append(_dot(p.astype(bf16), vcmp_ref[0, g]))

        psum = p[0:tq] + p[tq:2 * tq] + p[2 * tq:3 * tq] + p[3 * tq:4 * tq]
        hi, lo = _split_bf16(psum)
        imp = _nt_dot(ovt, hi) + _nt_dot(ovt, lo)
        jrow = lax.broadcasted_iota(jnp.int32, (n_sel, tq), 0)
        tl = t0 + lax.broadcasted_iota(jnp.int32, (n_sel, tq), 1)
        cur = lax.shift_right_logical(tl, 6)
        forced = (jrow == 0) | (jrow == cur) | (jrow == cur - 1)
        imp = jnp.where(forced, FORCE, imp)
        imp = jnp.where(jrow * SEL_BLOCK <= tl, imp, NEG)
        sels.append(_top_n_blocks(imp, n_sel, top_n).T.astype(bf16))

        s = add_bias(_nt_dot(q4, kw_ref[0, g, pl.ds(win_start, win_span), :]), win_bias)
        m = jnp.max(s, axis=-1, keepdims=True)
        acc = _dot(jnp.exp2(s - m).astype(bf16), vw_ref[0, g, pl.ds(win_start, win_span), :])
        o_win.append(acc / acc[:, HEAD_DIM:HEAD_DIM + 1])

    def sel_bias(g):
        def fn(k0):
            jj = lax.broadcasted_iota(jnp.int32, (n_sel, KC), 0)
            kk = lax.shift_right_logical(k0 + lax.broadcasted_iota(jnp.int32, (n_sel, KC), 1), 6)
            expand = jnp.where(jj == kk, 1.0, 0.0).astype(bf16)
            chosen = _dot(sels[g], expand)
            kpos = k0 + lax.broadcasted_iota(jnp.int32, (1, KC), 1)
            return jnp.where(kpos <= t_col, jnp.where(chosen > 0.5, 0.0, NEG), NEG)
        return fn

    o_sel = _masked_flash(q4s, ks_ref, vs_ref, n_chunks, [sel_bias(g) for g in range(KV_GROUPS)], tq)

    group_outs = []
    for g in range(KV_GROUPS):
        def gate(branch):
            cols = [misc[:, (g * HEADS_PER_GROUP + r) * 3 + branch:(g * HEADS_PER_GROUP + r) * 3 + branch + 1]
                    for r in range(HEADS_PER_GROUP)]
            return jnp.concatenate(cols, axis=0)

        og = gate(0) * o_cmp[g] + gate(1) * o_sel[g] + gate(2) * o_win[g]
        group_outs.append(_group_output(og, lo64, tq))

    o = jnp.concatenate(group_outs, axis=1) * zn_ref[0].astype(f32)
    o_ref[0] = o.astype(bf16)


def _nsa_attention(qn, kcmp, vcmp, ks, vs, kw, vw, misc, zn, ovt):
    B, S, _ = qn.shape
    tq = TQ_NSA
    n_cmp_pad = kcmp.shape[2]
    n_sel = S // SEL_BLOCK
    row = lambda c: pl.BlockSpec((1, tq, c), lambda b, i: (b, i, 0))
    whole = lambda n: pl.BlockSpec((1, KV_GROUPS, n, LANES), lambda b, i: (b, 0, 0, 0))
    kern = functools.partial(_nsa_kernel, n_cmp_pad=n_cmp_pad, n_sel=n_sel, top_n=min(SEL_TOPN, n_sel))
    return pl.pallas_call(
        kern,
        grid=(B, S // tq),
        in_specs=[row(512), whole(n_cmp_pad), whole(n_cmp_pad), whole(S), whole(S), whole(S), whole(S),
                  row(LANES), row(512), pl.BlockSpec((n_sel, n_cmp_pad), lambda b, i: (0, 0))],
        out_specs=row(512),
        out_shape=jax.ShapeDtypeStruct((B, S, 512), bf16),
        compiler_params=pltpu.CompilerParams(
            dimension_semantics=("arbitrary", "arbitrary"), vmem_limit_bytes=VMEM_LIMIT),
        name="nsa_attention",
    )(qn, kcmp, vcmp, ks, vs, kw, vw, misc, zn, ovt)


N_SLABS = 32
_TRANSPOSE_STAGES = ((16, 0x0000FFFF), (8, 0x00FF00FF), (4, 0x0F0F0F0F), (2, 0x33333333), (1, 0x55555555))


def _ordered_bits(v):
    return v ^ (jnp.right_shift(v, 31) & jnp.int32(0x7FFFFFFF))


def _bit_planes(score_ref, plane_ref):
    for stage, (j, mask) in enumerate(_TRANSPOSE_STAGES):
        if stage == 0:
            load = lambda k: _ordered_bits(pltpu.bitcast(score_ref[k], jnp.int32))
        else:
            load = lambda k: plane_ref[k]

        def body(p, carry, j=j, mask=mask, load=load):
            k = jnp.left_shift(p & ~(j - 1), 1) | (p & (j - 1))
            lo = load(k)
            hi = load(k + j)
            t = (lo ^ lax.shift_right_logical(hi, j)) & mask
            plane_ref[k] = lo ^ t
            plane_ref[k + j] = hi ^ jnp.left_shift(t, j)
            return carry

        lax.fori_loop(0, N_SLABS // 2, body, 0)


def _dsa_kernel(qd_ref, qi_ref, misc_ref, ki_ref, kd_ref, vd_ref, zd_ref, tri_ref,
                o_ref, score_scr, plane_scr, bias_scr, *, top_k, tq):
    t0 = pl.program_id(1) * tq
    lane = lax.broadcasted_iota(jnp.int32, (1, LANES), 1)
    lo64 = lane < 64
    t_col = t0 + lax.broadcasted_iota(jnp.int32, (tq, 1), 0)
    t_row = t0 + lax.broadcasted_iota(jnp.int32, (1, tq), 1)
    n_chunks = (t0 + tq + KC - 1) // KC
    slabs_per_chunk = KC // LANES
    int_min = jnp.int32(-2 ** 31)

    def chunk_slabs(c):
        return pl.ds(c * slabs_per_chunk, slabs_per_chunk)

    def to_col(row):
        return jnp.broadcast_to(row, (SUBLANES, tq)).T[:, 0:1]

    qi = qi_ref[0]
    qi_heads = [_half_masked(qi[:, (h // 2) * LANES:(h // 2 + 1) * LANES], h, lo64) for h in range(IDX_HEADS)]
    misc_t = misc_ref[0].T
    wi = [misc_t[MISC_WI + h:MISC_WI + h + 1, :] for h in range(IDX_HEADS)]

    def score_body(c, carry):
        k0 = pl.multiple_of(c * KC, KC)
        kib = ki_ref[0, pl.ds(k0, KC), :]
        sc = jnp.zeros((KC, tq), f32)
        for h in range(IDX_HEADS):
            sc = sc + jnp.maximum(_nt_dot(kib, qi_heads[h]), 0.0) * wi[h]
        sc = jnp.where(sc == 0.0, 0.0, sc)
        kpos = k0 + lax.broadcasted_iota(jnp.int32, (KC, 1), 0)
        sc = jnp.where(kpos <= t_row, sc, NON_CAUSAL_MARK)
        score_scr[chunk_slabs(c)] = sc.reshape(slabs_per_chunk, LANES, tq)
        return carry

    lax.fori_loop(0, n_chunks, score_body, 0)

    def fill_body(c, carry):
        score_scr[chunk_slabs(c)] = jnp.full((slabs_per_chunk, LANES, tq), NON_CAUSAL_MARK, f32)
        return carry

    lax.fori_loop(n_chunks, N_SLABS // slabs_per_chunk, fill_body, 0)

    _bit_planes(score_scr, plane_scr)
    kf = float(top_k)

    def select_bit(i, carry):
        alive, above, thr = carry
        plane = plane_scr[i] ^ jnp.where(i == 0, jnp.int32(-1), jnp.int32(0))
        ones = alive & plane
        c1 = jnp.sum(lax.population_count(ones).astype(f32), axis=0, keepdims=True)
        take = (above + c1) >= kf
        alive = jnp.where(take, ones, alive ^ ones)
        above = jnp.where(take, above, above + c1)
        thr = jnp.where(take, thr | jnp.left_shift(jnp.int32(1), 31 - i), thr)
        return alive, above, thr

    init = (jnp.full((LANES, tq), -1, jnp.int32), jnp.zeros((1, tq), f32), jnp.zeros((1, tq), jnp.int32))
    _, above, thr_u = lax.fori_loop(0, 32, select_bit, init)

    def key_to_score(key_row):
        return pltpu.bitcast(_ordered_bits(key_row), f32)

    def write_bias(thr_row, need_row):
        thr = to_col(thr_row)
        need = to_col(need_row)

        def body(c, carry):
            ties_before, n_above = carry
            k0 = pl.multiple_of(c * KC, KC)
            blk = score_scr[chunk_slabs(c)].reshape(KC, tq).T
            gt = jnp.where(blk > thr, 1.0, 0.0)
            eq = jnp.where(blk == thr, 1.0, 0.0)
            prefix = ties_before + _dot(eq.astype(bf16), tri_ref[...])
            chosen = gt + jnp.where(prefix <= need, eq, 0.0)
            kpos = k0 + lax.broadcasted_iota(jnp.int32, (1, KC), 1)
            bias_scr[:, pl.ds(k0, KC)] = jnp.where(kpos <= t_col, jnp.where(chosen > 0.5, 0.0, NEG), NEG)
            return prefix[:, KC - 1:KC], n_above + jnp.sum(gt, axis=-1, keepdims=True)

        zero = jnp.zeros((tq, 1), f32)
        return lax.fori_loop(0, n_chunks, body, (zero, zero))

    ties, n_above = write_bias(key_to_score(thr_u ^ int_min), kf - above)
    verified = (n_above == to_col(above)) & (n_above < kf) & (n_above + ties >= kf)

    @pl.when(jnp.sum(jnp.where(verified, 0.0, 1.0)) > 0.0)
    def _():
        def count(pred):
            def body(c, acc):
                blk = score_scr[chunk_slabs(c)].reshape(KC, tq)
                return acc + jnp.sum(jnp.where(pred(blk), 1.0, 0.0), axis=0, keepdims=True)
            return lax.fori_loop(0, N_SLABS // slabs_per_chunk, body, jnp.zeros((1, tq), f32))

        key = jnp.where(count(lambda b: b >= 0.0) >= kf, jnp.int32(0), int_min)

        def bit(i, key):
            cand = key + jnp.left_shift(jnp.int32(1), 30 - i)
            cand_f = key_to_score(cand)
            return jnp.where(count(lambda b: b >= cand_f) >= kf, cand, key)

        thr_f = key_to_score(lax.fori_loop(0, 31, bit, key))
        write_bias(thr_f, kf - count(lambda b: b > thr_f))

    def bias_fn(k0):
        return bias_scr[:, pl.ds(k0, KC)]

    qd = qd_ref[0]
    q4s = [_stack_group_queries(qd, g, lo64) for g in range(KV_GROUPS)]
    outs = _masked_flash(q4s, kd_ref, vd_ref, n_chunks, [bias_fn] * KV_GROUPS, tq)
    o = jnp.concatenate([_group_output(og, lo64, tq) for og in outs], axis=1) * zd_ref[0].astype(f32)
    o_ref[0] = o.astype(bf16)


def _dsa_attention(qd, qi, misc, ki, kd, vd, zd):
    B, S, _ = qd.shape
    tq = TQ_DSA
    assert S <= N_SLABS * LANES
    tri = jnp.asarray(np.triu(np.ones((KC, KC), np.float32)), bf16)
    row = lambda c: pl.BlockSpec((1, tq, c), lambda b, i: (b, i, 0))
    whole = pl.BlockSpec((1, KV_GROUPS, S, LANES), lambda b, i: (b, 0, 0, 0))
    return pl.pallas_call(
        functools.partial(_dsa_kernel, top_k=min(DSA_TOPK_MAX, S // 4), tq=tq),
        grid=(B, S // tq),
        in_specs=[row(512), row(256), row(LANES), pl.BlockSpec((1, S, LANES), lambda b, i: (b, 0, 0)),
                  whole, whole, row(512), pl.BlockSpec((KC, KC), lambda b, i: (0, 0))],
        out_specs=row(512),
        out_shape=jax.ShapeDtypeStruct((B, S, 512), bf16),
        scratch_shapes=[pltpu.VMEM((N_SLABS, LANES, tq), f32), pltpu.VMEM((N_SLABS, LANES, tq), jnp.int32),
                        pltpu.VMEM((tq, S), f32)],
        compiler_params=pltpu.CompilerParams(
            dimension_semantics=("arbitrary", "arbitrary"), vmem_limit_bytes=VMEM_LIMIT),
        name="dsa_attention",
    )(qd, qi, misc, ki, kd, vd, zd, tri)


def _outproj_kernel(x_ref, on_ref, od_ref, w_ref, o_ref):
    half = w_ref.shape[0] // 2
    o_ref[0] = x_ref[0] + _dot(on_ref[0], w_ref[0:half, :]) + _dot(od_ref[0], w_ref[half:, :])


def _out_projection(x, o_nsa, o_dsa, w_out):
    B, S, D = x.shape
    tm = min(TM_PROJ, S)
    row = lambda c: pl.BlockSpec((1, tm, c), lambda b, i: (b, i, 0))
    return pl.pallas_call(
        _outproj_kernel,
        grid=(B, S // tm),
        in_specs=[row(D), row(512), row(512), pl.BlockSpec(w_out.shape, lambda b, i: (0, 0))],
        out_specs=row(D),
        out_shape=jax.ShapeDtypeStruct((B, S, D), f32),
        compiler_params=pltpu.CompilerParams(
            dimension_semantics=("arbitrary", "arbitrary"), vmem_limit_bytes=VMEM_LIMIT),
        name="out_projection",
    )(x, o_nsa, o_dsa, w_out)


def _rope_tables(pos):
    half = HEAD_DIM // 2
    inv_freq = ROPE_THETA ** (-jnp.arange(half, dtype=f32) / half)
    ang = pos[:, None] * inv_freq[None, :]
    cos, sin = jnp.cos(ang), jnp.sin(ang)
    cos_t = jnp.tile(cos, (1, LANES // half))
    sin_t = jnp.tile(jnp.concatenate([-sin, sin], axis=1), (1, LANES // HEAD_DIM))
    return cos_t, sin_t


def _overlap_t(seq, n_cmp_pad):
    n_cmp = (seq - CMP_BLOCK) // CMP_STRIDE + 1
    n_sel = seq // SEL_BLOCK
    c_start = np.arange(n_cmp) * CMP_STRIDE
    j_start = np.arange(n_sel) * SEL_BLOCK
    ov = np.clip(np.minimum(c_start[:, None] + CMP_BLOCK, j_start[None, :] + SEL_BLOCK)
                 - np.maximum(c_start[:, None], j_start[None, :]), 0, None).astype(np.float32) / CMP_BLOCK
    out = np.zeros((n_sel, n_cmp_pad), np.float32)
    out[:, :n_cmp] = ov.T
    return out


def _block_diag2(w):
    z = jnp.zeros_like(w)
    return jnp.concatenate([jnp.concatenate([w, z], axis=-1), jnp.concatenate([z, w], axis=-1)], axis=-2)


def _layer(x, norm_gain, w_in, nsa_q_gain, nsa_kc_gain, nsa_ks_gain, nsa_kw_gain,
           cmp_pe_k, cmp_k_w1, cmp_k_b1, cmp_k_w2, cmp_pe_v, cmp_v_w1, cmp_v_b1, cmp_v_w2,
           dsa_q_gain, dsa_k_gain, w_out):
    B, S, _ = x.shape
    assert S % KC == 0 and S >= WINDOW + TQ_NSA
    n_cmp_pad = S // CMP_STRIDE

    w_perm = _permuted_weight(w_in)
    head_of_lane = np.arange(2 * LANES) // HEAD_DIM
    bd = jnp.asarray(head_of_lane[:, None] == head_of_lane[None, :], bf16)
    dup = lambda v: jnp.tile(v.reshape(1, -1), (1, 2))
    quad = lambda v: jnp.tile(v.reshape(1, -1), (1, 4))
    gains = jnp.concatenate([quad(nsa_q_gain), jnp.concatenate([dup(nsa_ks_gain), dup(nsa_kw_gain)], axis=1),
                             quad(dsa_q_gain), quad(dsa_k_gain), jnp.ones((4, 2 * LANES), f32)], axis=0)
    cos_t, sin_t = _rope_tables(jnp.arange(S, dtype=f32))

    (qn, qd, qi, ki, ks, kw, kd, vs, vw, vd, kc, vc, zn, zd, misc) = _in_projection(
        x, norm_gain.reshape(1, -1), w_perm, bd, cos_t, sin_t, gains)

    cmp_pos = (jnp.arange(n_cmp_pad) * CMP_STRIDE + CMP_BLOCK - 1).astype(f32)
    cos_c, sin_c = _rope_tables(cmp_pos)

    def cmp_weights(pe, w1, b1, w2):
        w1_bd = _block_diag2(w1.reshape(CMP_BLOCK, HEAD_DIM, CMP_HIDDEN)).astype(bf16)
        return jnp.tile(pe, (1, 2)), w1_bd, dup(b1), _block_diag2(w2).astype(bf16)

    kcmp = _compress(kc, *cmp_weights(cmp_pe_k, cmp_k_w1, cmp_k_b1, cmp_k_w2), bd[:LANES, :LANES], dup(nsa_kc_gain),
                     cos_c, sin_c, is_key=True)
    vcmp = _compress(vc, *cmp_weights(cmp_pe_v, cmp_v_w1, cmp_v_b1, cmp_v_w2), bd[:LANES, :LANES], dup(nsa_kc_gain),
                     cos_c, sin_c, is_key=False)

    ovt = jnp.asarray(_overlap_t(S, n_cmp_pad), bf16)
    o_nsa = _nsa_attention(qn, kcmp, vcmp, ks, vs, kw, vw, misc, zn, ovt)

    o_dsa = _dsa_attention(qd, qi, misc, ki, kd, vd, zd)

    return _out_projection(x, o_nsa, o_dsa, w_out.astype(bf16))


def kernel(x, norm_gain, w_in, nsa_q_gain, nsa_kc_gain, nsa_ks_gain, nsa_kw_gain, cmp_pe_k, cmp_k_w1,
           cmp_k_b1, cmp_k_w2, cmp_pe_v, cmp_v_w1, cmp_v_b1, cmp_v_w2, dsa_q_gain, dsa_k_gain, w_out):
    for l in range(norm_gain.shape[0]):
        x = _layer(x, norm_gain[l], w_in[l], nsa_q_gain[l], nsa_kc_gain[l], nsa_ks_gain[l], nsa_kw_gain[l],
                   cmp_pe_k[l], cmp_k_w1[l], cmp_k_b1[l], cmp_k_w2[l], cmp_pe_v[l], cmp_v_w1[l], cmp_v_b1[l],
                   cmp_v_w2[l], dsa_q_gain[l], dsa_k_gain[l], w_out[l])
    return x
```

```python
import functools

import numpy as np
import jax
import jax.numpy as jnp
from jax import lax
from jax.experimental import pallas as pl
from jax.experimental.pallas import tpu as pltpu

D_MODEL = 1024
HEAD_DIM = 64
NSA_HEADS = 8
DSA_HEADS = 8
KV_GROUPS = 2
HEADS_PER_GROUP = 4
IDX_HEADS = 4
CMP_BLOCK = 32
CMP_STRIDE = 16
CMP_HIDDEN = 256
SEL_BLOCK = 64
SEL_TOPN = 16
WINDOW = 512
DSA_TOPK_MAX = 256
ROPE_THETA = 10000.0
EPS = 1e-6
NEG = -1e30
FORCE = 1e6
ATTN_SCALE = HEAD_DIM ** -0.5
IDX_SCALE = HEAD_DIM ** -0.5
WI_SCALE = IDX_HEADS ** -0.5
LOG2E = 1.4426950408889634
Q_SCALE = ATTN_SCALE * LOG2E

LANES = 128
SUBLANES = 8
M_FLOOR = -5e29
NON_CAUSAL_MARK = -3.0e38
MAX_FIXED_SHIFT = 30.0
NORM_SLACK = 1.03
VMEM_LIMIT = 56 * 1024 * 1024

IN_WIDTHS = (512, 128, 128, 128, 128, 128, 128, 24, 512, 512, 128, 128, 256, 64, 4, 512)
IN_NAMES = ("q_n", "kc", "vc", "ks", "vs", "kw", "vw", "gate", "z_n",
            "q_d", "k_d", "v_d", "qi", "ki", "wi", "z_d")
IN_COLS = sum(IN_WIDTHS)

CH_QN, CH_KS, CH_KW, CH_QD, CH_KD, CH_QI, CH_KI = 0, 4, 5, 6, 10, 11, 13
CH_KC, CH_VC, CH_VS, CH_VW, CH_VD, CH_ZN, CH_ZD, CH_MISC = 14, 15, 16, 17, 18, 19, 23, 27
N_CHUNKS = 28
MISC_WI = 24

TM_PROJ = 512
TQ_NSA = 256
TQ_DSA = 256
KC = 512

f32 = jnp.float32
bf16 = jnp.bfloat16


def _permuted_weight(w_in):
    off = dict(zip(IN_NAMES, np.cumsum((0,) + IN_WIDTHS[:-1])))
    width = dict(zip(IN_NAMES, IN_WIDTHS))
    order = ("q_n", "ks", "kw", "q_d", "k_d", "qi", "ki", "ki", "kc", "vc", "vs", "vw", "v_d", "z_n", "z_d",
             "gate", "wi")
    w16 = w_in.astype(bf16)
    parts = [w16[:, off[n]:off[n] + width[n]] for n in order]
    used = sum(width[n] for n in order)
    parts.append(jnp.zeros((D_MODEL, N_CHUNKS * LANES - used), bf16))
    return jnp.concatenate(parts, axis=1)


def _nt_dot(a, b):
    return lax.dot_general(a, b, (((1,), (1,)), ((), ())), preferred_element_type=f32)


def _dot(a, b):
    return jnp.dot(a, b, preferred_element_type=f32)


def _split_bf16(v):
    hi = v.astype(bf16)
    lo = (v - hi.astype(f32)).astype(bf16)
    return hi, lo


def _head_rms_norm(y, bd, gain_row):
    ssq = _dot((y * y).astype(bf16), bd)
    return y * lax.rsqrt(ssq * (1.0 / HEAD_DIM) + EPS) * gain_row


def _rope(y, cos, sin_signed, lo32):
    partner = jnp.where(lo32, pltpu.roll(y, LANES - 32, 1), pltpu.roll(y, 32, 1))
    return y * cos + partner * sin_signed


def _dup_halves(y, lo64):
    r = pltpu.roll(y, 64, 1)
    return jnp.where(lo64, y, r), jnp.where(lo64, r, y)


def _value_with_ones(y, lo64):
    return jnp.where(lo64, y, 1.0), jnp.where(lo64, pltpu.roll(y, 64, 1), 1.0)


def _inproj_kernel(x_ref, ng_ref, w_ref, bd_ref, cos_ref, sin_ref, gains_ref,
                   qn_ref, qd_ref, qi_ref, ki_ref, ks_ref, kw_ref, kd_ref,
                   vs_ref, vw_ref, vd_ref, kc_ref, vc_ref, zn_ref, zd_ref, misc_ref):
    x = x_ref[0]
    ms = jnp.mean(x * x, axis=-1, keepdims=True)
    h = (x * lax.rsqrt(ms + EPS) * ng_ref[...]).astype(bf16)
    cos = cos_ref[...]
    sin = sin_ref[...]
    bd = bd_ref[...]
    lane = lax.broadcasted_iota(jnp.int32, (1, LANES), 1)
    lo32 = (lane % 64) < 32
    lo64 = lane < 64

    def proj(c0, n):
        return _dot(h, w_ref[:, c0 * LANES:(c0 + n) * LANES])

    def chunk(y, i):
        return y[:, i * LANES:(i + 1) * LANES]

    def normed_rope(y, gain_idx):
        w = y.shape[1]
        yn = _head_rms_norm(y, bd[:w, :w], gains_ref[gain_idx:gain_idx + 1, 0:w])
        return [_rope(chunk(yn, i), cos, sin, lo32) for i in range(w // LANES)]

    for c0, gain_idx, ref in ((CH_QN, 0, qn_ref), (CH_QD, 2, qd_ref)):
        y = proj(c0, 4)
        for i in range(2):
            for j, r in enumerate(normed_rope(y[:, 2 * i * LANES:(2 * i + 2) * LANES], gain_idx)):
                ref[0, :, (2 * i + j) * LANES:(2 * i + j + 1) * LANES] = (r * Q_SCALE).astype(bf16)
    y = proj(CH_QI, 2)
    for i in range(2):
        qi_ref[0, :, i * LANES:(i + 1) * LANES] = (_rope(chunk(y, i), cos, sin, lo32) * IDX_SCALE).astype(bf16)
    ki_ref[0] = _rope(proj(CH_KI, 1), cos, sin, lo32).astype(bf16)

    k_pairs = normed_rope(proj(CH_KS, 2), 1) + normed_rope(proj(CH_KD, 1), 3)
    for r, ref in zip(k_pairs, (ks_ref, kw_ref, kd_ref)):
        a, b = _dup_halves(r, lo64)
        ref[0, 0] = a.astype(bf16)
        ref[0, 1] = b.astype(bf16)
    for c0, ref in ((CH_VS, vs_ref), (CH_VW, vw_ref), (CH_VD, vd_ref)):
        a, b = _value_with_ones(proj(c0, 1), lo64)
        ref[0, 0] = a.astype(bf16)
        ref[0, 1] = b.astype(bf16)

    kc_ref[0] = proj(CH_KC, 1)
    vc_ref[0] = proj(CH_VC, 1)
    z = proj(CH_ZN, 4)
    zn_ref[0] = (z * jax.nn.sigmoid(z)).astype(bf16)
    z = proj(CH_ZD, 4)
    zd_ref[0] = (z * jax.nn.sigmoid(z)).astype(bf16)
    m = proj(CH_MISC, 1)
    misc_ref[0] = jnp.where(lane < MISC_WI, jax.nn.sigmoid(m), m * WI_SCALE)


def _in_projection(x, norm_gain, w_perm, bd, cos_t, sin_t, gains):
    B, S, _ = x.shape
    tm = min(TM_PROJ, S)
    grid = (B, S // tm)
    row = lambda c: pl.BlockSpec((1, tm, c), lambda b, i: (b, i, 0))
    dup = pl.BlockSpec((1, KV_GROUPS, tm, LANES), lambda b, i: (b, 0, i, 0))
    const = lambda shape: pl.BlockSpec(shape, lambda b, i: tuple(0 for _ in shape))
    tab = pl.BlockSpec((tm, LANES), lambda b, i: (i, 0))
    sds = jax.ShapeDtypeStruct
    out_shape = (
        sds((B, S, 512), bf16), sds((B, S, 512), bf16), sds((B, S, 256), bf16), sds((B, S, LANES), bf16),
        sds((B, KV_GROUPS, S, LANES), bf16), sds((B, KV_GROUPS, S, LANES), bf16),
        sds((B, KV_GROUPS, S, LANES), bf16), sds((B, KV_GROUPS, S, LANES), bf16),
        sds((B, KV_GROUPS, S, LANES), bf16), sds((B, KV_GROUPS, S, LANES), bf16),
        sds((B, S, LANES), f32), sds((B, S, LANES), f32),
        sds((B, S, 512), bf16), sds((B, S, 512), bf16), sds((B, S, LANES), f32),
    )
    out_specs = (row(512), row(512), row(256), row(LANES), dup, dup, dup, dup, dup, dup,
                 row(LANES), row(LANES), row(512), row(512), row(LANES))
    return pl.pallas_call(
        _inproj_kernel,
        grid=grid,
        in_specs=[row(D_MODEL), const((1, D_MODEL)), const((D_MODEL, N_CHUNKS * LANES)),
                  const((2 * LANES, 2 * LANES)), tab, tab, const((8, 2 * LANES))],
        out_specs=out_specs,
        out_shape=out_shape,
        compiler_params=pltpu.CompilerParams(
            dimension_semantics=("arbitrary", "arbitrary"), vmem_limit_bytes=VMEM_LIMIT),
        name="in_projection",
    )(x, norm_gain, w_perm, bd, cos_t, sin_t, gains)


def _compress_kernel(src_ref, pe_ref, w1_ref, b1_ref, w2_ref, bd_ref, gain_ref, cos_ref, sin_ref,
                     out_ref, *, n_rows, is_key):
    half = CMP_BLOCK // 2
    acc_a = jnp.zeros((n_rows, 2 * CMP_HIDDEN), f32)
    acc_b = jnp.zeros((n_rows, 2 * CMP_HIDDEN), f32)
    for l in range(half):
        rows = src_ref[0, pl.ds(l, n_rows, stride=CMP_STRIDE), :]
        acc_a = acc_a + _dot((rows + pe_ref[l:l + 1, :]).astype(bf16), w1_ref[l])
        acc_b = acc_b + _dot((rows + pe_ref[l + half:l + half + 1, :]).astype(bf16), w1_ref[l + half])
    pre = acc_a + pltpu.roll(acc_b, n_rows - 1, 0) + b1_ref[...]
    hid = pre * jax.nn.sigmoid(pre)
    out = _dot(hid.astype(bf16), w2_ref[...])
    lane = lax.broadcasted_iota(jnp.int32, (1, LANES), 1)
    if is_key:
        out = _head_rms_norm(out, bd_ref[...], gain_ref[...])
        out = _rope(out, cos_ref[...], sin_ref[...], (lane % 64) < 32)
        a, b = _dup_halves(out, lane < 64)
    else:
        a, b = _value_with_ones(out, lane < 64)
    out_ref[0, 0] = a.astype(bf16)
    out_ref[0, 1] = b.astype(bf16)


def _compress(src, pe_dup, w1_bd, b1_dup, w2_bd, bd, gain_dup, cos_c, sin_c, *, is_key):
    B, S, _ = src.shape
    n_rows = S // CMP_STRIDE
    const = lambda shape: pl.BlockSpec(shape, lambda b: tuple(0 for _ in shape))
    return pl.pallas_call(
        functools.partial(_compress_kernel, n_rows=n_rows, is_key=is_key),
        grid=(B,),
        in_specs=[pl.BlockSpec((1, S, LANES), lambda b: (b, 0, 0)),
                  const((CMP_BLOCK, LANES)), const((CMP_BLOCK, LANES, 2 * CMP_HIDDEN)),
                  const((1, 2 * CMP_HIDDEN)), const((2 * CMP_HIDDEN, LANES)), const((LANES, LANES)),
                  const((1, LANES)), const((n_rows, LANES)), const((n_rows, LANES))],
        out_specs=pl.BlockSpec((1, KV_GROUPS, n_rows, LANES), lambda b: (b, 0, 0, 0)),
        out_shape=jax.ShapeDtypeStruct((B, KV_GROUPS, n_rows, LANES), bf16),
        compiler_params=pltpu.CompilerParams(
            dimension_semantics=("arbitrary",), vmem_limit_bytes=VMEM_LIMIT),
        name="compress_k" if is_key else "compress_v",
    )(src, pe_dup, w1_bd, b1_dup, w2_bd, bd, gain_dup, cos_c, sin_c)


def _half_masked(pair, h, lo64):
    keep = lo64 if h % 2 == 0 else jnp.logical_not(lo64)
    return jnp.where(keep, pair, jnp.zeros_like(pair))


def _stack_group_queries(q, g, lo64):
    slabs = []
    for r in range(HEADS_PER_GROUP):
        h = g * HEADS_PER_GROUP + r
        slabs.append(_half_masked(q[:, (h // 2) * LANES:(h // 2 + 1) * LANES], h, lo64))
    return jnp.concatenate(slabs, axis=0)


def _masked_flash(q4s, k_ref, v_ref, n_chunks, bias_fns, tq, bound_ref, shift_scrs):
    rows = HEADS_PER_GROUP * tq
    bound = bound_ref[0]

    def scores(g, k0):
        s = _nt_dot(q4s[g], k_ref[0, g, pl.ds(k0, KC), :])
        return (s.reshape(HEADS_PER_GROUP, tq, KC) + bias_fns[g](k0)[None]).reshape(rows, KC)

    for g in range(KV_GROUPS):
        shift_scrs[g][...] = jnp.full((rows, 1), bound, f32)

    @pl.when(bound > MAX_FIXED_SHIFT)
    def _():
        def body(c, ms):
            k0 = pl.multiple_of(c * KC, KC)
            return tuple(jnp.maximum(ms[g], jnp.max(scores(g, k0), axis=-1, keepdims=True))
                         for g in range(KV_GROUPS))

        init = tuple(jnp.full((rows, 1), M_FLOOR, f32) for _ in range(KV_GROUPS))
        for g, m in enumerate(lax.fori_loop(0, n_chunks, body, init)):
            shift_scrs[g][...] = m

    def body(c, accs):
        k0 = pl.multiple_of(c * KC, KC)
        new = []
        for g in range(KV_GROUPS):
            p = jnp.exp2(scores(g, k0) - shift_scrs[g][...]).astype(bf16)
            new.append(accs[g] + _dot(p, v_ref[0, g, pl.ds(k0, KC), :]))
        return tuple(new)

    init = tuple(jnp.zeros((rows, LANES), f32) for _ in range(KV_GROUPS))
    out = lax.fori_loop(0, n_chunks, body, init)
    return [acc / acc[:, HEAD_DIM:HEAD_DIM + 1] for acc in out]


def _group_output(og, lo64, tq):
    pairs = []
    for p in range(2):
        even = og[(2 * p) * tq:(2 * p + 1) * tq]
        odd = og[(2 * p + 1) * tq:(2 * p + 2) * tq]
        pairs.append(jnp.where(lo64, even, pltpu.roll(odd, HEAD_DIM, 1)))
    return jnp.concatenate(pairs, axis=1)


def _top_n_blocks(imp, n_sel, top_n):
    groups = [imp[SUBLANES * a:SUBLANES * (a + 1)] for a in range(n_sel // SUBLANES)]
    jsub = lax.broadcasted_iota(jnp.int32, groups[0].shape, 0)
    ranks = [jnp.zeros(groups[0].shape, f32) for _ in groups]
    for i in range(n_sel):
        row = imp[i:i + 1, :]
        for a, grp in enumerate(groups):
            if SUBLANES * a > i:
                beats = row >= grp
            elif SUBLANES * (a + 1) - 1 < i:
                beats = row > grp
            else:
                tie = jnp.where(jsub + SUBLANES * a > i, 1.0, 0.0)
                ranks[a] = ranks[a] + jnp.where(row > grp, 1.0, jnp.where(row == grp, tie, 0.0))
                continue
            ranks[a] = ranks[a] + jnp.where(beats, 1.0, 0.0)
    rank = jnp.concatenate(ranks, axis=0)
    return jnp.where(rank < top_n, 1.0, 0.0)


def _nsa_kernel(bound_ref, qn_ref, kcmp_ref, vcmp_ref, ks_ref, vs_ref, kw_ref, vw_ref, misc_ref, zn_ref, ovt_ref,
                o_ref, shift0_scr, shift1_scr, *, n_cmp_pad, n_sel, top_n):
    tq = TQ_NSA
    t0 = pl.program_id(1) * tq
    lane = lax.broadcasted_iota(jnp.int32, (1, LANES), 1)
    lo64 = lane < 64
    q = qn_ref[0]
    misc = misc_ref[0]
    t_col = t0 + lax.broadcasted_iota(jnp.int32, (tq, 1), 0)
    n_chunks = (t0 + tq + KC - 1) // KC
    win_span = WINDOW + tq
    win_start = pl.multiple_of(jnp.maximum(t0 - WINDOW, 0), tq)
    ovt = ovt_ref[...]
    rows = HEADS_PER_GROUP * tq

    def add_bias(s, bias):
        return (s.reshape(HEADS_PER_GROUP, tq, s.shape[-1]) + bias[None]).reshape(rows, s.shape[-1])

    cmp_end = lax.broadcasted_iota(jnp.int32, (1, n_cmp_pad), 1) * CMP_STRIDE + (CMP_BLOCK - 1)
    cmp_bias = jnp.where(cmp_end <= t_col, 0.0, NEG)
    win_diff = t_col - (win_start + lax.broadcasted_iota(jnp.int32, (1, win_span), 1))
    win_bias = jnp.where((win_diff >= 0) & (win_diff < WINDOW), 0.0, NEG)

    q4s, o_cmp, o_win, sels = [], [], [], []
    for g in range(KV_GROUPS):
        q4 = _stack_group_queries(q, g, lo64)
        q4s.append(q4)

        s = add_bias(_nt_dot(q4, kcmp_ref[0, g]), cmp_bias)
        m = jnp.maximum(jnp.max(s, axis=-1, keepdims=True), M_FLOOR)
        e = jnp.exp2(s - m)
        l = jnp.sum(e, axis=-1, keepdims=True)
        p = e * (1.0 / jnp.maximum(l, 1e-30))
        o_cmp.append(_dot(p.astype(bf16), vcmp_ref[0, g]))

        psum = p[0:tq] + p[tq:2 * tq] + p[2 * tq:3 * tq] + p[3 * tq:4 * tq]
        hi, lo = _split_bf16(psum)
        imp = _nt_dot(ovt, hi) + _nt_dot(ovt, lo)
        jrow = lax.broadcasted_iota(jnp.int32, (n_sel, tq), 0)
        tl = t0 + lax.broadcasted_iota(jnp.int32, (n_sel, tq), 1)
        cur = lax.shift_right_logical(tl, 6)
        forced = (jrow == 0) | (jrow == cur) | (jrow == cur - 1)
        imp = jnp.where(forced, FORCE, imp)
        imp = jnp.where(jrow * SEL_BLOCK <= tl, imp, NEG)
        sels.append(_top_n_blocks(imp, n_sel, top_n).T.astype(bf16))

        s = add_bias(_nt_dot(q4, kw_ref[0, g, pl.ds(win_start, win_span), :]), win_bias)
        m = jnp.max(s, axis=-1, keepdims=True)
        acc = _dot(jnp.exp2(s - m).astype(bf16), vw_ref[0, g, pl.ds(win_start, win_span), :])
        o_win.append(acc / acc[:, HEAD_DIM:HEAD_DIM + 1])

    def sel_bias(g):
        def fn(k0):
            jj = lax.broadcasted_iota(jnp.int32, (n_sel, KC), 0)
            kk = lax.shift_right_logical(k0 + lax.broadcasted_iota(jnp.int32, (n_sel, KC), 1), 6)
            expand = jnp.where(jj == kk, 1.0, 0.0).astype(bf16)
            chosen = _dot(sels[g], expand)
            kpos = k0 + lax.broadcasted_iota(jnp.int32, (1, KC), 1)
            return jnp.where(kpos <= t_col, jnp.where(chosen > 0.5, 0.0, NEG), NEG)
        return fn

    o_sel = _masked_flash(q4s, ks_ref, vs_ref, n_chunks, [sel_bias(g) for g in range(KV_GROUPS)], tq,
                          bound_ref, (shift0_scr, shift1_scr))

    group_outs = []
    for g in range(KV_GROUPS):
        def gate(branch):
            cols = [misc[:, (g * HEADS_PER_GROUP + r) * 3 + branch:(g * HEADS_PER_GROUP + r) * 3 + branch + 1]
                    for r in range(HEADS_PER_GROUP)]
            return jnp.concatenate(cols, axis=0)

        og = gate(0) * o_cmp[g] + gate(1) * o_sel[g] + gate(2) * o_win[g]
        group_outs.append(_group_output(og, lo64, tq))

    o = jnp.concatenate(group_outs, axis=1) * zn_ref[0].astype(f32)
    o_ref[0] = o.astype(bf16)


def _score_bound(q_gain, k_gain):
    return (Q_SCALE * HEAD_DIM * NORM_SLACK * jnp.max(jnp.abs(q_gain)) * jnp.max(jnp.abs(k_gain))).reshape(1)


_SMEM_SCALAR = pl.BlockSpec(memory_space=pltpu.SMEM)


def _shift_scratch(tq):
    return [pltpu.VMEM((HEADS_PER_GROUP * tq, 1), f32)] * KV_GROUPS


def _nsa_attention(bound, qn, kcmp, vcmp, ks, vs, kw, vw, misc, zn, ovt):
    B, S, _ = qn.shape
    tq = TQ_NSA
    n_cmp_pad = kcmp.shape[2]
    n_sel = S // SEL_BLOCK
    row = lambda c: pl.BlockSpec((1, tq, c), lambda b, i: (b, i, 0))
    whole = lambda n: pl.BlockSpec((1, KV_GROUPS, n, LANES), lambda b, i: (b, 0, 0, 0))
    kern = functools.partial(_nsa_kernel, n_cmp_pad=n_cmp_pad, n_sel=n_sel, top_n=min(SEL_TOPN, n_sel))
    return pl.pallas_call(
        kern,
        grid=(B, S // tq),
        in_specs=[_SMEM_SCALAR, row(512), whole(n_cmp_pad), whole(n_cmp_pad), whole(S), whole(S), whole(S),
                  whole(S), row(LANES), row(512), pl.BlockSpec((n_sel, n_cmp_pad), lambda b, i: (0, 0))],
        out_specs=row(512),
        out_shape=jax.ShapeDtypeStruct((B, S, 512), bf16),
        scratch_shapes=_shift_scratch(tq),
        compiler_params=pltpu.CompilerParams(
            dimension_semantics=("arbitrary", "arbitrary"), vmem_limit_bytes=VMEM_LIMIT),
        name="nsa_attention",
    )(bound, qn, kcmp, vcmp, ks, vs, kw, vw, misc, zn, ovt)


N_SLABS = 32
_TRANSPOSE_STAGES = ((16, 0x0000FFFF), (8, 0x00FF00FF), (4, 0x0F0F0F0F), (2, 0x33333333), (1, 0x55555555))


def _ordered_bits(v):
    return v ^ (jnp.right_shift(v, 31) & jnp.int32(0x7FFFFFFF))


def _bit_planes(score_ref, plane_ref):
    for stage, (j, mask) in enumerate(_TRANSPOSE_STAGES):
        if stage == 0:
            load = lambda k: _ordered_bits(pltpu.bitcast(score_ref[k], jnp.int32))
        else:
            load = lambda k: plane_ref[k]

        def body(p, carry, j=j, mask=mask, load=load):
            k = jnp.left_shift(p & ~(j - 1), 1) | (p & (j - 1))
            lo = load(k)
            hi = load(k + j)
            t = (lo ^ lax.shift_right_logical(hi, j)) & mask
            plane_ref[k] = lo ^ t
            plane_ref[k + j] = hi ^ jnp.left_shift(t, j)
            return carry

        lax.fori_loop(0, N_SLABS // 2, body, 0)


def _dsa_kernel(bound_ref, qd_ref, qi_ref, misc_ref, ki_ref, kd_ref, vd_ref, zd_ref, tri_ref,
                o_ref, score_scr, plane_scr, bias_scr, shift0_scr, shift1_scr, *, top_k, tq):
    t0 = pl.program_id(1) * tq
    lane = lax.broadcasted_iota(jnp.int32, (1, LANES), 1)
    lo64 = lane < 64
    t_col = t0 + lax.broadcasted_iota(jnp.int32, (tq, 1), 0)
    t_row = t0 + lax.broadcasted_iota(jnp.int32, (1, tq), 1)
    n_chunks = (t0 + tq + KC - 1) // KC
    slabs_per_chunk = KC // LANES
    int_min = jnp.int32(-2 ** 31)

    def chunk_slabs(c):
        return pl.ds(c * slabs_per_chunk, slabs_per_chunk)

    def to_col(row):
        return jnp.broadcast_to(row, (SUBLANES, tq)).T[:, 0:1]

    qi = qi_ref[0]
    qi_heads = [_half_masked(qi[:, (h // 2) * LANES:(h // 2 + 1) * LANES], h, lo64) for h in range(IDX_HEADS)]
    misc_t = misc_ref[0].T
    wi = [misc_t[MISC_WI + h:MISC_WI + h + 1, :] for h in range(IDX_HEADS)]

    def score_body(c, carry):
        k0 = pl.multiple_of(c * KC, KC)
        kib = ki_ref[0, pl.ds(k0, KC), :]
        sc = jnp.zeros((KC, tq), f32)
        for h in range(IDX_HEADS):
            sc = sc + jnp.maximum(_nt_dot(kib, qi_heads[h]), 0.0) * wi[h]
        sc = jnp.where(sc == 0.0, 0.0, sc)
        kpos = k0 + lax.broadcasted_iota(jnp.int32, (KC, 1), 0)
        sc = jnp.where(kpos <= t_row, sc, NON_CAUSAL_MARK)
        score_scr[chunk_slabs(c)] = sc.reshape(slabs_per_chunk, LANES, tq)
        return carry

    lax.fori_loop(0, n_chunks, score_body, 0)

    def fill_body(c, carry):
        score_scr[chunk_slabs(c)] = jnp.full((slabs_per_chunk, LANES, tq), NON_CAUSAL_MARK, f32)
        return carry

    lax.fori_loop(n_chunks, N_SLABS // slabs_per_chunk, fill_body, 0)

    _bit_planes(score_scr, plane_scr)
    kf = float(top_k)

    def select_bit(i, carry):
        alive, above, thr = carry
        plane = plane_scr[i] ^ jnp.where(i == 0, jnp.int32(-1), jnp.int32(0))
        ones = alive & plane
        c1 = jnp.sum(lax.population_count(ones).astype(f32), axis=0, keepdims=True)
        take = (above + c1) >= kf
        alive = jnp.where(take, ones, alive ^ ones)
        above = jnp.where(take, above, above + c1)
        thr = jnp.where(take, thr | jnp.left_shift(jnp.int32(1), 31 - i), thr)
        return alive, above, thr

    init = (jnp.full((LANES, tq), -1, jnp.int32), jnp.zeros((1, tq), f32), jnp.zeros((1, tq), jnp.int32))
    _, above, thr_u = lax.fori_loop(0, 32, select_bit, init)

    def key_to_score(key_row):
        return pltpu.bitcast(_ordered_bits(key_row), f32)

    def write_bias(thr_row, need_row):
        thr = to_col(thr_row)
        need = to_col(need_row)

        def body(c, carry):
            ties_before, n_above = carry
            k0 = pl.multiple_of(c * KC, KC)
            blk = score_scr[chunk_slabs(c)].reshape(KC, tq).T
            gt = jnp.where(blk > thr, 1.0, 0.0)
            eq = jnp.where(blk == thr, 1.0, 0.0)
            prefix = ties_before + _dot(eq.astype(bf16), tri_ref[...])
            chosen = gt + jnp.where(prefix <= need, eq, 0.0)
            kpos = k0 + lax.broadcasted_iota(jnp.int32, (1, KC), 1)
            bias_scr[:, pl.ds(k0, KC)] = jnp.where(kpos <= t_col, jnp.where(chosen > 0.5, 0.0, NEG), NEG)
            return prefix[:, KC - 1:KC], n_above + jnp.sum(gt, axis=-1, keepdims=True)

        zero = jnp.zeros((tq, 1), f32)
        return lax.fori_loop(0, n_chunks, body, (zero, zero))

    ties, n_above = write_bias(key_to_score(thr_u ^ int_min), kf - above)
    verified = (n_above == to_col(above)) & (n_above < kf) & (n_above + ties >= kf)

    @pl.when(jnp.sum(jnp.where(verified, 0.0, 1.0)) > 0.0)
    def _():
        def count(pred):
            def body(c, acc):
                blk = score_scr[chunk_slabs(c)].reshape(KC, tq)
                return acc + jnp.sum(jnp.where(pred(blk), 1.0, 0.0), axis=0, keepdims=True)
            return lax.fori_loop(0, N_SLABS // slabs_per_chunk, body, jnp.zeros((1, tq), f32))

        key = jnp.where(count(lambda b: b >= 0.0) >= kf, jnp.int32(0), int_min)

        def bit(i, key):
            cand = key + jnp.left_shift(jnp.int32(1), 30 - i)
            cand_f = key_to_score(cand)
            return jnp.where(count(lambda b: b >= cand_f) >= kf, cand, key)

        thr_f = key_to_score(lax.fori_loop(0, 31, bit, key))
        write_bias(thr_f, kf - count(lambda b: b > thr_f))

    def bias_fn(k0):
        return bias_scr[:, pl.ds(k0, KC)]

    qd = qd_ref[0]
    q4s = [_stack_group_queries(qd, g, lo64) for g in range(KV_GROUPS)]
    outs = _masked_flash(q4s, kd_ref, vd_ref, n_chunks, [bias_fn] * KV_GROUPS, tq,
                         bound_ref, (shift0_scr, shift1_scr))
    o = jnp.concatenate([_group_output(og, lo64, tq) for og in outs], axis=1) * zd_ref[0].astype(f32)
    o_ref[0] = o.astype(bf16)


def _dsa_attention(bound, qd, qi, misc, ki, kd, vd, zd):
    B, S, _ = qd.shape
    tq = TQ_DSA
    assert S <= N_SLABS * LANES
    tri = jnp.asarray(np.triu(np.ones((KC, KC), np.float32)), bf16)
    row = lambda c: pl.BlockSpec((1, tq, c), lambda b, i: (b, i, 0))
    whole = pl.BlockSpec((1, KV_GROUPS, S, LANES), lambda b, i: (b, 0, 0, 0))
    return pl.pallas_call(
        functools.partial(_dsa_kernel, top_k=min(DSA_TOPK_MAX, S // 4), tq=tq),
        grid=(B, S // tq),
        in_specs=[_SMEM_SCALAR, row(512), row(256), row(LANES), pl.BlockSpec((1, S, LANES), lambda b, i: (b, 0, 0)),
                  whole, whole, row(512), pl.BlockSpec((KC, KC), lambda b, i: (0, 0))],
        out_specs=row(512),
        out_shape=jax.ShapeDtypeStruct((B, S, 512), bf16),
        scratch_shapes=[pltpu.VMEM((N_SLABS, LANES, tq), f32), pltpu.VMEM((N_SLABS, LANES, tq), jnp.int32),
                        pltpu.VMEM((tq, S), f32)] + _shift_scratch(tq),
        compiler_params=pltpu.CompilerParams(
            dimension_semantics=("arbitrary", "arbitrary"), vmem_limit_bytes=VMEM_LIMIT),
        name="dsa_attention",
    )(bound, qd, qi, misc, ki, kd, vd, zd, tri)


def _outproj_kernel(x_ref, on_ref, od_ref, w_ref, o_ref):
    half = w_ref.shape[0] // 2
    o_ref[0] = x_ref[0] + _dot(on_ref[0], w_ref[0:half, :]) + _dot(od_ref[0], w_ref[half:, :])


def _out_projection(x, o_nsa, o_dsa, w_out):
    B, S, D = x.shape
    tm = min(TM_PROJ, S)
    row = lambda c: pl.BlockSpec((1, tm, c), lambda b, i: (b, i, 0))
    return pl.pallas_call(
        _outproj_kernel,
        grid=(B, S // tm),
        in_specs=[row(D), row(512), row(512), pl.BlockSpec(w_out.shape, lambda b, i: (0, 0))],
        out_specs=row(D),
        out_shape=jax.ShapeDtypeStruct((B, S, D), f32),
        compiler_params=pltpu.CompilerParams(
            dimension_semantics=("arbitrary", "arbitrary"), vmem_limit_bytes=VMEM_LIMIT),
        name="out_projection",
    )(x, o_nsa, o_dsa, w_out)


def _rope_tables(pos):
    half = HEAD_DIM // 2
    inv_freq = ROPE_THETA ** (-jnp.arange(half, dtype=f32) / half)
    ang = pos[:, None] * inv_freq[None, :]
    cos, sin = jnp.cos(ang), jnp.sin(ang)
    cos_t = jnp.tile(cos, (1, LANES // half))
    sin_t = jnp.tile(jnp.concatenate([-sin, sin], axis=1), (1, LANES // HEAD_DIM))
    return cos_t, sin_t


def _overlap_t(seq, n_cmp_pad):
    n_cmp = (seq - CMP_BLOCK) // CMP_STRIDE + 1
    n_sel = seq // SEL_BLOCK
    c_start = np.arange(n_cmp) * CMP_STRIDE
    j_start = np.arange(n_sel) * SEL_BLOCK
    ov = np.clip(np.minimum(c_start[:, None] + CMP_BLOCK, j_start[None, :] + SEL_BLOCK)
                 - np.maximum(c_start[:, None], j_start[None, :]), 0, None).astype(np.float32) / CMP_BLOCK
    out = np.zeros((n_sel, n_cmp_pad), np.float32)
    out[:, :n_cmp] = ov.T
    return out


def _block_diag2(w):
    z = jnp.zeros_like(w)
    return jnp.concatenate([jnp.concatenate([w, z], axis=-1), jnp.concatenate([z, w], axis=-1)], axis=-2)


def _layer(x, norm_gain, w_in, nsa_q_gain, nsa_kc_gain, nsa_ks_gain, nsa_kw_gain,
           cmp_pe_k, cmp_k_w1, cmp_k_b1, cmp_k_w2, cmp_pe_v, cmp_v_w1, cmp_v_b1, cmp_v_w2,
           dsa_q_gain, dsa_k_gain, w_out):
    B, S, _ = x.shape
    assert S % KC == 0 and S >= WINDOW + TQ_NSA
    n_cmp_pad = S // CMP_STRIDE

    w_perm = _permuted_weight(w_in)
    head_of_lane = np.arange(2 * LANES) // HEAD_DIM
    bd = jnp.asarray(head_of_lane[:, None] == head_of_lane[None, :], bf16)
    dup = lambda v: jnp.tile(v.reshape(1, -1), (1, 2))
    quad = lambda v: jnp.tile(v.reshape(1, -1), (1, 4))
    gains = jnp.concatenate([quad(nsa_q_gain), jnp.concatenate([dup(nsa_ks_gain), dup(nsa_kw_gain)], axis=1),
                             quad(dsa_q_gain), quad(dsa_k_gain), jnp.ones((4, 2 * LANES), f32)], axis=0)
    cos_t, sin_t = _rope_tables(jnp.arange(S, dtype=f32))

    (qn, qd, qi, ki, ks, kw, kd, vs, vw, vd, kc, vc, zn, zd, misc) = _in_projection(
        x, norm_gain.reshape(1, -1), w_perm, bd, cos_t, sin_t, gains)

    cmp_pos = (jnp.arange(n_cmp_pad) * CMP_STRIDE + CMP_BLOCK - 1).astype(f32)
    cos_c, sin_c = _rope_tables(cmp_pos)

    def cmp_weights(pe, w1, b1, w2):
        w1_bd = _block_diag2(w1.reshape(CMP_BLOCK, HEAD_DIM, CMP_HIDDEN)).astype(bf16)
        return jnp.tile(pe, (1, 2)), w1_bd, dup(b1), _block_diag2(w2).astype(bf16)

    kcmp = _compress(kc, *cmp_weights(cmp_pe_k, cmp_k_w1, cmp_k_b1, cmp_k_w2), bd[:LANES, :LANES], dup(nsa_kc_gain),
                     cos_c, sin_c, is_key=True)
    vcmp = _compress(vc, *cmp_weights(cmp_pe_v, cmp_v_w1, cmp_v_b1, cmp_v_w2), bd[:LANES, :LANES], dup(nsa_kc_gain),
                     cos_c, sin_c, is_key=False)

    ovt = jnp.asarray(_overlap_t(S, n_cmp_pad), bf16)
    o_nsa = _nsa_attention(_score_bound(nsa_q_gain, nsa_ks_gain), qn, kcmp, vcmp, ks, vs, kw, vw, misc, zn, ovt)

    o_dsa = _dsa_attention(_score_bound(dsa_q_gain, dsa_k_gain), qd, qi, misc, ki, kd, vd, zd)

    return _out_projection(x, o_nsa, o_dsa, w_out.astype(bf16))


def kernel(x, norm_gain, w_in, nsa_q_gain, nsa_kc_gain, nsa_ks_gain, nsa_kw_gain, cmp_pe_k, cmp_k_w1,
           cmp_k_b1, cmp_k_w2, cmp_pe_v, cmp_v_w1, cmp_v_b1, cmp_v_w2, dsa_q_gain, dsa_k_gain, w_out):
    for l in range(norm_gain.shape[0]):
        x = _layer(x, norm_gain[l], w_in[l], nsa_q_gain[l], nsa_kc_gain[l], nsa_ks_gain[l], nsa_kw_gain[l],
                   cmp_pe_k[l], cmp_k_w1[l], cmp_k_b1[l], cmp_k_w2[l], cmp_pe_v[l], cmp_v_w1[l], cmp_v_b1[l],
                   cmp_v_w2[l], dsa_q_gain[l], dsa_k_gain[l], w_out[l])
    return x
```

```python
import functools

import numpy as np
import jax
import jax.numpy as jnp
from jax import lax
from jax.experimental import pallas as pl
from jax.experimental.pallas import tpu as pltpu

D_MODEL = 1024
HEAD_DIM = 64
NSA_HEADS = 8
DSA_HEADS = 8
KV_GROUPS = 2
HEADS_PER_GROUP = 4
IDX_HEADS = 4
CMP_BLOCK = 32
CMP_STRIDE = 16
CMP_HIDDEN = 256
SEL_BLOCK = 64
SEL_TOPN = 16
WINDOW = 512
DSA_TOPK_MAX = 256
ROPE_THETA = 10000.0
EPS = 1e-6
NEG = -1e30
FORCE = 1e6
ATTN_SCALE = HEAD_DIM ** -0.5
IDX_SCALE = HEAD_DIM ** -0.5
WI_SCALE = IDX_HEADS ** -0.5
LOG2E = 1.4426950408889634
Q_SCALE = ATTN_SCALE * LOG2E

LANES = 128
SUBLANES = 8
M_FLOOR = -5e29
NON_CAUSAL_MARK = -3.0e38
MAX_FIXED_SHIFT = 30.0
NORM_SLACK = 1.03
VMEM_LIMIT = 56 * 1024 * 1024

IN_WIDTHS = (512, 128, 128, 128, 128, 128, 128, 24, 512, 512, 128, 128, 256, 64, 4, 512)
IN_NAMES = ("q_n", "kc", "vc", "ks", "vs", "kw", "vw", "gate", "z_n",
            "q_d", "k_d", "v_d", "qi", "ki", "wi", "z_d")
IN_COLS = sum(IN_WIDTHS)

CH_QN, CH_KS, CH_KW, CH_QD, CH_KD, CH_QI, CH_KI = 0, 4, 5, 6, 10, 11, 13
CH_KC, CH_VC, CH_VS, CH_VW, CH_VD, CH_ZN, CH_ZD, CH_MISC = 14, 15, 16, 17, 18, 19, 23, 27
N_CHUNKS = 28
MISC_WI = 24

TM_PROJ = 512
TQ_NSA = 256
TQ_DSA = 256
KC = 512

f32 = jnp.float32
bf16 = jnp.bfloat16


def _permuted_weight(w_in):
    off = dict(zip(IN_NAMES, np.cumsum((0,) + IN_WIDTHS[:-1])))
    width = dict(zip(IN_NAMES, IN_WIDTHS))
    order = ("q_n", "ks", "kw", "q_d", "k_d", "qi", "ki", "ki", "kc", "vc", "vs", "vw", "v_d", "z_n", "z_d",
             "gate", "wi")
    w16 = w_in.astype(bf16)
    parts = [w16[:, off[n]:off[n] + width[n]] for n in order]
    used = sum(width[n] for n in order)
    parts.append(jnp.zeros((D_MODEL, N_CHUNKS * LANES - used), bf16))
    return jnp.concatenate(parts, axis=1)


def _nt_dot(a, b):
    return lax.dot_general(a, b, (((1,), (1,)), ((), ())), preferred_element_type=f32)


def _dot(a, b):
    return jnp.dot(a, b, preferred_element_type=f32)


def _split_bf16(v):
    hi = v.astype(bf16)
    lo = (v - hi.astype(f32)).astype(bf16)
    return hi, lo


def _head_rms_norm(y, bd, gain_row):
    ssq = _dot((y * y).astype(bf16), bd)
    return y * lax.rsqrt(ssq * (1.0 / HEAD_DIM) + EPS) * gain_row


def _rope(y, cos, sin_signed, lo32):
    partner = jnp.where(lo32, pltpu.roll(y, LANES - 32, 1), pltpu.roll(y, 32, 1))
    return y * cos + partner * sin_signed


def _dup_halves(y, lo64):
    r = pltpu.roll(y, 64, 1)
    return jnp.where(lo64, y, r), jnp.where(lo64, r, y)


def _value_with_ones(y, lo64):
    return jnp.where(lo64, y, 1.0), jnp.where(lo64, pltpu.roll(y, 64, 1), 1.0)


def _inproj_kernel(x_ref, ng_ref, w_ref, bd_ref, cos_ref, sin_ref, gains_ref,
                   qn_ref, qd_ref, qi_ref, ki_ref, ks_ref, kw_ref, kd_ref,
                   vs_ref, vw_ref, vd_ref, kc_ref, vc_ref, zn_ref, zd_ref, misc_ref):
    x = x_ref[0]
    ms = jnp.mean(x * x, axis=-1, keepdims=True)
    h = (x * lax.rsqrt(ms + EPS) * ng_ref[...]).astype(bf16)
    cos = cos_ref[...]
    sin = sin_ref[...]
    bd = bd_ref[...]
    lane = lax.broadcasted_iota(jnp.int32, (1, LANES), 1)
    lo32 = (lane % 64) < 32
    lo64 = lane < 64

    def proj(c0, n):
        return _dot(h, w_ref[:, c0 * LANES:(c0 + n) * LANES])

    def chunk(y, i):
        return y[:, i * LANES:(i + 1) * LANES]

    def normed_rope(y, gain_idx):
        w = y.shape[1]
        yn = _head_rms_norm(y, bd[:w, :w], gains_ref[gain_idx:gain_idx + 1, 0:w])
        return [_rope(chunk(yn, i), cos, sin, lo32) for i in range(w // LANES)]

    for c0, gain_idx, ref in ((CH_QN, 0, qn_ref), (CH_QD, 2, qd_ref)):
        y = proj(c0, 4)
        for i in range(2):
            for j, r in enumerate(normed_rope(y[:, 2 * i * LANES:(2 * i + 2) * LANES], gain_idx)):
                ref[0, :, (2 * i + j) * LANES:(2 * i + j + 1) * LANES] = (r * Q_SCALE).astype(bf16)
    y = proj(CH_QI, 2)
    for i in range(2):
        qi_ref[0, :, i * LANES:(i + 1) * LANES] = (_rope(chunk(y, i), cos, sin, lo32) * IDX_SCALE).astype(bf16)
    ki_ref[0] = _rope(proj(CH_KI, 1), cos, sin, lo32).astype(bf16)

    k_pairs = normed_rope(proj(CH_KS, 2), 1) + normed_rope(proj(CH_KD, 1), 3)
    for r, ref in zip(k_pairs, (ks_ref, kw_ref, kd_ref)):
        a, b = _dup_halves(r, lo64)
        ref[0, 0] = a.astype(bf16)
        ref[0, 1] = b.astype(bf16)
    for c0, ref in ((CH_VS, vs_ref), (CH_VW, vw_ref), (CH_VD, vd_ref)):
        a, b = _value_with_ones(proj(c0, 1), lo64)
        ref[0, 0] = a.astype(bf16)
        ref[0, 1] = b.astype(bf16)

    kc_ref[0] = proj(CH_KC, 1)
    vc_ref[0] = proj(CH_VC, 1)
    z = proj(CH_ZN, 4)
    zn_ref[0] = (z * jax.nn.sigmoid(z)).astype(bf16)
    z = proj(CH_ZD, 4)
    zd_ref[0] = (z * jax.nn.sigmoid(z)).astype(bf16)
    m = proj(CH_MISC, 1)
    misc_ref[0] = jnp.where(lane < MISC_WI, jax.nn.sigmoid(m), m * WI_SCALE)


def _in_projection(x, norm_gain, w_perm, bd, cos_t, sin_t, gains):
    B, S, _ = x.shape
    tm = min(TM_PROJ, S)
    grid = (B, S // tm)
    row = lambda c: pl.BlockSpec((1, tm, c), lambda b, i: (b, i, 0))
    dup = pl.BlockSpec((1, KV_GROUPS, tm, LANES), lambda b, i: (b, 0, i, 0))
    const = lambda shape: pl.BlockSpec(shape, lambda b, i: tuple(0 for _ in shape))
    tab = pl.BlockSpec((tm, LANES), lambda b, i: (i, 0))
    sds = jax.ShapeDtypeStruct
    out_shape = (
        sds((B, S, 512), bf16), sds((B, S, 512), bf16), sds((B, S, 256), bf16), sds((B, S, LANES), bf16),
        sds((B, KV_GROUPS, S, LANES), bf16), sds((B, KV_GROUPS, S, LANES), bf16),
        sds((B, KV_GROUPS, S, LANES), bf16), sds((B, KV_GROUPS, S, LANES), bf16),
        sds((B, KV_GROUPS, S, LANES), bf16), sds((B, KV_GROUPS, S, LANES), bf16),
        sds((B, S, LANES), f32), sds((B, S, LANES), f32),
        sds((B, S, 512), bf16), sds((B, S, 512), bf16), sds((B, S, LANES), f32),
    )
    out_specs = (row(512), row(512), row(256), row(LANES), dup, dup, dup, dup, dup, dup,
                 row(LANES), row(LANES), row(512), row(512), row(LANES))
    return pl.pallas_call(
        _inproj_kernel,
        grid=grid,
        in_specs=[row(D_MODEL), const((1, D_MODEL)), const((D_MODEL, N_CHUNKS * LANES)),
                  const((2 * LANES, 2 * LANES)), tab, tab, const((8, 2 * LANES))],
        out_specs=out_specs,
        out_shape=out_shape,
        compiler_params=pltpu.CompilerParams(
            dimension_semantics=("arbitrary", "arbitrary"), vmem_limit_bytes=VMEM_LIMIT),
        name="in_projection",
    )(x, norm_gain, w_perm, bd, cos_t, sin_t, gains)


def _compress_kernel(src_ref, pe_ref, w1_ref, b1_ref, w2_ref, bd_ref, gain_ref, cos_ref, sin_ref,
                     out_ref, *, n_rows, is_key):
    half = CMP_BLOCK // 2
    acc_a = jnp.zeros((n_rows, 2 * CMP_HIDDEN), f32)
    acc_b = jnp.zeros((n_rows, 2 * CMP_HIDDEN), f32)
    for l in range(half):
        rows = src_ref[0, pl.ds(l, n_rows, stride=CMP_STRIDE), :]
        acc_a = acc_a + _dot((rows + pe_ref[l:l + 1, :]).astype(bf16), w1_ref[l])
        acc_b = acc_b + _dot((rows + pe_ref[l + half:l + half + 1, :]).astype(bf16), w1_ref[l + half])
    pre = acc_a + pltpu.roll(acc_b, n_rows - 1, 0) + b1_ref[...]
    hid = pre * jax.nn.sigmoid(pre)
    out = _dot(hid.astype(bf16), w2_ref[...])
    lane = lax.broadcasted_iota(jnp.int32, (1, LANES), 1)
    if is_key:
        out = _head_rms_norm(out, bd_ref[...], gain_ref[...])
        out = _rope(out, cos_ref[...], sin_ref[...], (lane % 64) < 32)
        a, b = _dup_halves(out, lane < 64)
    else:
        a, b = _value_with_ones(out, lane < 64)
    out_ref[0, 0] = a.astype(bf16)
    out_ref[0, 1] = b.astype(bf16)


def _compress(src, pe_dup, w1_bd, b1_dup, w2_bd, bd, gain_dup, cos_c, sin_c, *, is_key):
    B, S, _ = src.shape
    n_rows = S // CMP_STRIDE
    const = lambda shape: pl.BlockSpec(shape, lambda b: tuple(0 for _ in shape))
    return pl.pallas_call(
        functools.partial(_compress_kernel, n_rows=n_rows, is_key=is_key),
        grid=(B,),
        in_specs=[pl.BlockSpec((1, S, LANES), lambda b: (b, 0, 0)),
                  const((CMP_BLOCK, LANES)), const((CMP_BLOCK, LANES, 2 * CMP_HIDDEN)),
                  const((1, 2 * CMP_HIDDEN)), const((2 * CMP_HIDDEN, LANES)), const((LANES, LANES)),
                  const((1, LANES)), const((n_rows, LANES)), const((n_rows, LANES))],
        out_specs=pl.BlockSpec((1, KV_GROUPS, n_rows, LANES), lambda b: (b, 0, 0, 0)),
        out_shape=jax.ShapeDtypeStruct((B, KV_GROUPS, n_rows, LANES), bf16),
        compiler_params=pltpu.CompilerParams(
            dimension_semantics=("arbitrary",), vmem_limit_bytes=VMEM_LIMIT),
        name="compress_k" if is_key else "compress_v",
    )(src, pe_dup, w1_bd, b1_dup, w2_bd, bd, gain_dup, cos_c, sin_c)


def _half_masked(pair, h, lo64):
    keep = lo64 if h % 2 == 0 else jnp.logical_not(lo64)
    return jnp.where(keep, pair, jnp.zeros_like(pair))


def _stack_group_queries(q, g, lo64):
    slabs = []
    for r in range(HEADS_PER_GROUP):
        h = g * HEADS_PER_GROUP + r
        slabs.append(_half_masked(q[:, (h // 2) * LANES:(h // 2 + 1) * LANES], h, lo64))
    return jnp.concatenate(slabs, axis=0)


def _masked_flash(q4s, k_ref, v_ref, n_chunks, bias_fns, tq, bound_ref, shift_scrs):
    rows = HEADS_PER_GROUP * tq
    bound = bound_ref[0]

    def scores(g, k0):
        s = _nt_dot(q4s[g], k_ref[0, g, pl.ds(k0, KC), :])
        return (s.reshape(HEADS_PER_GROUP, tq, KC) + bias_fns[g](k0)[None]).reshape(rows, KC)

    for g in range(KV_GROUPS):
        shift_scrs[g][...] = jnp.full((rows, 1), bound, f32)

    @pl.when(bound > MAX_FIXED_SHIFT)
    def _():
        def body(c, ms):
            k0 = pl.multiple_of(c * KC, KC)
            return tuple(jnp.maximum(ms[g], jnp.max(scores(g, k0), axis=-1, keepdims=True))
                         for g in range(KV_GROUPS))

        init = tuple(jnp.full((rows, 1), M_FLOOR, f32) for _ in range(KV_GROUPS))
        for g, m in enumerate(lax.fori_loop(0, n_chunks, body, init)):
            shift_scrs[g][...] = m

    def body(c, accs):
        k0 = pl.multiple_of(c * KC, KC)
        new = []
        for g in range(KV_GROUPS):
            p = jnp.exp2(scores(g, k0) - shift_scrs[g][...]).astype(bf16)
            new.append(accs[g] + _dot(p, v_ref[0, g, pl.ds(k0, KC), :]))
        return tuple(new)

    init = tuple(jnp.zeros((rows, LANES), f32) for _ in range(KV_GROUPS))
    out = lax.fori_loop(0, n_chunks, body, init)
    return [acc / acc[:, HEAD_DIM:HEAD_DIM + 1] for acc in out]


def _group_output(og, lo64, tq):
    pairs = []
    for p in range(2):
        even = og[(2 * p) * tq:(2 * p + 1) * tq]
        odd = og[(2 * p + 1) * tq:(2 * p + 2) * tq]
        pairs.append(jnp.where(lo64, even, pltpu.roll(odd, HEAD_DIM, 1)))
    return jnp.concatenate(pairs, axis=1)


def _top_n_blocks(imp, n_sel, top_n):
    groups = [imp[SUBLANES * a:SUBLANES * (a + 1)] for a in range(n_sel // SUBLANES)]
    jsub = lax.broadcasted_iota(jnp.int32, groups[0].shape, 0)
    ranks = [jnp.zeros(groups[0].shape, f32) for _ in groups]
    for i in range(n_sel):
        row = imp[i:i + 1, :]
        for a, grp in enumerate(groups):
            if SUBLANES * a > i:
                beats = row >= grp
            elif SUBLANES * (a + 1) - 1 < i:
                beats = row > grp
            else:
                tie = jnp.where(jsub + SUBLANES * a > i, 1.0, 0.0)
                ranks[a] = ranks[a] + jnp.where(row > grp, 1.0, jnp.where(row == grp, tie, 0.0))
                continue
            ranks[a] = ranks[a] + jnp.where(beats, 1.0, 0.0)
    rank = jnp.concatenate(ranks, axis=0)
    return jnp.where(rank < top_n, 1.0, 0.0)


def _nsa_kernel(bound_ref, qn_ref, kcmp_ref, vcmp_ref, ks_ref, vs_ref, kw_ref, vw_ref, misc_ref, zn_ref, ovt_ref,
                expand_ref, o_ref, shift0_scr, shift1_scr, bias_scr, *, n_cmp_pad, n_sel, top_n):
    tq = TQ_NSA
    t0 = pl.program_id(1) * tq
    lane = lax.broadcasted_iota(jnp.int32, (1, LANES), 1)
    lo64 = lane < 64
    q = qn_ref[0]
    misc = misc_ref[0]
    t_col = t0 + lax.broadcasted_iota(jnp.int32, (tq, 1), 0)
    n_chunks = (t0 + tq + KC - 1) // KC
    win_span = WINDOW + tq
    win_start = pl.multiple_of(jnp.maximum(t0 - WINDOW, 0), tq)
    ovt = ovt_ref[...]
    rows = HEADS_PER_GROUP * tq

    def add_bias(s, bias):
        return (s.reshape(HEADS_PER_GROUP, tq, s.shape[-1]) + bias[None]).reshape(rows, s.shape[-1])

    cmp_end = lax.broadcasted_iota(jnp.int32, (1, n_cmp_pad), 1) * CMP_STRIDE + (CMP_BLOCK - 1)
    cmp_bias = jnp.where(cmp_end <= t_col, 0.0, NEG)
    win_diff = t_col - (win_start + lax.broadcasted_iota(jnp.int32, (1, win_span), 1))
    win_bias = jnp.where((win_diff >= 0) & (win_diff < WINDOW), 0.0, NEG)

    q4s, o_cmp, o_win, sels = [], [], [], []
    for g in range(KV_GROUPS):
        q4 = _stack_group_queries(q, g, lo64)
        q4s.append(q4)

        s = add_bias(_nt_dot(q4, kcmp_ref[0, g]), cmp_bias)
        m = jnp.maximum(jnp.max(s, axis=-1, keepdims=True), M_FLOOR)
        e = jnp.exp2(s - m)
        l = jnp.sum(e, axis=-1, keepdims=True)
        p = e * (1.0 / jnp.maximum(l, 1e-30))
        o_cmp.append(_dot(p.astype(bf16), vcmp_ref[0, g]))

        psum = p[0:tq] + p[tq:2 * tq] + p[2 * tq:3 * tq] + p[3 * tq:4 * tq]
        hi, lo = _split_bf16(psum)
        imp = _nt_dot(ovt, hi) + _nt_dot(ovt, lo)
        jrow = lax.broadcasted_iota(jnp.int32, (n_sel, tq), 0)
        tl = t0 + lax.broadcasted_iota(jnp.int32, (n_sel, tq), 1)
        cur = lax.shift_right_logical(tl, 6)
        forced = (jrow == 0) | (jrow == cur) | (jrow == cur - 1)
        imp = jnp.where(forced, FORCE, imp)
        imp = jnp.where(jrow * SEL_BLOCK <= tl, imp, NEG)
        pad_row = lax.broadcasted_iota(jnp.int32, (LANES - n_sel, tq), 0)
        chosen = jnp.concatenate([_top_n_blocks(imp, n_sel, top_n) * -NEG, jnp.where(pad_row == 0, NEG, 0.0)], axis=0)
        sels.append(chosen.T.astype(bf16))

        s = add_bias(_nt_dot(q4, kw_ref[0, g, pl.ds(win_start, win_span), :]), win_bias)
        m = jnp.max(s, axis=-1, keepdims=True)
        acc = _dot(jnp.exp2(s - m).astype(bf16), vw_ref[0, g, pl.ds(win_start, win_span), :])
        o_win.append(acc / acc[:, HEAD_DIM:HEAD_DIM + 1])

    def bias_body(c, carry):
        k0 = pl.multiple_of(c * KC, KC)
        for g in range(KV_GROUPS):
            bias_scr[g, :, pl.ds(k0, KC)] = _dot(sels[g], expand_ref[:, pl.ds(k0, KC)])
        return carry

    lax.fori_loop(0, n_chunks, bias_body, 0)
    k_last = pl.multiple_of((n_chunks - 1) * KC, KC)
    causal = jnp.where(k_last + lax.broadcasted_iota(jnp.int32, (1, KC), 1) <= t_col, 0.0, NEG)
    for g in range(KV_GROUPS):
        bias_scr[g, :, pl.ds(k_last, KC)] = bias_scr[g, :, pl.ds(k_last, KC)] + causal

    o_sel = _masked_flash(q4s, ks_ref, vs_ref, n_chunks,
                          [lambda k0, g=g: bias_scr[g, :, pl.ds(k0, KC)] for g in range(KV_GROUPS)], tq,
                          bound_ref, (shift0_scr, shift1_scr))

    group_outs = []
    for g in range(KV_GROUPS):
        def gate(branch):
            cols = [misc[:, (g * HEADS_PER_GROUP + r) * 3 + branch:(g * HEADS_PER_GROUP + r) * 3 + branch + 1]
                    for r in range(HEADS_PER_GROUP)]
            return jnp.concatenate(cols, axis=0)

        og = gate(0) * o_cmp[g] + gate(1) * o_sel[g] + gate(2) * o_win[g]
        group_outs.append(_group_output(og, lo64, tq))

    o = jnp.concatenate(group_outs, axis=1) * zn_ref[0].astype(f32)
    o_ref[0] = o.astype(bf16)


def _score_bound(q_gain, k_gain):
    return (Q_SCALE * HEAD_DIM * NORM_SLACK * jnp.max(jnp.abs(q_gain)) * jnp.max(jnp.abs(k_gain))).reshape(1)


_SMEM_SCALAR = pl.BlockSpec(memory_space=pltpu.SMEM)


def _shift_scratch(tq):
    return [pltpu.VMEM((HEADS_PER_GROUP * tq, 1), f32)] * KV_GROUPS


def _nsa_attention(bound, qn, kcmp, vcmp, ks, vs, kw, vw, misc, zn, ovt):
    B, S, _ = qn.shape
    tq = TQ_NSA
    n_cmp_pad = kcmp.shape[2]
    n_sel = S // SEL_BLOCK
    row = lambda c: pl.BlockSpec((1, tq, c), lambda b, i: (b, i, 0))
    whole = lambda n: pl.BlockSpec((1, KV_GROUPS, n, LANES), lambda b, i: (b, 0, 0, 0))
    kern = functools.partial(_nsa_kernel, n_cmp_pad=n_cmp_pad, n_sel=n_sel, top_n=min(SEL_TOPN, n_sel))
    assert n_sel < LANES
    expand = np.zeros((LANES, S), np.float32)
    expand[np.arange(S) // SEL_BLOCK, np.arange(S)] = 1.0
    expand[n_sel] = 1.0
    expand = jnp.asarray(expand, bf16)
    return pl.pallas_call(
        kern,
        grid=(B, S // tq),
        in_specs=[_SMEM_SCALAR, row(512), whole(n_cmp_pad), whole(n_cmp_pad), whole(S), whole(S), whole(S),
                  whole(S), row(LANES), row(512), pl.BlockSpec((n_sel, n_cmp_pad), lambda b, i: (0, 0)),
                  pl.BlockSpec((LANES, S), lambda b, i: (0, 0))],
        out_specs=row(512),
        out_shape=jax.ShapeDtypeStruct((B, S, 512), bf16),
        scratch_shapes=_shift_scratch(tq) + [pltpu.VMEM((KV_GROUPS, tq, S), f32)],
        compiler_params=pltpu.CompilerParams(
            dimension_semantics=("arbitrary", "arbitrary"), vmem_limit_bytes=VMEM_LIMIT),
        name="nsa_attention",
    )(bound, qn, kcmp, vcmp, ks, vs, kw, vw, misc, zn, ovt, expand)


N_SLABS = 32
_TRANSPOSE_STAGES = ((16, 0x0000FFFF), (8, 0x00FF00FF), (4, 0x0F0F0F0F), (2, 0x33333333), (1, 0x55555555))


def _ordered_bits(v):
    return v ^ (jnp.right_shift(v, 31) & jnp.int32(0x7FFFFFFF))


def _bit_planes(score_ref, plane_ref):
    for stage, (j, mask) in enumerate(_TRANSPOSE_STAGES):
        if stage == 0:
            load = lambda k: _ordered_bits(pltpu.bitcast(score_ref[k], jnp.int32))
        else:
            load = lambda k: plane_ref[k]

        def body(p, carry, j=j, mask=mask, load=load):
            k = jnp.left_shift(p & ~(j - 1), 1) | (p & (j - 1))
            lo = load(k)
            hi = load(k + j)
            t = (lo ^ lax.shift_right_logical(hi, j)) & mask
            plane_ref[k] = lo ^ t
            plane_ref[k + j] = hi ^ jnp.left_shift(t, j)
            return carry

        lax.fori_loop(0, N_SLABS // 2, body, 0)


def _dsa_kernel(bound_ref, qd_ref, qi_ref, misc_ref, ki_ref, kd_ref, vd_ref, zd_ref, tri_ref,
                o_ref, score_scr, plane_scr, bias_scr, shift0_scr, shift1_scr, *, top_k, tq):
    t0 = pl.program_id(1) * tq
    lane = lax.broadcasted_iota(jnp.int32, (1, LANES), 1)
    lo64 = lane < 64
    t_col = t0 + lax.broadcasted_iota(jnp.int32, (tq, 1), 0)
    t_row = t0 + lax.broadcasted_iota(jnp.int32, (1, tq), 1)
    n_chunks = (t0 + tq + KC - 1) // KC
    slabs_per_chunk = KC // LANES
    int_min = jnp.int32(-2 ** 31)

    def chunk_slabs(c):
        return pl.ds(c * slabs_per_chunk, slabs_per_chunk)

    def to_col(row):
        return jnp.broadcast_to(row, (SUBLANES, tq)).T[:, 0:1]

    qi = qi_ref[0]
    qi_heads = [_half_masked(qi[:, (h // 2) * LANES:(h // 2 + 1) * LANES], h, lo64) for h in range(IDX_HEADS)]
    misc_t = misc_ref[0].T
    wi = [misc_t[MISC_WI + h:MISC_WI + h + 1, :] for h in range(IDX_HEADS)]

    def score_body(c, carry):
        k0 = pl.multiple_of(c * KC, KC)
        kib = ki_ref[0, pl.ds(k0, KC), :]
        sc = jnp.zeros((KC, tq), f32)
        for h in range(IDX_HEADS):
            sc = sc + jnp.maximum(_nt_dot(kib, qi_heads[h]), 0.0) * wi[h]
        sc = jnp.where(sc == 0.0, 0.0, sc)
        kpos = k0 + lax.broadcasted_iota(jnp.int32, (KC, 1), 0)
        sc = jnp.where(kpos <= t_row, sc, NON_CAUSAL_MARK)
        score_scr[chunk_slabs(c)] = sc.reshape(slabs_per_chunk, LANES, tq)
        return carry

    lax.fori_loop(0, n_chunks, score_body, 0)

    def fill_body(c, carry):
        score_scr[chunk_slabs(c)] = jnp.full((slabs_per_chunk, LANES, tq), NON_CAUSAL_MARK, f32)
        return carry

    lax.fori_loop(n_chunks, N_SLABS // slabs_per_chunk, fill_body, 0)

    _bit_planes(score_scr, plane_scr)
    kf = float(top_k)

    def select_bit(i, carry):
        alive, above, thr = carry
        plane = plane_scr[i] ^ jnp.where(i == 0, jnp.int32(-1), jnp.int32(0))
        ones = alive & plane
        c1 = jnp.sum(lax.population_count(ones).astype(f32), axis=0, keepdims=True)
        take = (above + c1) >= kf
        alive = jnp.where(take, ones, alive ^ ones)
        above = jnp.where(take, above, above + c1)
        thr = jnp.where(take, thr | jnp.left_shift(jnp.int32(1), 31 - i), thr)
        return alive, above, thr

    init = (jnp.full((LANES, tq), -1, jnp.int32), jnp.zeros((1, tq), f32), jnp.zeros((1, tq), jnp.int32))
    _, above, thr_u = lax.fori_loop(0, 32, select_bit, init)

    def key_to_score(key_row):
        return pltpu.bitcast(_ordered_bits(key_row), f32)

    def write_bias(thr_row, need_row):
        thr = to_col(thr_row)
        need = to_col(need_row)

        def body(c, carry):
            ties_before, n_above = carry
            k0 = pl.multiple_of(c * KC, KC)
            blk = score_scr[chunk_slabs(c)].reshape(KC, tq).T
            gt = jnp.where(blk > thr, 1.0, 0.0)
            eq = jnp.where(blk == thr, 1.0, 0.0)
            prefix = ties_before + _dot(eq.astype(bf16), tri_ref[...])
            chosen = gt + jnp.where(prefix <= need, eq, 0.0)
            kpos = k0 + lax.broadcasted_iota(jnp.int32, (1, KC), 1)
            bias_scr[:, pl.ds(k0, KC)] = jnp.where(kpos <= t_col, jnp.where(chosen > 0.5, 0.0, NEG), NEG)
            return prefix[:, KC - 1:KC], n_above + jnp.sum(gt, axis=-1, keepdims=True)

        zero = jnp.zeros((tq, 1), f32)
        return lax.fori_loop(0, n_chunks, body, (zero, zero))

    ties, n_above = write_bias(key_to_score(thr_u ^ int_min), kf - above)
    verified = (n_above == to_col(above)) & (n_above < kf) & (n_above + ties >= kf)

    @pl.when(jnp.sum(jnp.where(verified, 0.0, 1.0)) > 0.0)
    def _():
        def count(pred):
            def body(c, acc):
                blk = score_scr[chunk_slabs(c)].reshape(KC, tq)
                return acc + jnp.sum(jnp.where(pred(blk), 1.0, 0.0), axis=0, keepdims=True)
            return lax.fori_loop(0, N_SLABS // slabs_per_chunk, body, jnp.zeros((1, tq), f32))

        key = jnp.where(count(lambda b: b >= 0.0) >= kf, jnp.int32(0), int_min)

        def bit(i, key):
            cand = key + jnp.left_shift(jnp.int32(1), 30 - i)
            cand_f = key_to_score(cand)
            return jnp.where(count(lambda b: b >= cand_f) >= kf, cand, key)

        thr_f = key_to_score(lax.fori_loop(0, 31, bit, key))
        write_bias(thr_f, kf - count(lambda b: b > thr_f))

    def bias_fn(k0):
        return bias_scr[:, pl.ds(k0, KC)]

    qd = qd_ref[0]
    q4s = [_stack_group_queries(qd, g, lo64) for g in range(KV_GROUPS)]
    outs = _masked_flash(q4s, kd_ref, vd_ref, n_chunks, [bias_fn] * KV_GROUPS, tq,
                         bound_ref, (shift0_scr, shift1_scr))
    o = jnp.concatenate([_group_output(og, lo64, tq) for og in outs], axis=1) * zd_ref[0].astype(f32)
    o_ref[0] = o.astype(bf16)


def _dsa_attention(bound, qd, qi, misc, ki, kd, vd, zd):
    B, S, _ = qd.shape
    tq = TQ_DSA
    assert S <= N_SLABS * LANES
    tri = jnp.asarray(np.triu(np.ones((KC, KC), np.float32)), bf16)
    row = lambda c: pl.BlockSpec((1, tq, c), lambda b, i: (b, i, 0))
    whole = pl.BlockSpec((1, KV_GROUPS, S, LANES), lambda b, i: (b, 0, 0, 0))
    return pl.pallas_call(
        functools.partial(_dsa_kernel, top_k=min(DSA_TOPK_MAX, S // 4), tq=tq),
        grid=(B, S // tq),
        in_specs=[_SMEM_SCALAR, row(512), row(256), row(LANES), pl.BlockSpec((1, S, LANES), lambda b, i: (b, 0, 0)),
                  whole, whole, row(512), pl.BlockSpec((KC, KC), lambda b, i: (0, 0))],
        out_specs=row(512),
        out_shape=jax.ShapeDtypeStruct((B, S, 512), bf16),
        scratch_shapes=[pltpu.VMEM((N_SLABS, LANES, tq), f32), pltpu.VMEM((N_SLABS, LANES, tq), jnp.int32),
                        pltpu.VMEM((tq, S), f32)] + _shift_scratch(tq),
        compiler_params=pltpu.CompilerParams(
            dimension_semantics=("arbitrary", "arbitrary"), vmem_limit_bytes=VMEM_LIMIT),
        name="dsa_attention",
    )(bound, qd, qi, misc, ki, kd, vd, zd, tri)


def _outproj_kernel(x_ref, on_ref, od_ref, w_ref, o_ref):
    half = w_ref.shape[0] // 2
    o_ref[0] = x_ref[0] + _dot(on_ref[0], w_ref[0:half, :]) + _dot(od_ref[0], w_ref[half:, :])


def _out_projection(x, o_nsa, o_dsa, w_out):
    B, S, D = x.shape
    tm = min(TM_PROJ, S)
    row = lambda c: pl.BlockSpec((1, tm, c), lambda b, i: (b, i, 0))
    return pl.pallas_call(
        _outproj_kernel,
        grid=(B, S // tm),
        in_specs=[row(D), row(512), row(512), pl.BlockSpec(w_out.shape, lambda b, i: (0, 0))],
        out_specs=row(D),
        out_shape=jax.ShapeDtypeStruct((B, S, D), f32),
        compiler_params=pltpu.CompilerParams(
            dimension_semantics=("arbitrary", "arbitrary"), vmem_limit_bytes=VMEM_LIMIT),
        name="out_projection",
    )(x, o_nsa, o_dsa, w_out)


def _rope_tables(pos):
    half = HEAD_DIM // 2
    inv_freq = ROPE_THETA ** (-jnp.arange(half, dtype=f32) / half)
    ang = pos[:, None] * inv_freq[None, :]
    cos, sin = jnp.cos(ang), jnp.sin(ang)
    cos_t = jnp.tile(cos, (1, LANES // half))
    sin_t = jnp.tile(jnp.concatenate([-sin, sin], axis=1), (1, LANES // HEAD_DIM))
    return cos_t, sin_t


def _overlap_t(seq, n_cmp_pad):
    n_cmp = (seq - CMP_BLOCK) // CMP_STRIDE + 1
    n_sel = seq // SEL_BLOCK
    c_start = np.arange(n_cmp) * CMP_STRIDE
    j_start = np.arange(n_sel) * SEL_BLOCK
    ov = np.clip(np.minimum(c_start[:, None] + CMP_BLOCK, j_start[None, :] + SEL_BLOCK)
                 - np.maximum(c_start[:, None], j_start[None, :]), 0, None).astype(np.float32) / CMP_BLOCK
    out = np.zeros((n_sel, n_cmp_pad), np.float32)
    out[:, :n_cmp] = ov.T
    return out


def _block_diag2(w):
    z = jnp.zeros_like(w)
    return jnp.concatenate([jnp.concatenate([w, z], axis=-1), jnp.concatenate([z, w], axis=-1)], axis=-2)


def _layer(x, norm_gain, w_in, nsa_q_gain, nsa_kc_gain, nsa_ks_gain, nsa_kw_gain,
           cmp_pe_k, cmp_k_w1, cmp_k_b1, cmp_k_w2, cmp_pe_v, cmp_v_w1, cmp_v_b1, cmp_v_w2,
           dsa_q_gain, dsa_k_gain, w_out):
    B, S, _ = x.shape
    assert S % KC == 0 and S >= WINDOW + TQ_NSA
    n_cmp_pad = S // CMP_STRIDE

    w_perm = _permuted_weight(w_in)
    head_of_lane = np.arange(2 * LANES) // HEAD_DIM
    bd = jnp.asarray(head_of_lane[:, None] == head_of_lane[None, :], bf16)
    dup = lambda v: jnp.tile(v.reshape(1, -1), (1, 2))
    quad = lambda v: jnp.tile(v.reshape(1, -1), (1, 4))
    gains = jnp.concatenate([quad(nsa_q_gain), jnp.concatenate([dup(nsa_ks_gain), dup(nsa_kw_gain)], axis=1),
                             quad(dsa_q_gain), quad(dsa_k_gain), jnp.ones((4, 2 * LANES), f32)], axis=0)
    cos_t, sin_t = _rope_tables(jnp.arange(S, dtype=f32))

    (qn, qd, qi, ki, ks, kw, kd, vs, vw, vd, kc, vc, zn, zd, misc) = _in_projection(
        x, norm_gain.reshape(1, -1), w_perm, bd, cos_t, sin_t, gains)

    cmp_pos = (jnp.arange(n_cmp_pad) * CMP_STRIDE + CMP_BLOCK - 1).astype(f32)
    cos_c, sin_c = _rope_tables(cmp_pos)

    def cmp_weights(pe, w1, b1, w2):
        w1_bd = _block_diag2(w1.reshape(CMP_BLOCK, HEAD_DIM, CMP_HIDDEN)).astype(bf16)
        return jnp.tile(pe, (1, 2)), w1_bd, dup(b1), _block_diag2(w2).astype(bf16)

    kcmp = _compress(kc, *cmp_weights(cmp_pe_k, cmp_k_w1, cmp_k_b1, cmp_k_w2), bd[:LANES, :LANES], dup(nsa_kc_gain),
                     cos_c, sin_c, is_key=True)
    vcmp = _compress(vc, *cmp_weights(cmp_pe_v, cmp_v_w1, cmp_v_b1, cmp_v_w2), bd[:LANES, :LANES], dup(nsa_kc_gain),
                     cos_c, sin_c, is_key=False)

    ovt = jnp.asarray(_overlap_t(S, n_cmp_pad), bf16)
    o_nsa = _nsa_attention(_score_bound(nsa_q_gain, nsa_ks_gain), qn, kcmp, vcmp, ks, vs, kw, vw, misc, zn, ovt)

    o_dsa = _dsa_attention(_score_bound(dsa_q_gain, dsa_k_gain), qd, qi, misc, ki, kd, vd, zd)

    return _out_projection(x, o_nsa, o_dsa, w_out.astype(bf16))


def kernel(x, norm_gain, w_in, nsa_q_gain, nsa_kc_gain, nsa_ks_gain, nsa_kw_gain, cmp_pe_k, cmp_k_w1,
           cmp_k_b1, cmp_k_w2, cmp_pe_v, cmp_v_w1, cmp_v_b1, cmp_v_w2, dsa_q_gain, dsa_k_gain, w_out):
    for l in range(norm_gain.shape[0]):
        x = _layer(x, norm_gain[l], w_in[l], nsa_q_gain[l], nsa_kc_gain[l], nsa_ks_gain[l], nsa_kw_gain[l],
                   cmp_pe_k[l], cmp_k_w1[l], cmp_k_b1[l], cmp_k_w2[l], cmp_pe_v[l], cmp_v_w1[l], cmp_v_b1[l],
                   cmp_v_w2[l], dsa_q_gain[l], dsa_k_gain[l], w_out[l])
    return x
```

```python
import functools

import numpy as np
import jax
import jax.numpy as jnp
from jax import lax
from jax.experimental import pallas as pl
from jax.experimental.pallas import tpu as pltpu

D_MODEL = 1024
HEAD_DIM = 64
NSA_HEADS = 8
DSA_HEADS = 8
KV_GROUPS = 2
HEADS_PER_GROUP = 4
IDX_HEADS = 4
CMP_BLOCK = 32
CMP_STRIDE = 16
CMP_HIDDEN = 256
SEL_BLOCK = 64
SEL_TOPN = 16
WINDOW = 512
DSA_TOPK_MAX = 256
ROPE_THETA = 10000.0
EPS = 1e-6
NEG = -1e30
FORCE = 1e6
ATTN_SCALE = HEAD_DIM ** -0.5
IDX_SCALE = HEAD_DIM ** -0.5
WI_SCALE = IDX_HEADS ** -0.5
LOG2E = 1.4426950408889634
Q_SCALE = ATTN_SCALE * LOG2E

LANES = 128
SUBLANES = 8
M_FLOOR = -5e29
NON_CAUSAL_MARK = -3.0e38
MAX_FIXED_SHIFT = 30.0
NORM_SLACK = 1.03
VMEM_LIMIT = 56 * 1024 * 1024

IN_WIDTHS = (512, 128, 128, 128, 128, 128, 128, 24, 512, 512, 128, 128, 256, 64, 4, 512)
IN_NAMES = ("q_n", "kc", "vc", "ks", "vs", "kw", "vw", "gate", "z_n",
            "q_d", "k_d", "v_d", "qi", "ki", "wi", "z_d")
IN_COLS = sum(IN_WIDTHS)

CH_QN, CH_KS, CH_KW, CH_QD, CH_KD, CH_QI, CH_KI = 0, 4, 5, 6, 10, 11, 13
CH_KC, CH_VC, CH_VS, CH_VW, CH_VD, CH_ZN, CH_ZD, CH_MISC = 14, 15, 16, 17, 18, 19, 23, 27
N_CHUNKS = 28
MISC_WI = 24

TM_PROJ = 512
TQ_NSA = 256
TQ_DSA = 256
KC = 512

f32 = jnp.float32
bf16 = jnp.bfloat16


def _permuted_weight(w_in):
    off = dict(zip(IN_NAMES, np.cumsum((0,) + IN_WIDTHS[:-1])))
    width = dict(zip(IN_NAMES, IN_WIDTHS))
    order = ("q_n", "ks", "kw", "q_d", "k_d", "qi", "ki", "ki", "kc", "vc", "vs", "vw", "v_d", "z_n", "z_d",
             "gate", "wi")
    w16 = w_in.astype(bf16)
    parts = [w16[:, off[n]:off[n] + width[n]] for n in order]
    used = sum(width[n] for n in order)
    parts.append(jnp.zeros((D_MODEL, N_CHUNKS * LANES - used), bf16))
    return jnp.concatenate(parts, axis=1)


def _nt_dot(a, b):
    return lax.dot_general(a, b, (((1,), (1,)), ((), ())), preferred_element_type=f32)


def _dot(a, b):
    return jnp.dot(a, b, preferred_element_type=f32)


def _split_bf16(v):
    hi = v.astype(bf16)
    lo = (v - hi.astype(f32)).astype(bf16)
    return hi, lo


def _head_rms_norm(y, bd, gain_row):
    ssq = _dot((y * y).astype(bf16), bd)
    return y * lax.rsqrt(ssq * (1.0 / HEAD_DIM) + EPS) * gain_row


def _rope(y, cos, sin_signed, lo32):
    partner = jnp.where(lo32, pltpu.roll(y, LANES - 32, 1), pltpu.roll(y, 32, 1))
    return y * cos + partner * sin_signed


def _dup_halves(y, lo64):
    r = pltpu.roll(y, 64, 1)
    return jnp.where(lo64, y, r), jnp.where(lo64, r, y)


def _value_with_ones(y, lo64):
    return jnp.where(lo64, y, 1.0), jnp.where(lo64, pltpu.roll(y, 64, 1), 1.0)


def _inproj_kernel(x_ref, ng_ref, w_ref, bd_ref, cos_ref, sin_ref, gains_ref,
                   qn_ref, qd_ref, qi_ref, ki_ref, ks_ref, kw_ref, kd_ref,
                   vs_ref, vw_ref, vd_ref, kc_ref, vc_ref, zn_ref, zd_ref, misc_ref):
    x = x_ref[0]
    ms = jnp.mean(x * x, axis=-1, keepdims=True)
    h = (x * lax.rsqrt(ms + EPS) * ng_ref[...]).astype(bf16)
    cos = cos_ref[...]
    sin = sin_ref[...]
    bd = bd_ref[...]
    lane = lax.broadcasted_iota(jnp.int32, (1, LANES), 1)
    lo32 = (lane % 64) < 32
    lo64 = lane < 64

    def proj(c0, n):
        return _dot(h, w_ref[:, c0 * LANES:(c0 + n) * LANES])

    def chunk(y, i):
        return y[:, i * LANES:(i + 1) * LANES]

    def normed_rope(y, gain_idx):
        w = y.shape[1]
        yn = _head_rms_norm(y, bd[:w, :w], gains_ref[gain_idx:gain_idx + 1, 0:w])
        return [_rope(chunk(yn, i), cos, sin, lo32) for i in range(w // LANES)]

    for c0, gain_idx, ref in ((CH_QN, 0, qn_ref), (CH_QD, 2, qd_ref)):
        y = proj(c0, 4)
        for i in range(2):
            for j, r in enumerate(normed_rope(y[:, 2 * i * LANES:(2 * i + 2) * LANES], gain_idx)):
                ref[0, :, (2 * i + j) * LANES:(2 * i + j + 1) * LANES] = (r * Q_SCALE).astype(bf16)
    y = proj(CH_QI, 2)
    for i in range(2):
        qi_ref[0, :, i * LANES:(i + 1) * LANES] = (_rope(chunk(y, i), cos, sin, lo32) * IDX_SCALE).astype(bf16)
    ki_ref[0] = _rope(proj(CH_KI, 1), cos, sin, lo32).astype(bf16)

    k_pairs = normed_rope(proj(CH_KS, 2), 1) + normed_rope(proj(CH_KD, 1), 3)
    for r, ref in zip(k_pairs, (ks_ref, kw_ref, kd_ref)):
        a, b = _dup_halves(r, lo64)
        ref[0, 0] = a.astype(bf16)
        ref[0, 1] = b.astype(bf16)
    for c0, ref in ((CH_VS, vs_ref), (CH_VW, vw_ref), (CH_VD, vd_ref)):
        a, b = _value_with_ones(proj(c0, 1), lo64)
        ref[0, 0] = a.astype(bf16)
        ref[0, 1] = b.astype(bf16)

    kc_ref[0] = proj(CH_KC, 1)
    vc_ref[0] = proj(CH_VC, 1)
    z = proj(CH_ZN, 4)
    zn_ref[0] = (z * jax.nn.sigmoid(z)).astype(bf16)
    z = proj(CH_ZD, 4)
    zd_ref[0] = (z * jax.nn.sigmoid(z)).astype(bf16)
    m = proj(CH_MISC, 1)
    misc_ref[0] = jnp.where(lane < MISC_WI, jax.nn.sigmoid(m), m * WI_SCALE)


def _in_projection(x, norm_gain, w_perm, bd, cos_t, sin_t, gains):
    B, S, _ = x.shape
    tm = min(TM_PROJ, S)
    grid = (B, S // tm)
    row = lambda c: pl.BlockSpec((1, tm, c), lambda b, i: (b, i, 0))
    dup = pl.BlockSpec((1, KV_GROUPS, tm, LANES), lambda b, i: (b, 0, i, 0))
    const = lambda shape: pl.BlockSpec(shape, lambda b, i: tuple(0 for _ in shape))
    tab = pl.BlockSpec((tm, LANES), lambda b, i: (i, 0))
    sds = jax.ShapeDtypeStruct
    out_shape = (
        sds((B, S, 512), bf16), sds((B, S, 512), bf16), sds((B, S, 256), bf16), sds((B, S, LANES), bf16),
        sds((B, KV_GROUPS, S, LANES), bf16), sds((B, KV_GROUPS, S, LANES), bf16),
        sds((B, KV_GROUPS, S, LANES), bf16), sds((B, KV_GROUPS, S, LANES), bf16),
        sds((B, KV_GROUPS, S, LANES), bf16), sds((B, KV_GROUPS, S, LANES), bf16),
        sds((B, S, LANES), f32), sds((B, S, LANES), f32),
        sds((B, S, 512), bf16), sds((B, S, 512), bf16), sds((B, S, LANES), f32),
    )
    out_specs = (row(512), row(512), row(256), row(LANES), dup, dup, dup, dup, dup, dup,
                 row(LANES), row(LANES), row(512), row(512), row(LANES))
    return pl.pallas_call(
        _inproj_kernel,
        grid=grid,
        in_specs=[row(D_MODEL), const((1, D_MODEL)), const((D_MODEL, N_CHUNKS * LANES)),
                  const((2 * LANES, 2 * LANES)), tab, tab, const((8, 2 * LANES))],
        out_specs=out_specs,
        out_shape=out_shape,
        compiler_params=pltpu.CompilerParams(
            dimension_semantics=("arbitrary", "arbitrary"), vmem_limit_bytes=VMEM_LIMIT),
        name="in_projection",
    )(x, norm_gain, w_perm, bd, cos_t, sin_t, gains)


def _compress_kernel(src_ref, pe_ref, w1_ref, b1_ref, w2_ref, bd_ref, gain_ref, cos_ref, sin_ref,
                     out_ref, *, n_rows, is_key):
    half = CMP_BLOCK // 2
    acc_a = jnp.zeros((n_rows, 2 * CMP_HIDDEN), f32)
    acc_b = jnp.zeros((n_rows, 2 * CMP_HIDDEN), f32)
    for l in range(half):
        rows = src_ref[0, pl.ds(l, n_rows, stride=CMP_STRIDE), :]
        acc_a = acc_a + _dot((rows + pe_ref[l:l + 1, :]).astype(bf16), w1_ref[l])
        acc_b = acc_b + _dot((rows + pe_ref[l + half:l + half + 1, :]).astype(bf16), w1_ref[l + half])
    pre = acc_a + pltpu.roll(acc_b, n_rows - 1, 0) + b1_ref[...]
    hid = pre * jax.nn.sigmoid(pre)
    out = _dot(hid.astype(bf16), w2_ref[...])
    lane = lax.broadcasted_iota(jnp.int32, (1, LANES), 1)
    if is_key:
        out = _head_rms_norm(out, bd_ref[...], gain_ref[...])
        out = _rope(out, cos_ref[...], sin_ref[...], (lane % 64) < 32)
        a, b = _dup_halves(out, lane < 64)
    else:
        a, b = _value_with_ones(out, lane < 64)
    out_ref[0, 0] = a.astype(bf16)
    out_ref[0, 1] = b.astype(bf16)


def _compress(src, pe_dup, w1_bd, b1_dup, w2_bd, bd, gain_dup, cos_c, sin_c, *, is_key):
    B, S, _ = src.shape
    n_rows = S // CMP_STRIDE
    const = lambda shape: pl.BlockSpec(shape, lambda b: tuple(0 for _ in shape))
    return pl.pallas_call(
        functools.partial(_compress_kernel, n_rows=n_rows, is_key=is_key),
        grid=(B,),
        in_specs=[pl.BlockSpec((1, S, LANES), lambda b: (b, 0, 0)),
                  const((CMP_BLOCK, LANES)), const((CMP_BLOCK, LANES, 2 * CMP_HIDDEN)),
                  const((1, 2 * CMP_HIDDEN)), const((2 * CMP_HIDDEN, LANES)), const((LANES, LANES)),
                  const((1, LANES)), const((n_rows, LANES)), const((n_rows, LANES))],
        out_specs=pl.BlockSpec((1, KV_GROUPS, n_rows, LANES), lambda b: (b, 0, 0, 0)),
        out_shape=jax.ShapeDtypeStruct((B, KV_GROUPS, n_rows, LANES), bf16),
        compiler_params=pltpu.CompilerParams(
            dimension_semantics=("arbitrary",), vmem_limit_bytes=VMEM_LIMIT),
        name="compress_k" if is_key else "compress_v",
    )(src, pe_dup, w1_bd, b1_dup, w2_bd, bd, gain_dup, cos_c, sin_c)


def _half_masked(pair, h, lo64):
    keep = lo64 if h % 2 == 0 else jnp.logical_not(lo64)
    return jnp.where(keep, pair, jnp.zeros_like(pair))


def _stack_group_queries(q, g, lo64):
    slabs = []
    for r in range(HEADS_PER_GROUP):
        h = g * HEADS_PER_GROUP + r
        slabs.append(_half_masked(q[:, (h // 2) * LANES:(h // 2 + 1) * LANES], h, lo64))
    return jnp.concatenate(slabs, axis=0)


def _masked_flash(q4s, k_ref, v_ref, n_chunks, bias_fns, tq, bound_ref, shift_scrs):
    rows = HEADS_PER_GROUP * tq
    bound = bound_ref[0]

    def scores(g, k0):
        s = _nt_dot(q4s[g], k_ref[0, g, pl.ds(k0, KC), :])
        return (s.reshape(HEADS_PER_GROUP, tq, KC) + bias_fns[g](k0)[None]).reshape(rows, KC)

    for g in range(KV_GROUPS):
        shift_scrs[g][...] = jnp.full((rows, 1), bound, f32)

    @pl.when(bound > MAX_FIXED_SHIFT)
    def _():
        def body(c, ms):
            k0 = pl.multiple_of(c * KC, KC)
            return tuple(jnp.maximum(ms[g], jnp.max(scores(g, k0), axis=-1, keepdims=True))
                         for g in range(KV_GROUPS))

        init = tuple(jnp.full((rows, 1), M_FLOOR, f32) for _ in range(KV_GROUPS))
        for g, m in enumerate(lax.fori_loop(0, n_chunks, body, init)):
            shift_scrs[g][...] = m

    def body(c, accs):
        k0 = pl.multiple_of(c * KC, KC)
        new = []
        for g in range(KV_GROUPS):
            p = jnp.exp2(scores(g, k0) - shift_scrs[g][...]).astype(bf16)
            new.append(accs[g] + _dot(p, v_ref[0, g, pl.ds(k0, KC), :]))
        return tuple(new)

    init = tuple(jnp.zeros((rows, LANES), f32) for _ in range(KV_GROUPS))
    out = lax.fori_loop(0, n_chunks, body, init)
    return [acc / acc[:, HEAD_DIM:HEAD_DIM + 1] for acc in out]


def _group_output(og, lo64, tq):
    pairs = []
    for p in range(2):
        even = og[(2 * p) * tq:(2 * p + 1) * tq]
        odd = og[(2 * p + 1) * tq:(2 * p + 2) * tq]
        pairs.append(jnp.where(lo64, even, pltpu.roll(odd, HEAD_DIM, 1)))
    return jnp.concatenate(pairs, axis=1)


def _top_n_blocks(imp, n_sel, top_n):
    groups = [imp[SUBLANES * a:SUBLANES * (a + 1)] for a in range(n_sel // SUBLANES)]
    jsub = lax.broadcasted_iota(jnp.int32, groups[0].shape, 0)
    ranks = [jnp.zeros(groups[0].shape, f32) for _ in groups]
    for i in range(n_sel):
        row = imp[i:i + 1, :]
        for a, grp in enumerate(groups):
            if SUBLANES * a > i:
                beats = row >= grp
            elif SUBLANES * (a + 1) - 1 < i:
                beats = row > grp
            else:
                tie = jnp.where(jsub + SUBLANES * a > i, 1.0, 0.0)
                ranks[a] = ranks[a] + jnp.where(row > grp, 1.0, jnp.where(row == grp, tie, 0.0))
                continue
            ranks[a] = ranks[a] + jnp.where(beats, 1.0, 0.0)
    rank = jnp.concatenate(ranks, axis=0)
    return jnp.where(rank < top_n, 1.0, 0.0)


def _nsa_kernel(bound_ref, qn_ref, kcmp_ref, vcmp_ref, ks_ref, vs_ref, kw_ref, vw_ref, misc_ref, zn_ref, ovt_ref,
                expand_ref, o_ref, shift0_scr, shift1_scr, bias_scr, *, n_cmp_pad, n_sel, top_n):
    tq = TQ_NSA
    t0 = pl.program_id(1) * tq
    lane = lax.broadcasted_iota(jnp.int32, (1, LANES), 1)
    lo64 = lane < 64
    q = qn_ref[0]
    misc = misc_ref[0]
    t_col = t0 + lax.broadcasted_iota(jnp.int32, (tq, 1), 0)
    n_chunks = (t0 + tq + KC - 1) // KC
    win_span = WINDOW + tq
    win_start = pl.multiple_of(jnp.maximum(t0 - WINDOW, 0), tq)
    ovt = ovt_ref[...]
    rows = HEADS_PER_GROUP * tq

    def add_bias(s, bias):
        return (s.reshape(HEADS_PER_GROUP, tq, s.shape[-1]) + bias[None]).reshape(rows, s.shape[-1])

    cmp_end = lax.broadcasted_iota(jnp.int32, (1, n_cmp_pad), 1) * CMP_STRIDE + (CMP_BLOCK - 1)
    cmp_bias = jnp.where(cmp_end <= t_col, 0.0, NEG)
    win_diff = t_col - (win_start + lax.broadcasted_iota(jnp.int32, (1, win_span), 1))
    win_bias = jnp.where((win_diff >= 0) & (win_diff < WINDOW), 0.0, NEG)

    q4s, o_cmp, o_win, sels = [], [], [], []
    for g in range(KV_GROUPS):
        q4 = _stack_group_queries(q, g, lo64)
        q4s.append(q4)

        s = add_bias(_nt_dot(q4, kcmp_ref[0, g]), cmp_bias)
        m = jnp.maximum(jnp.max(s, axis=-1, keepdims=True), M_FLOOR)
        e = jnp.exp2(s - m)
        l = jnp.sum(e, axis=-1, keepdims=True)
        p = e * (1.0 / jnp.maximum(l, 1e-30))
        o_cmp.append(_dot(p.astype(bf16), vcmp_ref[0, g]))

        psum = p[0:tq] + p[tq:2 * tq] + p[2 * tq:3 * tq] + p[3 * tq:4 * tq]
        hi, lo = _split_bf16(psum)
        imp = _nt_dot(ovt, hi) + _nt_dot(ovt, lo)
        jrow = lax.broadcasted_iota(jnp.int32, (n_sel, tq), 0)
        tl = t0 + lax.broadcasted_iota(jnp.int32, (n_sel, tq), 1)
        cur = lax.shift_right_logical(tl, 6)
        forced = (jrow == 0) | (jrow == cur) | (jrow == cur - 1)
        imp = jnp.where(forced, FORCE, imp)
        imp = jnp.where(jrow * SEL_BLOCK <= tl, imp, NEG)
        pad_row = lax.broadcasted_iota(jnp.int32, (LANES - n_sel, tq), 0)
        chosen = jnp.concatenate([_top_n_blocks(imp, n_sel, top_n) * -NEG, jnp.where(pad_row == 0, NEG, 0.0)], axis=0)
        sels.append(chosen.T.astype(bf16))

        s = add_bias(_nt_dot(q4, kw_ref[0, g, pl.ds(win_start, win_span), :]), win_bias)
        m = jnp.max(s, axis=-1, keepdims=True)
        acc = _dot(jnp.exp2(s - m).astype(bf16), vw_ref[0, g, pl.ds(win_start, win_span), :])
        o_win.append(acc / acc[:, HEAD_DIM:HEAD_DIM + 1])

    def bias_body(c, carry):
        k0 = pl.multiple_of(c * KC, KC)
        for g in range(KV_GROUPS):
            bias_scr[g, :, pl.ds(k0, KC)] = _dot(sels[g], expand_ref[:, pl.ds(k0, KC)])
        return carry

    lax.fori_loop(0, n_chunks, bias_body, 0)
    k_last = pl.multiple_of((n_chunks - 1) * KC, KC)
    causal = jnp.where(k_last + lax.broadcasted_iota(jnp.int32, (1, KC), 1) <= t_col, 0.0, NEG)
    for g in range(KV_GROUPS):
        bias_scr[g, :, pl.ds(k_last, KC)] = bias_scr[g, :, pl.ds(k_last, KC)] + causal

    o_sel = _masked_flash(q4s, ks_ref, vs_ref, n_chunks,
                          [lambda k0, g=g: bias_scr[g, :, pl.ds(k0, KC)] for g in range(KV_GROUPS)], tq,
                          bound_ref, (shift0_scr, shift1_scr))

    group_outs = []
    for g in range(KV_GROUPS):
        def gate(branch):
            cols = [misc[:, (g * HEADS_PER_GROUP + r) * 3 + branch:(g * HEADS_PER_GROUP + r) * 3 + branch + 1]
                    for r in range(HEADS_PER_GROUP)]
            return jnp.concatenate(cols, axis=0)

        og = gate(0) * o_cmp[g] + gate(1) * o_sel[g] + gate(2) * o_win[g]
        group_outs.append(_group_output(og, lo64, tq))

    o = jnp.concatenate(group_outs, axis=1) * zn_ref[0].astype(f32)
    o_ref[0] = o.astype(bf16)


def _score_bound(q_gain, k_gain):
    return (Q_SCALE * HEAD_DIM * NORM_SLACK * jnp.max(jnp.abs(q_gain)) * jnp.max(jnp.abs(k_gain))).reshape(1)


_SMEM_SCALAR = pl.BlockSpec(memory_space=pltpu.SMEM)


def _shift_scratch(tq):
    return [pltpu.VMEM((HEADS_PER_GROUP * tq, 1), f32)] * KV_GROUPS


def _nsa_attention(bound, qn, kcmp, vcmp, ks, vs, kw, vw, misc, zn, ovt):
    B, S, _ = qn.shape
    tq = TQ_NSA
    n_cmp_pad = kcmp.shape[2]
    n_sel = S // SEL_BLOCK
    row = lambda c: pl.BlockSpec((1, tq, c), lambda b, i: (b, i, 0))
    whole = lambda n: pl.BlockSpec((1, KV_GROUPS, n, LANES), lambda b, i: (b, 0, 0, 0))
    kern = functools.partial(_nsa_kernel, n_cmp_pad=n_cmp_pad, n_sel=n_sel, top_n=min(SEL_TOPN, n_sel))
    assert n_sel < LANES
    expand = np.zeros((LANES, S), np.float32)
    expand[np.arange(S) // SEL_BLOCK, np.arange(S)] = 1.0
    expand[n_sel] = 1.0
    expand = jnp.asarray(expand, bf16)
    return pl.pallas_call(
        kern,
        grid=(B, S // tq),
        in_specs=[_SMEM_SCALAR, row(512), whole(n_cmp_pad), whole(n_cmp_pad), whole(S), whole(S), whole(S),
                  whole(S), row(LANES), row(512), pl.BlockSpec((n_sel, n_cmp_pad), lambda b, i: (0, 0)),
                  pl.BlockSpec((LANES, S), lambda b, i: (0, 0))],
        out_specs=row(512),
        out_shape=jax.ShapeDtypeStruct((B, S, 512), bf16),
        scratch_shapes=_shift_scratch(tq) + [pltpu.VMEM((KV_GROUPS, tq, S), f32)],
        compiler_params=pltpu.CompilerParams(
            dimension_semantics=("arbitrary", "arbitrary"), vmem_limit_bytes=VMEM_LIMIT),
        name="nsa_attention",
    )(bound, qn, kcmp, vcmp, ks, vs, kw, vw, misc, zn, ovt, expand)


N_SLABS = 32
_TRANSPOSE_STAGES = ((16, 0x0000FFFF), (8, 0x00FF00FF), (4, 0x0F0F0F0F), (2, 0x33333333), (1, 0x55555555))


def _ordered_bits(v):
    return v ^ (jnp.right_shift(v, 31) & jnp.int32(0x7FFFFFFF))


def _bit_planes(score_ref, plane_ref):
    for stage, (j, mask) in enumerate(_TRANSPOSE_STAGES):
        if stage == 0:
            load = lambda k: _ordered_bits(pltpu.bitcast(score_ref[k], jnp.int32))
        else:
            load = lambda k: plane_ref[k]

        def body(p, carry, j=j, mask=mask, load=load):
            k = jnp.left_shift(p & ~(j - 1), 1) | (p & (j - 1))
            lo = load(k)
            hi = load(k + j)
            t = (lo ^ lax.shift_right_logical(hi, j)) & mask
            plane_ref[k] = lo ^ t
            plane_ref[k + j] = hi ^ jnp.left_shift(t, j)
            return carry

        lax.fori_loop(0, N_SLABS // 2, body, 0)


def _dsa_kernel(bound_ref, qd_ref, qi_ref, misc_ref, ki_ref, kd_ref, vd_ref, zd_ref, tri_ref, negeye_ref,
                o_ref, score_scr, plane_scr, bias_scr, shift0_scr, shift1_scr, general_scr, *, top_k, tq):
    t0 = pl.program_id(1) * tq
    lane = lax.broadcasted_iota(jnp.int32, (1, LANES), 1)
    lo64 = lane < 64
    t_col = t0 + lax.broadcasted_iota(jnp.int32, (tq, 1), 0)
    t_row = t0 + lax.broadcasted_iota(jnp.int32, (1, tq), 1)
    n_chunks = (t0 + tq + KC - 1) // KC
    slabs_per_chunk = KC // LANES
    int_min = jnp.int32(-2 ** 31)

    def chunk_slabs(c):
        return pl.ds(c * slabs_per_chunk, slabs_per_chunk)

    def to_col(row):
        return jnp.broadcast_to(row, (SUBLANES, tq)).T[:, 0:1]

    qi = qi_ref[0]
    qi_heads = [_half_masked(qi[:, (h // 2) * LANES:(h // 2 + 1) * LANES], h, lo64) for h in range(IDX_HEADS)]
    misc_t = misc_ref[0].T
    wi = [misc_t[MISC_WI + h:MISC_WI + h + 1, :] for h in range(IDX_HEADS)]

    def score_body(c, carry):
        k0 = pl.multiple_of(c * KC, KC)
        kib = ki_ref[0, pl.ds(k0, KC), :]
        sc = jnp.zeros((KC, tq), f32)
        for h in range(IDX_HEADS):
            sc = sc + jnp.maximum(_nt_dot(kib, qi_heads[h]), 0.0) * wi[h]
        sc = jnp.where(sc == 0.0, 0.0, sc)
        kpos = k0 + lax.broadcasted_iota(jnp.int32, (KC, 1), 0)
        sc = jnp.where(kpos <= t_row, sc, NON_CAUSAL_MARK)
        score_scr[chunk_slabs(c)] = sc.reshape(slabs_per_chunk, LANES, tq)
        return carry

    lax.fori_loop(0, n_chunks, score_body, 0)

    def fill_body(c, carry):
        score_scr[chunk_slabs(c)] = jnp.full((slabs_per_chunk, LANES, tq), NON_CAUSAL_MARK, f32)
        return carry

    lax.fori_loop(n_chunks, N_SLABS // slabs_per_chunk, fill_body, 0)

    _bit_planes(score_scr, plane_scr)
    kf = float(top_k)

    def select_bit(i, carry):
        alive, above, thr = carry
        plane = plane_scr[i] ^ jnp.where(i == 0, jnp.int32(-1), jnp.int32(0))
        ones = alive & plane
        c1 = jnp.sum(lax.population_count(ones).astype(f32), axis=0, keepdims=True)
        take = (above + c1) >= kf
        alive = jnp.where(take, ones, alive ^ ones)
        above = jnp.where(take, above, above + c1)
        thr = jnp.where(take, thr | jnp.left_shift(jnp.int32(1), 31 - i), thr)
        return alive, above, thr

    init = (jnp.full((LANES, tq), -1, jnp.int32), jnp.zeros((1, tq), f32), jnp.zeros((1, tq), jnp.int32))
    alive, above, thr_u = lax.fori_loop(0, 32, select_bit, init)
    n_tied = jnp.sum(lax.population_count(alive).astype(f32), axis=0, keepdims=True)

    def key_to_score(key_row):
        return pltpu.bitcast(_ordered_bits(key_row), f32)

    def write_bias_no_ties(thr):
        def chunk(c):
            k0 = pl.multiple_of(c * KC, KC)
            admitted = score_scr[chunk_slabs(c)].reshape(KC, tq) >= thr
            bias_scr[:, pl.ds(k0, KC)] = _nt_dot(negeye_ref[...], jnp.where(admitted, 0.0, 1.0).astype(bf16))
            return jnp.sum(jnp.where(admitted, 1.0, 0.0), axis=0, keepdims=True)

        def body(i, n):
            return n + chunk(2 * i) + chunk(2 * i + 1)

        return lax.fori_loop(0, (n_chunks + 1) // 2, body, jnp.zeros((1, tq), f32))

    def write_bias(thr_row, need_row):
        thr = to_col(thr_row)
        need = to_col(need_row)

        def body(c, carry):
            ties_before, n_above = carry
            k0 = pl.multiple_of(c * KC, KC)
            blk = score_scr[chunk_slabs(c)].reshape(KC, tq).T
            gt = jnp.where(blk > thr, 1.0, 0.0)
            eq = jnp.where(blk == thr, 1.0, 0.0)
            prefix = ties_before + _dot(eq.astype(bf16), tri_ref[...])
            chosen = gt + jnp.where(prefix <= need, eq, 0.0)
            kpos = k0 + lax.broadcasted_iota(jnp.int32, (1, KC), 1)
            bias_scr[:, pl.ds(k0, KC)] = jnp.where(kpos <= t_col, jnp.where(chosen > 0.5, 0.0, NEG), NEG)
            return prefix[:, KC - 1:KC], n_above + jnp.sum(gt, axis=-1, keepdims=True)

        zero = jnp.zeros((tq, 1), f32)
        return lax.fori_loop(0, n_chunks, body, (zero, zero))

    def count_false(cond):
        return jnp.sum(jnp.where(cond, 0.0, 1.0))

    thr_radix = key_to_score(thr_u ^ int_min)
    general_scr[0] = jnp.int32(1)

    @pl.when(count_false(above + n_tied == kf) == 0.0)
    def _():
        general_scr[0] = (count_false(write_bias_no_ties(thr_radix) == kf) > 0.0).astype(jnp.int32)

    @pl.when(general_scr[0] > 0)
    def _():
        ties, n_above = write_bias(thr_radix, kf - above)
        verified = (n_above == to_col(above)) & (n_above < kf) & (n_above + ties >= kf)

        @pl.when(count_false(verified) > 0.0)
        def _():
            def count(pred):
                def body(c, acc):
                    blk = score_scr[chunk_slabs(c)].reshape(KC, tq)
                    return acc + jnp.sum(jnp.where(pred(blk), 1.0, 0.0), axis=0, keepdims=True)
                return lax.fori_loop(0, N_SLABS // slabs_per_chunk, body, jnp.zeros((1, tq), f32))

            key = jnp.where(count(lambda b: b >= 0.0) >= kf, jnp.int32(0), int_min)

            def bit(i, key):
                cand = key + jnp.left_shift(jnp.int32(1), 30 - i)
                cand_f = key_to_score(cand)
                return jnp.where(count(lambda b: b >= cand_f) >= kf, cand, key)

            thr_f = key_to_score(lax.fori_loop(0, 31, bit, key))
            write_bias(thr_f, kf - count(lambda b: b > thr_f))

    def bias_fn(k0):
        return bias_scr[:, pl.ds(k0, KC)]

    qd = qd_ref[0]
    q4s = [_stack_group_queries(qd, g, lo64) for g in range(KV_GROUPS)]
    outs = _masked_flash(q4s, kd_ref, vd_ref, n_chunks, [bias_fn] * KV_GROUPS, tq,
                         bound_ref, (shift0_scr, shift1_scr))
    o = jnp.concatenate([_group_output(og, lo64, tq) for og in outs], axis=1) * zd_ref[0].astype(f32)
    o_ref[0] = o.astype(bf16)


def _dsa_attention(bound, qd, qi, misc, ki, kd, vd, zd):
    B, S, _ = qd.shape
    tq = TQ_DSA
    assert S <= N_SLABS * LANES
    tri = jnp.asarray(np.triu(np.ones((KC, KC), np.float32)), bf16)
    negeye = jnp.asarray(np.eye(tq, dtype=np.float32) * NEG, bf16)
    row = lambda c: pl.BlockSpec((1, tq, c), lambda b, i: (b, i, 0))
    whole = pl.BlockSpec((1, KV_GROUPS, S, LANES), lambda b, i: (b, 0, 0, 0))
    return pl.pallas_call(
        functools.partial(_dsa_kernel, top_k=min(DSA_TOPK_MAX, S // 4), tq=tq),
        grid=(B, S // tq),
        in_specs=[_SMEM_SCALAR, row(512), row(256), row(LANES), pl.BlockSpec((1, S, LANES), lambda b, i: (b, 0, 0)),
                  whole, whole, row(512), pl.BlockSpec((KC, KC), lambda b, i: (0, 0)),
                  pl.BlockSpec((tq, tq), lambda b, i: (0, 0))],
        out_specs=row(512),
        out_shape=jax.ShapeDtypeStruct((B, S, 512), bf16),
        scratch_shapes=[pltpu.VMEM((N_SLABS, LANES, tq), f32), pltpu.VMEM((N_SLABS, LANES, tq), jnp.int32),
                        pltpu.VMEM((tq, S), f32)] + _shift_scratch(tq) + [pltpu.SMEM((1,), jnp.int32)],
        compiler_params=pltpu.CompilerParams(
            dimension_semantics=("arbitrary", "arbitrary"), vmem_limit_bytes=VMEM_LIMIT),
        name="dsa_attention",
    )(bound, qd, qi, misc, ki, kd, vd, zd, tri, negeye)


def _outproj_kernel(x_ref, on_ref, od_ref, w_ref, o_ref):
    half = w_ref.shape[0] // 2
    o_ref[0] = x_ref[0] + _dot(on_ref[0], w_ref[0:half, :]) + _dot(od_ref[0], w_ref[half:, :])


def _out_projection(x, o_nsa, o_dsa, w_out):
    B, S, D = x.shape
    tm = min(TM_PROJ, S)
    row = lambda c: pl.BlockSpec((1, tm, c), lambda b, i: (b, i, 0))
    return pl.pallas_call(
        _outproj_kernel,
        grid=(B, S // tm),
        in_specs=[row(D), row(512), row(512), pl.BlockSpec(w_out.shape, lambda b, i: (0, 0))],
        out_specs=row(D),
        out_shape=jax.ShapeDtypeStruct((B, S, D), f32),
        compiler_params=pltpu.CompilerParams(
            dimension_semantics=("arbitrary", "arbitrary"), vmem_limit_bytes=VMEM_LIMIT),
        name="out_projection",
    )(x, o_nsa, o_dsa, w_out)


def _rope_tables(pos):
    half = HEAD_DIM // 2
    inv_freq = ROPE_THETA ** (-jnp.arange(half, dtype=f32) / half)
    ang = pos[:, None] * inv_freq[None, :]
    cos, sin = jnp.cos(ang), jnp.sin(ang)
    cos_t = jnp.tile(cos, (1, LANES // half))
    sin_t = jnp.tile(jnp.concatenate([-sin, sin], axis=1), (1, LANES // HEAD_DIM))
    return cos_t, sin_t


def _overlap_t(seq, n_cmp_pad):
    n_cmp = (seq - CMP_BLOCK) // CMP_STRIDE + 1
    n_sel = seq // SEL_BLOCK
    c_start = np.arange(n_cmp) * CMP_STRIDE
    j_start = np.arange(n_sel) * SEL_BLOCK
    ov = np.clip(np.minimum(c_start[:, None] + CMP_BLOCK, j_start[None, :] + SEL_BLOCK)
                 - np.maximum(c_start[:, None], j_start[None, :]), 0, None).astype(np.float32) / CMP_BLOCK
    out = np.zeros((n_sel, n_cmp_pad), np.float32)
    out[:, :n_cmp] = ov.T
    return out


def _block_diag2(w):
    z = jnp.zeros_like(w)
    return jnp.concatenate([jnp.concatenate([w, z], axis=-1), jnp.concatenate([z, w], axis=-1)], axis=-2)


def _layer(x, norm_gain, w_in, nsa_q_gain, nsa_kc_gain, nsa_ks_gain, nsa_kw_gain,
           cmp_pe_k, cmp_k_w1, cmp_k_b1, cmp_k_w2, cmp_pe_v, cmp_v_w1, cmp_v_b1, cmp_v_w2,
           dsa_q_gain, dsa_k_gain, w_out):
    B, S, _ = x.shape
    assert S % KC == 0 and S >= WINDOW + TQ_NSA
    n_cmp_pad = S // CMP_STRIDE

    w_perm = _permuted_weight(w_in)
    head_of_lane = np.arange(2 * LANES) // HEAD_DIM
    bd = jnp.asarray(head_of_lane[:, None] == head_of_lane[None, :], bf16)
    dup = lambda v: jnp.tile(v.reshape(1, -1), (1, 2))
    quad = lambda v: jnp.tile(v.reshape(1, -1), (1, 4))
    gains = jnp.concatenate([quad(nsa_q_gain), jnp.concatenate([dup(nsa_ks_gain), dup(nsa_kw_gain)], axis=1),
                             quad(dsa_q_gain), quad(dsa_k_gain), jnp.ones((4, 2 * LANES), f32)], axis=0)
    cos_t, sin_t = _rope_tables(jnp.arange(S, dtype=f32))

    (qn, qd, qi, ki, ks, kw, kd, vs, vw, vd, kc, vc, zn, zd, misc) = _in_projection(
        x, norm_gain.reshape(1, -1), w_perm, bd, cos_t, sin_t, gains)

    cmp_pos = (jnp.arange(n_cmp_pad) * CMP_STRIDE + CMP_BLOCK - 1).astype(f32)
    cos_c, sin_c = _rope_tables(cmp_pos)

    def cmp_weights(pe, w1, b1, w2):
        w1_bd = _block_diag2(w1.reshape(CMP_BLOCK, HEAD_DIM, CMP_HIDDEN)).astype(bf16)
        return jnp.tile(pe, (1, 2)), w1_bd, dup(b1), _block_diag2(w2).astype(bf16)

    kcmp = _compress(kc, *cmp_weights(cmp_pe_k, cmp_k_w1, cmp_k_b1, cmp_k_w2), bd[:LANES, :LANES], dup(nsa_kc_gain),
                     cos_c, sin_c, is_key=True)
    vcmp = _compress(vc, *cmp_weights(cmp_pe_v, cmp_v_w1, cmp_v_b1, cmp_v_w2), bd[:LANES, :LANES], dup(nsa_kc_gain),
                     cos_c, sin_c, is_key=False)

    ovt = jnp.asarray(_overlap_t(S, n_cmp_pad), bf16)
    o_nsa = _nsa_attention(_score_bound(nsa_q_gain, nsa_ks_gain), qn, kcmp, vcmp, ks, vs, kw, vw, misc, zn, ovt)

    o_dsa = _dsa_attention(_score_bound(dsa_q_gain, dsa_k_gain), qd, qi, misc, ki, kd, vd, zd)

    return _out_projection(x, o_nsa, o_dsa, w_out.astype(bf16))


def kernel(x, norm_gain, w_in, nsa_q_gain, nsa_kc_gain, nsa_ks_gain, nsa_kw_gain, cmp_pe_k, cmp_k_w1,
           cmp_k_b1, cmp_k_w2, cmp_pe_v, cmp_v_w1, cmp_v_b1, cmp_v_w2, dsa_q_gain, dsa_k_gain, w_out):
    for l in range(norm_gain.shape[0]):
        x = _layer(x, norm_gain[l], w_in[l], nsa_q_gain[l], nsa_kc_gain[l], nsa_ks_gain[l], nsa_kw_gain[l],
                   cmp_pe_k[l], cmp_k_w1[l], cmp_k_b1[l], cmp_k_w2[l], cmp_pe_v[l], cmp_v_w1[l], cmp_v_b1[l],
                   cmp_v_w2[l], dsa_q_gain[l], dsa_k_gain[l], w_out[l])
    return x
```

```python
import functools

import numpy as np
import jax
import jax.numpy as jnp
from jax import lax
from jax.experimental import pallas as pl
from jax.experimental.pallas import tpu as pltpu

D_MODEL = 1024
HEAD_DIM = 64
NSA_HEADS = 8
DSA_HEADS = 8
KV_GROUPS = 2
HEADS_PER_GROUP = 4
IDX_HEADS = 4
CMP_BLOCK = 32
CMP_STRIDE = 16
CMP_HIDDEN = 256
SEL_BLOCK = 64
SEL_TOPN = 16
WINDOW = 512
DSA_TOPK_MAX = 256
ROPE_THETA = 10000.0
EPS = 1e-6
NEG = -1e30
FORCE = 1e6
ATTN_SCALE = HEAD_DIM ** -0.5
IDX_SCALE = HEAD_DIM ** -0.5
WI_SCALE = IDX_HEADS ** -0.5
LOG2E = 1.4426950408889634
Q_SCALE = ATTN_SCALE * LOG2E

LANES = 128
SUBLANES = 8
M_FLOOR = -5e29
NON_CAUSAL_MARK = -3.0e38
MAX_FIXED_SHIFT = 30.0
NORM_SLACK = 1.03
VMEM_LIMIT = 56 * 1024 * 1024

IN_WIDTHS = (512, 128, 128, 128, 128, 128, 128, 24, 512, 512, 128, 128, 256, 64, 4, 512)
IN_NAMES = ("q_n", "kc", "vc", "ks", "vs", "kw", "vw", "gate", "z_n",
            "q_d", "k_d", "v_d", "qi", "ki", "wi", "z_d")
IN_COLS = sum(IN_WIDTHS)

CH_QN, CH_KS, CH_KW, CH_QD, CH_KD, CH_QI, CH_KI = 0, 4, 5, 6, 10, 11, 13
CH_KC, CH_VC, CH_VS, CH_VW, CH_VD, CH_ZN, CH_ZD, CH_MISC = 14, 15, 16, 17, 18, 19, 23, 27
N_CHUNKS = 28
MISC_WI = 24

TM_PROJ = 512
TQ_NSA = 256
TQ_DSA = 256
KC = 512

f32 = jnp.float32
bf16 = jnp.bfloat16


def _permuted_weight(w_in):
    off = dict(zip(IN_NAMES, np.cumsum((0,) + IN_WIDTHS[:-1])))
    width = dict(zip(IN_NAMES, IN_WIDTHS))
    order = ("q_n", "ks", "kw", "q_d", "k_d", "qi", "ki", "ki", "kc", "vc", "vs", "vw", "v_d", "z_n", "z_d",
             "gate", "wi")
    w16 = w_in.astype(bf16)
    parts = [w16[:, off[n]:off[n] + width[n]] for n in order]
    used = sum(width[n] for n in order)
    parts.append(jnp.zeros((D_MODEL, N_CHUNKS * LANES - used), bf16))
    return jnp.concatenate(parts, axis=1)


def _nt_dot(a, b):
    return lax.dot_general(a, b, (((1,), (1,)), ((), ())), preferred_element_type=f32)


def _dot(a, b):
    return jnp.dot(a, b, preferred_element_type=f32)


def _split_bf16(v):
    hi = v.astype(bf16)
    lo = (v - hi.astype(f32)).astype(bf16)
    return hi, lo


def _head_rms_norm(y, bd, gain_row):
    ssq = _dot((y * y).astype(bf16), bd)
    return y * lax.rsqrt(ssq * (1.0 / HEAD_DIM) + EPS) * gain_row


def _rope(y, cos, sin_signed, lo32):
    partner = jnp.where(lo32, pltpu.roll(y, LANES - 32, 1), pltpu.roll(y, 32, 1))
    return y * cos + partner * sin_signed


def _dup_halves(y, lo64):
    r = pltpu.roll(y, 64, 1)
    return jnp.where(lo64, y, r), jnp.where(lo64, r, y)


def _value_with_ones(y, lo64):
    return jnp.where(lo64, y, 1.0), jnp.where(lo64, pltpu.roll(y, 64, 1), 1.0)


def _inproj_kernel(x_ref, ng_ref, w_ref, bd_ref, cos_ref, sin_ref, gains_ref,
                   qn_ref, qd_ref, qi_ref, ki_ref, ks_ref, kw_ref, kd_ref,
                   vs_ref, vw_ref, vd_ref, kc_ref, vc_ref, zn_ref, zd_ref, misc_ref):
    x = x_ref[0]
    ms = jnp.mean(x * x, axis=-1, keepdims=True)
    h = (x * lax.rsqrt(ms + EPS) * ng_ref[...]).astype(bf16)
    cos = cos_ref[...]
    sin = sin_ref[...]
    bd = bd_ref[...]
    lane = lax.broadcasted_iota(jnp.int32, (1, LANES), 1)
    lo32 = (lane % 64) < 32
    lo64 = lane < 64

    def proj(c0, n):
        return _dot(h, w_ref[:, c0 * LANES:(c0 + n) * LANES])

    def chunk(y, i):
        return y[:, i * LANES:(i + 1) * LANES]

    def normed_rope(y, gain_idx):
        w = y.shape[1]
        yn = _head_rms_norm(y, bd[:w, :w], gains_ref[gain_idx:gain_idx + 1, 0:w])
        return [_rope(chunk(yn, i), cos, sin, lo32) for i in range(w // LANES)]

    for c0, gain_idx, ref in ((CH_QN, 0, qn_ref), (CH_QD, 2, qd_ref)):
        y = proj(c0, 4)
        for i in range(2):
            for j, r in enumerate(normed_rope(y[:, 2 * i * LANES:(2 * i + 2) * LANES], gain_idx)):
                ref[0, :, (2 * i + j) * LANES:(2 * i + j + 1) * LANES] = (r * Q_SCALE).astype(bf16)
    y = proj(CH_QI, 2)
    for i in range(2):
        qi_ref[0, :, i * LANES:(i + 1) * LANES] = (_rope(chunk(y, i), cos, sin, lo32) * IDX_SCALE).astype(bf16)
    ki_ref[0] = _rope(proj(CH_KI, 1), cos, sin, lo32).astype(bf16)

    k_pairs = normed_rope(proj(CH_KS, 2), 1) + normed_rope(proj(CH_KD, 1), 3)
    for r, ref in zip(k_pairs, (ks_ref, kw_ref, kd_ref)):
        a, b = _dup_halves(r, lo64)
        ref[0, 0] = a.astype(bf16)
        ref[0, 1] = b.astype(bf16)
    for c0, ref in ((CH_VS, vs_ref), (CH_VW, vw_ref), (CH_VD, vd_ref)):
        a, b = _value_with_ones(proj(c0, 1), lo64)
        ref[0, 0] = a.astype(bf16)
        ref[0, 1] = b.astype(bf16)

    kc_ref[0] = proj(CH_KC, 1)
    vc_ref[0] = proj(CH_VC, 1)
    z = proj(CH_ZN, 4)
    zn_ref[0] = (z * jax.nn.sigmoid(z)).astype(bf16)
    z = proj(CH_ZD, 4)
    zd_ref[0] = (z * jax.nn.sigmoid(z)).astype(bf16)
    m = proj(CH_MISC, 1)
    misc_ref[0] = jnp.where(lane < MISC_WI, jax.nn.sigmoid(m), m * WI_SCALE)


def _in_projection(x, norm_gain, w_perm, bd, cos_t, sin_t, gains):
    B, S, _ = x.shape
    tm = min(TM_PROJ, S)
    grid = (B, S // tm)
    row = lambda c: pl.BlockSpec((1, tm, c), lambda b, i: (b, i, 0))
    dup = pl.BlockSpec((1, KV_GROUPS, tm, LANES), lambda b, i: (b, 0, i, 0))
    const = lambda shape: pl.BlockSpec(shape, lambda b, i: tuple(0 for _ in shape))
    tab = pl.BlockSpec((tm, LANES), lambda b, i: (i, 0))
    sds = jax.ShapeDtypeStruct
    out_shape = (
        sds((B, S, 512), bf16), sds((B, S, 512), bf16), sds((B, S, 256), bf16), sds((B, S, LANES), bf16),
        sds((B, KV_GROUPS, S, LANES), bf16), sds((B, KV_GROUPS, S, LANES), bf16),
        sds((B, KV_GROUPS, S, LANES), bf16), sds((B, KV_GROUPS, S, LANES), bf16),
        sds((B, KV_GROUPS, S, LANES), bf16), sds((B, KV_GROUPS, S, LANES), bf16),
        sds((B, S, LANES), f32), sds((B, S, LANES), f32),
        sds((B, S, 512), bf16), sds((B, S, 512), bf16), sds((B, S, LANES), f32),
    )
    out_specs = (row(512), row(512), row(256), row(LANES), dup, dup, dup, dup, dup, dup,
                 row(LANES), row(LANES), row(512), row(512), row(LANES))
    return pl.pallas_call(
        _inproj_kernel,
        grid=grid,
        in_specs=[row(D_MODEL), const((1, D_MODEL)), const((D_MODEL, N_CHUNKS * LANES)),
                  const((2 * LANES, 2 * LANES)), tab, tab, const((8, 2 * LANES))],
        out_specs=out_specs,
        out_shape=out_shape,
        compiler_params=pltpu.CompilerParams(
            dimension_semantics=("arbitrary", "arbitrary"), vmem_limit_bytes=VMEM_LIMIT),
        name="in_projection",
    )(x, norm_gain, w_perm, bd, cos_t, sin_t, gains)


def _compress_kernel(src_ref, pe_ref, w1_ref, b1_ref, w2_ref, bd_ref, gain_ref, cos_ref, sin_ref,
                     out_ref, *, n_rows, is_key):
    half = CMP_BLOCK // 2
    acc_a = jnp.zeros((n_rows, 2 * CMP_HIDDEN), f32)
    acc_b = jnp.zeros((n_rows, 2 * CMP_HIDDEN), f32)
    for l in range(half):
        rows = src_ref[0, pl.ds(l, n_rows, stride=CMP_STRIDE), :]
        acc_a = acc_a + _dot((rows + pe_ref[l:l + 1, :]).astype(bf16), w1_ref[l])
        acc_b = acc_b + _dot((rows + pe_ref[l + half:l + half + 1, :]).astype(bf16), w1_ref[l + half])
    pre = acc_a + pltpu.roll(acc_b, n_rows - 1, 0) + b1_ref[...]
    hid = pre * jax.nn.sigmoid(pre)
    out = _dot(hid.astype(bf16), w2_ref[...])
    lane = lax.broadcasted_iota(jnp.int32, (1, LANES), 1)
    if is_key:
        out = _head_rms_norm(out, bd_ref[...], gain_ref[...])
        out = _rope(out, cos_ref[...], sin_ref[...], (lane % 64) < 32)
        a, b = _dup_halves(out, lane < 64)
    else:
        a, b = _value_with_ones(out, lane < 64)
    out_ref[0, 0] = a.astype(bf16)
    out_ref[0, 1] = b.astype(bf16)


def _compress(src, pe_dup, w1_bd, b1_dup, w2_bd, bd, gain_dup, cos_c, sin_c, *, is_key):
    B, S, _ = src.shape
    n_rows = S // CMP_STRIDE
    const = lambda shape: pl.BlockSpec(shape, lambda b: tuple(0 for _ in shape))
    return pl.pallas_call(
        functools.partial(_compress_kernel, n_rows=n_rows, is_key=is_key),
        grid=(B,),
        in_specs=[pl.BlockSpec((1, S, LANES), lambda b: (b, 0, 0)),
                  const((CMP_BLOCK, LANES)), const((CMP_BLOCK, LANES, 2 * CMP_HIDDEN)),
                  const((1, 2 * CMP_HIDDEN)), const((2 * CMP_HIDDEN, LANES)), const((LANES, LANES)),
                  const((1, LANES)), const((n_rows, LANES)), const((n_rows, LANES))],
        out_specs=pl.BlockSpec((1, KV_GROUPS, n_rows, LANES), lambda b: (b, 0, 0, 0)),
        out_shape=jax.ShapeDtypeStruct((B, KV_GROUPS, n_rows, LANES), bf16),
        compiler_params=pltpu.CompilerParams(
            dimension_semantics=("arbitrary",), vmem_limit_bytes=VMEM_LIMIT),
        name="compress_k" if is_key else "compress_v",
    )(src, pe_dup, w1_bd, b1_dup, w2_bd, bd, gain_dup, cos_c, sin_c)


def _half_masked(pair, h, lo64):
    keep = lo64 if h % 2 == 0 else jnp.logical_not(lo64)
    return jnp.where(keep, pair, jnp.zeros_like(pair))


def _stack_group_queries(q, g, lo64):
    slabs = []
    for r in range(HEADS_PER_GROUP):
        h = g * HEADS_PER_GROUP + r
        slabs.append(_half_masked(q[:, (h // 2) * LANES:(h // 2 + 1) * LANES], h, lo64))
    return jnp.concatenate(slabs, axis=0)


def _masked_flash(q4s, k_ref, v_ref, n_chunks, bias_fns, tq, bound_ref, shift_scrs):
    rows = HEADS_PER_GROUP * tq
    bound = bound_ref[0]

    def scores(g, k0):
        s = _nt_dot(q4s[g], k_ref[0, g, pl.ds(k0, KC), :])
        return (s.reshape(HEADS_PER_GROUP, tq, KC) + bias_fns[g](k0)[None]).reshape(rows, KC)

    for g in range(KV_GROUPS):
        shift_scrs[g][...] = jnp.full((rows, 1), bound, f32)

    @pl.when(bound > MAX_FIXED_SHIFT)
    def _():
        def body(c, ms):
            k0 = pl.multiple_of(c * KC, KC)
            return tuple(jnp.maximum(ms[g], jnp.max(scores(g, k0), axis=-1, keepdims=True))
                         for g in range(KV_GROUPS))

        init = tuple(jnp.full((rows, 1), M_FLOOR, f32) for _ in range(KV_GROUPS))
        for g, m in enumerate(lax.fori_loop(0, n_chunks, body, init)):
            shift_scrs[g][...] = m

    def body(c, accs):
        k0 = pl.multiple_of(c * KC, KC)
        new = []
        for g in range(KV_GROUPS):
            p = jnp.exp2(scores(g, k0) - shift_scrs[g][...]).astype(bf16)
            new.append(accs[g] + _dot(p, v_ref[0, g, pl.ds(k0, KC), :]))
        return tuple(new)

    init = tuple(jnp.zeros((rows, LANES), f32) for _ in range(KV_GROUPS))
    out = lax.fori_loop(0, n_chunks, body, init)
    return [acc / acc[:, HEAD_DIM:HEAD_DIM + 1] for acc in out]


def _group_output(og, lo64, tq):
    pairs = []
    for p in range(2):
        even = og[(2 * p) * tq:(2 * p + 1) * tq]
        odd = og[(2 * p + 1) * tq:(2 * p + 2) * tq]
        pairs.append(jnp.where(lo64, even, pltpu.roll(odd, HEAD_DIM, 1)))
    return jnp.concatenate(pairs, axis=1)


def _top_n_blocks(imp, n_sel, top_n):
    groups = [imp[SUBLANES * a:SUBLANES * (a + 1)] for a in range(n_sel // SUBLANES)]
    jsub = lax.broadcasted_iota(jnp.int32, groups[0].shape, 0)
    ranks = [jnp.zeros(groups[0].shape, f32) for _ in groups]
    for i in range(n_sel):
        row = imp[i:i + 1, :]
        for a, grp in enumerate(groups):
            if SUBLANES * a > i:
                beats = row >= grp
            elif SUBLANES * (a + 1) - 1 < i:
                beats = row > grp
            else:
                tie = jnp.where(jsub + SUBLANES * a > i, 1.0, 0.0)
                ranks[a] = ranks[a] + jnp.where(row > grp, 1.0, jnp.where(row == grp, tie, 0.0))
                continue
            ranks[a] = ranks[a] + jnp.where(beats, 1.0, 0.0)
    rank = jnp.concatenate(ranks, axis=0)
    return jnp.where(rank < top_n, 1.0, 0.0)


def _nsa_kernel(bound_ref, qn_ref, kcmp_ref, vcmp_ref, ks_ref, vs_ref, kw_ref, vw_ref, misc_ref, zn_ref, ovt_ref,
                expand_ref, o_ref, shift0_scr, shift1_scr, cshift0_scr, cshift1_scr, wshift0_scr, wshift1_scr,
                bias_scr, *, n_cmp_pad, n_sel, top_n):
    tq = TQ_NSA
    cmp_shift = [cshift0_scr, cshift1_scr]
    win_shift = [wshift0_scr, wshift1_scr]
    t0 = pl.program_id(1) * tq
    lane = lax.broadcasted_iota(jnp.int32, (1, LANES), 1)
    lo64 = lane < 64
    q = qn_ref[0]
    misc = misc_ref[0]
    t_col = t0 + lax.broadcasted_iota(jnp.int32, (tq, 1), 0)
    n_chunks = (t0 + tq + KC - 1) // KC
    win_span = WINDOW + tq
    win_start = pl.multiple_of(jnp.maximum(t0 - WINDOW, 0), tq)
    ovt = ovt_ref[...]
    rows = HEADS_PER_GROUP * tq

    def add_bias(s, bias):
        return (s.reshape(HEADS_PER_GROUP, tq, s.shape[-1]) + bias[None]).reshape(rows, s.shape[-1])

    cmp_end = lax.broadcasted_iota(jnp.int32, (1, n_cmp_pad), 1) * CMP_STRIDE + (CMP_BLOCK - 1)
    cmp_bias = jnp.where(cmp_end <= t_col, 0.0, NEG)
    win_diff = t_col - (win_start + lax.broadcasted_iota(jnp.int32, (1, win_span), 1))
    win_bias = jnp.where((win_diff >= 0) & (win_diff < WINDOW), 0.0, NEG)
    q4s = [_stack_group_queries(q, g, lo64) for g in range(KV_GROUPS)]

    def cmp_scores(g):
        return add_bias(_nt_dot(q4s[g], kcmp_ref[0, g]), cmp_bias)

    def win_scores(g):
        return add_bias(_nt_dot(q4s[g], kw_ref[0, g, pl.ds(win_start, win_span), :]), win_bias)

    for scr in cmp_shift + win_shift:
        scr[...] = jnp.full((rows, 1), bound_ref[0], f32)

    @pl.when(bound_ref[0] > MAX_FIXED_SHIFT)
    def _():
        for g in range(KV_GROUPS):
            cmp_shift[g][...] = jnp.maximum(jnp.max(cmp_scores(g), axis=-1, keepdims=True), M_FLOOR)
            win_shift[g][...] = jnp.max(win_scores(g), axis=-1, keepdims=True)

    o_cmp, o_win, sels = [], [], []
    for g in range(KV_GROUPS):
        e = jnp.exp2(cmp_scores(g) - cmp_shift[g][...])
        l = jnp.sum(e, axis=-1, keepdims=True)
        p = e * (1.0 / jnp.maximum(l, 1e-30))
        o_cmp.append(_dot(p.astype(bf16), vcmp_ref[0, g]))

        psum = p[0:tq] + p[tq:2 * tq] + p[2 * tq:3 * tq] + p[3 * tq:4 * tq]
        hi, lo = _split_bf16(psum)
        imp = _nt_dot(ovt, hi) + _nt_dot(ovt, lo)
        jrow = lax.broadcasted_iota(jnp.int32, (n_sel, tq), 0)
        tl = t0 + lax.broadcasted_iota(jnp.int32, (n_sel, tq), 1)
        cur = lax.shift_right_logical(tl, 6)
        forced = (jrow == 0) | (jrow == cur) | (jrow == cur - 1)
        imp = jnp.where(forced, FORCE, imp)
        imp = jnp.where(jrow * SEL_BLOCK <= tl, imp, NEG)
        pad_row = lax.broadcasted_iota(jnp.int32, (LANES - n_sel, tq), 0)
        chosen = jnp.concatenate([_top_n_blocks(imp, n_sel, top_n) * -NEG, jnp.where(pad_row == 0, NEG, 0.0)], axis=0)
        sels.append(chosen.T.astype(bf16))

        e = jnp.exp2(win_scores(g) - win_shift[g][...])
        acc = _dot(e.astype(bf16), vw_ref[0, g, pl.ds(win_start, win_span), :])
        o_win.append(acc / acc[:, HEAD_DIM:HEAD_DIM + 1])

    def bias_body(c, carry):
        k0 = pl.multiple_of(c * KC, KC)
        for g in range(KV_GROUPS):
            bias_scr[g, :, pl.ds(k0, KC)] = _dot(sels[g], expand_ref[:, pl.ds(k0, KC)])
        return carry

    lax.fori_loop(0, n_chunks, bias_body, 0)
    k_last = pl.multiple_of((n_chunks - 1) * KC, KC)
    causal = jnp.where(k_last + lax.broadcasted_iota(jnp.int32, (1, KC), 1) <= t_col, 0.0, NEG)
    for g in range(KV_GROUPS):
        bias_scr[g, :, pl.ds(k_last, KC)] = bias_scr[g, :, pl.ds(k_last, KC)] + causal

    o_sel = _masked_flash(q4s, ks_ref, vs_ref, n_chunks,
                          [lambda k0, g=g: bias_scr[g, :, pl.ds(k0, KC)] for g in range(KV_GROUPS)], tq,
                          bound_ref, (shift0_scr, shift1_scr))

    group_outs = []
    for g in range(KV_GROUPS):
        def gate(branch):
            cols = [misc[:, (g * HEADS_PER_GROUP + r) * 3 + branch:(g * HEADS_PER_GROUP + r) * 3 + branch + 1]
                    for r in range(HEADS_PER_GROUP)]
            return jnp.concatenate(cols, axis=0)

        og = gate(0) * o_cmp[g] + gate(1) * o_sel[g] + gate(2) * o_win[g]
        group_outs.append(_group_output(og, lo64, tq))

    o = jnp.concatenate(group_outs, axis=1) * zn_ref[0].astype(f32)
    o_ref[0] = o.astype(bf16)


def _score_bound(q_gain, *k_gains):
    k_max = functools.reduce(jnp.maximum, [jnp.max(jnp.abs(g)) for g in k_gains])
    return (Q_SCALE * HEAD_DIM * NORM_SLACK * jnp.max(jnp.abs(q_gain)) * k_max).reshape(1)


_SMEM_SCALAR = pl.BlockSpec(memory_space=pltpu.SMEM)


def _shift_scratch(tq):
    return [pltpu.VMEM((HEADS_PER_GROUP * tq, 1), f32)] * KV_GROUPS


def _nsa_attention(bound, qn, kcmp, vcmp, ks, vs, kw, vw, misc, zn, ovt):
    B, S, _ = qn.shape
    tq = TQ_NSA
    n_cmp_pad = kcmp.shape[2]
    n_sel = S // SEL_BLOCK
    row = lambda c: pl.BlockSpec((1, tq, c), lambda b, i: (b, i, 0))
    whole = lambda n: pl.BlockSpec((1, KV_GROUPS, n, LANES), lambda b, i: (b, 0, 0, 0))
    kern = functools.partial(_nsa_kernel, n_cmp_pad=n_cmp_pad, n_sel=n_sel, top_n=min(SEL_TOPN, n_sel))
    assert n_sel < LANES
    expand = np.zeros((LANES, S), np.float32)
    expand[np.arange(S) // SEL_BLOCK, np.arange(S)] = 1.0
    expand[n_sel] = 1.0
    expand = jnp.asarray(expand, bf16)
    return pl.pallas_call(
        kern,
        grid=(B, S // tq),
        in_specs=[_SMEM_SCALAR, row(512), whole(n_cmp_pad), whole(n_cmp_pad), whole(S), whole(S), whole(S),
                  whole(S), row(LANES), row(512), pl.BlockSpec((n_sel, n_cmp_pad), lambda b, i: (0, 0)),
                  pl.BlockSpec((LANES, S), lambda b, i: (0, 0))],
        out_specs=row(512),
        out_shape=jax.ShapeDtypeStruct((B, S, 512), bf16),
        scratch_shapes=3 * _shift_scratch(tq) + [pltpu.VMEM((KV_GROUPS, tq, S), f32)],
        compiler_params=pltpu.CompilerParams(
            dimension_semantics=("arbitrary", "arbitrary"), vmem_limit_bytes=VMEM_LIMIT),
        name="nsa_attention",
    )(bound, qn, kcmp, vcmp, ks, vs, kw, vw, misc, zn, ovt, expand)


N_SLABS = 32
_TRANSPOSE_STAGES = ((16, 0x0000FFFF), (8, 0x00FF00FF), (4, 0x0F0F0F0F), (2, 0x33333333), (1, 0x55555555))


def _ordered_bits(v):
    return v ^ (jnp.right_shift(v, 31) & jnp.int32(0x7FFFFFFF))


def _bit_planes(score_ref, plane_ref):
    for stage, (j, mask) in enumerate(_TRANSPOSE_STAGES):
        if stage == 0:
            load = lambda k: _ordered_bits(pltpu.bitcast(score_ref[k], jnp.int32))
        else:
            load = lambda k: plane_ref[k]

        def body(p, carry, j=j, mask=mask, load=load):
            k = jnp.left_shift(p & ~(j - 1), 1) | (p & (j - 1))
            lo = load(k)
            hi = load(k + j)
            t = (lo ^ lax.shift_right_logical(hi, j)) & mask
            plane_ref[k] = lo ^ t
            plane_ref[k + j] = hi ^ jnp.left_shift(t, j)
            return carry

        lax.fori_loop(0, N_SLABS // 2, body, 0)


def _dsa_kernel(bound_ref, qd_ref, qi_ref, misc_ref, ki_ref, kd_ref, vd_ref, zd_ref, tri_ref,
                o_ref, score_scr, plane_scr, bias_scr, shift0_scr, shift1_scr, *, top_k, tq):
    t0 = pl.program_id(1) * tq
    lane = lax.broadcasted_iota(jnp.int32, (1, LANES), 1)
    lo64 = lane < 64
    t_col = t0 + lax.broadcasted_iota(jnp.int32, (tq, 1), 0)
    t_row = t0 + lax.broadcasted_iota(jnp.int32, (1, tq), 1)
    n_chunks = (t0 + tq + KC - 1) // KC
    slabs_per_chunk = KC // LANES
    int_min = jnp.int32(-2 ** 31)

    def chunk_slabs(c):
        return pl.ds(c * slabs_per_chunk, slabs_per_chunk)

    def to_col(row):
        return jnp.broadcast_to(row, (SUBLANES, tq)).T[:, 0:1]

    qi = qi_ref[0]
    qi_heads = [_half_masked(qi[:, (h // 2) * LANES:(h // 2 + 1) * LANES], h, lo64) for h in range(IDX_HEADS)]
    misc_t = misc_ref[0].T
    wi = [misc_t[MISC_WI + h:MISC_WI + h + 1, :] for h in range(IDX_HEADS)]

    def score_body(c, carry):
        k0 = pl.multiple_of(c * KC, KC)
        kib = ki_ref[0, pl.ds(k0, KC), :]
        sc = jnp.zeros((KC, tq), f32)
        for h in range(IDX_HEADS):
            sc = sc + jnp.maximum(_nt_dot(kib, qi_heads[h]), 0.0) * wi[h]
        sc = jnp.where(sc == 0.0, 0.0, sc)
        kpos = k0 + lax.broadcasted_iota(jnp.int32, (KC, 1), 0)
        sc = jnp.where(kpos <= t_row, sc, NON_CAUSAL_MARK)
        score_scr[chunk_slabs(c)] = sc.reshape(slabs_per_chunk, LANES, tq)
        return carry

    lax.fori_loop(0, n_chunks, score_body, 0)

    def fill_body(c, carry):
        score_scr[chunk_slabs(c)] = jnp.full((slabs_per_chunk, LANES, tq), NON_CAUSAL_MARK, f32)
        return carry

    lax.fori_loop(n_chunks, N_SLABS // slabs_per_chunk, fill_body, 0)

    _bit_planes(score_scr, plane_scr)
    kf = float(top_k)

    def select_bit(i, carry):
        alive, above, thr = carry
        plane = plane_scr[i] ^ jnp.where(i == 0, jnp.int32(-1), jnp.int32(0))
        ones = alive & plane
        c1 = jnp.sum(lax.population_count(ones).astype(f32), axis=0, keepdims=True)
        take = (above + c1) >= kf
        alive = jnp.where(take, ones, alive ^ ones)
        above = jnp.where(take, above, above + c1)
        thr = jnp.where(take, thr | jnp.left_shift(jnp.int32(1), 31 - i), thr)
        return alive, above, thr

    init = (jnp.full((LANES, tq), -1, jnp.int32), jnp.zeros((1, tq), f32), jnp.zeros((1, tq), jnp.int32))
    _, above, thr_u = lax.fori_loop(0, 32, select_bit, init)

    def key_to_score(key_row):
        return pltpu.bitcast(_ordered_bits(key_row), f32)

    def write_bias(thr_row, need_row):
        thr = to_col(thr_row)
        need = to_col(need_row)

        def body(c, carry):
            ties_before, n_above = carry
            k0 = pl.multiple_of(c * KC, KC)
            blk = score_scr[chunk_slabs(c)].reshape(KC, tq).T
            gt = jnp.where(blk > thr, 1.0, 0.0)
            eq = jnp.where(blk == thr, 1.0, 0.0)
            prefix = ties_before + _dot(eq.astype(bf16), tri_ref[...])
            chosen = gt + jnp.where(prefix <= need, eq, 0.0)
            kpos = k0 + lax.broadcasted_iota(jnp.int32, (1, KC), 1)
            bias_scr[:, pl.ds(k0, KC)] = jnp.where(kpos <= t_col, jnp.where(chosen > 0.5, 0.0, NEG), NEG)
            return prefix[:, KC - 1:KC], n_above + jnp.sum(gt, axis=-1, keepdims=True)

        zero = jnp.zeros((tq, 1), f32)
        return lax.fori_loop(0, n_chunks, body, (zero, zero))

    ties, n_above = write_bias(key_to_score(thr_u ^ int_min), kf - above)
    verified = (n_above == to_col(above)) & (n_above < kf) & (n_above + ties >= kf)

    @pl.when(jnp.sum(jnp.where(verified, 0.0, 1.0)) > 0.0)
    def _():
        def count(pred):
            def body(c, acc):
                blk = score_scr[chunk_slabs(c)].reshape(KC, tq)
                return acc + jnp.sum(jnp.where(pred(blk), 1.0, 0.0), axis=0, keepdims=True)
            return lax.fori_loop(0, N_SLABS // slabs_per_chunk, body, jnp.zeros((1, tq), f32))

        key = jnp.where(count(lambda b: b >= 0.0) >= kf, jnp.int32(0), int_min)

        def bit(i, key):
            cand = key + jnp.left_shift(jnp.int32(1), 30 - i)
            cand_f = key_to_score(cand)
            return jnp.where(count(lambda b: b >= cand_f) >= kf, cand, key)

        thr_f = key_to_score(lax.fori_loop(0, 31, bit, key))
        write_bias(thr_f, kf - count(lambda b: b > thr_f))

    def bias_fn(k0):
        return bias_scr[:, pl.ds(k0, KC)]

    qd = qd_ref[0]
    q4s = [_stack_group_queries(qd, g, lo64) for g in range(KV_GROUPS)]
    outs = _masked_flash(q4s, kd_ref, vd_ref, n_chunks, [bias_fn] * KV_GROUPS, tq,
                         bound_ref, (shift0_scr, shift1_scr))
    o = jnp.concatenate([_group_output(og, lo64, tq) for og in outs], axis=1) * zd_ref[0].astype(f32)
    o_ref[0] = o.astype(bf16)


def _dsa_attention(bound, qd, qi, misc, ki, kd, vd, zd):
    B, S, _ = qd.shape
    tq = TQ_DSA
    assert S <= N_SLABS * LANES
    tri = jnp.asarray(np.triu(np.ones((KC, KC), np.float32)), bf16)
    row = lambda c: pl.BlockSpec((1, tq, c), lambda b, i: (b, i, 0))
    whole = pl.BlockSpec((1, KV_GROUPS, S, LANES), lambda b, i: (b, 0, 0, 0))
    return pl.pallas_call(
        functools.partial(_dsa_kernel, top_k=min(DSA_TOPK_MAX, S // 4), tq=tq),
        grid=(B, S // tq),
        in_specs=[_SMEM_SCALAR, row(512), row(256), row(LANES), pl.BlockSpec((1, S, LANES), lambda b, i: (b, 0, 0)),
                  whole, whole, row(512), pl.BlockSpec((KC, KC), lambda b, i: (0, 0))],
        out_specs=row(512),
        out_shape=jax.ShapeDtypeStruct((B, S, 512), bf16),
        scratch_shapes=[pltpu.VMEM((N_SLABS, LANES, tq), f32), pltpu.VMEM((N_SLABS, LANES, tq), jnp.int32),
                        pltpu.VMEM((tq, S), f32)] + _shift_scratch(tq),
        compiler_params=pltpu.CompilerParams(
            dimension_semantics=("arbitrary", "arbitrary"), vmem_limit_bytes=VMEM_LIMIT),
        name="dsa_attention",
    )(bound, qd, qi, misc, ki, kd, vd, zd, tri)


def _outproj_kernel(x_ref, on_ref, od_ref, w_ref, o_ref):
    half = w_ref.shape[0] // 2
    o_ref[0] = x_ref[0] + _dot(on_ref[0], w_ref[0:half, :]) + _dot(od_ref[0], w_ref[half:, :])


def _out_projection(x, o_nsa, o_dsa, w_out):
    B, S, D = x.shape
    tm = min(TM_PROJ, S)
    row = lambda c: pl.BlockSpec((1, tm, c), lambda b, i: (b, i, 0))
    return pl.pallas_call(
        _outproj_kernel,
        grid=(B, S // tm),
        in_specs=[row(D), row(512), row(512), pl.BlockSpec(w_out.shape, lambda b, i: (0, 0))],
        out_specs=row(D),
        out_shape=jax.ShapeDtypeStruct((B, S, D), f32),
        compiler_params=pltpu.CompilerParams(
            dimension_semantics=("arbitrary", "arbitrary"), vmem_limit_bytes=VMEM_LIMIT),
        name="out_projection",
    )(x, o_nsa, o_dsa, w_out)


def _rope_tables(pos):
    half = HEAD_DIM // 2
    inv_freq = ROPE_THETA ** (-jnp.arange(half, dtype=f32) / half)
    ang = pos[:, None] * inv_freq[None, :]
    cos, sin = jnp.cos(ang), jnp.sin(ang)
    cos_t = jnp.tile(cos, (1, LANES // half))
    sin_t = jnp.tile(jnp.concatenate([-sin, sin], axis=1), (1, LANES // HEAD_DIM))
    return cos_t, sin_t


def _overlap_t(seq, n_cmp_pad):
    n_cmp = (seq - CMP_BLOCK) // CMP_STRIDE + 1
    n_sel = seq // SEL_BLOCK
    c_start = np.arange(n_cmp) * CMP_STRIDE
    j_start = np.arange(n_sel) * SEL_BLOCK
    ov = np.clip(np.minimum(c_start[:, None] + CMP_BLOCK, j_start[None, :] + SEL_BLOCK)
                 - np.maximum(c_start[:, None], j_start[None, :]), 0, None).astype(np.float32) / CMP_BLOCK
    out = np.zeros((n_sel, n_cmp_pad), np.float32)
    out[:, :n_cmp] = ov.T
    return out


def _block_diag2(w):
    z = jnp.zeros_like(w)
    return jnp.concatenate([jnp.concatenate([w, z], axis=-1), jnp.concatenate([z, w], axis=-1)], axis=-2)


def _layer(x, norm_gain, w_in, nsa_q_gain, nsa_kc_gain, nsa_ks_gain, nsa_kw_gain,
           cmp_pe_k, cmp_k_w1, cmp_k_b1, cmp_k_w2, cmp_pe_v, cmp_v_w1, cmp_v_b1, cmp_v_w2,
           dsa_q_gain, dsa_k_gain, w_out):
    B, S, _ = x.shape
    assert S % KC == 0 and S >= WINDOW + TQ_NSA
    n_cmp_pad = S // CMP_STRIDE

    w_perm = _permuted_weight(w_in)
    head_of_lane = np.arange(2 * LANES) // HEAD_DIM
    bd = jnp.asarray(head_of_lane[:, None] == head_of_lane[None, :], bf16)
    dup = lambda v: jnp.tile(v.reshape(1, -1), (1, 2))
    quad = lambda v: jnp.tile(v.reshape(1, -1), (1, 4))
    gains = jnp.concatenate([quad(nsa_q_gain), jnp.concatenate([dup(nsa_ks_gain), dup(nsa_kw_gain)], axis=1),
                             quad(dsa_q_gain), quad(dsa_k_gain), jnp.ones((4, 2 * LANES), f32)], axis=0)
    cos_t, sin_t = _rope_tables(jnp.arange(S, dtype=f32))

    (qn, qd, qi, ki, ks, kw, kd, vs, vw, vd, kc, vc, zn, zd, misc) = _in_projection(
        x, norm_gain.reshape(1, -1), w_perm, bd, cos_t, sin_t, gains)

    cmp_pos = (jnp.arange(n_cmp_pad) * CMP_STRIDE + CMP_BLOCK - 1).astype(f32)
    cos_c, sin_c = _rope_tables(cmp_pos)

    def cmp_weights(pe, w1, b1, w2):
        w1_bd = _block_diag2(w1.reshape(CMP_BLOCK, HEAD_DIM, CMP_HIDDEN)).astype(bf16)
        return jnp.tile(pe, (1, 2)), w1_bd, dup(b1), _block_diag2(w2).astype(bf16)

    kcmp = _compress(kc, *cmp_weights(cmp_pe_k, cmp_k_w1, cmp_k_b1, cmp_k_w2), bd[:LANES, :LANES], dup(nsa_kc_gain),
                     cos_c, sin_c, is_key=True)
    vcmp = _compress(vc, *cmp_weights(cmp_pe_v, cmp_v_w1, cmp_v_b1, cmp_v_w2), bd[:LANES, :LANES], dup(nsa_kc_gain),
                     cos_c, sin_c, is_key=False)

    ovt = jnp.asarray(_overlap_t(S, n_cmp_pad), bf16)
    o_nsa = _nsa_attention(_score_bound(nsa_q_gain, nsa_kc_gain, nsa_ks_gain, nsa_kw_gain),
                           qn, kcmp, vcmp, ks, vs, kw, vw, misc, zn, ovt)

    o_dsa = _dsa_attention(_score_bound(dsa_q_gain, dsa_k_gain), qd, qi, misc, ki, kd, vd, zd)

    return _out_projection(x, o_nsa, o_dsa, w_out.astype(bf16))


def kernel(x, norm_gain, w_in, nsa_q_gain, nsa_kc_gain, nsa_ks_gain, nsa_kw_gain, cmp_pe_k, cmp_k_w1,
           cmp_k_b1, cmp_k_w2, cmp_pe_v, cmp_v_w1, cmp_v_b1, cmp_v_w2, dsa_q_gain, dsa_k_gain, w_out):
    for l in range(norm_gain.shape[0]):
        x = _layer(x, norm_gain[l], w_in[l], nsa_q_gain[l], nsa_kc_gain[l], nsa_ks_gain[l], nsa_kw_gain[l],
                   cmp_pe_k[l], cmp_k_w1[l], cmp_k_b1[l], cmp_k_w2[l], cmp_pe_v[l], cmp_v_w1[l], cmp_v_b1[l],
                   cmp_v_w2[l], dsa_q_gain[l], dsa_k_gain[l], w_out[l])
    return x
```

```python
import functools

import numpy as np
import jax
import jax.numpy as jnp
from jax import lax
from jax.experimental import pallas as pl
from jax.experimental.pallas import tpu as pltpu

D_MODEL = 1024
HEAD_DIM = 64
NSA_HEADS = 8
DSA_HEADS = 8
KV_GROUPS = 2
HEADS_PER_GROUP = 4
IDX_HEADS = 4
CMP_BLOCK = 32
CMP_STRIDE = 16
CMP_HIDDEN = 256
SEL_BLOCK = 64
SEL_TOPN = 16
WINDOW = 512
DSA_TOPK_MAX = 256
ROPE_THETA = 10000.0
EPS = 1e-6
NEG = -1e30
FORCE = 1e6
ATTN_SCALE = HEAD_DIM ** -0.5
IDX_SCALE = HEAD_DIM ** -0.5
WI_SCALE = IDX_HEADS ** -0.5
LOG2E = 1.4426950408889634
Q_SCALE = ATTN_SCALE * LOG2E

LANES = 128
SUBLANES = 8
M_FLOOR = -5e29
NON_CAUSAL_MARK = -3.0e38
MAX_FIXED_SHIFT = 30.0
NORM_SLACK = 1.03
VMEM_LIMIT = 56 * 1024 * 1024

IN_WIDTHS = (512, 128, 128, 128, 128, 128, 128, 24, 512, 512, 128, 128, 256, 64, 4, 512)
IN_NAMES = ("q_n", "kc", "vc", "ks", "vs", "kw", "vw", "gate", "z_n",
            "q_d", "k_d", "v_d", "qi", "ki", "wi", "z_d")
IN_COLS = sum(IN_WIDTHS)

CH_QN, CH_KS, CH_KW, CH_QD, CH_KD, CH_QI, CH_KI = 0, 4, 5, 6, 10, 11, 13
CH_KC, CH_VC, CH_VS, CH_VW, CH_VD, CH_ZN, CH_ZD, CH_MISC = 14, 15, 16, 17, 18, 19, 23, 27
N_CHUNKS = 28
MISC_WI = 24

TM_PROJ = 512
TQ_NSA = 256
TQ_DSA = 256
KC = 512

f32 = jnp.float32
bf16 = jnp.bfloat16


def _repack_weight(w_ref, w16_ref):
    off = dict(zip(IN_NAMES, np.cumsum((0,) + IN_WIDTHS[:-1])))
    width = dict(zip(IN_NAMES, IN_WIDTHS))
    order = ("q_n", "ks", "kw", "q_d", "k_d", "qi", "ki", "ki", "kc", "vc", "vs", "vw", "v_d", "z_n", "z_d",
             "gate", "wi")
    dst = 0
    for n in order:
        w16_ref[:, dst:dst + width[n]] = w_ref[:, off[n]:off[n] + width[n]].astype(bf16)
        dst += width[n]
    w16_ref[:, dst:] = jnp.zeros((D_MODEL, N_CHUNKS * LANES - dst), bf16)


def _nt_dot(a, b):
    return lax.dot_general(a, b, (((1,), (1,)), ((), ())), preferred_element_type=f32)


def _dot(a, b):
    return jnp.dot(a, b, preferred_element_type=f32)


def _split_bf16(v):
    hi = v.astype(bf16)
    lo = (v - hi.astype(f32)).astype(bf16)
    return hi, lo


def _head_rms_norm(y, bd, gain_row):
    ssq = _dot((y * y).astype(bf16), bd)
    return y * lax.rsqrt(ssq * (1.0 / HEAD_DIM) + EPS) * gain_row


def _rope(y, cos, sin_signed, lo32):
    partner = jnp.where(lo32, pltpu.roll(y, LANES - 32, 1), pltpu.roll(y, 32, 1))
    return y * cos + partner * sin_signed


def _dup_halves(y, lo64):
    r = pltpu.roll(y, 64, 1)
    return jnp.where(lo64, y, r), jnp.where(lo64, r, y)


def _value_with_ones(y, lo64):
    return jnp.where(lo64, y, 1.0), jnp.where(lo64, pltpu.roll(y, 64, 1), 1.0)


def _inproj_kernel(x_ref, ng_ref, w_ref, bd_ref, cos_ref, sin_ref, gains_ref,
                   qn_ref, qd_ref, qi_ref, ki_ref, ks_ref, kw_ref, kd_ref,
                   vs_ref, vw_ref, vd_ref, kc_ref, vc_ref, zn_ref, zd_ref, misc_ref, w16_ref):
    @pl.when((pl.program_id(0) == 0) & (pl.program_id(1) == 0))
    def _():
        _repack_weight(w_ref, w16_ref)

    x = x_ref[0]
    ms = jnp.mean(x * x, axis=-1, keepdims=True)
    h = (x * lax.rsqrt(ms + EPS) * ng_ref[...]).astype(bf16)
    cos = cos_ref[...]
    sin = sin_ref[...]
    bd = bd_ref[...]
    lane = lax.broadcasted_iota(jnp.int32, (1, LANES), 1)
    lo32 = (lane % 64) < 32
    lo64 = lane < 64

    def proj(c0, n):
        return _dot(h, w16_ref[:, c0 * LANES:(c0 + n) * LANES])

    def chunk(y, i):
        return y[:, i * LANES:(i + 1) * LANES]

    def normed_rope(y, gain_idx):
        w = y.shape[1]
        yn = _head_rms_norm(y, bd[:w, :w], gains_ref[gain_idx:gain_idx + 1, 0:w])
        return [_rope(chunk(yn, i), cos, sin, lo32) for i in range(w // LANES)]

    for c0, gain_idx, ref in ((CH_QN, 0, qn_ref), (CH_QD, 2, qd_ref)):
        y = proj(c0, 4)
        for i in range(2):
            for j, r in enumerate(normed_rope(y[:, 2 * i * LANES:(2 * i + 2) * LANES], gain_idx)):
                ref[0, :, (2 * i + j) * LANES:(2 * i + j + 1) * LANES] = (r * Q_SCALE).astype(bf16)
    y = proj(CH_QI, 2)
    for i in range(2):
        qi_ref[0, :, i * LANES:(i + 1) * LANES] = (_rope(chunk(y, i), cos, sin, lo32) * IDX_SCALE).astype(bf16)
    ki_ref[0] = _rope(proj(CH_KI, 1), cos, sin, lo32).astype(bf16)

    k_pairs = normed_rope(proj(CH_KS, 2), 1) + normed_rope(proj(CH_KD, 1), 3)
    for r, ref in zip(k_pairs, (ks_ref, kw_ref, kd_ref)):
        a, b = _dup_halves(r, lo64)
        ref[0, 0] = a.astype(bf16)
        ref[0, 1] = b.astype(bf16)
    for c0, ref in ((CH_VS, vs_ref), (CH_VW, vw_ref), (CH_VD, vd_ref)):
        a, b = _value_with_ones(proj(c0, 1), lo64)
        ref[0, 0] = a.astype(bf16)
        ref[0, 1] = b.astype(bf16)

    kc_ref[0] = proj(CH_KC, 1)
    vc_ref[0] = proj(CH_VC, 1)
    z = proj(CH_ZN, 4)
    zn_ref[0] = (z * jax.nn.sigmoid(z)).astype(bf16)
    z = proj(CH_ZD, 4)
    zd_ref[0] = (z * jax.nn.sigmoid(z)).astype(bf16)
    m = proj(CH_MISC, 1)
    misc_ref[0] = jnp.where(lane < MISC_WI, jax.nn.sigmoid(m), m * WI_SCALE)


def _in_projection(x, norm_gain, w_in, bd, cos_t, sin_t, gains):
    B, S, _ = x.shape
    tm = min(TM_PROJ, S)
    grid = (B, S // tm)
    row = lambda c: pl.BlockSpec((1, tm, c), lambda b, i: (b, i, 0))
    dup = pl.BlockSpec((1, KV_GROUPS, tm, LANES), lambda b, i: (b, 0, i, 0))
    const = lambda shape: pl.BlockSpec(shape, lambda b, i: tuple(0 for _ in shape))
    tab = pl.BlockSpec((tm, LANES), lambda b, i: (i, 0))
    sds = jax.ShapeDtypeStruct
    out_shape = (
        sds((B, S, 512), bf16), sds((B, S, 512), bf16), sds((B, S, 256), bf16), sds((B, S, LANES), bf16),
        sds((B, KV_GROUPS, S, LANES), bf16), sds((B, KV_GROUPS, S, LANES), bf16),
        sds((B, KV_GROUPS, S, LANES), bf16), sds((B, KV_GROUPS, S, LANES), bf16),
        sds((B, KV_GROUPS, S, LANES), bf16), sds((B, KV_GROUPS, S, LANES), bf16),
        sds((B, S, LANES), f32), sds((B, S, LANES), f32),
        sds((B, S, 512), bf16), sds((B, S, 512), bf16), sds((B, S, LANES), f32),
    )
    out_specs = (row(512), row(512), row(256), row(LANES), dup, dup, dup, dup, dup, dup,
                 row(LANES), row(LANES), row(512), row(512), row(LANES))
    return pl.pallas_call(
        _inproj_kernel,
        grid=grid,
        in_specs=[row(D_MODEL), const((1, D_MODEL)),
                  pl.BlockSpec((D_MODEL, IN_COLS), lambda b, i: (0, 0), pipeline_mode=pl.Buffered(1)),
                  const((2 * LANES, 2 * LANES)), tab, tab, const((8, 2 * LANES))],
        out_specs=out_specs,
        out_shape=out_shape,
        scratch_shapes=[pltpu.VMEM((D_MODEL, N_CHUNKS * LANES), bf16)],
        compiler_params=pltpu.CompilerParams(
            dimension_semantics=("arbitrary", "arbitrary"), vmem_limit_bytes=VMEM_LIMIT),
        name="in_projection",
    )(x, norm_gain, w_in, bd, cos_t, sin_t, gains)


def _compress_kernel(src_ref, pe_ref, w1_ref, b1_ref, w2_ref, bd_ref, gain_ref, cos_ref, sin_ref,
                     out_ref, *, n_rows, is_key):
    half = CMP_BLOCK // 2
    acc_a = jnp.zeros((n_rows, 2 * CMP_HIDDEN), f32)
    acc_b = jnp.zeros((n_rows, 2 * CMP_HIDDEN), f32)
    for l in range(half):
        rows = src_ref[0, pl.ds(l, n_rows, stride=CMP_STRIDE), :]
        acc_a = acc_a + _dot((rows + pe_ref[l:l + 1, :]).astype(bf16), w1_ref[l])
        acc_b = acc_b + _dot((rows + pe_ref[l + half:l + half + 1, :]).astype(bf16), w1_ref[l + half])
    pre = acc_a + pltpu.roll(acc_b, n_rows - 1, 0) + b1_ref[...]
    hid = pre * jax.nn.sigmoid(pre)
    out = _dot(hid.astype(bf16), w2_ref[...])
    lane = lax.broadcasted_iota(jnp.int32, (1, LANES), 1)
    if is_key:
        out = _head_rms_norm(out, bd_ref[...], gain_ref[...])
        out = _rope(out, cos_ref[...], sin_ref[...], (lane % 64) < 32)
        a, b = _dup_halves(out, lane < 64)
    else:
        a, b = _value_with_ones(out, lane < 64)
    out_ref[0, 0] = a.astype(bf16)
    out_ref[0, 1] = b.astype(bf16)


def _compress(src, pe_dup, w1_bd, b1_dup, w2_bd, bd, gain_dup, cos_c, sin_c, *, is_key):
    B, S, _ = src.shape
    n_rows = S // CMP_STRIDE
    const = lambda shape: pl.BlockSpec(shape, lambda b: tuple(0 for _ in shape))
    return pl.pallas_call(
        functools.partial(_compress_kernel, n_rows=n_rows, is_key=is_key),
        grid=(B,),
        in_specs=[pl.BlockSpec((1, S, LANES), lambda b: (b, 0, 0)),
                  const((CMP_BLOCK, LANES)), const((CMP_BLOCK, LANES, 2 * CMP_HIDDEN)),
                  const((1, 2 * CMP_HIDDEN)), const((2 * CMP_HIDDEN, LANES)), const((LANES, LANES)),
                  const((1, LANES)), const((n_rows, LANES)), const((n_rows, LANES))],
        out_specs=pl.BlockSpec((1, KV_GROUPS, n_rows, LANES), lambda b: (b, 0, 0, 0)),
        out_shape=jax.ShapeDtypeStruct((B, KV_GROUPS, n_rows, LANES), bf16),
        compiler_params=pltpu.CompilerParams(
            dimension_semantics=("arbitrary",), vmem_limit_bytes=VMEM_LIMIT),
        name="compress_k" if is_key else "compress_v",
    )(src, pe_dup, w1_bd, b1_dup, w2_bd, bd, gain_dup, cos_c, sin_c)


def _half_masked(pair, h, lo64):
    keep = lo64 if h % 2 == 0 else jnp.logical_not(lo64)
    return jnp.where(keep, pair, jnp.zeros_like(pair))


def _stack_group_queries(q, g, lo64):
    slabs = []
    for r in range(HEADS_PER_GROUP):
        h = g * HEADS_PER_GROUP + r
        slabs.append(_half_masked(q[:, (h // 2) * LANES:(h // 2 + 1) * LANES], h, lo64))
    return jnp.concatenate(slabs, axis=0)


def _masked_flash(q4s, k_ref, v_ref, n_chunks, bias_fns, tq, bound_ref, shift_scrs):
    rows = HEADS_PER_GROUP * tq
    bound = bound_ref[0]

    def scores(g, k0):
        s = _nt_dot(q4s[g], k_ref[0, g, pl.ds(k0, KC), :])
        return (s.reshape(HEADS_PER_GROUP, tq, KC) + bias_fns[g](k0)[None]).reshape(rows, KC)

    for g in range(KV_GROUPS):
        shift_scrs[g][...] = jnp.full((rows, 1), bound, f32)

    @pl.when(bound > MAX_FIXED_SHIFT)
    def _():
        def body(c, ms):
            k0 = pl.multiple_of(c * KC, KC)
            return tuple(jnp.maximum(ms[g], jnp.max(scores(g, k0), axis=-1, keepdims=True))
                         for g in range(KV_GROUPS))

        init = tuple(jnp.full((rows, 1), M_FLOOR, f32) for _ in range(KV_GROUPS))
        for g, m in enumerate(lax.fori_loop(0, n_chunks, body, init)):
            shift_scrs[g][...] = m

    def body(c, accs):
        k0 = pl.multiple_of(c * KC, KC)
        new = []
        for g in range(KV_GROUPS):
            p = jnp.exp2(scores(g, k0) - shift_scrs[g][...]).astype(bf16)
            new.append(accs[g] + _dot(p, v_ref[0, g, pl.ds(k0, KC), :]))
        return tuple(new)

    init = tuple(jnp.zeros((rows, LANES), f32) for _ in range(KV_GROUPS))
    out = lax.fori_loop(0, n_chunks, body, init)
    return [acc / acc[:, HEAD_DIM:HEAD_DIM + 1] for acc in out]


def _group_output(og, lo64, tq):
    pairs = []
    for p in range(2):
        even = og[(2 * p) * tq:(2 * p + 1) * tq]
        odd = og[(2 * p + 1) * tq:(2 * p + 2) * tq]
        pairs.append(jnp.where(lo64, even, pltpu.roll(odd, HEAD_DIM, 1)))
    return jnp.concatenate(pairs, axis=1)


def _top_n_blocks(imp, n_sel, top_n):
    groups = [imp[SUBLANES * a:SUBLANES * (a + 1)] for a in range(n_sel // SUBLANES)]
    jsub = lax.broadcasted_iota(jnp.int32, groups[0].shape, 0)
    ranks = [jnp.zeros(groups[0].shape, f32) for _ in groups]
    for i in range(n_sel):
        row = imp[i:i + 1, :]
        for a, grp in enumerate(groups):
            if SUBLANES * a > i:
                beats = row >= grp
            elif SUBLANES * (a + 1) - 1 < i:
                beats = row > grp
            else:
                tie = jnp.where(jsub + SUBLANES * a > i, 1.0, 0.0)
                ranks[a] = ranks[a] + jnp.where(row > grp, 1.0, jnp.where(row == grp, tie, 0.0))
                continue
            ranks[a] = ranks[a] + jnp.where(beats, 1.0, 0.0)
    rank = jnp.concatenate(ranks, axis=0)
    return jnp.where(rank < top_n, 1.0, 0.0)


def _nsa_kernel(bound_ref, qn_ref, kcmp_ref, vcmp_ref, ks_ref, vs_ref, kw_ref, vw_ref, misc_ref, zn_ref, ovt_ref,
                expand_ref, o_ref, shift0_scr, shift1_scr, cshift0_scr, cshift1_scr, wshift0_scr, wshift1_scr,
                bias_scr, *, n_cmp_pad, n_sel, top_n):
    tq = TQ_NSA
    cmp_shift = [cshift0_scr, cshift1_scr]
    win_shift = [wshift0_scr, wshift1_scr]
    t0 = pl.program_id(1) * tq
    lane = lax.broadcasted_iota(jnp.int32, (1, LANES), 1)
    lo64 = lane < 64
    q = qn_ref[0]
    misc = misc_ref[0]
    t_col = t0 + lax.broadcasted_iota(jnp.int32, (tq, 1), 0)
    n_chunks = (t0 + tq + KC - 1) // KC
    win_span = WINDOW + tq
    win_start = pl.multiple_of(jnp.maximum(t0 - WINDOW, 0), tq)
    ovt = ovt_ref[...]
    rows = HEADS_PER_GROUP * tq

    def add_bias(s, bias):
        return (s.reshape(HEADS_PER_GROUP, tq, s.shape[-1]) + bias[None]).reshape(rows, s.shape[-1])

    cmp_end = lax.broadcasted_iota(jnp.int32, (1, n_cmp_pad), 1) * CMP_STRIDE + (CMP_BLOCK - 1)
    cmp_bias = jnp.where(cmp_end <= t_col, 0.0, NEG)
    win_diff = t_col - (win_start + lax.broadcasted_iota(jnp.int32, (1, win_span), 1))
    win_bias = jnp.where((win_diff >= 0) & (win_diff < WINDOW), 0.0, NEG)
    q4s = [_stack_group_queries(q, g, lo64) for g in range(KV_GROUPS)]

    def cmp_scores(g):
        return add_bias(_nt_dot(q4s[g], kcmp_ref[0, g]), cmp_bias)

    def win_scores(g):
        return add_bias(_nt_dot(q4s[g], kw_ref[0, g, pl.ds(win_start, win_span), :]), win_bias)

    for scr in cmp_shift + win_shift:
        scr[...] = jnp.full((rows, 1), bound_ref[0], f32)

    @pl.when(bound_ref[0] > MAX_FIXED_SHIFT)
    def _():
        for g in range(KV_GROUPS):
            cmp_shift[g][...] = jnp.maximum(jnp.max(cmp_scores(g), axis=-1, keepdims=True), M_FLOOR)
            win_shift[g][...] = jnp.max(win_scores(g), axis=-1, keepdims=True)

    o_cmp, o_win, sels = [], [], []
    for g in range(KV_GROUPS):
        e = jnp.exp2(cmp_scores(g) - cmp_shift[g][...])
        l = jnp.sum(e, axis=-1, keepdims=True)
        p = e * (1.0 / jnp.maximum(l, 1e-30))
        o_cmp.append(_dot(p.astype(bf16), vcmp_ref[0, g]))

        psum = p[0:tq] + p[tq:2 * tq] + p[2 * tq:3 * tq] + p[3 * tq:4 * tq]
        hi, lo = _split_bf16(psum)
        imp = _nt_dot(ovt, hi) + _nt_dot(ovt, lo)
        jrow = lax.broadcasted_iota(jnp.int32, (n_sel, tq), 0)
        tl = t0 + lax.broadcasted_iota(jnp.int32, (n_sel, tq), 1)
        cur = lax.shift_right_logical(tl, 6)
        forced = (jrow == 0) | (jrow == cur) | (jrow == cur - 1)
        imp = jnp.where(forced, FORCE, imp)
        imp = jnp.where(jrow * SEL_BLOCK <= tl, imp, NEG)
        pad_row = lax.broadcasted_iota(jnp.int32, (LANES - n_sel, tq), 0)
        chosen = jnp.concatenate([_top_n_blocks(imp, n_sel, top_n) * -NEG, jnp.where(pad_row == 0, NEG, 0.0)], axis=0)
        sels.append(chosen.T.astype(bf16))

        e = jnp.exp2(win_scores(g) - win_shift[g][...])
        acc = _dot(e.astype(bf16), vw_ref[0, g, pl.ds(win_start, win_span), :])
        o_win.append(acc / acc[:, HEAD_DIM:HEAD_DIM + 1])

    def bias_body(c, carry):
        k0 = pl.multiple_of(c * KC, KC)
        for g in range(KV_GROUPS):
            bias_scr[g, :, pl.ds(k0, KC)] = _dot(sels[g], expand_ref[:, pl.ds(k0, KC)])
        return carry

    lax.fori_loop(0, n_chunks, bias_body, 0)
    k_last = pl.multiple_of((n_chunks - 1) * KC, KC)
    causal = jnp.where(k_last + lax.broadcasted_iota(jnp.int32, (1, KC), 1) <= t_col, 0.0, NEG)
    for g in range(KV_GROUPS):
        bias_scr[g, :, pl.ds(k_last, KC)] = bias_scr[g, :, pl.ds(k_last, KC)] + causal

    o_sel = _masked_flash(q4s, ks_ref, vs_ref, n_chunks,
                          [lambda k0, g=g: bias_scr[g, :, pl.ds(k0, KC)] for g in range(KV_GROUPS)], tq,
                          bound_ref, (shift0_scr, shift1_scr))

    group_outs = []
    for g in range(KV_GROUPS):
        def gate(branch):
            cols = [misc[:, (g * HEADS_PER_GROUP + r) * 3 + branch:(g * HEADS_PER_GROUP + r) * 3 + branch + 1]
                    for r in range(HEADS_PER_GROUP)]
            return jnp.concatenate(cols, axis=0)

        og = gate(0) * o_cmp[g] + gate(1) * o_sel[g] + gate(2) * o_win[g]
        group_outs.append(_group_output(og, lo64, tq))

    o = jnp.concatenate(group_outs, axis=1) * zn_ref[0].astype(f32)
    o_ref[0] = o.astype(bf16)


def _score_bound(q_gain, *k_gains):
    k_max = functools.reduce(jnp.maximum, [jnp.max(jnp.abs(g)) for g in k_gains])
    return (Q_SCALE * HEAD_DIM * NORM_SLACK * jnp.max(jnp.abs(q_gain)) * k_max).reshape(1)


_SMEM_SCALAR = pl.BlockSpec(memory_space=pltpu.SMEM)


def _shift_scratch(tq):
    return [pltpu.VMEM((HEADS_PER_GROUP * tq, 1), f32)] * KV_GROUPS


def _nsa_attention(bound, qn, kcmp, vcmp, ks, vs, kw, vw, misc, zn, ovt):
    B, S, _ = qn.shape
    tq = TQ_NSA
    n_cmp_pad = kcmp.shape[2]
    n_sel = S // SEL_BLOCK
    row = lambda c: pl.BlockSpec((1, tq, c), lambda b, i: (b, i, 0))
    whole = lambda n: pl.BlockSpec((1, KV_GROUPS, n, LANES), lambda b, i: (b, 0, 0, 0))
    kern = functools.partial(_nsa_kernel, n_cmp_pad=n_cmp_pad, n_sel=n_sel, top_n=min(SEL_TOPN, n_sel))
    assert n_sel < LANES
    expand = np.zeros((LANES, S), np.float32)
    expand[np.arange(S) // SEL_BLOCK, np.arange(S)] = 1.0
    expand[n_sel] = 1.0
    expand = jnp.asarray(expand, bf16)
    return pl.pallas_call(
        kern,
        grid=(B, S // tq),
        in_specs=[_SMEM_SCALAR, row(512), whole(n_cmp_pad), whole(n_cmp_pad), whole(S), whole(S), whole(S),
                  whole(S), row(LANES), row(512), pl.BlockSpec((n_sel, n_cmp_pad), lambda b, i: (0, 0)),
                  pl.BlockSpec((LANES, S), lambda b, i: (0, 0))],
        out_specs=row(512),
        out_shape=jax.ShapeDtypeStruct((B, S, 512), bf16),
        scratch_shapes=3 * _shift_scratch(tq) + [pltpu.VMEM((KV_GROUPS, tq, S), f32)],
        compiler_params=pltpu.CompilerParams(
            dimension_semantics=("arbitrary", "arbitrary"), vmem_limit_bytes=VMEM_LIMIT),
        name="nsa_attention",
    )(bound, qn, kcmp, vcmp, ks, vs, kw, vw, misc, zn, ovt, expand)


N_SLABS = 32
_TRANSPOSE_STAGES = ((16, 0x0000FFFF), (8, 0x00FF00FF), (4, 0x0F0F0F0F), (2, 0x33333333), (1, 0x55555555))


def _ordered_bits(v):
    return v ^ (jnp.right_shift(v, 31) & jnp.int32(0x7FFFFFFF))


def _bit_planes(score_ref, plane_ref):
    for stage, (j, mask) in enumerate(_TRANSPOSE_STAGES):
        if stage == 0:
            load = lambda k: _ordered_bits(pltpu.bitcast(score_ref[k], jnp.int32))
        else:
            load = lambda k: plane_ref[k]

        def body(p, carry, j=j, mask=mask, load=load):
            k = jnp.left_shift(p & ~(j - 1), 1) | (p & (j - 1))
            lo = load(k)
            hi = load(k + j)
            t = (lo ^ lax.shift_right_logical(hi, j)) & mask
            plane_ref[k] = lo ^ t
            plane_ref[k + j] = hi ^ jnp.left_shift(t, j)
            return carry

        lax.fori_loop(0, N_SLABS // 2, body, 0)


def _dsa_kernel(bound_ref, qd_ref, qi_ref, misc_ref, ki_ref, kd_ref, vd_ref, zd_ref, tri_ref,
                o_ref, score_scr, plane_scr, bias_scr, shift0_scr, shift1_scr, *, top_k, tq):
    t0 = pl.program_id(1) * tq
    lane = lax.broadcasted_iota(jnp.int32, (1, LANES), 1)
    lo64 = lane < 64
    t_col = t0 + lax.broadcasted_iota(jnp.int32, (tq, 1), 0)
    t_row = t0 + lax.broadcasted_iota(jnp.int32, (1, tq), 1)
    n_chunks = (t0 + tq + KC - 1) // KC
    slabs_per_chunk = KC // LANES
    int_min = jnp.int32(-2 ** 31)

    def chunk_slabs(c):
        return pl.ds(c * slabs_per_chunk, slabs_per_chunk)

    def to_col(row):
        return jnp.broadcast_to(row, (SUBLANES, tq)).T[:, 0:1]

    qi = qi_ref[0]
    qi_heads = [_half_masked(qi[:, (h // 2) * LANES:(h // 2 + 1) * LANES], h, lo64) for h in range(IDX_HEADS)]
    misc_t = misc_ref[0].T
    wi = [misc_t[MISC_WI + h:MISC_WI + h + 1, :] for h in range(IDX_HEADS)]

    def score_chunk(c, has_later_keys):
        k0 = pl.multiple_of(c * KC, KC)
        kib = ki_ref[0, pl.ds(k0, KC), :]
        sc = jnp.zeros((KC, tq), f32)
        for h in range(IDX_HEADS):
            sc = sc + jnp.maximum(_nt_dot(kib, qi_heads[h]), 0.0) * wi[h]
        sc = jnp.where(sc == 0.0, 0.0, sc)
        if has_later_keys:
            kpos = k0 + lax.broadcasted_iota(jnp.int32, (KC, 1), 0)
            sc = jnp.where(kpos <= t_row, sc, NON_CAUSAL_MARK)
        score_scr[chunk_slabs(c)] = sc.reshape(slabs_per_chunk, LANES, tq)

    def score_body(c, carry):
        score_chunk(c, False)
        return carry

    lax.fori_loop(0, n_chunks - 1, score_body, 0)
    score_chunk(n_chunks - 1, True)

    def fill_body(c, carry):
        score_scr[chunk_slabs(c)] = jnp.full((slabs_per_chunk, LANES, tq), NON_CAUSAL_MARK, f32)
        return carry

    lax.fori_loop(n_chunks, N_SLABS // slabs_per_chunk, fill_body, 0)

    _bit_planes(score_scr, plane_scr)
    kf = float(top_k)

    def select_bit(i, carry):
        alive, above, thr = carry
        plane = plane_scr[i] ^ jnp.where(i == 0, jnp.int32(-1), jnp.int32(0))
        ones = alive & plane
        c1 = jnp.sum(lax.population_count(ones).astype(f32), axis=0, keepdims=True)
        take = (above + c1) >= kf
        alive = jnp.where(take, ones, alive ^ ones)
        above = jnp.where(take, above, above + c1)
        thr = jnp.where(take, thr | jnp.left_shift(jnp.int32(1), 31 - i), thr)
        return alive, above, thr

    init = (jnp.full((LANES, tq), -1, jnp.int32), jnp.zeros((1, tq), f32), jnp.zeros((1, tq), jnp.int32))
    _, above, thr_u = lax.fori_loop(0, 32, select_bit, init)

    def key_to_score(key_row):
        return pltpu.bitcast(_ordered_bits(key_row), f32)

    def write_bias(thr_row, need_row):
        thr = to_col(thr_row)
        need = to_col(need_row)

        def body(c, carry):
            ties_before, n_above = carry
            k0 = pl.multiple_of(c * KC, KC)
            blk = score_scr[chunk_slabs(c)].reshape(KC, tq).T
            gt = jnp.where(blk > thr, 1.0, 0.0)
            eq = jnp.where(blk == thr, 1.0, 0.0)
            prefix = ties_before + _dot(eq.astype(bf16), tri_ref[...])
            chosen = gt + jnp.where(prefix <= need, eq, 0.0)
            kpos = k0 + lax.broadcasted_iota(jnp.int32, (1, KC), 1)
            bias_scr[:, pl.ds(k0, KC)] = jnp.where(kpos <= t_col, jnp.where(chosen > 0.5, 0.0, NEG), NEG)
            return prefix[:, KC - 1:KC], n_above + jnp.sum(gt, axis=-1, keepdims=True)

        zero = jnp.zeros((tq, 1), f32)
        return lax.fori_loop(0, n_chunks, body, (zero, zero))

    ties, n_above = write_bias(key_to_score(thr_u ^ int_min), kf - above)
    verified = (n_above == to_col(above)) & (n_above < kf) & (n_above + ties >= kf)

    @pl.when(jnp.sum(jnp.where(verified, 0.0, 1.0)) > 0.0)
    def _():
        def count(pred):
            def body(c, acc):
                blk = score_scr[chunk_slabs(c)].reshape(KC, tq)
                return acc + jnp.sum(jnp.where(pred(blk), 1.0, 0.0), axis=0, keepdims=True)
            return lax.fori_loop(0, N_SLABS // slabs_per_chunk, body, jnp.zeros((1, tq), f32))

        key = jnp.where(count(lambda b: b >= 0.0) >= kf, jnp.int32(0), int_min)

        def bit(i, key):
            cand = key + jnp.left_shift(jnp.int32(1), 30 - i)
            cand_f = key_to_score(cand)
            return jnp.where(count(lambda b: b >= cand_f) >= kf, cand, key)

        thr_f = key_to_score(lax.fori_loop(0, 31, bit, key))
        write_bias(thr_f, kf - count(lambda b: b > thr_f))

    def bias_fn(k0):
        return bias_scr[:, pl.ds(k0, KC)]

    qd = qd_ref[0]
    q4s = [_stack_group_queries(qd, g, lo64) for g in range(KV_GROUPS)]
    outs = _masked_flash(q4s, kd_ref, vd_ref, n_chunks, [bias_fn] * KV_GROUPS, tq,
                         bound_ref, (shift0_scr, shift1_scr))
    o = jnp.concatenate([_group_output(og, lo64, tq) for og in outs], axis=1) * zd_ref[0].astype(f32)
    o_ref[0] = o.astype(bf16)


def _dsa_attention(bound, qd, qi, misc, ki, kd, vd, zd):
    B, S, _ = qd.shape
    tq = TQ_DSA
    assert S <= N_SLABS * LANES and KC % tq == 0
    tri = jnp.asarray(np.triu(np.ones((KC, KC), np.float32)), bf16)
    row = lambda c: pl.BlockSpec((1, tq, c), lambda b, i: (b, i, 0))
    whole = pl.BlockSpec((1, KV_GROUPS, S, LANES), lambda b, i: (b, 0, 0, 0))
    return pl.pallas_call(
        functools.partial(_dsa_kernel, top_k=min(DSA_TOPK_MAX, S // 4), tq=tq),
        grid=(B, S // tq),
        in_specs=[_SMEM_SCALAR, row(512), row(256), row(LANES), pl.BlockSpec((1, S, LANES), lambda b, i: (b, 0, 0)),
                  whole, whole, row(512), pl.BlockSpec((KC, KC), lambda b, i: (0, 0))],
        out_specs=row(512),
        out_shape=jax.ShapeDtypeStruct((B, S, 512), bf16),
        scratch_shapes=[pltpu.VMEM((N_SLABS, LANES, tq), f32), pltpu.VMEM((N_SLABS, LANES, tq), jnp.int32),
                        pltpu.VMEM((tq, S), f32)] + _shift_scratch(tq),
        compiler_params=pltpu.CompilerParams(
            dimension_semantics=("arbitrary", "arbitrary"), vmem_limit_bytes=VMEM_LIMIT),
        name="dsa_attention",
    )(bound, qd, qi, misc, ki, kd, vd, zd, tri)


def _outproj_kernel(x_ref, on_ref, od_ref, w_ref, o_ref):
    half = w_ref.shape[0] // 2
    o_ref[0] = x_ref[0] + _dot(on_ref[0], w_ref[0:half, :]) + _dot(od_ref[0], w_ref[half:, :])


def _out_projection(x, o_nsa, o_dsa, w_out):
    B, S, D = x.shape
    tm = min(TM_PROJ, S)
    row = lambda c: pl.BlockSpec((1, tm, c), lambda b, i: (b, i, 0))
    return pl.pallas_call(
        _outproj_kernel,
        grid=(B, S // tm),
        in_specs=[row(D), row(512), row(512), pl.BlockSpec(w_out.shape, lambda b, i: (0, 0))],
        out_specs=row(D),
        out_shape=jax.ShapeDtypeStruct((B, S, D), f32),
        compiler_params=pltpu.CompilerParams(
            dimension_semantics=("arbitrary", "arbitrary"), vmem_limit_bytes=VMEM_LIMIT),
        name="out_projection",
    )(x, o_nsa, o_dsa, w_out)


def _rope_tables(pos):
    half = HEAD_DIM // 2
    inv_freq = ROPE_THETA ** (-jnp.arange(half, dtype=f32) / half)
    ang = pos[:, None] * inv_freq[None, :]
    cos, sin = jnp.cos(ang), jnp.sin(ang)
    cos_t = jnp.tile(cos, (1, LANES // half))
    sin_t = jnp.tile(jnp.concatenate([-sin, sin], axis=1), (1, LANES // HEAD_DIM))
    return cos_t, sin_t


def _overlap_t(seq, n_cmp_pad):
    n_cmp = (seq - CMP_BLOCK) // CMP_STRIDE + 1
    n_sel = seq // SEL_BLOCK
    c_start = np.arange(n_cmp) * CMP_STRIDE
    j_start = np.arange(n_sel) * SEL_BLOCK
    ov = np.clip(np.minimum(c_start[:, None] + CMP_BLOCK, j_start[None, :] + SEL_BLOCK)
                 - np.maximum(c_start[:, None], j_start[None, :]), 0, None).astype(np.float32) / CMP_BLOCK
    out = np.zeros((n_sel, n_cmp_pad), np.float32)
    out[:, :n_cmp] = ov.T
    return out


def _block_diag2(w):
    z = jnp.zeros_like(w)
    return jnp.concatenate([jnp.concatenate([w, z], axis=-1), jnp.concatenate([z, w], axis=-1)], axis=-2)


def _layer(x, norm_gain, w_in, nsa_q_gain, nsa_kc_gain, nsa_ks_gain, nsa_kw_gain,
           cmp_pe_k, cmp_k_w1, cmp_k_b1, cmp_k_w2, cmp_pe_v, cmp_v_w1, cmp_v_b1, cmp_v_w2,
           dsa_q_gain, dsa_k_gain, w_out):
    B, S, _ = x.shape
    assert S % KC == 0 and S >= WINDOW + TQ_NSA
    n_cmp_pad = S // CMP_STRIDE

    head_of_lane = np.arange(2 * LANES) // HEAD_DIM
    bd = jnp.asarray(head_of_lane[:, None] == head_of_lane[None, :], bf16)
    dup = lambda v: jnp.tile(v.reshape(1, -1), (1, 2))
    quad = lambda v: jnp.tile(v.reshape(1, -1), (1, 4))
    gains = jnp.concatenate([quad(nsa_q_gain), jnp.concatenate([dup(nsa_ks_gain), dup(nsa_kw_gain)], axis=1),
                             quad(dsa_q_gain), quad(dsa_k_gain), jnp.ones((4, 2 * LANES), f32)], axis=0)
    cos_t, sin_t = _rope_tables(jnp.arange(S, dtype=f32))

    (qn, qd, qi, ki, ks, kw, kd, vs, vw, vd, kc, vc, zn, zd, misc) = _in_projection(
        x, norm_gain.reshape(1, -1), w_in, bd, cos_t, sin_t, gains)

    cmp_pos = (jnp.arange(n_cmp_pad) * CMP_STRIDE + CMP_BLOCK - 1).astype(f32)
    cos_c, sin_c = _rope_tables(cmp_pos)

    def cmp_weights(pe, w1, b1, w2):
        w1_bd = _block_diag2(w1.reshape(CMP_BLOCK, HEAD_DIM, CMP_HIDDEN)).astype(bf16)
        return jnp.tile(pe, (1, 2)), w1_bd, dup(b1), _block_diag2(w2).astype(bf16)

    kcmp = _compress(kc, *cmp_weights(cmp_pe_k, cmp_k_w1, cmp_k_b1, cmp_k_w2), bd[:LANES, :LANES], dup(nsa_kc_gain),
                     cos_c, sin_c, is_key=True)
    vcmp = _compress(vc, *cmp_weights(cmp_pe_v, cmp_v_w1, cmp_v_b1, cmp_v_w2), bd[:LANES, :LANES], dup(nsa_kc_gain),
                     cos_c, sin_c, is_key=False)

    ovt = jnp.asarray(_overlap_t(S, n_cmp_pad), bf16)
    o_nsa = _nsa_attention(_score_bound(nsa_q_gain, nsa_kc_gain, nsa_ks_gain, nsa_kw_gain),
                           qn, kcmp, vcmp, ks, vs, kw, vw, misc, zn, ovt)

    o_dsa = _dsa_attention(_score_bound(dsa_q_gain, dsa_k_gain), qd, qi, misc, ki, kd, vd, zd)

    return _out_projection(x, o_nsa, o_dsa, w_out.astype(bf16))


def kernel(x, norm_gain, w_in, nsa_q_gain, nsa_kc_gain, nsa_ks_gain, nsa_kw_gain, cmp_pe_k, cmp_k_w1,
           cmp_k_b1, cmp_k_w2, cmp_pe_v, cmp_v_w1, cmp_v_b1, cmp_v_w2, dsa_q_gain, dsa_k_gain, w_out):
    for l in range(norm_gain.shape[0]):
        x = _layer(x, norm_gain[l], w_in[l], nsa_q_gain[l], nsa_kc_gain[l], nsa_ks_gain[l], nsa_kw_gain[l],
                   cmp_pe_k[l], cmp_k_w1[l], cmp_k_b1[l], cmp_k_w2[l], cmp_pe_v[l], cmp_v_w1[l], cmp_v_b1[l],
                   cmp_v_w2[l], dsa_q_gain[l], dsa_k_gain[l], w_out[l])
    return x
```

```python
import functools

import numpy as np
import jax
import jax.numpy as jnp
from jax import lax
from jax.experimental import pallas as pl
from jax.experimental.pallas import tpu as pltpu

D_MODEL = 1024
HEAD_DIM = 64
NSA_HEADS = 8
DSA_HEADS = 8
KV_GROUPS = 2
HEADS_PER_GROUP = 4
IDX_HEADS = 4
CMP_BLOCK = 32
CMP_STRIDE = 16
CMP_HIDDEN = 256
SEL_BLOCK = 64
SEL_TOPN = 16
WINDOW = 512
DSA_TOPK_MAX = 256
ROPE_THETA = 10000.0
EPS = 1e-6
NEG = -1e30
FORCE = 1e6
ATTN_SCALE = HEAD_DIM ** -0.5
IDX_SCALE = HEAD_DIM ** -0.5
WI_SCALE = IDX_HEADS ** -0.5
LOG2E = 1.4426950408889634
Q_SCALE = ATTN_SCALE * LOG2E

LANES = 128
SUBLANES = 8
M_FLOOR = -5e29
NON_CAUSAL_MARK = -3.0e38
MAX_FIXED_SHIFT = 30.0
NORM_SLACK = 1.03
VMEM_LIMIT = 56 * 1024 * 1024

IN_WIDTHS = (512, 128, 128, 128, 128, 128, 128, 24, 512, 512, 128, 128, 256, 64, 4, 512)
IN_NAMES = ("q_n", "kc", "vc", "ks", "vs", "kw", "vw", "gate", "z_n",
            "q_d", "k_d", "v_d", "qi", "ki", "wi", "z_d")
IN_COLS = sum(IN_WIDTHS)

CH_QN, CH_KS, CH_KW, CH_QD, CH_KD, CH_QI, CH_KI = 0, 4, 5, 6, 10, 11, 13
CH_KC, CH_VC, CH_VS, CH_VW, CH_VD, CH_ZN, CH_ZD, CH_MISC = 14, 15, 16, 17, 18, 19, 23, 27
N_CHUNKS = 28
MISC_WI = 24

TM_PROJ = 512
TQ_NSA = 256
TQ_DSA = 256
KC = 512

f32 = jnp.float32
bf16 = jnp.bfloat16


def _repack_weight(w_ref, w16_ref):
    off = dict(zip(IN_NAMES, np.cumsum((0,) + IN_WIDTHS[:-1])))
    width = dict(zip(IN_NAMES, IN_WIDTHS))
    order = ("q_n", "ks", "kw", "q_d", "k_d", "qi", "ki", "ki", "kc", "vc", "vs", "vw", "v_d", "z_n", "z_d",
             "gate", "wi")
    dst = 0
    for n in order:
        w16_ref[:, dst:dst + width[n]] = w_ref[0, :, off[n]:off[n] + width[n]].astype(bf16)
        dst += width[n]
    w16_ref[:, dst:] = jnp.zeros((D_MODEL, N_CHUNKS * LANES - dst), bf16)


def _nt_dot(a, b):
    return lax.dot_general(a, b, (((1,), (1,)), ((), ())), preferred_element_type=f32)


def _dot(a, b):
    return jnp.dot(a, b, preferred_element_type=f32)


def _split_bf16(v):
    hi = v.astype(bf16)
    lo = (v - hi.astype(f32)).astype(bf16)
    return hi, lo


def _head_rms_norm(y, bd, gain_row):
    ssq = _dot((y * y).astype(bf16), bd)
    return y * lax.rsqrt(ssq * (1.0 / HEAD_DIM) + EPS) * gain_row


def _rope(y, cos, sin_signed, lo32):
    partner = jnp.where(lo32, pltpu.roll(y, LANES - 32, 1), pltpu.roll(y, 32, 1))
    return y * cos + partner * sin_signed


def _dup_halves(y, lo64):
    r = pltpu.roll(y, 64, 1)
    return jnp.where(lo64, y, r), jnp.where(lo64, r, y)


def _value_with_ones(y, lo64):
    return jnp.where(lo64, y, 1.0), jnp.where(lo64, pltpu.roll(y, 64, 1), 1.0)


def _inproj_kernel(x_ref, ng_ref, w_ref, bd_ref, cos_ref, sin_ref, gains_ref,
                   qn_ref, qd_ref, qi_ref, ki_ref, ks_ref, kw_ref, kd_ref,
                   vs_ref, vw_ref, vd_ref, kc_ref, vc_ref, zn_ref, zd_ref, misc_ref, w16_ref):
    @pl.when((pl.program_id(0) == 0) & (pl.program_id(1) == 0))
    def _():
        _repack_weight(w_ref, w16_ref)

    x = x_ref[0]
    ms = jnp.mean(x * x, axis=-1, keepdims=True)
    h = (x * lax.rsqrt(ms + EPS) * ng_ref[...]).astype(bf16)
    cos = cos_ref[...]
    sin = sin_ref[...]
    bd = bd_ref[...]
    lane = lax.broadcasted_iota(jnp.int32, (1, LANES), 1)
    lo32 = (lane % 64) < 32
    lo64 = lane < 64

    def proj(c0, n):
        return _dot(h, w16_ref[:, c0 * LANES:(c0 + n) * LANES])

    def chunk(y, i):
        return y[:, i * LANES:(i + 1) * LANES]

    def normed_rope(y, gain_idx):
        w = y.shape[1]
        yn = _head_rms_norm(y, bd[:w, :w], gains_ref[gain_idx:gain_idx + 1, 0:w])
        return [_rope(chunk(yn, i), cos, sin, lo32) for i in range(w // LANES)]

    for c0, gain_idx, ref in ((CH_QN, 0, qn_ref), (CH_QD, 2, qd_ref)):
        y = proj(c0, 4)
        for i in range(2):
            for j, r in enumerate(normed_rope(y[:, 2 * i * LANES:(2 * i + 2) * LANES], gain_idx)):
                ref[0, :, (2 * i + j) * LANES:(2 * i + j + 1) * LANES] = (r * Q_SCALE).astype(bf16)
    y = proj(CH_QI, 2)
    for i in range(2):
        qi_ref[0, :, i * LANES:(i + 1) * LANES] = (_rope(chunk(y, i), cos, sin, lo32) * IDX_SCALE).astype(bf16)
    ki_ref[0] = _rope(proj(CH_KI, 1), cos, sin, lo32).astype(bf16)

    k_pairs = normed_rope(proj(CH_KS, 2), 1) + normed_rope(proj(CH_KD, 1), 3)
    for r, ref in zip(k_pairs, (ks_ref, kw_ref, kd_ref)):
        a, b = _dup_halves(r, lo64)
        ref[0, 0] = a.astype(bf16)
        ref[0, 1] = b.astype(bf16)
    for c0, ref in ((CH_VS, vs_ref), (CH_VW, vw_ref), (CH_VD, vd_ref)):
        a, b = _value_with_ones(proj(c0, 1), lo64)
        ref[0, 0] = a.astype(bf16)
        ref[0, 1] = b.astype(bf16)

    kc_ref[0] = proj(CH_KC, 1)
    vc_ref[0] = proj(CH_VC, 1)
    z = proj(CH_ZN, 4)
    zn_ref[0] = (z * jax.nn.sigmoid(z)).astype(bf16)
    z = proj(CH_ZD, 4)
    zd_ref[0] = (z * jax.nn.sigmoid(z)).astype(bf16)
    m = proj(CH_MISC, 1)
    misc_ref[0] = jnp.where(lane < MISC_WI, jax.nn.sigmoid(m), m * WI_SCALE)


def _in_projection(x, norm_gain, w_in, bd, cos_t, sin_t, gains):
    B, S, _ = x.shape
    tm = min(TM_PROJ, S)
    grid = (B, S // tm)
    row = lambda c: pl.BlockSpec((1, tm, c), lambda b, i: (b, i, 0))
    dup = pl.BlockSpec((1, KV_GROUPS, tm, LANES), lambda b, i: (b, 0, i, 0))
    const = lambda shape: pl.BlockSpec(shape, lambda b, i: tuple(0 for _ in shape))
    tab = pl.BlockSpec((tm, LANES), lambda b, i: (i, 0))
    sds = jax.ShapeDtypeStruct
    out_shape = (
        sds((B, S, 512), bf16), sds((B, S, 512), bf16), sds((B, S, 256), bf16), sds((B, S, LANES), bf16),
        sds((B, KV_GROUPS, S, LANES), bf16), sds((B, KV_GROUPS, S, LANES), bf16),
        sds((B, KV_GROUPS, S, LANES), bf16), sds((B, KV_GROUPS, S, LANES), bf16),
        sds((B, KV_GROUPS, S, LANES), bf16), sds((B, KV_GROUPS, S, LANES), bf16),
        sds((B, S, LANES), f32), sds((B, S, LANES), f32),
        sds((B, S, 512), bf16), sds((B, S, 512), bf16), sds((B, S, LANES), f32),
    )
    out_specs = (row(512), row(512), row(256), row(LANES), dup, dup, dup, dup, dup, dup,
                 row(LANES), row(LANES), row(512), row(512), row(LANES))
    return pl.pallas_call(
        _inproj_kernel,
        grid=grid,
        in_specs=[row(D_MODEL), const((1, D_MODEL)),
                  pl.BlockSpec((1, D_MODEL, IN_COLS), lambda b, i: (0, 0, 0), pipeline_mode=pl.Buffered(1)),
                  const((2 * LANES, 2 * LANES)), tab, tab, const((8, 2 * LANES))],
        out_specs=out_specs,
        out_shape=out_shape,
        scratch_shapes=[pltpu.VMEM((D_MODEL, N_CHUNKS * LANES), bf16)],
        compiler_params=pltpu.CompilerParams(
            dimension_semantics=("arbitrary", "arbitrary"), vmem_limit_bytes=VMEM_LIMIT),
        name="in_projection",
    )(x, norm_gain, w_in, bd, cos_t, sin_t, gains)


def _compress_kernel(src_ref, pe_ref, w1_ref, b1_ref, w2_ref, bd_ref, gain_ref, cos_ref, sin_ref,
                     out_ref, *, n_rows, is_key):
    half = CMP_BLOCK // 2
    acc_a = jnp.zeros((n_rows, 2 * CMP_HIDDEN), f32)
    acc_b = jnp.zeros((n_rows, 2 * CMP_HIDDEN), f32)
    for l in range(half):
        rows = src_ref[0, pl.ds(l, n_rows, stride=CMP_STRIDE), :]
        acc_a = acc_a + _dot((rows + pe_ref[l:l + 1, :]).astype(bf16), w1_ref[l])
        acc_b = acc_b + _dot((rows + pe_ref[l + half:l + half + 1, :]).astype(bf16), w1_ref[l + half])
    pre = acc_a + pltpu.roll(acc_b, n_rows - 1, 0) + b1_ref[...]
    hid = pre * jax.nn.sigmoid(pre)
    out = _dot(hid.astype(bf16), w2_ref[...])
    lane = lax.broadcasted_iota(jnp.int32, (1, LANES), 1)
    if is_key:
        out = _head_rms_norm(out, bd_ref[...], gain_ref[...])
        out = _rope(out, cos_ref[...], sin_ref[...], (lane % 64) < 32)
        a, b = _dup_halves(out, lane < 64)
    else:
        a, b = _value_with_ones(out, lane < 64)
    out_ref[0, 0] = a.astype(bf16)
    out_ref[0, 1] = b.astype(bf16)


def _compress(src, pe_dup, w1_bd, b1_dup, w2_bd, bd, gain_dup, cos_c, sin_c, *, is_key):
    B, S, _ = src.shape
    n_rows = S // CMP_STRIDE
    const = lambda shape: pl.BlockSpec(shape, lambda b: tuple(0 for _ in shape))
    return pl.pallas_call(
        functools.partial(_compress_kernel, n_rows=n_rows, is_key=is_key),
        grid=(B,),
        in_specs=[pl.BlockSpec((1, S, LANES), lambda b: (b, 0, 0)),
                  const((CMP_BLOCK, LANES)), const((CMP_BLOCK, LANES, 2 * CMP_HIDDEN)),
                  const((1, 2 * CMP_HIDDEN)), const((2 * CMP_HIDDEN, LANES)), const((LANES, LANES)),
                  const((1, LANES)), const((n_rows, LANES)), const((n_rows, LANES))],
        out_specs=pl.BlockSpec((1, KV_GROUPS, n_rows, LANES), lambda b: (b, 0, 0, 0)),
        out_shape=jax.ShapeDtypeStruct((B, KV_GROUPS, n_rows, LANES), bf16),
        compiler_params=pltpu.CompilerParams(
            dimension_semantics=("arbitrary",), vmem_limit_bytes=VMEM_LIMIT),
        name="compress_k" if is_key else "compress_v",
    )(src, pe_dup, w1_bd, b1_dup, w2_bd, bd, gain_dup, cos_c, sin_c)


def _half_masked(pair, h, lo64):
    keep = lo64 if h % 2 == 0 else jnp.logical_not(lo64)
    return jnp.where(keep, pair, jnp.zeros_like(pair))


def _stack_group_queries(q, g, lo64):
    slabs = []
    for r in range(HEADS_PER_GROUP):
        h = g * HEADS_PER_GROUP + r
        slabs.append(_half_masked(q[:, (h // 2) * LANES:(h // 2 + 1) * LANES], h, lo64))
    return jnp.concatenate(slabs, axis=0)


def _masked_flash(q4s, k_ref, v_ref, n_chunks, bias_fns, tq, bound_ref, shift_scrs):
    rows = HEADS_PER_GROUP * tq
    bound = bound_ref[0]

    def scores(g, k0):
        s = _nt_dot(q4s[g], k_ref[0, g, pl.ds(k0, KC), :])
        return (s.reshape(HEADS_PER_GROUP, tq, KC) + bias_fns[g](k0)[None]).reshape(rows, KC)

    for g in range(KV_GROUPS):
        shift_scrs[g][...] = jnp.full((rows, 1), bound, f32)

    @pl.when(bound > MAX_FIXED_SHIFT)
    def _():
        def body(c, ms):
            k0 = pl.multiple_of(c * KC, KC)
            return tuple(jnp.maximum(ms[g], jnp.max(scores(g, k0), axis=-1, keepdims=True))
                         for g in range(KV_GROUPS))

        init = tuple(jnp.full((rows, 1), M_FLOOR, f32) for _ in range(KV_GROUPS))
        for g, m in enumerate(lax.fori_loop(0, n_chunks, body, init)):
            shift_scrs[g][...] = m

    def body(c, accs):
        k0 = pl.multiple_of(c * KC, KC)
        new = []
        for g in range(KV_GROUPS):
            p = jnp.exp2(scores(g, k0) - shift_scrs[g][...]).astype(bf16)
            new.append(accs[g] + _dot(p, v_ref[0, g, pl.ds(k0, KC), :]))
        return tuple(new)

    init = tuple(jnp.zeros((rows, LANES), f32) for _ in range(KV_GROUPS))
    out = lax.fori_loop(0, n_chunks, body, init)
    return [acc / acc[:, HEAD_DIM:HEAD_DIM + 1] for acc in out]


def _group_output(og, lo64, tq):
    pairs = []
    for p in range(2):
        even = og[(2 * p) * tq:(2 * p + 1) * tq]
        odd = og[(2 * p + 1) * tq:(2 * p + 2) * tq]
        pairs.append(jnp.where(lo64, even, pltpu.roll(odd, HEAD_DIM, 1)))
    return jnp.concatenate(pairs, axis=1)


def _top_n_blocks(imp, n_sel, top_n):
    groups = [imp[SUBLANES * a:SUBLANES * (a + 1)] for a in range(n_sel // SUBLANES)]
    jsub = lax.broadcasted_iota(jnp.int32, groups[0].shape, 0)
    ranks = [jnp.zeros(groups[0].shape, f32) for _ in groups]
    for i in range(n_sel):
        row = imp[i:i + 1, :]
        for a, grp in enumerate(groups):
            if SUBLANES * a > i:
                beats = row >= grp
            elif SUBLANES * (a + 1) - 1 < i:
                beats = row > grp
            else:
                tie = jnp.where(jsub + SUBLANES * a > i, 1.0, 0.0)
                ranks[a] = ranks[a] + jnp.where(row > grp, 1.0, jnp.where(row == grp, tie, 0.0))
                continue
            ranks[a] = ranks[a] + jnp.where(beats, 1.0, 0.0)
    rank = jnp.concatenate(ranks, axis=0)
    return jnp.where(rank < top_n, 1.0, 0.0)


def _nsa_kernel(bound_ref, qn_ref, kcmp_ref, vcmp_ref, ks_ref, vs_ref, kw_ref, vw_ref, misc_ref, zn_ref, ovt_ref,
                expand_ref, o_ref, shift0_scr, shift1_scr, cshift0_scr, cshift1_scr, wshift0_scr, wshift1_scr,
                bias_scr, *, n_cmp_pad, n_sel, top_n):
    tq = TQ_NSA
    cmp_shift = [cshift0_scr, cshift1_scr]
    win_shift = [wshift0_scr, wshift1_scr]
    t0 = pl.program_id(1) * tq
    lane = lax.broadcasted_iota(jnp.int32, (1, LANES), 1)
    lo64 = lane < 64
    q = qn_ref[0]
    misc = misc_ref[0]
    t_col = t0 + lax.broadcasted_iota(jnp.int32, (tq, 1), 0)
    n_chunks = (t0 + tq + KC - 1) // KC
    win_span = WINDOW + tq
    win_start = pl.multiple_of(jnp.maximum(t0 - WINDOW, 0), tq)
    ovt = ovt_ref[...]
    rows = HEADS_PER_GROUP * tq

    def add_bias(s, bias):
        return (s.reshape(HEADS_PER_GROUP, tq, s.shape[-1]) + bias[None]).reshape(rows, s.shape[-1])

    cmp_end = lax.broadcasted_iota(jnp.int32, (1, n_cmp_pad), 1) * CMP_STRIDE + (CMP_BLOCK - 1)
    cmp_bias = jnp.where(cmp_end <= t_col, 0.0, NEG)
    win_diff = t_col - (win_start + lax.broadcasted_iota(jnp.int32, (1, win_span), 1))
    win_bias = jnp.where((win_diff >= 0) & (win_diff < WINDOW), 0.0, NEG)
    q4s = [_stack_group_queries(q, g, lo64) for g in range(KV_GROUPS)]

    def cmp_scores(g):
        return add_bias(_nt_dot(q4s[g], kcmp_ref[0, g]), cmp_bias)

    def win_scores(g):
        return add_bias(_nt_dot(q4s[g], kw_ref[0, g, pl.ds(win_start, win_span), :]), win_bias)

    for scr in cmp_shift + win_shift:
        scr[...] = jnp.full((rows, 1), bound_ref[0], f32)

    @pl.when(bound_ref[0] > MAX_FIXED_SHIFT)
    def _():
        for g in range(KV_GROUPS):
            cmp_shift[g][...] = jnp.maximum(jnp.max(cmp_scores(g), axis=-1, keepdims=True), M_FLOOR)
            win_shift[g][...] = jnp.max(win_scores(g), axis=-1, keepdims=True)

    o_cmp, o_win, sels = [], [], []
    for g in range(KV_GROUPS):
        e = jnp.exp2(cmp_scores(g) - cmp_shift[g][...])
        l = jnp.sum(e, axis=-1, keepdims=True)
        p = e * (1.0 / jnp.maximum(l, 1e-30))
        o_cmp.append(_dot(p.astype(bf16), vcmp_ref[0, g]))

        psum = p[0:tq] + p[tq:2 * tq] + p[2 * tq:3 * tq] + p[3 * tq:4 * tq]
        hi, lo = _split_bf16(psum)
        imp = _nt_dot(ovt, hi) + _nt_dot(ovt, lo)
        jrow = lax.broadcasted_iota(jnp.int32, (n_sel, tq), 0)
        tl = t0 + lax.broadcasted_iota(jnp.int32, (n_sel, tq), 1)
        cur = lax.shift_right_logical(tl, 6)
        forced = (jrow == 0) | (jrow == cur) | (jrow == cur - 1)
        imp = jnp.where(forced, FORCE, imp)
        imp = jnp.where(jrow * SEL_BLOCK <= tl, imp, NEG)
        pad_row = lax.broadcasted_iota(jnp.int32, (LANES - n_sel, tq), 0)
        chosen = jnp.concatenate([_top_n_blocks(imp, n_sel, top_n) * -NEG, jnp.where(pad_row == 0, NEG, 0.0)], axis=0)
        sels.append(chosen.T.astype(bf16))

        e = jnp.exp2(win_scores(g) - win_shift[g][...])
        acc = _dot(e.astype(bf16), vw_ref[0, g, pl.ds(win_start, win_span), :])
        o_win.append(acc / acc[:, HEAD_DIM:HEAD_DIM + 1])

    def bias_body(c, carry):
        k0 = pl.multiple_of(c * KC, KC)
        for g in range(KV_GROUPS):
            bias_scr[g, :, pl.ds(k0, KC)] = _dot(sels[g], expand_ref[:, pl.ds(k0, KC)])
        return carry

    lax.fori_loop(0, n_chunks, bias_body, 0)
    k_last = pl.multiple_of((n_chunks - 1) * KC, KC)
    causal = jnp.where(k_last + lax.broadcasted_iota(jnp.int32, (1, KC), 1) <= t_col, 0.0, NEG)
    for g in range(KV_GROUPS):
        bias_scr[g, :, pl.ds(k_last, KC)] = bias_scr[g, :, pl.ds(k_last, KC)] + causal

    o_sel = _masked_flash(q4s, ks_ref, vs_ref, n_chunks,
                          [lambda k0, g=g: bias_scr[g, :, pl.ds(k0, KC)] for g in range(KV_GROUPS)], tq,
                          bound_ref, (shift0_scr, shift1_scr))

    group_outs = []
    for g in range(KV_GROUPS):
        def gate(branch):
            cols = [misc[:, (g * HEADS_PER_GROUP + r) * 3 + branch:(g * HEADS_PER_GROUP + r) * 3 + branch + 1]
                    for r in range(HEADS_PER_GROUP)]
            return jnp.concatenate(cols, axis=0)

        og = gate(0) * o_cmp[g] + gate(1) * o_sel[g] + gate(2) * o_win[g]
        group_outs.append(_group_output(og, lo64, tq))

    o = jnp.concatenate(group_outs, axis=1) * zn_ref[0].astype(f32)
    o_ref[0] = o.astype(bf16)


def _score_bound(q_gain, *k_gains):
    k_max = functools.reduce(jnp.maximum, [jnp.max(jnp.abs(g)) for g in k_gains])
    return (Q_SCALE * HEAD_DIM * NORM_SLACK * jnp.max(jnp.abs(q_gain)) * k_max).reshape(1)


_SMEM_SCALAR = pl.BlockSpec(memory_space=pltpu.SMEM)


def _shift_scratch(tq):
    return [pltpu.VMEM((HEADS_PER_GROUP * tq, 1), f32)] * KV_GROUPS


def _nsa_attention(bound, qn, kcmp, vcmp, ks, vs, kw, vw, misc, zn, ovt):
    B, S, _ = qn.shape
    tq = TQ_NSA
    n_cmp_pad = kcmp.shape[2]
    n_sel = S // SEL_BLOCK
    row = lambda c: pl.BlockSpec((1, tq, c), lambda b, i: (b, i, 0))
    whole = lambda n: pl.BlockSpec((1, KV_GROUPS, n, LANES), lambda b, i: (b, 0, 0, 0))
    kern = functools.partial(_nsa_kernel, n_cmp_pad=n_cmp_pad, n_sel=n_sel, top_n=min(SEL_TOPN, n_sel))
    assert n_sel < LANES
    expand = np.zeros((LANES, S), np.float32)
    expand[np.arange(S) // SEL_BLOCK, np.arange(S)] = 1.0
    expand[n_sel] = 1.0
    expand = jnp.asarray(expand, bf16)
    return pl.pallas_call(
        kern,
        grid=(B, S // tq),
        in_specs=[_SMEM_SCALAR, row(512), whole(n_cmp_pad), whole(n_cmp_pad), whole(S), whole(S), whole(S),
                  whole(S), row(LANES), row(512), pl.BlockSpec((n_sel, n_cmp_pad), lambda b, i: (0, 0)),
                  pl.BlockSpec((LANES, S), lambda b, i: (0, 0))],
        out_specs=row(512),
        out_shape=jax.ShapeDtypeStruct((B, S, 512), bf16),
        scratch_shapes=3 * _shift_scratch(tq) + [pltpu.VMEM((KV_GROUPS, tq, S), f32)],
        compiler_params=pltpu.CompilerParams(
            dimension_semantics=("arbitrary", "arbitrary"), vmem_limit_bytes=VMEM_LIMIT),
        name="nsa_attention",
    )(bound, qn, kcmp, vcmp, ks, vs, kw, vw, misc, zn, ovt, expand)


N_SLABS = 32
_TRANSPOSE_STAGES = ((16, 0x0000FFFF), (8, 0x00FF00FF), (4, 0x0F0F0F0F), (2, 0x33333333), (1, 0x55555555))


def _ordered_bits(v):
    return v ^ (jnp.right_shift(v, 31) & jnp.int32(0x7FFFFFFF))


def _bit_planes(score_ref, plane_ref, n_words):
    stages = [s for s in _TRANSPOSE_STAGES if s[0] < n_words]
    for stage, (j, mask) in enumerate(stages):
        if stage == 0:
            load = lambda k: _ordered_bits(pltpu.bitcast(score_ref[k], jnp.int32))
        else:
            load = lambda k: plane_ref[k]

        def body(p, carry, j=j, mask=mask, load=load):
            k = jnp.left_shift(p & ~(j - 1), 1) | (p & (j - 1))
            lo = load(k)
            hi = load(k + j)
            t = (lo ^ lax.shift_right_logical(hi, j)) & mask
            plane_ref[k] = lo ^ t
            plane_ref[k + j] = hi ^ jnp.left_shift(t, j)
            return carry

        lax.fori_loop(0, n_words // 2, body, 0)


def _dsa_kernel(bound_ref, qd_ref, qi_ref, misc_ref, ki_ref, kd_ref, vd_ref, zd_ref, tri_ref,
                o_ref, score_scr, plane_scr, bias_scr, shift0_scr, shift1_scr, select_scr, *, top_k, tq):
    t0 = pl.program_id(1) * tq
    lane = lax.broadcasted_iota(jnp.int32, (1, LANES), 1)
    lo64 = lane < 64
    t_col = t0 + lax.broadcasted_iota(jnp.int32, (tq, 1), 0)
    t_row = t0 + lax.broadcasted_iota(jnp.int32, (1, tq), 1)
    n_chunks = (t0 + tq + KC - 1) // KC
    slabs_per_chunk = KC // LANES
    int_min = jnp.int32(-2 ** 31)

    def chunk_slabs(c):
        return pl.ds(c * slabs_per_chunk, slabs_per_chunk)

    def to_col(row):
        return jnp.broadcast_to(row, (SUBLANES, tq)).T[:, 0:1]

    qi = qi_ref[0]
    qi_heads = [_half_masked(qi[:, (h // 2) * LANES:(h // 2 + 1) * LANES], h, lo64) for h in range(IDX_HEADS)]
    misc_t = misc_ref[0].T
    wi = [misc_t[MISC_WI + h:MISC_WI + h + 1, :] for h in range(IDX_HEADS)]

    def score_chunk(c, has_later_keys):
        k0 = pl.multiple_of(c * KC, KC)
        kib = ki_ref[0, pl.ds(k0, KC), :]
        sc = jnp.zeros((KC, tq), f32)
        for h in range(IDX_HEADS):
            sc = sc + jnp.maximum(_nt_dot(kib, qi_heads[h]), 0.0) * wi[h]
        sc = jnp.where(sc == 0.0, 0.0, sc)
        if has_later_keys:
            kpos = k0 + lax.broadcasted_iota(jnp.int32, (KC, 1), 0)
            sc = jnp.where(kpos <= t_row, sc, NON_CAUSAL_MARK)
        score_scr[chunk_slabs(c)] = sc.reshape(slabs_per_chunk, LANES, tq)

    def score_body(c, carry):
        score_chunk(c, False)
        return carry

    lax.fori_loop(0, n_chunks - 1, score_body, 0)
    score_chunk(n_chunks - 1, True)

    def fill_body(c, carry):
        score_scr[chunk_slabs(c)] = jnp.full((slabs_per_chunk, LANES, tq), NON_CAUSAL_MARK, f32)
        return carry

    lax.fori_loop(n_chunks, N_SLABS // slabs_per_chunk, fill_body, 0)

    kf = float(top_k)

    def radix_select(n_words):
        _bit_planes(score_scr, plane_scr, n_words)
        all_ones = jnp.int32(-1 if n_words == 32 else 0xFFFF)

        def select_bit(i, carry):
            alive, above, thr = carry
            if n_words == 32:
                plane = plane_scr[i]
            else:
                word = plane_scr[i & 15]
                plane = jnp.where(i < 16, lax.shift_right_logical(word, 16), word) & all_ones
            plane = plane ^ jnp.where(i == 0, all_ones, jnp.int32(0))
            ones = alive & plane
            c1 = jnp.sum(lax.population_count(ones).astype(f32), axis=0, keepdims=True)
            take = (above + c1) >= kf
            alive = jnp.where(take, ones, alive ^ ones)
            above = jnp.where(take, above, above + c1)
            thr = jnp.where(take, thr | jnp.left_shift(jnp.int32(1), 31 - i), thr)
            return alive, above, thr

        init = (jnp.full((LANES, tq), all_ones, jnp.int32), jnp.zeros((1, tq), f32), jnp.zeros((1, tq), jnp.int32))
        _, above, thr_u = lax.fori_loop(0, 32, select_bit, init)
        select_scr[0:1, :] = thr_u
        select_scr[1:2, :] = pltpu.bitcast(above, jnp.int32)

    few_slabs = n_chunks * slabs_per_chunk <= 16

    @pl.when(few_slabs)
    def _():
        radix_select(16)

    @pl.when(jnp.logical_not(few_slabs))
    def _():
        radix_select(32)

    thr_u = select_scr[0:1, :]
    above = pltpu.bitcast(select_scr[1:2, :], f32)

    def key_to_score(key_row):
        return pltpu.bitcast(_ordered_bits(key_row), f32)

    def write_bias(thr_row, need_row):
        thr = to_col(thr_row)
        need = to_col(need_row)

        def body(c, carry):
            ties_before, n_above = carry
            k0 = pl.multiple_of(c * KC, KC)
            blk = score_scr[chunk_slabs(c)].reshape(KC, tq).T
            gt = jnp.where(blk > thr, 1.0, 0.0)
            eq = jnp.where(blk == thr, 1.0, 0.0)
            prefix = ties_before + _dot(eq.astype(bf16), tri_ref[...])
            chosen = gt + jnp.where(prefix <= need, eq, 0.0)
            kpos = k0 + lax.broadcasted_iota(jnp.int32, (1, KC), 1)
            bias_scr[:, pl.ds(k0, KC)] = jnp.where(kpos <= t_col, jnp.where(chosen > 0.5, 0.0, NEG), NEG)
            return prefix[:, KC - 1:KC], n_above + jnp.sum(gt, axis=-1, keepdims=True)

        zero = jnp.zeros((tq, 1), f32)
        return lax.fori_loop(0, n_chunks, body, (zero, zero))

    ties, n_above = write_bias(key_to_score(thr_u ^ int_min), kf - above)
    verified = (n_above == to_col(above)) & (n_above < kf) & (n_above + ties >= kf)

    @pl.when(jnp.sum(jnp.where(verified, 0.0, 1.0)) > 0.0)
    def _():
        def count(pred):
            def body(c, acc):
                blk = score_scr[chunk_slabs(c)].reshape(KC, tq)
                return acc + jnp.sum(jnp.where(pred(blk), 1.0, 0.0), axis=0, keepdims=True)
            return lax.fori_loop(0, N_SLABS // slabs_per_chunk, body, jnp.zeros((1, tq), f32))

        key = jnp.where(count(lambda b: b >= 0.0) >= kf, jnp.int32(0), int_min)

        def bit(i, key):
            cand = key + jnp.left_shift(jnp.int32(1), 30 - i)
            cand_f = key_to_score(cand)
            return jnp.where(count(lambda b: b >= cand_f) >= kf, cand, key)

        thr_f = key_to_score(lax.fori_loop(0, 31, bit, key))
        write_bias(thr_f, kf - count(lambda b: b > thr_f))

    def bias_fn(k0):
        return bias_scr[:, pl.ds(k0, KC)]

    qd = qd_ref[0]
    q4s = [_stack_group_queries(qd, g, lo64) for g in range(KV_GROUPS)]
    outs = _masked_flash(q4s, kd_ref, vd_ref, n_chunks, [bias_fn] * KV_GROUPS, tq,
                         bound_ref, (shift0_scr, shift1_scr))
    o = jnp.concatenate([_group_output(og, lo64, tq) for og in outs], axis=1) * zd_ref[0].astype(f32)
    o_ref[0] = o.astype(bf16)


def _dsa_attention(bound, qd, qi, misc, ki, kd, vd, zd):
    B, S, _ = qd.shape
    tq = TQ_DSA
    assert S <= N_SLABS * LANES and KC % tq == 0
    tri = jnp.asarray(np.triu(np.ones((KC, KC), np.float32)), bf16)
    row = lambda c: pl.BlockSpec((1, tq, c), lambda b, i: (b, i, 0))
    whole = pl.BlockSpec((1, KV_GROUPS, S, LANES), lambda b, i: (b, 0, 0, 0))
    return pl.pallas_call(
        functools.partial(_dsa_kernel, top_k=min(DSA_TOPK_MAX, S // 4), tq=tq),
        grid=(B, S // tq),
        in_specs=[_SMEM_SCALAR, row(512), row(256), row(LANES), pl.BlockSpec((1, S, LANES), lambda b, i: (b, 0, 0)),
                  whole, whole, row(512), pl.BlockSpec((KC, KC), lambda b, i: (0, 0))],
        out_specs=row(512),
        out_shape=jax.ShapeDtypeStruct((B, S, 512), bf16),
        scratch_shapes=[pltpu.VMEM((N_SLABS, LANES, tq), f32), pltpu.VMEM((N_SLABS, LANES, tq), jnp.int32),
                        pltpu.VMEM((tq, S), f32)] + _shift_scratch(tq) + [pltpu.VMEM((SUBLANES, tq), jnp.int32)],
        compiler_params=pltpu.CompilerParams(
            dimension_semantics=("arbitrary", "arbitrary"), vmem_limit_bytes=VMEM_LIMIT),
        name="dsa_attention",
    )(bound, qd, qi, misc, ki, kd, vd, zd, tri)


def _outproj_kernel(x_ref, on_ref, od_ref, w_ref, o_ref):
    half = w_ref.shape[0] // 2
    o_ref[0] = x_ref[0] + _dot(on_ref[0], w_ref[0:half, :]) + _dot(od_ref[0], w_ref[half:, :])


def _out_projection(x, o_nsa, o_dsa, w_out):
    B, S, D = x.shape
    tm = min(TM_PROJ, S)
    row = lambda c: pl.BlockSpec((1, tm, c), lambda b, i: (b, i, 0))
    return pl.pallas_call(
        _outproj_kernel,
        grid=(B, S // tm),
        in_specs=[row(D), row(512), row(512), pl.BlockSpec(w_out.shape, lambda b, i: (0, 0))],
        out_specs=row(D),
        out_shape=jax.ShapeDtypeStruct((B, S, D), f32),
        compiler_params=pltpu.CompilerParams(
            dimension_semantics=("arbitrary", "arbitrary"), vmem_limit_bytes=VMEM_LIMIT),
        name="out_projection",
    )(x, o_nsa, o_dsa, w_out)


def _rope_tables(pos):
    half = HEAD_DIM // 2
    inv_freq = ROPE_THETA ** (-jnp.arange(half, dtype=f32) / half)
    ang = pos[:, None] * inv_freq[None, :]
    cos, sin = jnp.cos(ang), jnp.sin(ang)
    cos_t = jnp.tile(cos, (1, LANES // half))
    sin_t = jnp.tile(jnp.concatenate([-sin, sin], axis=1), (1, LANES // HEAD_DIM))
    return cos_t, sin_t


def _overlap_t(seq, n_cmp_pad):
    n_cmp = (seq - CMP_BLOCK) // CMP_STRIDE + 1
    n_sel = seq // SEL_BLOCK
    c_start = np.arange(n_cmp) * CMP_STRIDE
    j_start = np.arange(n_sel) * SEL_BLOCK
    ov = np.clip(np.minimum(c_start[:, None] + CMP_BLOCK, j_start[None, :] + SEL_BLOCK)
                 - np.maximum(c_start[:, None], j_start[None, :]), 0, None).astype(np.float32) / CMP_BLOCK
    out = np.zeros((n_sel, n_cmp_pad), np.float32)
    out[:, :n_cmp] = ov.T
    return out


def _block_diag2(w):
    z = jnp.zeros_like(w)
    return jnp.concatenate([jnp.concatenate([w, z], axis=-1), jnp.concatenate([z, w], axis=-1)], axis=-2)


def _layer(x, norm_gain, w_in, nsa_q_gain, nsa_kc_gain, nsa_ks_gain, nsa_kw_gain,
           cmp_pe_k, cmp_k_w1, cmp_k_b1, cmp_k_w2, cmp_pe_v, cmp_v_w1, cmp_v_b1, cmp_v_w2,
           dsa_q_gain, dsa_k_gain, w_out):
    B, S, _ = x.shape
    assert S % KC == 0 and S >= WINDOW + TQ_NSA
    n_cmp_pad = S // CMP_STRIDE

    head_of_lane = np.arange(2 * LANES) // HEAD_DIM
    bd = jnp.asarray(head_of_lane[:, None] == head_of_lane[None, :], bf16)
    dup = lambda v: jnp.tile(v.reshape(1, -1), (1, 2))
    quad = lambda v: jnp.tile(v.reshape(1, -1), (1, 4))
    gains = jnp.concatenate([quad(nsa_q_gain), jnp.concatenate([dup(nsa_ks_gain), dup(nsa_kw_gain)], axis=1),
                             quad(dsa_q_gain), quad(dsa_k_gain), jnp.ones((4, 2 * LANES), f32)], axis=0)
    cos_t, sin_t = _rope_tables(jnp.arange(S, dtype=f32))

    (qn, qd, qi, ki, ks, kw, kd, vs, vw, vd, kc, vc, zn, zd, misc) = _in_projection(
        x, norm_gain.reshape(1, -1), w_in, bd, cos_t, sin_t, gains)

    cmp_pos = (jnp.arange(n_cmp_pad) * CMP_STRIDE + CMP_BLOCK - 1).astype(f32)
    cos_c, sin_c = _rope_tables(cmp_pos)

    def cmp_weights(pe, w1, b1, w2):
        w1_bd = _block_diag2(w1.reshape(CMP_BLOCK, HEAD_DIM, CMP_HIDDEN)).astype(bf16)
        return jnp.tile(pe, (1, 2)), w1_bd, dup(b1), _block_diag2(w2).astype(bf16)

    kcmp = _compress(kc, *cmp_weights(cmp_pe_k, cmp_k_w1, cmp_k_b1, cmp_k_w2), bd[:LANES, :LANES], dup(nsa_kc_gain),
                     cos_c, sin_c, is_key=True)
    vcmp = _compress(vc, *cmp_weights(cmp_pe_v, cmp_v_w1, cmp_v_b1, cmp_v_w2), bd[:LANES, :LANES], dup(nsa_kc_gain),
                     cos_c, sin_c, is_key=False)

    ovt = jnp.asarray(_overlap_t(S, n_cmp_pad), bf16)
    o_nsa = _nsa_attention(_score_bound(nsa_q_gain, nsa_kc_gain, nsa_ks_gain, nsa_kw_gain),
                           qn, kcmp, vcmp, ks, vs, kw, vw, misc, zn, ovt)

    o_dsa = _dsa_attention(_score_bound(dsa_q_gain, dsa_k_gain), qd, qi, misc, ki, kd, vd, zd)

    return _out_projection(x, o_nsa, o_dsa, w_out.astype(bf16))


def kernel(x, norm_gain, w_in, nsa_q_gain, nsa_kc_gain, nsa_ks_gain, nsa_kw_gain, cmp_pe_k, cmp_k_w1,
           cmp_k_b1, cmp_k_w2, cmp_pe_v, cmp_v_w1, cmp_v_b1, cmp_v_w2, dsa_q_gain, dsa_k_gain, w_out):
    for l in range(norm_gain.shape[0]):
        x = _layer(x, norm_gain[l], w_in[l:l + 1], nsa_q_gain[l], nsa_kc_gain[l], nsa_ks_gain[l], nsa_kw_gain[l],
                   cmp_pe_k[l], cmp_k_w1[l], cmp_k_b1[l], cmp_k_w2[l], cmp_pe_v[l], cmp_v_w1[l], cmp_v_b1[l],
                   cmp_v_w2[l], dsa_q_gain[l], dsa_k_gain[l], w_out[l])
    return x
```

```python
import functools

import numpy as np
import jax
import jax.numpy as jnp
from jax import lax
from jax.experimental import pallas as pl
from jax.experimental.pallas import tpu as pltpu

D_MODEL = 1024
HEAD_DIM = 64
NSA_HEADS = 8
DSA_HEADS = 8
KV_GROUPS = 2
HEADS_PER_GROUP = 4
IDX_HEADS = 4
CMP_BLOCK = 32
CMP_STRIDE = 16
CMP_HIDDEN = 256
SEL_BLOCK = 64
SEL_TOPN = 16
WINDOW = 512
DSA_TOPK_MAX = 256
ROPE_THETA = 10000.0
EPS = 1e-6
NEG = -1e30
FORCE = 1e6
ATTN_SCALE = HEAD_DIM ** -0.5
IDX_SCALE = HEAD_DIM ** -0.5
WI_SCALE = IDX_HEADS ** -0.5
LOG2E = 1.4426950408889634
Q_SCALE = ATTN_SCALE * LOG2E

LANES = 128
SUBLANES = 8
M_FLOOR = -5e29
NON_CAUSAL_MARK = -3.0e38
MAX_FIXED_SHIFT = 30.0
NORM_SLACK = 1.03
VMEM_LIMIT = 56 * 1024 * 1024

IN_WIDTHS = (512, 128, 128, 128, 128, 128, 128, 24, 512, 512, 128, 128, 256, 64, 4, 512)
IN_NAMES = ("q_n", "kc", "vc", "ks", "vs", "kw", "vw", "gate", "z_n",
            "q_d", "k_d", "v_d", "qi", "ki", "wi", "z_d")
IN_COLS = sum(IN_WIDTHS)

CH_QN, CH_KS, CH_KW, CH_QD, CH_KD, CH_QI, CH_KI = 0, 4, 5, 6, 10, 11, 13
CH_KC, CH_VC, CH_VS, CH_VW, CH_VD, CH_ZN, CH_ZD, CH_MISC = 14, 15, 16, 17, 18, 19, 23, 27
N_CHUNKS = 28
MISC_WI = 24

TM_PROJ = 512
TM_OUT = 1024
TQ_NSA = 256
TQ_DSA = 256
KC = 512

f32 = jnp.float32
bf16 = jnp.bfloat16


def _repack_weight(w_ref, w16_ref):
    off = dict(zip(IN_NAMES, np.cumsum((0,) + IN_WIDTHS[:-1])))
    width = dict(zip(IN_NAMES, IN_WIDTHS))
    order = ("q_n", "ks", "kw", "q_d", "k_d", "qi", "ki", "ki", "kc", "vc", "vs", "vw", "v_d", "z_n", "z_d",
             "gate", "wi")
    dst = 0
    for n in order:
        w16_ref[:, dst:dst + width[n]] = w_ref[0, :, off[n]:off[n] + width[n]].astype(bf16)
        dst += width[n]
    w16_ref[:, dst:] = jnp.zeros((D_MODEL, N_CHUNKS * LANES - dst), bf16)


def _nt_dot(a, b):
    return lax.dot_general(a, b, (((1,), (1,)), ((), ())), preferred_element_type=f32)


def _dot(a, b):
    return jnp.dot(a, b, preferred_element_type=f32)


def _split_bf16(v):
    hi = v.astype(bf16)
    lo = (v - hi.astype(f32)).astype(bf16)
    return hi, lo


def _head_rms_norm(y, bd, gain_row):
    ssq = _dot((y * y).astype(bf16), bd)
    return y * lax.rsqrt(ssq * (1.0 / HEAD_DIM) + EPS) * gain_row


def _rope(y, cos, sin_signed, lo32):
    partner = jnp.where(lo32, pltpu.roll(y, LANES - 32, 1), pltpu.roll(y, 32, 1))
    return y * cos + partner * sin_signed


def _dup_halves(y, lo64):
    r = pltpu.roll(y, 64, 1)
    return jnp.where(lo64, y, r), jnp.where(lo64, r, y)


def _value_with_ones(y, lo64):
    return jnp.where(lo64, y, 1.0), jnp.where(lo64, pltpu.roll(y, 64, 1), 1.0)


def _inproj_kernel(x_ref, ng_ref, w_ref, bd_ref, cos_ref, sin_ref, gains_ref,
                   qn_ref, qd_ref, qi_ref, ki_ref, ks_ref, kw_ref, kd_ref,
                   vs_ref, vw_ref, vd_ref, kc_ref, vc_ref, zn_ref, zd_ref, misc_ref, w16_ref):
    @pl.when((pl.program_id(0) == 0) & (pl.program_id(1) == 0))
    def _():
        _repack_weight(w_ref, w16_ref)

    x = x_ref[0]
    ms = jnp.mean(x * x, axis=-1, keepdims=True)
    h = (x * lax.rsqrt(ms + EPS) * ng_ref[...]).astype(bf16)
    cos = cos_ref[...]
    sin = sin_ref[...]
    bd = bd_ref[...]
    lane = lax.broadcasted_iota(jnp.int32, (1, LANES), 1)
    lo32 = (lane % 64) < 32
    lo64 = lane < 64

    def proj(c0, n):
        return _dot(h, w16_ref[:, c0 * LANES:(c0 + n) * LANES])

    def chunk(y, i):
        return y[:, i * LANES:(i + 1) * LANES]

    def normed_rope(y, gain_idx):
        w = y.shape[1]
        yn = _head_rms_norm(y, bd[:w, :w], gains_ref[gain_idx:gain_idx + 1, 0:w])
        return [_rope(chunk(yn, i), cos, sin, lo32) for i in range(w // LANES)]

    for c0, gain_idx, ref in ((CH_QN, 0, qn_ref), (CH_QD, 2, qd_ref)):
        y = proj(c0, 4)
        for i in range(2):
            for j, r in enumerate(normed_rope(y[:, 2 * i * LANES:(2 * i + 2) * LANES], gain_idx)):
                ref[0, :, (2 * i + j) * LANES:(2 * i + j + 1) * LANES] = (r * Q_SCALE).astype(bf16)
    y = proj(CH_QI, 2)
    for i in range(2):
        qi_ref[0, :, i * LANES:(i + 1) * LANES] = (_rope(chunk(y, i), cos, sin, lo32) * IDX_SCALE).astype(bf16)
    ki_ref[0] = _rope(proj(CH_KI, 1), cos, sin, lo32).astype(bf16)

    k_pairs = normed_rope(proj(CH_KS, 2), 1) + normed_rope(proj(CH_KD, 1), 3)
    for r, ref in zip(k_pairs, (ks_ref, kw_ref, kd_ref)):
        a, b = _dup_halves(r, lo64)
        ref[0, 0] = a.astype(bf16)
        ref[0, 1] = b.astype(bf16)
    for c0, ref in ((CH_VS, vs_ref), (CH_VW, vw_ref), (CH_VD, vd_ref)):
        a, b = _value_with_ones(proj(c0, 1), lo64)
        ref[0, 0] = a.astype(bf16)
        ref[0, 1] = b.astype(bf16)

    kc_ref[0] = proj(CH_KC, 1)
    vc_ref[0] = proj(CH_VC, 1)
    z = proj(CH_ZN, 4)
    zn_ref[0] = (z * jax.nn.sigmoid(z)).astype(bf16)
    z = proj(CH_ZD, 4)
    zd_ref[0] = (z * jax.nn.sigmoid(z)).astype(bf16)
    m = proj(CH_MISC, 1)
    misc_ref[0] = jnp.where(lane < MISC_WI, jax.nn.sigmoid(m), m * WI_SCALE)


def _in_projection(x, norm_gain, w_in, bd, cos_t, sin_t, gains):
    B, S, _ = x.shape
    tm = min(TM_PROJ, S)
    grid = (B, S // tm)
    row = lambda c: pl.BlockSpec((1, tm, c), lambda b, i: (b, i, 0))
    dup = pl.BlockSpec((1, KV_GROUPS, tm, LANES), lambda b, i: (b, 0, i, 0))
    const = lambda shape: pl.BlockSpec(shape, lambda b, i: tuple(0 for _ in shape))
    tab = pl.BlockSpec((tm, LANES), lambda b, i: (i, 0))
    sds = jax.ShapeDtypeStruct
    out_shape = (
        sds((B, S, 512), bf16), sds((B, S, 512), bf16), sds((B, S, 256), bf16), sds((B, S, LANES), bf16),
        sds((B, KV_GROUPS, S, LANES), bf16), sds((B, KV_GROUPS, S, LANES), bf16),
        sds((B, KV_GROUPS, S, LANES), bf16), sds((B, KV_GROUPS, S, LANES), bf16),
        sds((B, KV_GROUPS, S, LANES), bf16), sds((B, KV_GROUPS, S, LANES), bf16),
        sds((B, S, LANES), f32), sds((B, S, LANES), f32),
        sds((B, S, 512), bf16), sds((B, S, 512), bf16), sds((B, S, LANES), f32),
    )
    out_specs = (row(512), row(512), row(256), row(LANES), dup, dup, dup, dup, dup, dup,
                 row(LANES), row(LANES), row(512), row(512), row(LANES))
    return pl.pallas_call(
        _inproj_kernel,
        grid=grid,
        in_specs=[row(D_MODEL), const((1, D_MODEL)),
                  pl.BlockSpec((1, D_MODEL, IN_COLS), lambda b, i: (0, 0, 0), pipeline_mode=pl.Buffered(1)),
                  const((2 * LANES, 2 * LANES)), tab, tab, const((8, 2 * LANES))],
        out_specs=out_specs,
        out_shape=out_shape,
        scratch_shapes=[pltpu.VMEM((D_MODEL, N_CHUNKS * LANES), bf16)],
        compiler_params=pltpu.CompilerParams(
            dimension_semantics=("arbitrary", "arbitrary"), vmem_limit_bytes=VMEM_LIMIT),
        name="in_projection",
    )(x, norm_gain, w_in, bd, cos_t, sin_t, gains)


def _compress_kernel(src_ref, pe_ref, w1_ref, b1_ref, w2_ref, bd_ref, gain_ref, cos_ref, sin_ref,
                     out_ref, *, n_rows, is_key):
    half = CMP_BLOCK // 2
    acc_a = jnp.zeros((n_rows, 2 * CMP_HIDDEN), f32)
    acc_b = jnp.zeros((n_rows, 2 * CMP_HIDDEN), f32)
    for l in range(half):
        rows = src_ref[0, pl.ds(l, n_rows, stride=CMP_STRIDE), :]
        acc_a = acc_a + _dot((rows + pe_ref[l:l + 1, :]).astype(bf16), w1_ref[l])
        acc_b = acc_b + _dot((rows + pe_ref[l + half:l + half + 1, :]).astype(bf16), w1_ref[l + half])
    pre = acc_a + pltpu.roll(acc_b, n_rows - 1, 0) + b1_ref[...]
    hid = pre * jax.nn.sigmoid(pre)
    out = _dot(hid.astype(bf16), w2_ref[...])
    lane = lax.broadcasted_iota(jnp.int32, (1, LANES), 1)
    if is_key:
        out = _head_rms_norm(out, bd_ref[...], gain_ref[...])
        out = _rope(out, cos_ref[...], sin_ref[...], (lane % 64) < 32)
        a, b = _dup_halves(out, lane < 64)
    else:
        a, b = _value_with_ones(out, lane < 64)
    out_ref[0, 0] = a.astype(bf16)
    out_ref[0, 1] = b.astype(bf16)


def _compress(src, pe_dup, w1_bd, b1_dup, w2_bd, bd, gain_dup, cos_c, sin_c, *, is_key):
    B, S, _ = src.shape
    n_rows = S // CMP_STRIDE
    const = lambda shape: pl.BlockSpec(shape, lambda b: tuple(0 for _ in shape))
    return pl.pallas_call(
        functools.partial(_compress_kernel, n_rows=n_rows, is_key=is_key),
        grid=(B,),
        in_specs=[pl.BlockSpec((1, S, LANES), lambda b: (b, 0, 0)),
                  const((CMP_BLOCK, LANES)), const((CMP_BLOCK, LANES, 2 * CMP_HIDDEN)),
                  const((1, 2 * CMP_HIDDEN)), const((2 * CMP_HIDDEN, LANES)), const((LANES, LANES)),
                  const((1, LANES)), const((n_rows, LANES)), const((n_rows, LANES))],
        out_specs=pl.BlockSpec((1, KV_GROUPS, n_rows, LANES), lambda b: (b, 0, 0, 0)),
        out_shape=jax.ShapeDtypeStruct((B, KV_GROUPS, n_rows, LANES), bf16),
        compiler_params=pltpu.CompilerParams(
            dimension_semantics=("arbitrary",), vmem_limit_bytes=VMEM_LIMIT),
        name="compress_k" if is_key else "compress_v",
    )(src, pe_dup, w1_bd, b1_dup, w2_bd, bd, gain_dup, cos_c, sin_c)


def _half_masked(pair, h, lo64):
    keep = lo64 if h % 2 == 0 else jnp.logical_not(lo64)
    return jnp.where(keep, pair, jnp.zeros_like(pair))


def _stack_group_queries(q, g, lo64):
    slabs = []
    for r in range(HEADS_PER_GROUP):
        h = g * HEADS_PER_GROUP + r
        slabs.append(_half_masked(q[:, (h // 2) * LANES:(h // 2 + 1) * LANES], h, lo64))
    return jnp.concatenate(slabs, axis=0)


def _masked_flash(q4s, k_ref, v_ref, n_chunks, bias_fns, tq, bound_ref, shift_scrs):
    rows = HEADS_PER_GROUP * tq
    bound = bound_ref[0]

    def scores(g, k0):
        s = _nt_dot(q4s[g], k_ref[0, g, pl.ds(k0, KC), :])
        return (s.reshape(HEADS_PER_GROUP, tq, KC) + bias_fns[g](k0)[None]).reshape(rows, KC)

    for g in range(KV_GROUPS):
        shift_scrs[g][...] = jnp.full((rows, 1), bound, f32)

    @pl.when(bound > MAX_FIXED_SHIFT)
    def _():
        def body(c, ms):
            k0 = pl.multiple_of(c * KC, KC)
            return tuple(jnp.maximum(ms[g], jnp.max(scores(g, k0), axis=-1, keepdims=True))
                         for g in range(KV_GROUPS))

        init = tuple(jnp.full((rows, 1), M_FLOOR, f32) for _ in range(KV_GROUPS))
        for g, m in enumerate(lax.fori_loop(0, n_chunks, body, init)):
            shift_scrs[g][...] = m

    def body(c, accs):
        k0 = pl.multiple_of(c * KC, KC)
        new = []
        for g in range(KV_GROUPS):
            p = jnp.exp2(scores(g, k0) - shift_scrs[g][...]).astype(bf16)
            new.append(accs[g] + _dot(p, v_ref[0, g, pl.ds(k0, KC), :]))
        return tuple(new)

    def pair_body(i, accs):
        return body(2 * i + 1, body(2 * i, accs))

    init = tuple(jnp.zeros((rows, LANES), f32) for _ in range(KV_GROUPS))
    out = lax.fori_loop(0, n_chunks // 2, pair_body, init)
    out = lax.fori_loop(2 * (n_chunks // 2), n_chunks, body, out)
    return [acc / acc[:, HEAD_DIM:HEAD_DIM + 1] for acc in out]


def _group_output(og, lo64, tq):
    pairs = []
    for p in range(2):
        even = og[(2 * p) * tq:(2 * p + 1) * tq]
        odd = og[(2 * p + 1) * tq:(2 * p + 2) * tq]
        pairs.append(jnp.where(lo64, even, pltpu.roll(odd, HEAD_DIM, 1)))
    return jnp.concatenate(pairs, axis=1)


def _top_n_blocks(imp, n_sel, top_n):
    groups = [imp[SUBLANES * a:SUBLANES * (a + 1)] for a in range(n_sel // SUBLANES)]
    jsub = lax.broadcasted_iota(jnp.int32, groups[0].shape, 0)
    ranks = [jnp.zeros(groups[0].shape, f32) for _ in groups]
    for i in range(n_sel):
        row = imp[i:i + 1, :]
        for a, grp in enumerate(groups):
            if SUBLANES * a > i:
                beats = row >= grp
            elif SUBLANES * (a + 1) - 1 < i:
                beats = row > grp
            else:
                tie = jnp.where(jsub + SUBLANES * a > i, 1.0, 0.0)
                ranks[a] = ranks[a] + jnp.where(row > grp, 1.0, jnp.where(row == grp, tie, 0.0))
                continue
            ranks[a] = ranks[a] + jnp.where(beats, 1.0, 0.0)
    rank = jnp.concatenate(ranks, axis=0)
    return jnp.where(rank < top_n, 1.0, 0.0)


def _nsa_kernel(bound_ref, qn_ref, kcmp_ref, vcmp_ref, ks_ref, vs_ref, kw_ref, vw_ref, misc_ref, zn_ref, ovt_ref,
                expand_ref, o_ref, shift0_scr, shift1_scr, cshift0_scr, cshift1_scr, wshift0_scr, wshift1_scr,
                bias_scr, *, n_cmp_pad, n_sel, top_n):
    tq = TQ_NSA
    cmp_shift = [cshift0_scr, cshift1_scr]
    win_shift = [wshift0_scr, wshift1_scr]
    t0 = pl.program_id(1) * tq
    lane = lax.broadcasted_iota(jnp.int32, (1, LANES), 1)
    lo64 = lane < 64
    q = qn_ref[0]
    misc = misc_ref[0]
    t_col = t0 + lax.broadcasted_iota(jnp.int32, (tq, 1), 0)
    n_chunks = (t0 + tq + KC - 1) // KC
    win_span = WINDOW + tq
    win_start = pl.multiple_of(jnp.maximum(t0 - WINDOW, 0), tq)
    ovt = ovt_ref[...]
    rows = HEADS_PER_GROUP * tq

    def add_bias(s, bias):
        return (s.reshape(HEADS_PER_GROUP, tq, s.shape[-1]) + bias[None]).reshape(rows, s.shape[-1])

    cmp_end = lax.broadcasted_iota(jnp.int32, (1, n_cmp_pad), 1) * CMP_STRIDE + (CMP_BLOCK - 1)
    cmp_bias = jnp.where(cmp_end <= t_col, 0.0, NEG)
    win_diff = t_col - (win_start + lax.broadcasted_iota(jnp.int32, (1, win_span), 1))
    win_bias = jnp.where((win_diff >= 0) & (win_diff < WINDOW), 0.0, NEG)
    q4s = [_stack_group_queries(q, g, lo64) for g in range(KV_GROUPS)]

    def cmp_scores(g):
        return add_bias(_nt_dot(q4s[g], kcmp_ref[0, g]), cmp_bias)

    def win_scores(g):
        return add_bias(_nt_dot(q4s[g], kw_ref[0, g, pl.ds(win_start, win_span), :]), win_bias)

    for scr in cmp_shift + win_shift:
        scr[...] = jnp.full((rows, 1), bound_ref[0], f32)

    @pl.when(bound_ref[0] > MAX_FIXED_SHIFT)
    def _():
        for g in range(KV_GROUPS):
            cmp_shift[g][...] = jnp.maximum(jnp.max(cmp_scores(g), axis=-1, keepdims=True), M_FLOOR)
            win_shift[g][...] = jnp.max(win_scores(g), axis=-1, keepdims=True)

    o_cmp, o_win, sels = [], [], []
    for g in range(KV_GROUPS):
        e = jnp.exp2(cmp_scores(g) - cmp_shift[g][...])
        l = jnp.sum(e, axis=-1, keepdims=True)
        p = e * (1.0 / jnp.maximum(l, 1e-30))
        o_cmp.append(_dot(p.astype(bf16), vcmp_ref[0, g]))

        psum = p[0:tq] + p[tq:2 * tq] + p[2 * tq:3 * tq] + p[3 * tq:4 * tq]
        hi, lo = _split_bf16(psum)
        imp = _nt_dot(ovt, hi) + _nt_dot(ovt, lo)
        jrow = lax.broadcasted_iota(jnp.int32, (n_sel, tq), 0)
        tl = t0 + lax.broadcasted_iota(jnp.int32, (n_sel, tq), 1)
        cur = lax.shift_right_logical(tl, 6)
        forced = (jrow == 0) | (jrow == cur) | (jrow == cur - 1)
        imp = jnp.where(forced, FORCE, imp)
        imp = jnp.where(jrow * SEL_BLOCK <= tl, imp, NEG)
        pad_row = lax.broadcasted_iota(jnp.int32, (LANES - n_sel, tq), 0)
        chosen = jnp.concatenate([_top_n_blocks(imp, n_sel, top_n) * -NEG, jnp.where(pad_row == 0, NEG, 0.0)], axis=0)
        sels.append(chosen.T.astype(bf16))

        e = jnp.exp2(win_scores(g) - win_shift[g][...])
        acc = _dot(e.astype(bf16), vw_ref[0, g, pl.ds(win_start, win_span), :])
        o_win.append(acc / acc[:, HEAD_DIM:HEAD_DIM + 1])

    def bias_body(c, carry):
        k0 = pl.multiple_of(c * KC, KC)
        for g in range(KV_GROUPS):
            bias_scr[g, :, pl.ds(k0, KC)] = _dot(sels[g], expand_ref[:, pl.ds(k0, KC)])
        return carry

    lax.fori_loop(0, n_chunks, bias_body, 0)
    k_last = pl.multiple_of((n_chunks - 1) * KC, KC)
    causal = jnp.where(k_last + lax.broadcasted_iota(jnp.int32, (1, KC), 1) <= t_col, 0.0, NEG)
    for g in range(KV_GROUPS):
        bias_scr[g, :, pl.ds(k_last, KC)] = bias_scr[g, :, pl.ds(k_last, KC)] + causal

    o_sel = _masked_flash(q4s, ks_ref, vs_ref, n_chunks,
                          [lambda k0, g=g: bias_scr[g, :, pl.ds(k0, KC)] for g in range(KV_GROUPS)], tq,
                          bound_ref, (shift0_scr, shift1_scr))

    group_outs = []
    for g in range(KV_GROUPS):
        def gate(branch):
            cols = [misc[:, (g * HEADS_PER_GROUP + r) * 3 + branch:(g * HEADS_PER_GROUP + r) * 3 + branch + 1]
                    for r in range(HEADS_PER_GROUP)]
            return jnp.concatenate(cols, axis=0)

        og = gate(0) * o_cmp[g] + gate(1) * o_sel[g] + gate(2) * o_win[g]
        group_outs.append(_group_output(og, lo64, tq))

    o = jnp.concatenate(group_outs, axis=1) * zn_ref[0].astype(f32)
    o_ref[0] = o.astype(bf16)


def _score_bound(q_gain, *k_gains):
    k_max = functools.reduce(jnp.maximum, [jnp.max(jnp.abs(g)) for g in k_gains])
    return (Q_SCALE * HEAD_DIM * NORM_SLACK * jnp.max(jnp.abs(q_gain)) * k_max).reshape(1)


_SMEM_SCALAR = pl.BlockSpec(memory_space=pltpu.SMEM)


def _shift_scratch(tq):
    return [pltpu.VMEM((HEADS_PER_GROUP * tq, 1), f32)] * KV_GROUPS


def _nsa_attention(bound, qn, kcmp, vcmp, ks, vs, kw, vw, misc, zn, ovt):
    B, S, _ = qn.shape
    tq = TQ_NSA
    n_cmp_pad = kcmp.shape[2]
    n_sel = S // SEL_BLOCK
    row = lambda c: pl.BlockSpec((1, tq, c), lambda b, i: (b, i, 0))
    whole = lambda n: pl.BlockSpec((1, KV_GROUPS, n, LANES), lambda b, i: (b, 0, 0, 0))
    kern = functools.partial(_nsa_kernel, n_cmp_pad=n_cmp_pad, n_sel=n_sel, top_n=min(SEL_TOPN, n_sel))
    assert n_sel < LANES
    expand = np.zeros((LANES, S), np.float32)
    expand[np.arange(S) // SEL_BLOCK, np.arange(S)] = 1.0
    expand[n_sel] = 1.0
    expand = jnp.asarray(expand, bf16)
    return pl.pallas_call(
        kern,
        grid=(B, S // tq),
        in_specs=[_SMEM_SCALAR, row(512), whole(n_cmp_pad), whole(n_cmp_pad), whole(S), whole(S), whole(S),
                  whole(S), row(LANES), row(512), pl.BlockSpec((n_sel, n_cmp_pad), lambda b, i: (0, 0)),
                  pl.BlockSpec((LANES, S), lambda b, i: (0, 0))],
        out_specs=row(512),
        out_shape=jax.ShapeDtypeStruct((B, S, 512), bf16),
        scratch_shapes=3 * _shift_scratch(tq) + [pltpu.VMEM((KV_GROUPS, tq, S), f32)],
        compiler_params=pltpu.CompilerParams(
            dimension_semantics=("arbitrary", "arbitrary"), vmem_limit_bytes=VMEM_LIMIT),
        name="nsa_attention",
    )(bound, qn, kcmp, vcmp, ks, vs, kw, vw, misc, zn, ovt, expand)


N_SLABS = 32
_TRANSPOSE_STAGES = ((16, 0x0000FFFF), (8, 0x00FF00FF), (4, 0x0F0F0F0F), (2, 0x33333333), (1, 0x55555555))


def _ordered_bits(v):
    return v ^ (jnp.right_shift(v, 31) & jnp.int32(0x7FFFFFFF))


def _bit_planes(score_ref, plane_ref, n_words):
    stages = [s for s in _TRANSPOSE_STAGES if s[0] < n_words]
    for stage, (j, mask) in enumerate(stages):
        if stage == 0:
            load = lambda k: _ordered_bits(pltpu.bitcast(score_ref[k], jnp.int32))
        else:
            load = lambda k: plane_ref[k]

        def body(p, carry, j=j, mask=mask, load=load):
            k = jnp.left_shift(p & ~(j - 1), 1) | (p & (j - 1))
            lo = load(k)
            hi = load(k + j)
            t = (lo ^ lax.shift_right_logical(hi, j)) & mask
            plane_ref[k] = lo ^ t
            plane_ref[k + j] = hi ^ jnp.left_shift(t, j)
            return carry

        lax.fori_loop(0, n_words // 2, body, 0)


def _dsa_kernel(bound_ref, qd_ref, qi_ref, misc_ref, ki_ref, kd_ref, vd_ref, zd_ref, tri_ref,
                o_ref, score_scr, plane_scr, bias_scr, shift0_scr, shift1_scr, select_scr, *, top_k, tq):
    t0 = pl.program_id(1) * tq
    lane = lax.broadcasted_iota(jnp.int32, (1, LANES), 1)
    lo64 = lane < 64
    t_col = t0 + lax.broadcasted_iota(jnp.int32, (tq, 1), 0)
    t_row = t0 + lax.broadcasted_iota(jnp.int32, (1, tq), 1)
    n_chunks = (t0 + tq + KC - 1) // KC
    slabs_per_chunk = KC // LANES
    int_min = jnp.int32(-2 ** 31)

    def chunk_slabs(c):
        return pl.ds(c * slabs_per_chunk, slabs_per_chunk)

    def to_col(row):
        return jnp.broadcast_to(row, (SUBLANES, tq)).T[:, 0:1]

    qi = qi_ref[0]
    qi_heads = [_half_masked(qi[:, (h // 2) * LANES:(h // 2 + 1) * LANES], h, lo64) for h in range(IDX_HEADS)]
    misc_t = misc_ref[0].T
    wi = [misc_t[MISC_WI + h:MISC_WI + h + 1, :] for h in range(IDX_HEADS)]

    def score_chunk(c, has_later_keys):
        k0 = pl.multiple_of(c * KC, KC)
        kib = ki_ref[0, pl.ds(k0, KC), :]
        sc = jnp.zeros((KC, tq), f32)
        for h in range(IDX_HEADS):
            sc = sc + jnp.maximum(_nt_dot(kib, qi_heads[h]), 0.0) * wi[h]
        sc = jnp.where(sc == 0.0, 0.0, sc)
        if has_later_keys:
            kpos = k0 + lax.broadcasted_iota(jnp.int32, (KC, 1), 0)
            sc = jnp.where(kpos <= t_row, sc, NON_CAUSAL_MARK)
        score_scr[chunk_slabs(c)] = sc.reshape(slabs_per_chunk, LANES, tq)

    def score_body(c, carry):
        score_chunk(c, False)
        return carry

    lax.fori_loop(0, n_chunks - 1, score_body, 0)
    score_chunk(n_chunks - 1, True)

    def fill_body(c, carry):
        score_scr[chunk_slabs(c)] = jnp.full((slabs_per_chunk, LANES, tq), NON_CAUSAL_MARK, f32)
        return carry

    lax.fori_loop(n_chunks, N_SLABS // slabs_per_chunk, fill_body, 0)

    kf = float(top_k)

    def radix_select(n_words):
        _bit_planes(score_scr, plane_scr, n_words)
        all_ones = jnp.int32(-1 if n_words == 32 else 0xFFFF)

        def select_bit(i, carry):
            alive, above, thr = carry
            if n_words == 32:
                plane = plane_scr[i]
            else:
                word = plane_scr[i & 15]
                plane = jnp.where(i < 16, lax.shift_right_logical(word, 16), word) & all_ones
            plane = plane ^ jnp.where(i == 0, all_ones, jnp.int32(0))
            ones = alive & plane
            c1 = jnp.sum(lax.population_count(ones).astype(f32), axis=0, keepdims=True)
            take = (above + c1) >= kf
            alive = jnp.where(take, ones, alive ^ ones)
            above = jnp.where(take, above, above + c1)
            thr = jnp.where(take, thr | jnp.left_shift(jnp.int32(1), 31 - i), thr)
            return alive, above, thr

        init = (jnp.full((LANES, tq), all_ones, jnp.int32), jnp.zeros((1, tq), f32), jnp.zeros((1, tq), jnp.int32))
        _, above, thr_u = lax.fori_loop(0, 32, select_bit, init)
        select_scr[0:1, :] = thr_u
        select_scr[1:2, :] = pltpu.bitcast(above, jnp.int32)

    few_slabs = n_chunks * slabs_per_chunk <= 16

    @pl.when(few_slabs)
    def _():
        radix_select(16)

    @pl.when(jnp.logical_not(few_slabs))
    def _():
        radix_select(32)

    thr_u = select_scr[0:1, :]
    above = pltpu.bitcast(select_scr[1:2, :], f32)

    def key_to_score(key_row):
        return pltpu.bitcast(_ordered_bits(key_row), f32)

    def write_bias(thr_row, need_row):
        thr = to_col(thr_row)
        need = to_col(need_row)

        def body(c, carry):
            ties_before, n_above = carry
            k0 = pl.multiple_of(c * KC, KC)
            blk = score_scr[chunk_slabs(c)].reshape(KC, tq).T
            gt = jnp.where(blk > thr, 1.0, 0.0)
            eq = jnp.where(blk == thr, 1.0, 0.0)
            prefix = ties_before + _dot(eq.astype(bf16), tri_ref[...])
            chosen = gt + jnp.where(prefix <= need, eq, 0.0)
            kpos = k0 + lax.broadcasted_iota(jnp.int32, (1, KC), 1)
            bias_scr[:, pl.ds(k0, KC)] = jnp.where(kpos <= t_col, jnp.where(chosen > 0.5, 0.0, NEG), NEG)
            return prefix[:, KC - 1:KC], n_above + jnp.sum(gt, axis=-1, keepdims=True)

        zero = jnp.zeros((tq, 1), f32)
        return lax.fori_loop(0, n_chunks, body, (zero, zero))

    ties, n_above = write_bias(key_to_score(thr_u ^ int_min), kf - above)
    verified = (n_above == to_col(above)) & (n_above < kf) & (n_above + ties >= kf)

    @pl.when(jnp.sum(jnp.where(verified, 0.0, 1.0)) > 0.0)
    def _():
        def count(pred):
            def body(c, acc):
                blk = score_scr[chunk_slabs(c)].reshape(KC, tq)
                return acc + jnp.sum(jnp.where(pred(blk), 1.0, 0.0), axis=0, keepdims=True)
            return lax.fori_loop(0, N_SLABS // slabs_per_chunk, body, jnp.zeros((1, tq), f32))

        key = jnp.where(count(lambda b: b >= 0.0) >= kf, jnp.int32(0), int_min)

        def bit(i, key):
            cand = key + jnp.left_shift(jnp.int32(1), 30 - i)
            cand_f = key_to_score(cand)
            return jnp.where(count(lambda b: b >= cand_f) >= kf, cand, key)

        thr_f = key_to_score(lax.fori_loop(0, 31, bit, key))
        write_bias(thr_f, kf - count(lambda b: b > thr_f))

    def bias_fn(k0):
        return bias_scr[:, pl.ds(k0, KC)]

    qd = qd_ref[0]
    q4s = [_stack_group_queries(qd, g, lo64) for g in range(KV_GROUPS)]
    outs = _masked_flash(q4s, kd_ref, vd_ref, n_chunks, [bias_fn] * KV_GROUPS, tq,
                         bound_ref, (shift0_scr, shift1_scr))
    o = jnp.concatenate([_group_output(og, lo64, tq) for og in outs], axis=1) * zd_ref[0].astype(f32)
    o_ref[0] = o.astype(bf16)


def _dsa_attention(bound, qd, qi, misc, ki, kd, vd, zd):
    B, S, _ = qd.shape
    tq = TQ_DSA
    assert S <= N_SLABS * LANES and KC % tq == 0
    tri = jnp.asarray(np.triu(np.ones((KC, KC), np.float32)), bf16)
    row = lambda c: pl.BlockSpec((1, tq, c), lambda b, i: (b, i, 0))
    whole = pl.BlockSpec((1, KV_GROUPS, S, LANES), lambda b, i: (b, 0, 0, 0))
    return pl.pallas_call(
        functools.partial(_dsa_kernel, top_k=min(DSA_TOPK_MAX, S // 4), tq=tq),
        grid=(B, S // tq),
        in_specs=[_SMEM_SCALAR, row(512), row(256), row(LANES), pl.BlockSpec((1, S, LANES), lambda b, i: (b, 0, 0)),
                  whole, whole, row(512), pl.BlockSpec((KC, KC), lambda b, i: (0, 0))],
        out_specs=row(512),
        out_shape=jax.ShapeDtypeStruct((B, S, 512), bf16),
        scratch_shapes=[pltpu.VMEM((N_SLABS, LANES, tq), f32), pltpu.VMEM((N_SLABS, LANES, tq), jnp.int32),
                        pltpu.VMEM((tq, S), f32)] + _shift_scratch(tq) + [pltpu.VMEM((SUBLANES, tq), jnp.int32)],
        compiler_params=pltpu.CompilerParams(
            dimension_semantics=("arbitrary", "arbitrary"), vmem_limit_bytes=VMEM_LIMIT),
        name="dsa_attention",
    )(bound, qd, qi, misc, ki, kd, vd, zd, tri)


def _outproj_kernel(x_ref, on_ref, od_ref, w_ref, o_ref):
    half = w_ref.shape[0] // 2
    o_ref[0] = x_ref[0] + _dot(on_ref[0], w_ref[0:half, :]) + _dot(od_ref[0], w_ref[half:, :])


def _out_projection(x, o_nsa, o_dsa, w_out):
    B, S, D = x.shape
    tm = min(TM_OUT, S)
    row = lambda c: pl.BlockSpec((1, tm, c), lambda b, i: (b, i, 0))
    return pl.pallas_call(
        _outproj_kernel,
        grid=(B, S // tm),
        in_specs=[row(D), row(512), row(512), pl.BlockSpec(w_out.shape, lambda b, i: (0, 0))],
        out_specs=row(D),
        out_shape=jax.ShapeDtypeStruct((B, S, D), f32),
        compiler_params=pltpu.CompilerParams(
            dimension_semantics=("arbitrary", "arbitrary"), vmem_limit_bytes=VMEM_LIMIT),
        name="out_projection",
    )(x, o_nsa, o_dsa, w_out)


def _rope_tables(pos):
    half = HEAD_DIM // 2
    inv_freq = ROPE_THETA ** (-jnp.arange(half, dtype=f32) / half)
    ang = pos[:, None] * inv_freq[None, :]
    cos, sin = jnp.cos(ang), jnp.sin(ang)
    cos_t = jnp.tile(cos, (1, LANES // half))
    sin_t = jnp.tile(jnp.concatenate([-sin, sin], axis=1), (1, LANES // HEAD_DIM))
    return cos_t, sin_t


def _overlap_t(seq, n_cmp_pad):
    n_cmp = (seq - CMP_BLOCK) // CMP_STRIDE + 1
    n_sel = seq // SEL_BLOCK
    c_start = np.arange(n_cmp) * CMP_STRIDE
    j_start = np.arange(n_sel) * SEL_BLOCK
    ov = np.clip(np.minimum(c_start[:, None] + CMP_BLOCK, j_start[None, :] + SEL_BLOCK)
                 - np.maximum(c_start[:, None], j_start[None, :]), 0, None).astype(np.float32) / CMP_BLOCK
    out = np.zeros((n_sel, n_cmp_pad), np.float32)
    out[:, :n_cmp] = ov.T
    return out


def _block_diag2(w):
    z = jnp.zeros_like(w)
    return jnp.concatenate([jnp.concatenate([w, z], axis=-1), jnp.concatenate([z, w], axis=-1)], axis=-2)


def _layer(x, norm_gain, w_in, nsa_q_gain, nsa_kc_gain, nsa_ks_gain, nsa_kw_gain,
           cmp_pe_k, cmp_k_w1, cmp_k_b1, cmp_k_w2, cmp_pe_v, cmp_v_w1, cmp_v_b1, cmp_v_w2,
           dsa_q_gain, dsa_k_gain, w_out):
    B, S, _ = x.shape
    assert S % KC == 0 and S >= WINDOW + TQ_NSA
    n_cmp_pad = S // CMP_STRIDE

    head_of_lane = np.arange(2 * LANES) // HEAD_DIM
    bd = jnp.asarray(head_of_lane[:, None] == head_of_lane[None, :], bf16)
    dup = lambda v: jnp.tile(v.reshape(1, -1), (1, 2))
    quad = lambda v: jnp.tile(v.reshape(1, -1), (1, 4))
    gains = jnp.concatenate([quad(nsa_q_gain), jnp.concatenate([dup(nsa_ks_gain), dup(nsa_kw_gain)], axis=1),
                             quad(dsa_q_gain), quad(dsa_k_gain), jnp.ones((4, 2 * LANES), f32)], axis=0)
    cos_t, sin_t = _rope_tables(jnp.arange(S, dtype=f32))

    (qn, qd, qi, ki, ks, kw, kd, vs, vw, vd, kc, vc, zn, zd, misc) = _in_projection(
        x, norm_gain.reshape(1, -1), w_in, bd, cos_t, sin_t, gains)

    cmp_pos = (jnp.arange(n_cmp_pad) * CMP_STRIDE + CMP_BLOCK - 1).astype(f32)
    cos_c, sin_c = _rope_tables(cmp_pos)

    def cmp_weights(pe, w1, b1, w2):
        w1_bd = _block_diag2(w1.reshape(CMP_BLOCK, HEAD_DIM, CMP_HIDDEN)).astype(bf16)
        return jnp.tile(pe, (1, 2)), w1_bd, dup(b1), _block_diag2(w2).astype(bf16)

    kcmp = _compress(kc, *cmp_weights(cmp_pe_k, cmp_k_w1, cmp_k_b1, cmp_k_w2), bd[:LANES, :LANES], dup(nsa_kc_gain),
                     cos_c, sin_c, is_key=True)
    vcmp = _compress(vc, *cmp_weights(cmp_pe_v, cmp_v_w1, cmp_v_b1, cmp_v_w2), bd[:LANES, :LANES], dup(nsa_kc_gain),
                     cos_c, sin_c, is_key=False)

    ovt = jnp.asarray(_overlap_t(S, n_cmp_pad), bf16)
    o_nsa = _nsa_attention(_score_bound(nsa_q_gain, nsa_kc_gain, nsa_ks_gain, nsa_kw_gain),
                           qn, kcmp, vcmp, ks, vs, kw, vw, misc, zn, ovt)

    o_dsa = _dsa_attention(_score_bound(dsa_q_gain, dsa_k_gain), qd, qi, misc, ki, kd, vd, zd)

    return _out_projection(x, o_nsa, o_dsa, w_out.astype(bf16))


def kernel(x, norm_gain, w_in, nsa_q_gain, nsa_kc_gain, nsa_ks_gain, nsa_kw_gain, cmp_pe_k, cmp_k_w1,
           cmp_k_b1, cmp_k_w2, cmp_pe_v, cmp_v_w1, cmp_v_b1, cmp_v_w2, dsa_q_gain, dsa_k_gain, w_out):
    for l in range(norm_gain.shape[0]):
        x = _layer(x, norm_gain[l], w_in[l:l + 1], nsa_q_gain[l], nsa_kc_gain[l], nsa_ks_gain[l], nsa_kw_gain[l],
                   cmp_pe_k[l], cmp_k_w1[l], cmp_k_b1[l], cmp_k_w2[l], cmp_pe_v[l], cmp_v_w1[l], cmp_v_b1[l],
                   cmp_v_w2[l], dsa_q_gain[l], dsa_k_gain[l], w_out[l])
    return x
```

```python
import functools

import numpy as np
import jax
import jax.numpy as jnp
from jax import lax
from jax.experimental import pallas as pl
from jax.experimental.pallas import tpu as pltpu

D_MODEL = 1024
HEAD_DIM = 64
NSA_HEADS = 8
DSA_HEADS = 8
KV_GROUPS = 2
HEADS_PER_GROUP = 4
IDX_HEADS = 4
CMP_BLOCK = 32
CMP_STRIDE = 16
CMP_HIDDEN = 256
SEL_BLOCK = 64
SEL_TOPN = 16
WINDOW = 512
DSA_TOPK_MAX = 256
ROPE_THETA = 10000.0
EPS = 1e-6
NEG = -1e30
FORCE = 1e6
ATTN_SCALE = HEAD_DIM ** -0.5
IDX_SCALE = HEAD_DIM ** -0.5
WI_SCALE = IDX_HEADS ** -0.5
LOG2E = 1.4426950408889634
Q_SCALE = ATTN_SCALE * LOG2E

LANES = 128
SUBLANES = 8
M_FLOOR = -5e29
NON_CAUSAL_MARK = -3.0e38
MAX_FIXED_SHIFT = 30.0
NORM_SLACK = 1.03
VMEM_LIMIT = 56 * 1024 * 1024

IN_WIDTHS = (512, 128, 128, 128, 128, 128, 128, 24, 512, 512, 128, 128, 256, 64, 4, 512)
IN_NAMES = ("q_n", "kc", "vc", "ks", "vs", "kw", "vw", "gate", "z_n",
            "q_d", "k_d", "v_d", "qi", "ki", "wi", "z_d")
IN_COLS = sum(IN_WIDTHS)

CH_QN, CH_KS, CH_KW, CH_QD, CH_KD, CH_QI, CH_KI = 0, 4, 5, 6, 10, 11, 13
CH_KC, CH_VC, CH_VS, CH_VW, CH_VD, CH_ZN, CH_ZD, CH_MISC = 14, 15, 16, 17, 18, 19, 23, 27
N_CHUNKS = 28
MISC_WI = 24

TM_PROJ = 512
TM_OUT = 1024
TQ_NSA = 256
TQ_DSA = 256
KC = 512

f32 = jnp.float32
bf16 = jnp.bfloat16


def _repack_weight(w_ref, w16_ref):
    off = dict(zip(IN_NAMES, np.cumsum((0,) + IN_WIDTHS[:-1])))
    width = dict(zip(IN_NAMES, IN_WIDTHS))
    order = ("q_n", "ks", "kw", "q_d", "k_d", "qi", "ki", "ki", "kc", "vc", "vs", "vw", "v_d", "z_n", "z_d",
             "gate", "wi")
    dst = 0
    for n in order:
        w16_ref[:, dst:dst + width[n]] = w_ref[0, :, off[n]:off[n] + width[n]].astype(bf16)
        dst += width[n]
    w16_ref[:, dst:] = jnp.zeros((D_MODEL, N_CHUNKS * LANES - dst), bf16)


def _nt_dot(a, b):
    return lax.dot_general(a, b, (((1,), (1,)), ((), ())), preferred_element_type=f32)


def _dot(a, b):
    return jnp.dot(a, b, preferred_element_type=f32)


def _split_bf16(v):
    hi = v.astype(bf16)
    lo = (v - hi.astype(f32)).astype(bf16)
    return hi, lo


def _head_rms_norm(y, bd, gain_row):
    ssq = _dot((y * y).astype(bf16), bd)
    return y * lax.rsqrt(ssq * (1.0 / HEAD_DIM) + EPS) * gain_row


def _rope(y, cos, sin_signed, lo32):
    partner = jnp.where(lo32, pltpu.roll(y, LANES - 32, 1), pltpu.roll(y, 32, 1))
    return y * cos + partner * sin_signed


def _dup_halves(y, lo64):
    r = pltpu.roll(y, 64, 1)
    return jnp.where(lo64, y, r), jnp.where(lo64, r, y)


def _value_with_ones(y, lo64):
    return jnp.where(lo64, y, 1.0), jnp.where(lo64, pltpu.roll(y, 64, 1), 1.0)


def _inproj_kernel(x_ref, ng_ref, w_ref, bd_ref, cos_ref, sin_ref, gains_ref,
                   qn_ref, qd_ref, qi_ref, ki_ref, ks_ref, kw_ref, kd_ref,
                   vs_ref, vw_ref, vd_ref, kc_ref, vc_ref, zn_ref, zd_ref, misc_ref, w16_ref):
    @pl.when((pl.program_id(0) == 0) & (pl.program_id(1) == 0))
    def _():
        _repack_weight(w_ref, w16_ref)

    x = x_ref[0]
    ms = jnp.mean(x * x, axis=-1, keepdims=True)
    h = (x * lax.rsqrt(ms + EPS) * ng_ref[...]).astype(bf16)
    cos = cos_ref[...]
    sin = sin_ref[...]
    bd = bd_ref[...]
    lane = lax.broadcasted_iota(jnp.int32, (1, LANES), 1)
    lo32 = (lane % 64) < 32
    lo64 = lane < 64

    def proj(c0, n):
        return _dot(h, w16_ref[:, c0 * LANES:(c0 + n) * LANES])

    def chunk(y, i):
        return y[:, i * LANES:(i + 1) * LANES]

    def normed_rope(y, gain_idx):
        w = y.shape[1]
        yn = _head_rms_norm(y, bd[:w, :w], gains_ref[gain_idx:gain_idx + 1, 0:w])
        return [_rope(chunk(yn, i), cos, sin, lo32) for i in range(w // LANES)]

    for c0, gain_idx, ref in ((CH_QN, 0, qn_ref), (CH_QD, 2, qd_ref)):
        y = proj(c0, 4)
        for i in range(2):
            for j, r in enumerate(normed_rope(y[:, 2 * i * LANES:(2 * i + 2) * LANES], gain_idx)):
                ref[0, :, (2 * i + j) * LANES:(2 * i + j + 1) * LANES] = (r * Q_SCALE).astype(bf16)
    y = proj(CH_QI, 2)
    for i in range(2):
        qi_ref[0, :, i * LANES:(i + 1) * LANES] = (_rope(chunk(y, i), cos, sin, lo32) * IDX_SCALE).astype(bf16)
    ki_ref[0] = _rope(proj(CH_KI, 1), cos, sin, lo32).astype(bf16)

    k_pairs = normed_rope(proj(CH_KS, 2), 1) + normed_rope(proj(CH_KD, 1), 3)
    for r, ref in zip(k_pairs, (ks_ref, kw_ref, kd_ref)):
        a, b = _dup_halves(r, lo64)
        ref[0, 0] = a.astype(bf16)
        ref[0, 1] = b.astype(bf16)
    for c0, ref in ((CH_VS, vs_ref), (CH_VW, vw_ref), (CH_VD, vd_ref)):
        a, b = _value_with_ones(proj(c0, 1), lo64)
        ref[0, 0] = a.astype(bf16)
        ref[0, 1] = b.astype(bf16)

    kc_ref[0] = proj(CH_KC, 1)
    vc_ref[0] = proj(CH_VC, 1)
    z = proj(CH_ZN, 4)
    zn_ref[0] = (z * jax.nn.sigmoid(z)).astype(bf16)
    z = proj(CH_ZD, 4)
    zd_ref[0] = (z * jax.nn.sigmoid(z)).astype(bf16)
    m = proj(CH_MISC, 1)
    misc_ref[0] = jnp.where(lane < MISC_WI, jax.nn.sigmoid(m), m * WI_SCALE)


def _in_projection(x, norm_gain, w_in, bd, cos_t, sin_t, gains):
    B, S, _ = x.shape
    tm = min(TM_PROJ, S)
    grid = (B, S // tm)
    row = lambda c: pl.BlockSpec((1, tm, c), lambda b, i: (b, i, 0))
    dup = pl.BlockSpec((1, KV_GROUPS, tm, LANES), lambda b, i: (b, 0, i, 0))
    const = lambda shape: pl.BlockSpec(shape, lambda b, i: tuple(0 for _ in shape))
    tab = pl.BlockSpec((tm, LANES), lambda b, i: (i, 0))
    sds = jax.ShapeDtypeStruct
    out_shape = (
        sds((B, S, 512), bf16), sds((B, S, 512), bf16), sds((B, S, 256), bf16), sds((B, S, LANES), bf16),
        sds((B, KV_GROUPS, S, LANES), bf16), sds((B, KV_GROUPS, S, LANES), bf16),
        sds((B, KV_GROUPS, S, LANES), bf16), sds((B, KV_GROUPS, S, LANES), bf16),
        sds((B, KV_GROUPS, S, LANES), bf16), sds((B, KV_GROUPS, S, LANES), bf16),
        sds((B, S, LANES), f32), sds((B, S, LANES), f32),
        sds((B, S, 512), bf16), sds((B, S, 512), bf16), sds((B, S, LANES), f32),
    )
    out_specs = (row(512), row(512), row(256), row(LANES), dup, dup, dup, dup, dup, dup,
                 row(LANES), row(LANES), row(512), row(512), row(LANES))
    return pl.pallas_call(
        _inproj_kernel,
        grid=grid,
        in_specs=[row(D_MODEL), const((1, D_MODEL)),
                  pl.BlockSpec((1, D_MODEL, IN_COLS), lambda b, i: (0, 0, 0), pipeline_mode=pl.Buffered(1)),
                  const((2 * LANES, 2 * LANES)), tab, tab, const((8, 2 * LANES))],
        out_specs=out_specs,
        out_shape=out_shape,
        scratch_shapes=[pltpu.VMEM((D_MODEL, N_CHUNKS * LANES), bf16)],
        compiler_params=pltpu.CompilerParams(
            dimension_semantics=("arbitrary", "arbitrary"), vmem_limit_bytes=VMEM_LIMIT),
        name="in_projection",
    )(x, norm_gain, w_in, bd, cos_t, sin_t, gains)


def _compress_kernel(src_ref, pe_ref, w1_ref, b1_ref, w2_ref, bd_ref, gain_ref, cos_ref, sin_ref,
                     out_ref, *, n_rows, is_key):
    half = CMP_BLOCK // 2
    acc_a = jnp.zeros((n_rows, 2 * CMP_HIDDEN), f32)
    acc_b = jnp.zeros((n_rows, 2 * CMP_HIDDEN), f32)
    for l in range(half):
        rows = src_ref[0, pl.ds(l, n_rows, stride=CMP_STRIDE), :]
        acc_a = acc_a + _dot((rows + pe_ref[l:l + 1, :]).astype(bf16), w1_ref[l])
        acc_b = acc_b + _dot((rows + pe_ref[l + half:l + half + 1, :]).astype(bf16), w1_ref[l + half])
    pre = acc_a + pltpu.roll(acc_b, n_rows - 1, 0) + b1_ref[...]
    hid = pre * jax.nn.sigmoid(pre)
    out = _dot(hid.astype(bf16), w2_ref[...])
    lane = lax.broadcasted_iota(jnp.int32, (1, LANES), 1)
    if is_key:
        out = _head_rms_norm(out, bd_ref[...], gain_ref[...])
        out = _rope(out, cos_ref[...], sin_ref[...], (lane % 64) < 32)
        a, b = _dup_halves(out, lane < 64)
    else:
        a, b = _value_with_ones(out, lane < 64)
    out_ref[0, 0] = a.astype(bf16)
    out_ref[0, 1] = b.astype(bf16)


def _compress(src, pe_dup, w1_bd, b1_dup, w2_bd, bd, gain_dup, cos_c, sin_c, *, is_key):
    B, S, _ = src.shape
    n_rows = S // CMP_STRIDE
    const = lambda shape: pl.BlockSpec(shape, lambda b: tuple(0 for _ in shape))
    return pl.pallas_call(
        functools.partial(_compress_kernel, n_rows=n_rows, is_key=is_key),
        grid=(B,),
        in_specs=[pl.BlockSpec((1, S, LANES), lambda b: (b, 0, 0)),
                  const((CMP_BLOCK, LANES)), const((CMP_BLOCK, LANES, 2 * CMP_HIDDEN)),
                  const((1, 2 * CMP_HIDDEN)), const((2 * CMP_HIDDEN, LANES)), const((LANES, LANES)),
                  const((1, LANES)), const((n_rows, LANES)), const((n_rows, LANES))],
        out_specs=pl.BlockSpec((1, KV_GROUPS, n_rows, LANES), lambda b: (b, 0, 0, 0)),
        out_shape=jax.ShapeDtypeStruct((B, KV_GROUPS, n_rows, LANES), bf16),
        compiler_params=pltpu.CompilerParams(
            dimension_semantics=("arbitrary",), vmem_limit_bytes=VMEM_LIMIT),
        name="compress_k" if is_key else "compress_v",
    )(src, pe_dup, w1_bd, b1_dup, w2_bd, bd, gain_dup, cos_c, sin_c)


def _half_masked(pair, h, lo64):
    keep = lo64 if h % 2 == 0 else jnp.logical_not(lo64)
    return jnp.where(keep, pair, jnp.zeros_like(pair))


def _loop_by_pairs(n, body, init):
    carry = lax.fori_loop(0, n // 2, lambda i, cr: body(2 * i + 1, body(2 * i, cr)), init)
    return lax.fori_loop(2 * (n // 2), n, body, carry)


def _stack_group_queries(q, g, lo64):
    slabs = []
    for r in range(HEADS_PER_GROUP):
        h = g * HEADS_PER_GROUP + r
        slabs.append(_half_masked(q[:, (h // 2) * LANES:(h // 2 + 1) * LANES], h, lo64))
    return jnp.concatenate(slabs, axis=0)


def _masked_flash(q4s, k_ref, v_ref, n_chunks, bias_fns, tq, bound_ref, shift_scrs):
    rows = HEADS_PER_GROUP * tq
    bound = bound_ref[0]

    def scores(g, k0):
        s = _nt_dot(q4s[g], k_ref[0, g, pl.ds(k0, KC), :])
        return (s.reshape(HEADS_PER_GROUP, tq, KC) + bias_fns[g](k0)[None]).reshape(rows, KC)

    for g in range(KV_GROUPS):
        shift_scrs[g][...] = jnp.full((rows, 1), bound, f32)

    @pl.when(bound > MAX_FIXED_SHIFT)
    def _():
        def body(c, ms):
            k0 = pl.multiple_of(c * KC, KC)
            return tuple(jnp.maximum(ms[g], jnp.max(scores(g, k0), axis=-1, keepdims=True))
                         for g in range(KV_GROUPS))

        init = tuple(jnp.full((rows, 1), M_FLOOR, f32) for _ in range(KV_GROUPS))
        for g, m in enumerate(lax.fori_loop(0, n_chunks, body, init)):
            shift_scrs[g][...] = m

    def body(c, accs):
        k0 = pl.multiple_of(c * KC, KC)
        new = []
        for g in range(KV_GROUPS):
            p = jnp.exp2(scores(g, k0) - shift_scrs[g][...]).astype(bf16)
            new.append(accs[g] + _dot(p, v_ref[0, g, pl.ds(k0, KC), :]))
        return tuple(new)

    init = tuple(jnp.zeros((rows, LANES), f32) for _ in range(KV_GROUPS))
    out = _loop_by_pairs(n_chunks, body, init)
    return [acc / acc[:, HEAD_DIM:HEAD_DIM + 1] for acc in out]


def _group_output(og, lo64, tq):
    pairs = []
    for p in range(2):
        even = og[(2 * p) * tq:(2 * p + 1) * tq]
        odd = og[(2 * p + 1) * tq:(2 * p + 2) * tq]
        pairs.append(jnp.where(lo64, even, pltpu.roll(odd, HEAD_DIM, 1)))
    return jnp.concatenate(pairs, axis=1)


def _top_n_blocks(imp, n_sel, top_n):
    groups = [imp[SUBLANES * a:SUBLANES * (a + 1)] for a in range(n_sel // SUBLANES)]
    jsub = lax.broadcasted_iota(jnp.int32, groups[0].shape, 0)
    ranks = [jnp.zeros(groups[0].shape, f32) for _ in groups]
    for i in range(n_sel):
        row = imp[i:i + 1, :]
        for a, grp in enumerate(groups):
            if SUBLANES * a > i:
                beats = row >= grp
            elif SUBLANES * (a + 1) - 1 < i:
                beats = row > grp
            else:
                tie = jnp.where(jsub + SUBLANES * a > i, 1.0, 0.0)
                ranks[a] = ranks[a] + jnp.where(row > grp, 1.0, jnp.where(row == grp, tie, 0.0))
                continue
            ranks[a] = ranks[a] + jnp.where(beats, 1.0, 0.0)
    rank = jnp.concatenate(ranks, axis=0)
    return jnp.where(rank < top_n, 1.0, 0.0)


def _nsa_kernel(bound_ref, qn_ref, kcmp_ref, vcmp_ref, ks_ref, vs_ref, kw_ref, vw_ref, misc_ref, zn_ref, ovt_ref,
                expand_ref, o_ref, shift0_scr, shift1_scr, cshift0_scr, cshift1_scr, wshift0_scr, wshift1_scr,
                bias_scr, *, n_cmp_pad, n_sel, top_n):
    tq = TQ_NSA
    cmp_shift = [cshift0_scr, cshift1_scr]
    win_shift = [wshift0_scr, wshift1_scr]
    t0 = pl.program_id(1) * tq
    lane = lax.broadcasted_iota(jnp.int32, (1, LANES), 1)
    lo64 = lane < 64
    q = qn_ref[0]
    misc = misc_ref[0]
    t_col = t0 + lax.broadcasted_iota(jnp.int32, (tq, 1), 0)
    n_chunks = (t0 + tq + KC - 1) // KC
    win_span = WINDOW + tq
    win_start = pl.multiple_of(jnp.maximum(t0 - WINDOW, 0), tq)
    ovt = ovt_ref[...]
    rows = HEADS_PER_GROUP * tq

    def add_bias(s, bias):
        return (s.reshape(HEADS_PER_GROUP, tq, s.shape[-1]) + bias[None]).reshape(rows, s.shape[-1])

    cmp_end = lax.broadcasted_iota(jnp.int32, (1, n_cmp_pad), 1) * CMP_STRIDE + (CMP_BLOCK - 1)
    cmp_bias = jnp.where(cmp_end <= t_col, 0.0, NEG)
    win_diff = t_col - (win_start + lax.broadcasted_iota(jnp.int32, (1, win_span), 1))
    win_bias = jnp.where((win_diff >= 0) & (win_diff < WINDOW), 0.0, NEG)
    q4s = [_stack_group_queries(q, g, lo64) for g in range(KV_GROUPS)]

    def cmp_scores(g):
        return add_bias(_nt_dot(q4s[g], kcmp_ref[0, g]), cmp_bias)

    def win_scores(g):
        return add_bias(_nt_dot(q4s[g], kw_ref[0, g, pl.ds(win_start, win_span), :]), win_bias)

    for scr in cmp_shift + win_shift:
        scr[...] = jnp.full((rows, 1), bound_ref[0], f32)

    @pl.when(bound_ref[0] > MAX_FIXED_SHIFT)
    def _():
        for g in range(KV_GROUPS):
            cmp_shift[g][...] = jnp.maximum(jnp.max(cmp_scores(g), axis=-1, keepdims=True), M_FLOOR)
            win_shift[g][...] = jnp.max(win_scores(g), axis=-1, keepdims=True)

    o_cmp, o_win, sels = [], [], []
    for g in range(KV_GROUPS):
        e = jnp.exp2(cmp_scores(g) - cmp_shift[g][...])
        l = jnp.sum(e, axis=-1, keepdims=True)
        p = e * (1.0 / jnp.maximum(l, 1e-30))
        o_cmp.append(_dot(p.astype(bf16), vcmp_ref[0, g]))

        psum = p[0:tq] + p[tq:2 * tq] + p[2 * tq:3 * tq] + p[3 * tq:4 * tq]
        hi, lo = _split_bf16(psum)
        imp = _nt_dot(ovt, hi) + _nt_dot(ovt, lo)
        jrow = lax.broadcasted_iota(jnp.int32, (n_sel, tq), 0)
        tl = t0 + lax.broadcasted_iota(jnp.int32, (n_sel, tq), 1)
        cur = lax.shift_right_logical(tl, 6)
        forced = (jrow == 0) | (jrow == cur) | (jrow == cur - 1)
        imp = jnp.where(forced, FORCE, imp)
        imp = jnp.where(jrow * SEL_BLOCK <= tl, imp, NEG)
        pad_row = lax.broadcasted_iota(jnp.int32, (LANES - n_sel, tq), 0)
        chosen = jnp.concatenate([_top_n_blocks(imp, n_sel, top_n) * -NEG, jnp.where(pad_row == 0, NEG, 0.0)], axis=0)
        sels.append(chosen.T.astype(bf16))

        e = jnp.exp2(win_scores(g) - win_shift[g][...])
        acc = _dot(e.astype(bf16), vw_ref[0, g, pl.ds(win_start, win_span), :])
        o_win.append(acc / acc[:, HEAD_DIM:HEAD_DIM + 1])

    def bias_body(c, carry):
        k0 = pl.multiple_of(c * KC, KC)
        for g in range(KV_GROUPS):
            bias_scr[g, :, pl.ds(k0, KC)] = _dot(sels[g], expand_ref[:, pl.ds(k0, KC)])
        return carry

    _loop_by_pairs(n_chunks, bias_body, 0)
    k_last = pl.multiple_of((n_chunks - 1) * KC, KC)
    causal = jnp.where(k_last + lax.broadcasted_iota(jnp.int32, (1, KC), 1) <= t_col, 0.0, NEG)
    for g in range(KV_GROUPS):
        bias_scr[g, :, pl.ds(k_last, KC)] = bias_scr[g, :, pl.ds(k_last, KC)] + causal

    o_sel = _masked_flash(q4s, ks_ref, vs_ref, n_chunks,
                          [lambda k0, g=g: bias_scr[g, :, pl.ds(k0, KC)] for g in range(KV_GROUPS)], tq,
                          bound_ref, (shift0_scr, shift1_scr))

    group_outs = []
    for g in range(KV_GROUPS):
        def gate(branch):
            cols = [misc[:, (g * HEADS_PER_GROUP + r) * 3 + branch:(g * HEADS_PER_GROUP + r) * 3 + branch + 1]
                    for r in range(HEADS_PER_GROUP)]
            return jnp.concatenate(cols, axis=0)

        og = gate(0) * o_cmp[g] + gate(1) * o_sel[g] + gate(2) * o_win[g]
        group_outs.append(_group_output(og, lo64, tq))

    o = jnp.concatenate(group_outs, axis=1) * zn_ref[0].astype(f32)
    o_ref[0] = o.astype(bf16)


def _score_bound(q_gain, *k_gains):
    k_max = functools.reduce(jnp.maximum, [jnp.max(jnp.abs(g)) for g in k_gains])
    return (Q_SCALE * HEAD_DIM * NORM_SLACK * jnp.max(jnp.abs(q_gain)) * k_max).reshape(1)


_SMEM_SCALAR = pl.BlockSpec(memory_space=pltpu.SMEM)


def _shift_scratch(tq):
    return [pltpu.VMEM((HEADS_PER_GROUP * tq, 1), f32)] * KV_GROUPS


def _nsa_attention(bound, qn, kcmp, vcmp, ks, vs, kw, vw, misc, zn, ovt):
    B, S, _ = qn.shape
    tq = TQ_NSA
    n_cmp_pad = kcmp.shape[2]
    n_sel = S // SEL_BLOCK
    row = lambda c: pl.BlockSpec((1, tq, c), lambda b, i: (b, i, 0))
    whole = lambda n: pl.BlockSpec((1, KV_GROUPS, n, LANES), lambda b, i: (b, 0, 0, 0))
    kern = functools.partial(_nsa_kernel, n_cmp_pad=n_cmp_pad, n_sel=n_sel, top_n=min(SEL_TOPN, n_sel))
    assert n_sel < LANES
    expand = np.zeros((LANES, S), np.float32)
    expand[np.arange(S) // SEL_BLOCK, np.arange(S)] = 1.0
    expand[n_sel] = 1.0
    expand = jnp.asarray(expand, bf16)
    return pl.pallas_call(
        kern,
        grid=(B, S // tq),
        in_specs=[_SMEM_SCALAR, row(512), whole(n_cmp_pad), whole(n_cmp_pad), whole(S), whole(S), whole(S),
                  whole(S), row(LANES), row(512), pl.BlockSpec((n_sel, n_cmp_pad), lambda b, i: (0, 0)),
                  pl.BlockSpec((LANES, S), lambda b, i: (0, 0))],
        out_specs=row(512),
        out_shape=jax.ShapeDtypeStruct((B, S, 512), bf16),
        scratch_shapes=3 * _shift_scratch(tq) + [pltpu.VMEM((KV_GROUPS, tq, S), f32)],
        compiler_params=pltpu.CompilerParams(
            dimension_semantics=("arbitrary", "arbitrary"), vmem_limit_bytes=VMEM_LIMIT),
        name="nsa_attention",
    )(bound, qn, kcmp, vcmp, ks, vs, kw, vw, misc, zn, ovt, expand)


N_SLABS = 32
_TRANSPOSE_STAGES = ((16, 0x0000FFFF), (8, 0x00FF00FF), (4, 0x0F0F0F0F), (2, 0x33333333), (1, 0x55555555))


def _ordered_bits(v):
    return v ^ (jnp.right_shift(v, 31) & jnp.int32(0x7FFFFFFF))


def _bit_planes(score_ref, plane_ref, n_words):
    stages = [s for s in _TRANSPOSE_STAGES if s[0] < n_words]
    for stage, (j, mask) in enumerate(stages):
        if stage == 0:
            load = lambda k: _ordered_bits(pltpu.bitcast(score_ref[k], jnp.int32))
        else:
            load = lambda k: plane_ref[k]

        def body(p, carry, j=j, mask=mask, load=load):
            k = jnp.left_shift(p & ~(j - 1), 1) | (p & (j - 1))
            lo = load(k)
            hi = load(k + j)
            t = (lo ^ lax.shift_right_logical(hi, j)) & mask
            plane_ref[k] = lo ^ t
            plane_ref[k + j] = hi ^ jnp.left_shift(t, j)
            return carry

        lax.fori_loop(0, n_words // 2, body, 0)


def _dsa_kernel(bound_ref, qd_ref, qi_ref, misc_ref, ki_ref, kd_ref, vd_ref, zd_ref, tri_ref,
                o_ref, score_scr, plane_scr, bias_scr, shift0_scr, shift1_scr, select_scr, *, top_k, tq):
    t0 = pl.program_id(1) * tq
    lane = lax.broadcasted_iota(jnp.int32, (1, LANES), 1)
    lo64 = lane < 64
    t_col = t0 + lax.broadcasted_iota(jnp.int32, (tq, 1), 0)
    t_row = t0 + lax.broadcasted_iota(jnp.int32, (1, tq), 1)
    n_chunks = (t0 + tq + KC - 1) // KC
    slabs_per_chunk = KC // LANES
    int_min = jnp.int32(-2 ** 31)

    def chunk_slabs(c):
        return pl.ds(c * slabs_per_chunk, slabs_per_chunk)

    def to_col(row):
        return jnp.broadcast_to(row, (SUBLANES, tq)).T[:, 0:1]

    qi = qi_ref[0]
    qi_heads = [_half_masked(qi[:, (h // 2) * LANES:(h // 2 + 1) * LANES], h, lo64) for h in range(IDX_HEADS)]
    misc_t = misc_ref[0].T
    wi = [misc_t[MISC_WI + h:MISC_WI + h + 1, :] for h in range(IDX_HEADS)]

    def score_chunk(c, has_later_keys):
        k0 = pl.multiple_of(c * KC, KC)
        kib = ki_ref[0, pl.ds(k0, KC), :]
        sc = jnp.zeros((KC, tq), f32)
        for h in range(IDX_HEADS):
            sc = sc + jnp.maximum(_nt_dot(kib, qi_heads[h]), 0.0) * wi[h]
        sc = jnp.where(sc == 0.0, 0.0, sc)
        if has_later_keys:
            kpos = k0 + lax.broadcasted_iota(jnp.int32, (KC, 1), 0)
            sc = jnp.where(kpos <= t_row, sc, NON_CAUSAL_MARK)
        score_scr[chunk_slabs(c)] = sc.reshape(slabs_per_chunk, LANES, tq)

    def score_body(c, carry):
        score_chunk(c, False)
        return carry

    _loop_by_pairs(n_chunks - 1, score_body, 0)
    score_chunk(n_chunks - 1, True)

    def fill_body(c, carry):
        score_scr[chunk_slabs(c)] = jnp.full((slabs_per_chunk, LANES, tq), NON_CAUSAL_MARK, f32)
        return carry

    lax.fori_loop(n_chunks, N_SLABS // slabs_per_chunk, fill_body, 0)

    kf = float(top_k)

    def radix_select(n_words):
        _bit_planes(score_scr, plane_scr, n_words)
        all_ones = jnp.int32(-1 if n_words == 32 else 0xFFFF)

        def select_bit(i, carry):
            alive, above, thr = carry
            if n_words == 32:
                plane = plane_scr[i]
            else:
                word = plane_scr[i & 15]
                plane = jnp.where(i < 16, lax.shift_right_logical(word, 16), word) & all_ones
            plane = plane ^ jnp.where(i == 0, all_ones, jnp.int32(0))
            ones = alive & plane
            c1 = jnp.sum(lax.population_count(ones).astype(f32), axis=0, keepdims=True)
            take = (above + c1) >= kf
            alive = jnp.where(take, ones, alive ^ ones)
            above = jnp.where(take, above, above + c1)
            thr = jnp.where(take, thr | jnp.left_shift(jnp.int32(1), 31 - i), thr)
            return alive, above, thr

        init = (jnp.full((LANES, tq), all_ones, jnp.int32), jnp.zeros((1, tq), f32), jnp.zeros((1, tq), jnp.int32))
        _, above, thr_u = lax.fori_loop(0, 32, select_bit, init)
        select_scr[0:1, :] = thr_u
        select_scr[1:2, :] = pltpu.bitcast(above, jnp.int32)

    few_slabs = n_chunks * slabs_per_chunk <= 16

    @pl.when(few_slabs)
    def _():
        radix_select(16)

    @pl.when(jnp.logical_not(few_slabs))
    def _():
        radix_select(32)

    thr_u = select_scr[0:1, :]
    above = pltpu.bitcast(select_scr[1:2, :], f32)

    def key_to_score(key_row):
        return pltpu.bitcast(_ordered_bits(key_row), f32)

    def write_bias(thr_row, need_row):
        thr = to_col(thr_row)
        need = to_col(need_row)

        def body(c, carry):
            ties_before, n_above = carry
            k0 = pl.multiple_of(c * KC, KC)
            blk = score_scr[chunk_slabs(c)].reshape(KC, tq).T
            gt = jnp.where(blk > thr, 1.0, 0.0)
            eq = jnp.where(blk == thr, 1.0, 0.0)
            prefix = ties_before + _dot(eq.astype(bf16), tri_ref[...])
            chosen = gt + jnp.where(prefix <= need, eq, 0.0)
            kpos = k0 + lax.broadcasted_iota(jnp.int32, (1, KC), 1)
            bias_scr[:, pl.ds(k0, KC)] = jnp.where(kpos <= t_col, jnp.where(chosen > 0.5, 0.0, NEG), NEG)
            return prefix[:, KC - 1:KC], n_above + jnp.sum(gt, axis=-1, keepdims=True)

        zero = jnp.zeros((tq, 1), f32)
        return _loop_by_pairs(n_chunks, body, (zero, zero))

    ties, n_above = write_bias(key_to_score(thr_u ^ int_min), kf - above)
    verified = (n_above == to_col(above)) & (n_above < kf) & (n_above + ties >= kf)

    @pl.when(jnp.sum(jnp.where(verified, 0.0, 1.0)) > 0.0)
    def _():
        def count(pred):
            def body(c, acc):
                blk = score_scr[chunk_slabs(c)].reshape(KC, tq)
                return acc + jnp.sum(jnp.where(pred(blk), 1.0, 0.0), axis=0, keepdims=True)
            return lax.fori_loop(0, N_SLABS // slabs_per_chunk, body, jnp.zeros((1, tq), f32))

        key = jnp.where(count(lambda b: b >= 0.0) >= kf, jnp.int32(0), int_min)

        def bit(i, key):
            cand = key + jnp.left_shift(jnp.int32(1), 30 - i)
            cand_f = key_to_score(cand)
            return jnp.where(count(lambda b: b >= cand_f) >= kf, cand, key)

        thr_f = key_to_score(lax.fori_loop(0, 31, bit, key))
        write_bias(thr_f, kf - count(lambda b: b > thr_f))

    def bias_fn(k0):
        return bias_scr[:, pl.ds(k0, KC)]

    qd = qd_ref[0]
    q4s = [_stack_group_queries(qd, g, lo64) for g in range(KV_GROUPS)]
    outs = _masked_flash(q4s, kd_ref, vd_ref, n_chunks, [bias_fn] * KV_GROUPS, tq,
                         bound_ref, (shift0_scr, shift1_scr))
    o = jnp.concatenate([_group_output(og, lo64, tq) for og in outs], axis=1) * zd_ref[0].astype(f32)
    o_ref[0] = o.astype(bf16)


def _dsa_attention(bound, qd, qi, misc, ki, kd, vd, zd):
    B, S, _ = qd.shape
    tq = TQ_DSA
    assert S <= N_SLABS * LANES and KC % tq == 0
    tri = jnp.asarray(np.triu(np.ones((KC, KC), np.float32)), bf16)
    row = lambda c: pl.BlockSpec((1, tq, c), lambda b, i: (b, i, 0))
    whole = pl.BlockSpec((1, KV_GROUPS, S, LANES), lambda b, i: (b, 0, 0, 0))
    return pl.pallas_call(
        functools.partial(_dsa_kernel, top_k=min(DSA_TOPK_MAX, S // 4), tq=tq),
        grid=(B, S // tq),
        in_specs=[_SMEM_SCALAR, row(512), row(256), row(LANES), pl.BlockSpec((1, S, LANES), lambda b, i: (b, 0, 0)),
                  whole, whole, row(512), pl.BlockSpec((KC, KC), lambda b, i: (0, 0))],
        out_specs=row(512),
        out_shape=jax.ShapeDtypeStruct((B, S, 512), bf16),
        scratch_shapes=[pltpu.VMEM((N_SLABS, LANES, tq), f32), pltpu.VMEM((N_SLABS, LANES, tq), jnp.int32),
                        pltpu.VMEM((tq, S), f32)] + _shift_scratch(tq) + [pltpu.VMEM((SUBLANES, tq), jnp.int32)],
        compiler_params=pltpu.CompilerParams(
            dimension_semantics=("arbitrary", "arbitrary"), vmem_limit_bytes=VMEM_LIMIT),
        name="dsa_attention",
    )(bound, qd, qi, misc, ki, kd, vd, zd, tri)


def _outproj_kernel(x_ref, on_ref, od_ref, w_ref, o_ref):
    half = w_ref.shape[0] // 2
    o_ref[0] = x_ref[0] + _dot(on_ref[0], w_ref[0:half, :]) + _dot(od_ref[0], w_ref[half:, :])


def _out_projection(x, o_nsa, o_dsa, w_out):
    B, S, D = x.shape
    tm = min(TM_OUT, S)
    row = lambda c: pl.BlockSpec((1, tm, c), lambda b, i: (b, i, 0))
    return pl.pallas_call(
        _outproj_kernel,
        grid=(B, S // tm),
        in_specs=[row(D), row(512), row(512), pl.BlockSpec(w_out.shape, lambda b, i: (0, 0))],
        out_specs=row(D),
        out_shape=jax.ShapeDtypeStruct((B, S, D), f32),
        compiler_params=pltpu.CompilerParams(
            dimension_semantics=("arbitrary", "arbitrary"), vmem_limit_bytes=VMEM_LIMIT),
        name="out_projection",
    )(x, o_nsa, o_dsa, w_out)


def _rope_tables(pos):
    half = HEAD_DIM // 2
    inv_freq = ROPE_THETA ** (-jnp.arange(half, dtype=f32) / half)
    ang = pos[:, None] * inv_freq[None, :]
    cos, sin = jnp.cos(ang), jnp.sin(ang)
    cos_t = jnp.tile(cos, (1, LANES // half))
    sin_t = jnp.tile(jnp.concatenate([-sin, sin], axis=1), (1, LANES // HEAD_DIM))
    return cos_t, sin_t


def _overlap_t(seq, n_cmp_pad):
    n_cmp = (seq - CMP_BLOCK) // CMP_STRIDE + 1
    n_sel = seq // SEL_BLOCK
    c_start = np.arange(n_cmp) * CMP_STRIDE
    j_start = np.arange(n_sel) * SEL_BLOCK
    ov = np.clip(np.minimum(c_start[:, None] + CMP_BLOCK, j_start[None, :] + SEL_BLOCK)
                 - np.maximum(c_start[:, None], j_start[None, :]), 0, None).astype(np.float32) / CMP_BLOCK
    out = np.zeros((n_sel, n_cmp_pad), np.float32)
    out[:, :n_cmp] = ov.T
    return out


def _block_diag2(w):
    z = jnp.zeros_like(w)
    return jnp.concatenate([jnp.concatenate([w, z], axis=-1), jnp.concatenate([z, w], axis=-1)], axis=-2)


def _layer(x, norm_gain, w_in, nsa_q_gain, nsa_kc_gain, nsa_ks_gain, nsa_kw_gain,
           cmp_pe_k, cmp_k_w1, cmp_k_b1, cmp_k_w2, cmp_pe_v, cmp_v_w1, cmp_v_b1, cmp_v_w2,
           dsa_q_gain, dsa_k_gain, w_out):
    B, S, _ = x.shape
    assert S % KC == 0 and S >= WINDOW + TQ_NSA
    n_cmp_pad = S // CMP_STRIDE

    head_of_lane = np.arange(2 * LANES) // HEAD_DIM
    bd = jnp.asarray(head_of_lane[:, None] == head_of_lane[None, :], bf16)
    dup = lambda v: jnp.tile(v.reshape(1, -1), (1, 2))
    quad = lambda v: jnp.tile(v.reshape(1, -1), (1, 4))
    gains = jnp.concatenate([quad(nsa_q_gain), jnp.concatenate([dup(nsa_ks_gain), dup(nsa_kw_gain)], axis=1),
                             quad(dsa_q_gain), quad(dsa_k_gain), jnp.ones((4, 2 * LANES), f32)], axis=0)
    cos_t, sin_t = _rope_tables(jnp.arange(S, dtype=f32))

    (qn, qd, qi, ki, ks, kw, kd, vs, vw, vd, kc, vc, zn, zd, misc) = _in_projection(
        x, norm_gain.reshape(1, -1), w_in, bd, cos_t, sin_t, gains)

    cmp_pos = (jnp.arange(n_cmp_pad) * CMP_STRIDE + CMP_BLOCK - 1).astype(f32)
    cos_c, sin_c = _rope_tables(cmp_pos)

    def cmp_weights(pe, w1, b1, w2):
        w1_bd = _block_diag2(w1.reshape(CMP_BLOCK, HEAD_DIM, CMP_HIDDEN)).astype(bf16)
        return jnp.tile(pe, (1, 2)), w1_bd, dup(b1), _block_diag2(w2).astype(bf16)

    kcmp = _compress(kc, *cmp_weights(cmp_pe_k, cmp_k_w1, cmp_k_b1, cmp_k_w2), bd[:LANES, :LANES], dup(nsa_kc_gain),
                     cos_c, sin_c, is_key=True)
    vcmp = _compress(vc, *cmp_weights(cmp_pe_v, cmp_v_w1, cmp_v_b1, cmp_v_w2), bd[:LANES, :LANES], dup(nsa_kc_gain),
                     cos_c, sin_c, is_key=False)

    ovt = jnp.asarray(_overlap_t(S, n_cmp_pad), bf16)
    o_nsa = _nsa_attention(_score_bound(nsa_q_gain, nsa_kc_gain, nsa_ks_gain, nsa_kw_gain),
                           qn, kcmp, vcmp, ks, vs, kw, vw, misc, zn, ovt)

    o_dsa = _dsa_attention(_score_bound(dsa_q_gain, dsa_k_gain), qd, qi, misc, ki, kd, vd, zd)

    return _out_projection(x, o_nsa, o_dsa, w_out.astype(bf16))


def kernel(x, norm_gain, w_in, nsa_q_gain, nsa_kc_gain, nsa_ks_gain, nsa_kw_gain, cmp_pe_k, cmp_k_w1,
           cmp_k_b1, cmp_k_w2, cmp_pe_v, cmp_v_w1, cmp_v_b1, cmp_v_w2, dsa_q_gain, dsa_k_gain, w_out):
    for l in range(norm_gain.shape[0]):
        x = _layer(x, norm_gain[l], w_in[l:l + 1], nsa_q_gain[l], nsa_kc_gain[l], nsa_ks_gain[l], nsa_kw_gain[l],
                   cmp_pe_k[l], cmp_k_w1[l], cmp_k_b1[l], cmp_k_w2[l], cmp_pe_v[l], cmp_v_w1[l], cmp_v_b1[l],
                   cmp_v_w2[l], dsa_q_gain[l], dsa_k_gain[l], w_out[l])
    return x
```

```python
import functools

import numpy as np
import jax
import jax.numpy as jnp
from jax import lax
from jax.experimental import pallas as pl
from jax.experimental.pallas import tpu as pltpu

D_MODEL = 1024
HEAD_DIM = 64
NSA_HEADS = 8
DSA_HEADS = 8
KV_GROUPS = 2
HEADS_PER_GROUP = 4
IDX_HEADS = 4
CMP_BLOCK = 32
CMP_STRIDE = 16
CMP_HIDDEN = 256
SEL_BLOCK = 64
SEL_TOPN = 16
WINDOW = 512
DSA_TOPK_MAX = 256
ROPE_THETA = 10000.0
EPS = 1e-6
NEG = -1e30
FORCE = 1e6
ATTN_SCALE = HEAD_DIM ** -0.5
IDX_SCALE = HEAD_DIM ** -0.5
WI_SCALE = IDX_HEADS ** -0.5
LOG2E = 1.4426950408889634
Q_SCALE = ATTN_SCALE * LOG2E

LANES = 128
SUBLANES = 8
M_FLOOR = -5e29
NON_CAUSAL_MARK = -3.0e38
MAX_FIXED_SHIFT = 30.0
NORM_SLACK = 1.03
VMEM_LIMIT = 56 * 1024 * 1024

IN_WIDTHS = (512, 128, 128, 128, 128, 128, 128, 24, 512, 512, 128, 128, 256, 64, 4, 512)
IN_NAMES = ("q_n", "kc", "vc", "ks", "vs", "kw", "vw", "gate", "z_n",
            "q_d", "k_d", "v_d", "qi", "ki", "wi", "z_d")
IN_COLS = sum(IN_WIDTHS)

CH_QN, CH_KS, CH_KW, CH_QD, CH_KD, CH_QI, CH_KI = 0, 4, 5, 6, 10, 11, 13
CH_KC, CH_VC, CH_VS, CH_VW, CH_VD, CH_ZN, CH_ZD, CH_MISC = 14, 15, 16, 17, 18, 19, 23, 27
N_CHUNKS = 28
MISC_WI = 24

TM_PROJ = 512
TM_OUT = 1024
TQ_NSA = 256
TQ_DSA = 256
KC = 512

f32 = jnp.float32
bf16 = jnp.bfloat16


def _repack_weight(w_ref, w16_ref):
    off = dict(zip(IN_NAMES, np.cumsum((0,) + IN_WIDTHS[:-1])))
    width = dict(zip(IN_NAMES, IN_WIDTHS))
    order = ("q_n", "ks", "kw", "q_d", "k_d", "qi", "ki", "ki", "kc", "vc", "vs", "vw", "v_d", "z_n", "z_d",
             "gate", "wi")
    dst = 0
    for n in order:
        w16_ref[:, dst:dst + width[n]] = w_ref[0, :, off[n]:off[n] + width[n]].astype(bf16)
        dst += width[n]
    w16_ref[:, dst:] = jnp.zeros((D_MODEL, N_CHUNKS * LANES - dst), bf16)


def _nt_dot(a, b):
    return lax.dot_general(a, b, (((1,), (1,)), ((), ())), preferred_element_type=f32)


def _dot(a, b):
    return jnp.dot(a, b, preferred_element_type=f32)


def _split_bf16(v):
    hi = v.astype(bf16)
    lo = (v - hi.astype(f32)).astype(bf16)
    return hi, lo


def _head_rms_norm(y, bd, gain_row):
    ssq = _dot((y * y).astype(bf16), bd)
    return y * lax.rsqrt(ssq * (1.0 / HEAD_DIM) + EPS) * gain_row


def _rope(y, cos, sin_signed, lo32):
    partner = jnp.where(lo32, pltpu.roll(y, LANES - 32, 1), pltpu.roll(y, 32, 1))
    return y * cos + partner * sin_signed


def _dup_halves(y, lo64):
    r = pltpu.roll(y, 64, 1)
    return jnp.where(lo64, y, r), jnp.where(lo64, r, y)


def _value_with_ones(y, lo64):
    return jnp.where(lo64, y, 1.0), jnp.where(lo64, pltpu.roll(y, 64, 1), 1.0)


def _inproj_kernel(x_ref, ng_ref, w_ref, bd_ref, cos_ref, sin_ref, gains_ref,
                   qn_ref, qd_ref, qi_ref, ki_ref, ks_ref, kw_ref, kd_ref,
                   vs_ref, vw_ref, vd_ref, kc_ref, vc_ref, zn_ref, zd_ref, misc_ref, w16_ref):
    @pl.when((pl.program_id(0) == 0) & (pl.program_id(1) == 0))
    def _():
        _repack_weight(w_ref, w16_ref)

    x = x_ref[0]
    ms = jnp.mean(x * x, axis=-1, keepdims=True)
    h = (x * lax.rsqrt(ms + EPS) * ng_ref[...]).astype(bf16)
    cos = cos_ref[...]
    sin = sin_ref[...]
    bd = bd_ref[...]
    lane = lax.broadcasted_iota(jnp.int32, (1, LANES), 1)
    lo32 = (lane % 64) < 32
    lo64 = lane < 64

    def proj(c0, n):
        return _dot(h, w16_ref[:, c0 * LANES:(c0 + n) * LANES])

    def chunk(y, i):
        return y[:, i * LANES:(i + 1) * LANES]

    def normed_rope(y, gain_idx):
        w = y.shape[1]
        yn = _head_rms_norm(y, bd[:w, :w], gains_ref[gain_idx:gain_idx + 1, 0:w])
        return [_rope(chunk(yn, i), cos, sin, lo32) for i in range(w // LANES)]

    for c0, gain_idx, ref in ((CH_QN, 0, qn_ref), (CH_QD, 2, qd_ref)):
        y = proj(c0, 4)
        for i in range(2):
            for j, r in enumerate(normed_rope(y[:, 2 * i * LANES:(2 * i + 2) * LANES], gain_idx)):
                ref[0, :, (2 * i + j) * LANES:(2 * i + j + 1) * LANES] = (r * Q_SCALE).astype(bf16)
    y = proj(CH_QI, 2)
    for i in range(2):
        qi_ref[0, :, i * LANES:(i + 1) * LANES] = (_rope(chunk(y, i), cos, sin, lo32) * IDX_SCALE).astype(bf16)
    ki_ref[0] = _rope(proj(CH_KI, 1), cos, sin, lo32).astype(bf16)

    k_pairs = normed_rope(proj(CH_KS, 2), 1) + normed_rope(proj(CH_KD, 1), 3)
    for r, ref in zip(k_pairs, (ks_ref, kw_ref, kd_ref)):
        a, b = _dup_halves(r, lo64)
        ref[0, 0] = a.astype(bf16)
        ref[0, 1] = b.astype(bf16)
    for c0, ref in ((CH_VS, vs_ref), (CH_VW, vw_ref), (CH_VD, vd_ref)):
        a, b = _value_with_ones(proj(c0, 1), lo64)
        ref[0, 0] = a.astype(bf16)
        ref[0, 1] = b.astype(bf16)

    kc_ref[0] = proj(CH_KC, 1)
    vc_ref[0] = proj(CH_VC, 1)
    z = proj(CH_ZN, 4)
    zn_ref[0] = (z * jax.nn.sigmoid(z)).astype(bf16)
    z = proj(CH_ZD, 4)
    zd_ref[0] = (z * jax.nn.sigmoid(z)).astype(bf16)
    m = proj(CH_MISC, 1)
    misc_ref[0] = jnp.where(lane < MISC_WI, jax.nn.sigmoid(m), m * WI_SCALE)


def _in_projection(x, norm_gain, w_in, bd, cos_t, sin_t, gains):
    B, S, _ = x.shape
    tm = min(TM_PROJ, S)
    grid = (B, S // tm)
    row = lambda c: pl.BlockSpec((1, tm, c), lambda b, i: (b, i, 0))
    dup = pl.BlockSpec((1, KV_GROUPS, tm, LANES), lambda b, i: (b, 0, i, 0))
    const = lambda shape: pl.BlockSpec(shape, lambda b, i: tuple(0 for _ in shape))
    tab = pl.BlockSpec((tm, LANES), lambda b, i: (i, 0))
    sds = jax.ShapeDtypeStruct
    out_shape = (
        sds((B, S, 512), bf16), sds((B, S, 512), bf16), sds((B, S, 256), bf16), sds((B, S, LANES), bf16),
        sds((B, KV_GROUPS, S, LANES), bf16), sds((B, KV_GROUPS, S, LANES), bf16),
        sds((B, KV_GROUPS, S, LANES), bf16), sds((B, KV_GROUPS, S, LANES), bf16),
        sds((B, KV_GROUPS, S, LANES), bf16), sds((B, KV_GROUPS, S, LANES), bf16),
        sds((B, S, LANES), f32), sds((B, S, LANES), f32),
        sds((B, S, 512), bf16), sds((B, S, 512), bf16), sds((B, S, LANES), f32),
    )
    out_specs = (row(512), row(512), row(256), row(LANES), dup, dup, dup, dup, dup, dup,
                 row(LANES), row(LANES), row(512), row(512), row(LANES))
    return pl.pallas_call(
        _inproj_kernel,
        grid=grid,
        in_specs=[row(D_MODEL), const((1, D_MODEL)),
                  pl.BlockSpec((1, D_MODEL, IN_COLS), lambda b, i: (0, 0, 0), pipeline_mode=pl.Buffered(1)),
                  const((2 * LANES, 2 * LANES)), tab, tab, const((8, 2 * LANES))],
        out_specs=out_specs,
        out_shape=out_shape,
        scratch_shapes=[pltpu.VMEM((D_MODEL, N_CHUNKS * LANES), bf16)],
        compiler_params=pltpu.CompilerParams(
            dimension_semantics=("arbitrary", "arbitrary"), vmem_limit_bytes=VMEM_LIMIT),
        name="in_projection",
    )(x, norm_gain, w_in, bd, cos_t, sin_t, gains)


def _compress_kernel(src_ref, pe_ref, w1_ref, b1_ref, w2_ref, bd_ref, gain_ref, cos_ref, sin_ref,
                     out_ref, *, n_rows, is_key):
    half = CMP_BLOCK // 2
    acc_a = jnp.zeros((n_rows, 2 * CMP_HIDDEN), f32)
    acc_b = jnp.zeros((n_rows, 2 * CMP_HIDDEN), f32)
    for l in range(half):
        rows = src_ref[0, pl.ds(l, n_rows, stride=CMP_STRIDE), :]
        acc_a = acc_a + _dot((rows + pe_ref[l:l + 1, :]).astype(bf16), w1_ref[l])
        acc_b = acc_b + _dot((rows + pe_ref[l + half:l + half + 1, :]).astype(bf16), w1_ref[l + half])
    pre = acc_a + pltpu.roll(acc_b, n_rows - 1, 0) + b1_ref[...]
    hid = pre * jax.nn.sigmoid(pre)
    out = _dot(hid.astype(bf16), w2_ref[...])
    lane = lax.broadcasted_iota(jnp.int32, (1, LANES), 1)
    if is_key:
        out = _head_rms_norm(out, bd_ref[...], gain_ref[...])
        out = _rope(out, cos_ref[...], sin_ref[...], (lane % 64) < 32)
        a, b = _dup_halves(out, lane < 64)
    else:
        a, b = _value_with_ones(out, lane < 64)
    out_ref[0, 0] = a.astype(bf16)
    out_ref[0, 1] = b.astype(bf16)


def _compress(src, pe_dup, w1_bd, b1_dup, w2_bd, bd, gain_dup, cos_c, sin_c, *, is_key):
    B, S, _ = src.shape
    n_rows = S // CMP_STRIDE
    const = lambda shape: pl.BlockSpec(shape, lambda b: tuple(0 for _ in shape))
    return pl.pallas_call(
        functools.partial(_compress_kernel, n_rows=n_rows, is_key=is_key),
        grid=(B,),
        in_specs=[pl.BlockSpec((1, S, LANES), lambda b: (b, 0, 0)),
                  const((CMP_BLOCK, LANES)), const((CMP_BLOCK, LANES, 2 * CMP_HIDDEN)),
                  const((1, 2 * CMP_HIDDEN)), const((2 * CMP_HIDDEN, LANES)), const((LANES, LANES)),
                  const((1, LANES)), const((n_rows, LANES)), const((n_rows, LANES))],
        out_specs=pl.BlockSpec((1, KV_GROUPS, n_rows, LANES), lambda b: (b, 0, 0, 0)),
        out_shape=jax.ShapeDtypeStruct((B, KV_GROUPS, n_rows, LANES), bf16),
        compiler_params=pltpu.CompilerParams(
            dimension_semantics=("arbitrary",), vmem_limit_bytes=VMEM_LIMIT),
        name="compress_k" if is_key else "compress_v",
    )(src, pe_dup, w1_bd, b1_dup, w2_bd, bd, gain_dup, cos_c, sin_c)


def _half_masked(pair, h, lo64):
    keep = lo64 if h % 2 == 0 else jnp.logical_not(lo64)
    return jnp.where(keep, pair, jnp.zeros_like(pair))


def _loop_by_pairs(n, body, init, quads=False):
    done = 0
    if quads:
        four = lambda i, cr: body(4 * i + 3, body(4 * i + 2, body(4 * i + 1, body(4 * i, cr))))
        init = lax.fori_loop(0, n // 4, four, init)
        done = 4 * (n // 4)
    pair = lambda i, cr: body(2 * i + 1, body(2 * i, cr))
    carry = lax.fori_loop(done // 2, n // 2, pair, init)
    return lax.fori_loop(2 * (n // 2), n, body, carry)


def _stack_group_queries(q, g, lo64):
    slabs = []
    for r in range(HEADS_PER_GROUP):
        h = g * HEADS_PER_GROUP + r
        slabs.append(_half_masked(q[:, (h // 2) * LANES:(h // 2 + 1) * LANES], h, lo64))
    return jnp.concatenate(slabs, axis=0)


def _masked_flash(q4s, k_ref, v_ref, n_chunks, bias_fns, tq, bound_ref, shift_scrs):
    rows = HEADS_PER_GROUP * tq
    bound = bound_ref[0]

    def scores(g, k0):
        s = _nt_dot(q4s[g], k_ref[0, g, pl.ds(k0, KC), :])
        return (s.reshape(HEADS_PER_GROUP, tq, KC) + bias_fns[g](k0)[None]).reshape(rows, KC)

    for g in range(KV_GROUPS):
        shift_scrs[g][...] = jnp.full((rows, 1), bound, f32)

    @pl.when(bound > MAX_FIXED_SHIFT)
    def _():
        def body(c, ms):
            k0 = pl.multiple_of(c * KC, KC)
            return tuple(jnp.maximum(ms[g], jnp.max(scores(g, k0), axis=-1, keepdims=True))
                         for g in range(KV_GROUPS))

        init = tuple(jnp.full((rows, 1), M_FLOOR, f32) for _ in range(KV_GROUPS))
        for g, m in enumerate(lax.fori_loop(0, n_chunks, body, init)):
            shift_scrs[g][...] = m

    def body(c, accs):
        k0 = pl.multiple_of(c * KC, KC)
        new = []
        for g in range(KV_GROUPS):
            p = jnp.exp2(scores(g, k0) - shift_scrs[g][...]).astype(bf16)
            new.append(accs[g] + _dot(p, v_ref[0, g, pl.ds(k0, KC), :]))
        return tuple(new)

    init = tuple(jnp.zeros((rows, LANES), f32) for _ in range(KV_GROUPS))
    out = _loop_by_pairs(n_chunks, body, init, quads=True)
    return [acc / acc[:, HEAD_DIM:HEAD_DIM + 1] for acc in out]


def _group_output(og, lo64, tq):
    pairs = []
    for p in range(2):
        even = og[(2 * p) * tq:(2 * p + 1) * tq]
        odd = og[(2 * p + 1) * tq:(2 * p + 2) * tq]
        pairs.append(jnp.where(lo64, even, pltpu.roll(odd, HEAD_DIM, 1)))
    return jnp.concatenate(pairs, axis=1)


def _top_n_blocks(imp, n_sel, top_n):
    groups = [imp[SUBLANES * a:SUBLANES * (a + 1)] for a in range(n_sel // SUBLANES)]
    jsub = lax.broadcasted_iota(jnp.int32, groups[0].shape, 0)
    ranks = [jnp.zeros(groups[0].shape, f32) for _ in groups]
    for i in range(n_sel):
        row = imp[i:i + 1, :]
        for a, grp in enumerate(groups):
            if SUBLANES * a > i:
                beats = row >= grp
            elif SUBLANES * (a + 1) - 1 < i:
                beats = row > grp
            else:
                tie = jnp.where(jsub + SUBLANES * a > i, 1.0, 0.0)
                ranks[a] = ranks[a] + jnp.where(row > grp, 1.0, jnp.where(row == grp, tie, 0.0))
                continue
            ranks[a] = ranks[a] + jnp.where(beats, 1.0, 0.0)
    rank = jnp.concatenate(ranks, axis=0)
    return jnp.where(rank < top_n, 1.0, 0.0)


def _nsa_kernel(bound_ref, qn_ref, kcmp_ref, vcmp_ref, ks_ref, vs_ref, kw_ref, vw_ref, misc_ref, zn_ref, ovt_ref,
                expand_ref, o_ref, shift0_scr, shift1_scr, cshift0_scr, cshift1_scr, wshift0_scr, wshift1_scr,
                bias_scr, *, n_cmp_pad, n_sel, top_n):
    tq = TQ_NSA
    cmp_shift = [cshift0_scr, cshift1_scr]
    win_shift = [wshift0_scr, wshift1_scr]
    t0 = pl.program_id(1) * tq
    lane = lax.broadcasted_iota(jnp.int32, (1, LANES), 1)
    lo64 = lane < 64
    q = qn_ref[0]
    misc = misc_ref[0]
    t_col = t0 + lax.broadcasted_iota(jnp.int32, (tq, 1), 0)
    n_chunks = (t0 + tq + KC - 1) // KC
    win_span = WINDOW + tq
    win_start = pl.multiple_of(jnp.maximum(t0 - WINDOW, 0), tq)
    ovt = ovt_ref[...]
    rows = HEADS_PER_GROUP * tq

    def add_bias(s, bias):
        return (s.reshape(HEADS_PER_GROUP, tq, s.shape[-1]) + bias[None]).reshape(rows, s.shape[-1])

    cmp_end = lax.broadcasted_iota(jnp.int32, (1, n_cmp_pad), 1) * CMP_STRIDE + (CMP_BLOCK - 1)
    cmp_bias = jnp.where(cmp_end <= t_col, 0.0, NEG)
    win_diff = t_col - (win_start + lax.broadcasted_iota(jnp.int32, (1, win_span), 1))
    win_bias = jnp.where((win_diff >= 0) & (win_diff < WINDOW), 0.0, NEG)
    q4s = [_stack_group_queries(q, g, lo64) for g in range(KV_GROUPS)]

    def cmp_scores(g):
        return add_bias(_nt_dot(q4s[g], kcmp_ref[0, g]), cmp_bias)

    def win_scores(g):
        return add_bias(_nt_dot(q4s[g], kw_ref[0, g, pl.ds(win_start, win_span), :]), win_bias)

    for scr in cmp_shift + win_shift:
        scr[...] = jnp.full((rows, 1), bound_ref[0], f32)

    @pl.when(bound_ref[0] > MAX_FIXED_SHIFT)
    def _():
        for g in range(KV_GROUPS):
            cmp_shift[g][...] = jnp.maximum(jnp.max(cmp_scores(g), axis=-1, keepdims=True), M_FLOOR)
            win_shift[g][...] = jnp.max(win_scores(g), axis=-1, keepdims=True)

    o_cmp, o_win, sels = [], [], []
    for g in range(KV_GROUPS):
        e = jnp.exp2(cmp_scores(g) - cmp_shift[g][...])
        l = jnp.sum(e, axis=-1, keepdims=True)
        p = e * (1.0 / jnp.maximum(l, 1e-30))
        o_cmp.append(_dot(p.astype(bf16), vcmp_ref[0, g]))

        psum = p[0:tq] + p[tq:2 * tq] + p[2 * tq:3 * tq] + p[3 * tq:4 * tq]
        hi, lo = _split_bf16(psum)
        imp = _nt_dot(ovt, hi) + _nt_dot(ovt, lo)
        jrow = lax.broadcasted_iota(jnp.int32, (n_sel, tq), 0)
        tl = t0 + lax.broadcasted_iota(jnp.int32, (n_sel, tq), 1)
        cur = lax.shift_right_logical(tl, 6)
        forced = (jrow == 0) | (jrow == cur) | (jrow == cur - 1)
        imp = jnp.where(forced, FORCE, imp)
        imp = jnp.where(jrow * SEL_BLOCK <= tl, imp, NEG)
        pad_row = lax.broadcasted_iota(jnp.int32, (LANES - n_sel, tq), 0)
        chosen = jnp.concatenate([_top_n_blocks(imp, n_sel, top_n) * -NEG, jnp.where(pad_row == 0, NEG, 0.0)], axis=0)
        sels.append(chosen.T.astype(bf16))

        e = jnp.exp2(win_scores(g) - win_shift[g][...])
        acc = _dot(e.astype(bf16), vw_ref[0, g, pl.ds(win_start, win_span), :])
        o_win.append(acc / acc[:, HEAD_DIM:HEAD_DIM + 1])

    def bias_body(c, carry):
        k0 = pl.multiple_of(c * KC, KC)
        for g in range(KV_GROUPS):
            bias_scr[g, :, pl.ds(k0, KC)] = _dot(sels[g], expand_ref[:, pl.ds(k0, KC)])
        return carry

    _loop_by_pairs(n_chunks, bias_body, 0)
    k_last = pl.multiple_of((n_chunks - 1) * KC, KC)
    causal = jnp.where(k_last + lax.broadcasted_iota(jnp.int32, (1, KC), 1) <= t_col, 0.0, NEG)
    for g in range(KV_GROUPS):
        bias_scr[g, :, pl.ds(k_last, KC)] = bias_scr[g, :, pl.ds(k_last, KC)] + causal

    o_sel = _masked_flash(q4s, ks_ref, vs_ref, n_chunks,
                          [lambda k0, g=g: bias_scr[g, :, pl.ds(k0, KC)] for g in range(KV_GROUPS)], tq,
                          bound_ref, (shift0_scr, shift1_scr))

    group_outs = []
    for g in range(KV_GROUPS):
        def gate(branch):
            cols = [misc[:, (g * HEADS_PER_GROUP + r) * 3 + branch:(g * HEADS_PER_GROUP + r) * 3 + branch + 1]
                    for r in range(HEADS_PER_GROUP)]
            return jnp.concatenate(cols, axis=0)

        og = gate(0) * o_cmp[g] + gate(1) * o_sel[g] + gate(2) * o_win[g]
        group_outs.append(_group_output(og, lo64, tq))

    o = jnp.concatenate(group_outs, axis=1) * zn_ref[0].astype(f32)
    o_ref[0] = o.astype(bf16)


def _score_bound(q_gain, *k_gains):
    k_max = functools.reduce(jnp.maximum, [jnp.max(jnp.abs(g)) for g in k_gains])
    return (Q_SCALE * HEAD_DIM * NORM_SLACK * jnp.max(jnp.abs(q_gain)) * k_max).reshape(1)


_SMEM_SCALAR = pl.BlockSpec(memory_space=pltpu.SMEM)


def _shift_scratch(tq):
    return [pltpu.VMEM((HEADS_PER_GROUP * tq, 1), f32)] * KV_GROUPS


def _nsa_attention(bound, qn, kcmp, vcmp, ks, vs, kw, vw, misc, zn, ovt):
    B, S, _ = qn.shape
    tq = TQ_NSA
    n_cmp_pad = kcmp.shape[2]
    n_sel = S // SEL_BLOCK
    row = lambda c: pl.BlockSpec((1, tq, c), lambda b, i: (b, i, 0))
    whole = lambda n: pl.BlockSpec((1, KV_GROUPS, n, LANES), lambda b, i: (b, 0, 0, 0))
    kern = functools.partial(_nsa_kernel, n_cmp_pad=n_cmp_pad, n_sel=n_sel, top_n=min(SEL_TOPN, n_sel))
    assert n_sel < LANES
    expand = np.zeros((LANES, S), np.float32)
    expand[np.arange(S) // SEL_BLOCK, np.arange(S)] = 1.0
    expand[n_sel] = 1.0
    expand = jnp.asarray(expand, bf16)
    return pl.pallas_call(
        kern,
        grid=(B, S // tq),
        in_specs=[_SMEM_SCALAR, row(512), whole(n_cmp_pad), whole(n_cmp_pad), whole(S), whole(S), whole(S),
                  whole(S), row(LANES), row(512), pl.BlockSpec((n_sel, n_cmp_pad), lambda b, i: (0, 0)),
                  pl.BlockSpec((LANES, S), lambda b, i: (0, 0))],
        out_specs=row(512),
        out_shape=jax.ShapeDtypeStruct((B, S, 512), bf16),
        scratch_shapes=3 * _shift_scratch(tq) + [pltpu.VMEM((KV_GROUPS, tq, S), f32)],
        compiler_params=pltpu.CompilerParams(
            dimension_semantics=("arbitrary", "arbitrary"), vmem_limit_bytes=VMEM_LIMIT),
        name="nsa_attention",
    )(bound, qn, kcmp, vcmp, ks, vs, kw, vw, misc, zn, ovt, expand)


N_SLABS = 32
_TRANSPOSE_STAGES = ((16, 0x0000FFFF), (8, 0x00FF00FF), (4, 0x0F0F0F0F), (2, 0x33333333), (1, 0x55555555))


def _ordered_bits(v):
    return v ^ (jnp.right_shift(v, 31) & jnp.int32(0x7FFFFFFF))


def _bit_planes(score_ref, plane_ref, n_words):
    stages = [s for s in _TRANSPOSE_STAGES if s[0] < n_words]
    for stage, (j, mask) in enumerate(stages):
        if stage == 0:
            load = lambda k: _ordered_bits(pltpu.bitcast(score_ref[k], jnp.int32))
        else:
            load = lambda k: plane_ref[k]

        def body(p, carry, j=j, mask=mask, load=load):
            k = jnp.left_shift(p & ~(j - 1), 1) | (p & (j - 1))
            lo = load(k)
            hi = load(k + j)
            t = (lo ^ lax.shift_right_logical(hi, j)) & mask
            plane_ref[k] = lo ^ t
            plane_ref[k + j] = hi ^ jnp.left_shift(t, j)
            return carry

        lax.fori_loop(0, n_words // 2, body, 0)


def _dsa_kernel(bound_ref, qd_ref, qi_ref, misc_ref, ki_ref, kd_ref, vd_ref, zd_ref, tri_ref,
                o_ref, score_scr, plane_scr, bias_scr, shift0_scr, shift1_scr, select_scr, *, top_k, tq):
    t0 = pl.program_id(1) * tq
    lane = lax.broadcasted_iota(jnp.int32, (1, LANES), 1)
    lo64 = lane < 64
    t_col = t0 + lax.broadcasted_iota(jnp.int32, (tq, 1), 0)
    t_row = t0 + lax.broadcasted_iota(jnp.int32, (1, tq), 1)
    n_chunks = (t0 + tq + KC - 1) // KC
    slabs_per_chunk = KC // LANES
    int_min = jnp.int32(-2 ** 31)

    def chunk_slabs(c):
        return pl.ds(c * slabs_per_chunk, slabs_per_chunk)

    def to_col(row):
        return jnp.broadcast_to(row, (SUBLANES, tq)).T[:, 0:1]

    qi = qi_ref[0]
    qi_heads = [_half_masked(qi[:, (h // 2) * LANES:(h // 2 + 1) * LANES], h, lo64) for h in range(IDX_HEADS)]
    misc_t = misc_ref[0].T
    wi = [misc_t[MISC_WI + h:MISC_WI + h + 1, :] for h in range(IDX_HEADS)]

    def score_chunk(c, has_later_keys):
        k0 = pl.multiple_of(c * KC, KC)
        kib = ki_ref[0, pl.ds(k0, KC), :]
        sc = jnp.zeros((KC, tq), f32)
        for h in range(IDX_HEADS):
            sc = sc + jnp.maximum(_nt_dot(kib, qi_heads[h]), 0.0) * wi[h]
        sc = jnp.where(sc == 0.0, 0.0, sc)
        if has_later_keys:
            kpos = k0 + lax.broadcasted_iota(jnp.int32, (KC, 1), 0)
            sc = jnp.where(kpos <= t_row, sc, NON_CAUSAL_MARK)
        score_scr[chunk_slabs(c)] = sc.reshape(slabs_per_chunk, LANES, tq)

    def score_body(c, carry):
        score_chunk(c, False)
        return carry

    _loop_by_pairs(n_chunks - 1, score_body, 0)
    score_chunk(n_chunks - 1, True)

    def fill_body(c, carry):
        score_scr[chunk_slabs(c)] = jnp.full((slabs_per_chunk, LANES, tq), NON_CAUSAL_MARK, f32)
        return carry

    lax.fori_loop(n_chunks, N_SLABS // slabs_per_chunk, fill_body, 0)

    kf = float(top_k)

    def radix_select(n_words):
        _bit_planes(score_scr, plane_scr, n_words)
        all_ones = jnp.int32(-1 if n_words == 32 else 0xFFFF)

        def select_bit(i, carry):
            alive, above, thr = carry
            if n_words == 32:
                plane = plane_scr[i]
            else:
                word = plane_scr[i & 15]
                plane = jnp.where(i < 16, lax.shift_right_logical(word, 16), word) & all_ones
            plane = plane ^ jnp.where(i == 0, all_ones, jnp.int32(0))
            ones = alive & plane
            c1 = jnp.sum(lax.population_count(ones).astype(f32), axis=0, keepdims=True)
            take = (above + c1) >= kf
            alive = jnp.where(take, ones, alive ^ ones)
            above = jnp.where(take, above, above + c1)
            thr = jnp.where(take, thr | jnp.left_shift(jnp.int32(1), 31 - i), thr)
            return alive, above, thr

        init = (jnp.full((LANES, tq), all_ones, jnp.int32), jnp.zeros((1, tq), f32), jnp.zeros((1, tq), jnp.int32))
        _, above, thr_u = lax.fori_loop(0, 32, select_bit, init)
        select_scr[0:1, :] = thr_u
        select_scr[1:2, :] = pltpu.bitcast(above, jnp.int32)

    few_slabs = n_chunks * slabs_per_chunk <= 16

    @pl.when(few_slabs)
    def _():
        radix_select(16)

    @pl.when(jnp.logical_not(few_slabs))
    def _():
        radix_select(32)

    thr_u = select_scr[0:1, :]
    above = pltpu.bitcast(select_scr[1:2, :], f32)

    def key_to_score(key_row):
        return pltpu.bitcast(_ordered_bits(key_row), f32)

    def write_bias(thr_row, need_row):
        thr = to_col(thr_row)
        need = to_col(need_row)

        def body(c, carry):
            ties_before, n_above = carry
            k0 = pl.multiple_of(c * KC, KC)
            blk = score_scr[chunk_slabs(c)].reshape(KC, tq).T
            gt = jnp.where(blk > thr, 1.0, 0.0)
            eq = jnp.where(blk == thr, 1.0, 0.0)
            prefix = ties_before + _dot(eq.astype(bf16), tri_ref[...])
            chosen = gt + jnp.where(prefix <= need, eq, 0.0)
            kpos = k0 + lax.broadcasted_iota(jnp.int32, (1, KC), 1)
            bias_scr[:, pl.ds(k0, KC)] = jnp.where(kpos <= t_col, jnp.where(chosen > 0.5, 0.0, NEG), NEG)
            return prefix[:, KC - 1:KC], n_above + jnp.sum(gt, axis=-1, keepdims=True)

        zero = jnp.zeros((tq, 1), f32)
        return _loop_by_pairs(n_chunks, body, (zero, zero))

    ties, n_above = write_bias(key_to_score(thr_u ^ int_min), kf - above)
    verified = (n_above == to_col(above)) & (n_above < kf) & (n_above + ties >= kf)

    @pl.when(jnp.sum(jnp.where(verified, 0.0, 1.0)) > 0.0)
    def _():
        def count(pred):
            def body(c, acc):
                blk = score_scr[chunk_slabs(c)].reshape(KC, tq)
                return acc + jnp.sum(jnp.where(pred(blk), 1.0, 0.0), axis=0, keepdims=True)
            return lax.fori_loop(0, N_SLABS // slabs_per_chunk, body, jnp.zeros((1, tq), f32))

        key = jnp.where(count(lambda b: b >= 0.0) >= kf, jnp.int32(0), int_min)

        def bit(i, key):
            cand = key + jnp.left_shift(jnp.int32(1), 30 - i)
            cand_f = key_to_score(cand)
            return jnp.where(count(lambda b: b >= cand_f) >= kf, cand, key)

        thr_f = key_to_score(lax.fori_loop(0, 31, bit, key))
        write_bias(thr_f, kf - count(lambda b: b > thr_f))

    def bias_fn(k0):
        return bias_scr[:, pl.ds(k0, KC)]

    qd = qd_ref[0]
    q4s = [_stack_group_queries(qd, g, lo64) for g in range(KV_GROUPS)]
    outs = _masked_flash(q4s, kd_ref, vd_ref, n_chunks, [bias_fn] * KV_GROUPS, tq,
                         bound_ref, (shift0_scr, shift1_scr))
    o = jnp.concatenate([_group_output(og, lo64, tq) for og in outs], axis=1) * zd_ref[0].astype(f32)
    o_ref[0] = o.astype(bf16)


def _dsa_attention(bound, qd, qi, misc, ki, kd, vd, zd):
    B, S, _ = qd.shape
    tq = TQ_DSA
    assert S <= N_SLABS * LANES and KC % tq == 0
    tri = jnp.asarray(np.triu(np.ones((KC, KC), np.float32)), bf16)
    row = lambda c: pl.BlockSpec((1, tq, c), lambda b, i: (b, i, 0))
    whole = pl.BlockSpec((1, KV_GROUPS, S, LANES), lambda b, i: (b, 0, 0, 0))
    return pl.pallas_call(
        functools.partial(_dsa_kernel, top_k=min(DSA_TOPK_MAX, S // 4), tq=tq),
        grid=(B, S // tq),
        in_specs=[_SMEM_SCALAR, row(512), row(256), row(LANES), pl.BlockSpec((1, S, LANES), lambda b, i: (b, 0, 0)),
                  whole, whole, row(512), pl.BlockSpec((KC, KC), lambda b, i: (0, 0))],
        out_specs=row(512),
        out_shape=jax.ShapeDtypeStruct((B, S, 512), bf16),
        scratch_shapes=[pltpu.VMEM((N_SLABS, LANES, tq), f32), pltpu.VMEM((N_SLABS, LANES, tq), jnp.int32),
                        pltpu.VMEM((tq, S), f32)] + _shift_scratch(tq) + [pltpu.VMEM((SUBLANES, tq), jnp.int32)],
        compiler_params=pltpu.CompilerParams(
            dimension_semantics=("arbitrary", "arbitrary"), vmem_limit_bytes=VMEM_LIMIT),
        name="dsa_attention",
    )(bound, qd, qi, misc, ki, kd, vd, zd, tri)


def _outproj_kernel(x_ref, on_ref, od_ref, w_ref, o_ref):
    half = w_ref.shape[0] // 2
    o_ref[0] = x_ref[0] + _dot(on_ref[0], w_ref[0:half, :]) + _dot(od_ref[0], w_ref[half:, :])


def _out_projection(x, o_nsa, o_dsa, w_out):
    B, S, D = x.shape
    tm = min(TM_OUT, S)
    row = lambda c: pl.BlockSpec((1, tm, c), lambda b, i: (b, i, 0))
    return pl.pallas_call(
        _outproj_kernel,
        grid=(B, S // tm),
        in_specs=[row(D), row(512), row(512), pl.BlockSpec(w_out.shape, lambda b, i: (0, 0))],
        out_specs=row(D),
        out_shape=jax.ShapeDtypeStruct((B, S, D), f32),
        compiler_params=pltpu.CompilerParams(
            dimension_semantics=("arbitrary", "arbitrary"), vmem_limit_bytes=VMEM_LIMIT),
        name="out_projection",
    )(x, o_nsa, o_dsa, w_out)


def _rope_tables(pos):
    half = HEAD_DIM // 2
    inv_freq = ROPE_THETA ** (-jnp.arange(half, dtype=f32) / half)
    ang = pos[:, None] * inv_freq[None, :]
    cos, sin = jnp.cos(ang), jnp.sin(ang)
    cos_t = jnp.tile(cos, (1, LANES // half))
    sin_t = jnp.tile(jnp.concatenate([-sin, sin], axis=1), (1, LANES // HEAD_DIM))
    return cos_t, sin_t


def _overlap_t(seq, n_cmp_pad):
    n_cmp = (seq - CMP_BLOCK) // CMP_STRIDE + 1
    n_sel = seq // SEL_BLOCK
    c_start = np.arange(n_cmp) * CMP_STRIDE
    j_start = np.arange(n_sel) * SEL_BLOCK
    ov = np.clip(np.minimum(c_start[:, None] + CMP_BLOCK, j_start[None, :] + SEL_BLOCK)
                 - np.maximum(c_start[:, None], j_start[None, :]), 0, None).astype(np.float32) / CMP_BLOCK
    out = np.zeros((n_sel, n_cmp_pad), np.float32)
    out[:, :n_cmp] = ov.T
    return out


def _block_diag2(w):
    z = jnp.zeros_like(w)
    return jnp.concatenate([jnp.concatenate([w, z], axis=-1), jnp.concatenate([z, w], axis=-1)], axis=-2)


def _layer(x, norm_gain, w_in, nsa_q_gain, nsa_kc_gain, nsa_ks_gain, nsa_kw_gain,
           cmp_pe_k, cmp_k_w1, cmp_k_b1, cmp_k_w2, cmp_pe_v, cmp_v_w1, cmp_v_b1, cmp_v_w2,
           dsa_q_gain, dsa_k_gain, w_out):
    B, S, _ = x.shape
    assert S % KC == 0 and S >= WINDOW + TQ_NSA
    n_cmp_pad = S // CMP_STRIDE

    head_of_lane = np.arange(2 * LANES) // HEAD_DIM
    bd = jnp.asarray(head_of_lane[:, None] == head_of_lane[None, :], bf16)
    dup = lambda v: jnp.tile(v.reshape(1, -1), (1, 2))
    quad = lambda v: jnp.tile(v.reshape(1, -1), (1, 4))
    gains = jnp.concatenate([quad(nsa_q_gain), jnp.concatenate([dup(nsa_ks_gain), dup(nsa_kw_gain)], axis=1),
                             quad(dsa_q_gain), quad(dsa_k_gain), jnp.ones((4, 2 * LANES), f32)], axis=0)
    cos_t, sin_t = _rope_tables(jnp.arange(S, dtype=f32))

    (qn, qd, qi, ki, ks, kw, kd, vs, vw, vd, kc, vc, zn, zd, misc) = _in_projection(
        x, norm_gain.reshape(1, -1), w_in, bd, cos_t, sin_t, gains)

    cmp_pos = (jnp.arange(n_cmp_pad) * CMP_STRIDE + CMP_BLOCK - 1).astype(f32)
    cos_c, sin_c = _rope_tables(cmp_pos)

    def cmp_weights(pe, w1, b1, w2):
        w1_bd = _block_diag2(w1.reshape(CMP_BLOCK, HEAD_DIM, CMP_HIDDEN)).astype(bf16)
        return jnp.tile(pe, (1, 2)), w1_bd, dup(b1), _block_diag2(w2).astype(bf16)

    kcmp = _compress(kc, *cmp_weights(cmp_pe_k, cmp_k_w1, cmp_k_b1, cmp_k_w2), bd[:LANES, :LANES], dup(nsa_kc_gain),
                     cos_c, sin_c, is_key=True)
    vcmp = _compress(vc, *cmp_weights(cmp_pe_v, cmp_v_w1, cmp_v_b1, cmp_v_w2), bd[:LANES, :LANES], dup(nsa_kc_gain),
                     cos_c, sin_c, is_key=False)

    ovt = jnp.asarray(_overlap_t(S, n_cmp_pad), bf16)
    o_nsa = _nsa_attention(_score_bound(nsa_q_gain, nsa_kc_gain, nsa_ks_gain, nsa_kw_gain),
                           qn, kcmp, vcmp, ks, vs, kw, vw, misc, zn, ovt)

    o_dsa = _dsa_attention(_score_bound(dsa_q_gain, dsa_k_gain), qd, qi, misc, ki, kd, vd, zd)

    return _out_projection(x, o_nsa, o_dsa, w_out.astype(bf16))


def kernel(x, norm_gain, w_in, nsa_q_gain, nsa_kc_gain, nsa_ks_gain, nsa_kw_gain, cmp_pe_k, cmp_k_w1,
           cmp_k_b1, cmp_k_w2, cmp_pe_v, cmp_v_w1, cmp_v_b1, cmp_v_w2, dsa_q_gain, dsa_k_gain, w_out):
    for l in range(norm_gain.shape[0]):
        x = _layer(x, norm_gain[l], w_in[l:l + 1], nsa_q_gain[l], nsa_kc_gain[l], nsa_ks_gain[l], nsa_kw_gain[l],
                   cmp_pe_k[l], cmp_k_w1[l], cmp_k_b1[l], cmp_k_w2[l], cmp_pe_v[l], cmp_v_w1[l], cmp_v_b1[l],
                   cmp_v_w2[l], dsa_q_gain[l], dsa_k_gain[l], w_out[l])
    return x
```

```python
import functools

import numpy as np
import jax
import jax.numpy as jnp
from jax import lax
from jax.experimental import pallas as pl
from jax.experimental.pallas import tpu as pltpu

D_MODEL = 1024
HEAD_DIM = 64
NSA_HEADS = 8
DSA_HEADS = 8
KV_GROUPS = 2
HEADS_PER_GROUP = 4
IDX_HEADS = 4
CMP_BLOCK = 32
CMP_STRIDE = 16
CMP_HIDDEN = 256
SEL_BLOCK = 64
SEL_TOPN = 16
WINDOW = 512
DSA_TOPK_MAX = 256
ROPE_THETA = 10000.0
EPS = 1e-6
NEG = -1e30
FORCE = 1e6
ATTN_SCALE = HEAD_DIM ** -0.5
IDX_SCALE = HEAD_DIM ** -0.5
WI_SCALE = IDX_HEADS ** -0.5
LOG2E = 1.4426950408889634
Q_SCALE = ATTN_SCALE * LOG2E

LANES = 128
SUBLANES = 8
M_FLOOR = -5e29
NON_CAUSAL_MARK = -3.0e38
MAX_FIXED_SHIFT = 30.0
NORM_SLACK = 1.03
VMEM_LIMIT = 56 * 1024 * 1024

IN_WIDTHS = (512, 128, 128, 128, 128, 128, 128, 24, 512, 512, 128, 128, 256, 64, 4, 512)
IN_NAMES = ("q_n", "kc", "vc", "ks", "vs", "kw", "vw", "gate", "z_n",
            "q_d", "k_d", "v_d", "qi", "ki", "wi", "z_d")
IN_COLS = sum(IN_WIDTHS)

CH_QN, CH_KS, CH_KW, CH_QD, CH_KD, CH_QI, CH_KI = 0, 4, 5, 6, 10, 11, 13
CH_KC, CH_VC, CH_VS, CH_VW, CH_VD, CH_ZN, CH_ZD, CH_MISC = 14, 15, 16, 17, 18, 19, 23, 27
N_CHUNKS = 28
MISC_WI = 24

TM_PROJ = 512
TM_OUT = 1024
TQ_NSA = 256
TQ_DSA = 256
KC = 512

f32 = jnp.float32
bf16 = jnp.bfloat16


def _repack_weight(w_ref, w16_ref):
    off = dict(zip(IN_NAMES, np.cumsum((0,) + IN_WIDTHS[:-1])))
    width = dict(zip(IN_NAMES, IN_WIDTHS))
    order = ("q_n", "ks", "kw", "q_d", "k_d", "qi", "ki", "ki", "kc", "vc", "vs", "vw", "v_d", "z_n", "z_d",
             "gate", "wi")
    dst = 0
    for n in order:
        w16_ref[:, dst:dst + width[n]] = w_ref[0, :, off[n]:off[n] + width[n]].astype(bf16)
        dst += width[n]
    w16_ref[:, dst:] = jnp.zeros((D_MODEL, N_CHUNKS * LANES - dst), bf16)


def _nt_dot(a, b):
    return lax.dot_general(a, b, (((1,), (1,)), ((), ())), preferred_element_type=f32)


def _dot(a, b):
    return jnp.dot(a, b, preferred_element_type=f32)


def _split_bf16(v):
    hi = v.astype(bf16)
    lo = (v - hi.astype(f32)).astype(bf16)
    return hi, lo


def _head_rms_norm(y, bd, gain_row):
    ssq = _dot((y * y).astype(bf16), bd)
    return y * lax.rsqrt(ssq * (1.0 / HEAD_DIM) + EPS) * gain_row


def _rope(y, cos, sin_signed, lo32):
    partner = jnp.where(lo32, pltpu.roll(y, LANES - 32, 1), pltpu.roll(y, 32, 1))
    return y * cos + partner * sin_signed


def _dup_halves(y, lo64):
    r = pltpu.roll(y, 64, 1)
    return jnp.where(lo64, y, r), jnp.where(lo64, r, y)


def _value_with_ones(y, lo64):
    return jnp.where(lo64, y, 1.0), jnp.where(lo64, pltpu.roll(y, 64, 1), 1.0)


def _inproj_kernel(x_ref, ng_ref, w_ref, bd_ref, cos_ref, sin_ref, gains_ref,
                   qn_ref, qd_ref, qi_ref, ki_ref, ks_ref, kw_ref, kd_ref,
                   vs_ref, vw_ref, vd_ref, kc_ref, vc_ref, zn_ref, zd_ref, misc_ref, w16_ref):
    @pl.when((pl.program_id(0) == 0) & (pl.program_id(1) == 0))
    def _():
        _repack_weight(w_ref, w16_ref)

    x = x_ref[0]
    ms = jnp.mean(x * x, axis=-1, keepdims=True)
    h = (x * lax.rsqrt(ms + EPS) * ng_ref[...]).astype(bf16)
    cos = cos_ref[...]
    sin = sin_ref[...]
    bd = bd_ref[...]
    lane = lax.broadcasted_iota(jnp.int32, (1, LANES), 1)
    lo32 = (lane % 64) < 32
    lo64 = lane < 64

    def proj(c0, n):
        return _dot(h, w16_ref[:, c0 * LANES:(c0 + n) * LANES])

    def chunk(y, i):
        return y[:, i * LANES:(i + 1) * LANES]

    def normed_rope(y, gain_idx):
        w = y.shape[1]
        yn = _head_rms_norm(y, bd[:w, :w], gains_ref[gain_idx:gain_idx + 1, 0:w])
        return [_rope(chunk(yn, i), cos, sin, lo32) for i in range(w // LANES)]

    for c0, gain_idx, ref in ((CH_QN, 0, qn_ref), (CH_QD, 2, qd_ref)):
        y = proj(c0, 4)
        for i in range(2):
            for j, r in enumerate(normed_rope(y[:, 2 * i * LANES:(2 * i + 2) * LANES], gain_idx)):
                ref[0, :, (2 * i + j) * LANES:(2 * i + j + 1) * LANES] = (r * Q_SCALE).astype(bf16)
    y = proj(CH_QI, 2)
    for i in range(2):
        qi_ref[0, :, i * LANES:(i + 1) * LANES] = (_rope(chunk(y, i), cos, sin, lo32) * IDX_SCALE).astype(bf16)
    ki_ref[0] = _rope(proj(CH_KI, 1), cos, sin, lo32).astype(bf16)

    k_pairs = normed_rope(proj(CH_KS, 2), 1) + normed_rope(proj(CH_KD, 1), 3)
    for r, ref in zip(k_pairs, (ks_ref, kw_ref, kd_ref)):
        a, b = _dup_halves(r, lo64)
        ref[0, 0] = a.astype(bf16)
        ref[0, 1] = b.astype(bf16)
    for c0, ref in ((CH_VS, vs_ref), (CH_VW, vw_ref), (CH_VD, vd_ref)):
        a, b = _value_with_ones(proj(c0, 1), lo64)
        ref[0, 0] = a.astype(bf16)
        ref[0, 1] = b.astype(bf16)

    kc_ref[0] = proj(CH_KC, 1)
    vc_ref[0] = proj(CH_VC, 1)
    z = proj(CH_ZN, 4)
    zn_ref[0] = (z * jax.nn.sigmoid(z)).astype(bf16)
    z = proj(CH_ZD, 4)
    zd_ref[0] = (z * jax.nn.sigmoid(z)).astype(bf16)
    m = proj(CH_MISC, 1)
    misc_ref[0] = jnp.where(lane < MISC_WI, jax.nn.sigmoid(m), m * WI_SCALE)


def _in_projection(x, norm_gain, w_in, bd, cos_t, sin_t, gains):
    B, S, _ = x.shape
    tm = min(TM_PROJ, S)
    grid = (B, S // tm)
    row = lambda c: pl.BlockSpec((1, tm, c), lambda b, i: (b, i, 0))
    dup = pl.BlockSpec((1, KV_GROUPS, tm, LANES), lambda b, i: (b, 0, i, 0))
    const = lambda shape: pl.BlockSpec(shape, lambda b, i: tuple(0 for _ in shape))
    tab = pl.BlockSpec((tm, LANES), lambda b, i: (i, 0))
    sds = jax.ShapeDtypeStruct
    out_shape = (
        sds((B, S, 512), bf16), sds((B, S, 512), bf16), sds((B, S, 256), bf16), sds((B, S, LANES), bf16),
        sds((B, KV_GROUPS, S, LANES), bf16), sds((B, KV_GROUPS, S, LANES), bf16),
        sds((B, KV_GROUPS, S, LANES), bf16), sds((B, KV_GROUPS, S, LANES), bf16),
        sds((B, KV_GROUPS, S, LANES), bf16), sds((B, KV_GROUPS, S, LANES), bf16),
        sds((B, S, LANES), f32), sds((B, S, LANES), f32),
        sds((B, S, 512), bf16), sds((B, S, 512), bf16), sds((B, S, LANES), f32),
    )
    out_specs = (row(512), row(512), row(256), row(LANES), dup, dup, dup, dup, dup, dup,
                 row(LANES), row(LANES), row(512), row(512), row(LANES))
    return pl.pallas_call(
        _inproj_kernel,
        grid=grid,
        in_specs=[row(D_MODEL), const((1, D_MODEL)),
                  pl.BlockSpec((1, D_MODEL, IN_COLS), lambda b, i: (0, 0, 0), pipeline_mode=pl.Buffered(1)),
                  const((2 * LANES, 2 * LANES)), tab, tab, const((8, 2 * LANES))],
        out_specs=out_specs,
        out_shape=out_shape,
        scratch_shapes=[pltpu.VMEM((D_MODEL, N_CHUNKS * LANES), bf16)],
        compiler_params=pltpu.CompilerParams(
            dimension_semantics=("arbitrary", "arbitrary"), vmem_limit_bytes=VMEM_LIMIT),
        name="in_projection",
    )(x, norm_gain, w_in, bd, cos_t, sin_t, gains)


def _compress_kernel(src_ref, pe_ref, w1_ref, b1_ref, w2_ref, bd_ref, gain_ref, cos_ref, sin_ref,
                     out_ref, *, n_rows, is_key):
    half = CMP_BLOCK // 2
    acc_a = jnp.zeros((n_rows, 2 * CMP_HIDDEN), f32)
    acc_b = jnp.zeros((n_rows, 2 * CMP_HIDDEN), f32)
    for l in range(half):
        rows = src_ref[0, pl.ds(l, n_rows, stride=CMP_STRIDE), :]
        acc_a = acc_a + _dot((rows + pe_ref[l:l + 1, :]).astype(bf16), w1_ref[l])
        acc_b = acc_b + _dot((rows + pe_ref[l + half:l + half + 1, :]).astype(bf16), w1_ref[l + half])
    pre = acc_a + pltpu.roll(acc_b, n_rows - 1, 0) + b1_ref[...]
    hid = pre * jax.nn.sigmoid(pre)
    out = _dot(hid.astype(bf16), w2_ref[...])
    lane = lax.broadcasted_iota(jnp.int32, (1, LANES), 1)
    if is_key:
        out = _head_rms_norm(out, bd_ref[...], gain_ref[...])
        out = _rope(out, cos_ref[...], sin_ref[...], (lane % 64) < 32)
        a, b = _dup_halves(out, lane < 64)
    else:
        a, b = _value_with_ones(out, lane < 64)
    out_ref[0, 0] = a.astype(bf16)
    out_ref[0, 1] = b.astype(bf16)


def _compress(src, pe_dup, w1_bd, b1_dup, w2_bd, bd, gain_dup, cos_c, sin_c, *, is_key):
    B, S, _ = src.shape
    n_rows = S // CMP_STRIDE
    const = lambda shape: pl.BlockSpec(shape, lambda b: tuple(0 for _ in shape))
    return pl.pallas_call(
        functools.partial(_compress_kernel, n_rows=n_rows, is_key=is_key),
        grid=(B,),
        in_specs=[pl.BlockSpec((1, S, LANES), lambda b: (b, 0, 0)),
                  const((CMP_BLOCK, LANES)), const((CMP_BLOCK, LANES, 2 * CMP_HIDDEN)),
                  const((1, 2 * CMP_HIDDEN)), const((2 * CMP_HIDDEN, LANES)), const((LANES, LANES)),
                  const((1, LANES)), const((n_rows, LANES)), const((n_rows, LANES))],
        out_specs=pl.BlockSpec((1, KV_GROUPS, n_rows, LANES), lambda b: (b, 0, 0, 0)),
        out_shape=jax.ShapeDtypeStruct((B, KV_GROUPS, n_rows, LANES), bf16),
        compiler_params=pltpu.CompilerParams(
            dimension_semantics=("arbitrary",), vmem_limit_bytes=VMEM_LIMIT),
        name="compress_k" if is_key else "compress_v",
    )(src, pe_dup, w1_bd, b1_dup, w2_bd, bd, gain_dup, cos_c, sin_c)


def _half_masked(pair, h, lo64):
    keep = lo64 if h % 2 == 0 else jnp.logical_not(lo64)
    return jnp.where(keep, pair, jnp.zeros_like(pair))


def _loop_by_pairs(n, body, init, quads=False):
    done = 0
    if quads:
        four = lambda i, cr: body(4 * i + 3, body(4 * i + 2, body(4 * i + 1, body(4 * i, cr))))
        init = lax.fori_loop(0, n // 4, four, init)
        done = 4 * (n // 4)
    pair = lambda i, cr: body(2 * i + 1, body(2 * i, cr))
    carry = lax.fori_loop(done // 2, n // 2, pair, init)
    return lax.fori_loop(2 * (n // 2), n, body, carry)


def _stack_group_queries(q, g, lo64):
    slabs = []
    for r in range(HEADS_PER_GROUP):
        h = g * HEADS_PER_GROUP + r
        slabs.append(_half_masked(q[:, (h // 2) * LANES:(h // 2 + 1) * LANES], h, lo64))
    return jnp.concatenate(slabs, axis=0)


def _masked_flash(q4s, k_ref, v_ref, n_keys, bias_fns, tq, bound_ref, shift_scrs):
    rows = HEADS_PER_GROUP * tq
    bound = bound_ref[0]
    n_chunks = (n_keys + KC - 1) // KC
    n_full = n_keys // KC

    def scores(g, k0, w=KC):
        s = _nt_dot(q4s[g], k_ref[0, g, pl.ds(k0, w), :])
        return (s.reshape(HEADS_PER_GROUP, tq, w) + bias_fns[g](k0, w)[None]).reshape(rows, w)

    for g in range(KV_GROUPS):
        shift_scrs[g][...] = jnp.full((rows, 1), bound, f32)

    @pl.when(bound > MAX_FIXED_SHIFT)
    def _():
        def body(c, ms):
            k0 = pl.multiple_of(c * KC, KC)
            return tuple(jnp.maximum(ms[g], jnp.max(scores(g, k0), axis=-1, keepdims=True))
                         for g in range(KV_GROUPS))

        init = tuple(jnp.full((rows, 1), M_FLOOR, f32) for _ in range(KV_GROUPS))
        for g, m in enumerate(lax.fori_loop(0, n_chunks, body, init)):
            shift_scrs[g][...] = m

    def step(k0, w, accs):
        new = []
        for g in range(KV_GROUPS):
            p = jnp.exp2(scores(g, k0, w) - shift_scrs[g][...]).astype(bf16)
            new.append(accs[g] + _dot(p, v_ref[0, g, pl.ds(k0, w), :]))
        return tuple(new)

    def body(c, accs):
        return step(pl.multiple_of(c * KC, KC), KC, accs)

    def half_body(i, accs):
        return step(pl.multiple_of(n_full * KC, KC // 2), KC // 2, accs)

    init = tuple(jnp.zeros((rows, LANES), f32) for _ in range(KV_GROUPS))
    out = _loop_by_pairs(n_full, body, init, quads=True)
    out = lax.fori_loop(0, (n_keys - n_full * KC) // (KC // 2), half_body, out)
    return [acc / acc[:, HEAD_DIM:HEAD_DIM + 1] for acc in out]


def _group_output(og, lo64, tq):
    pairs = []
    for p in range(2):
        even = og[(2 * p) * tq:(2 * p + 1) * tq]
        odd = og[(2 * p + 1) * tq:(2 * p + 2) * tq]
        pairs.append(jnp.where(lo64, even, pltpu.roll(odd, HEAD_DIM, 1)))
    return jnp.concatenate(pairs, axis=1)


def _top_n_blocks(imp, n_sel, top_n):
    groups = [imp[SUBLANES * a:SUBLANES * (a + 1)] for a in range(n_sel // SUBLANES)]
    jsub = lax.broadcasted_iota(jnp.int32, groups[0].shape, 0)
    ranks = [jnp.zeros(groups[0].shape, f32) for _ in groups]
    for i in range(n_sel):
        row = imp[i:i + 1, :]
        for a, grp in enumerate(groups):
            if SUBLANES * a > i:
                beats = row >= grp
            elif SUBLANES * (a + 1) - 1 < i:
                beats = row > grp
            else:
                tie = jnp.where(jsub + SUBLANES * a > i, 1.0, 0.0)
                ranks[a] = ranks[a] + jnp.where(row > grp, 1.0, jnp.where(row == grp, tie, 0.0))
                continue
            ranks[a] = ranks[a] + jnp.where(beats, 1.0, 0.0)
    rank = jnp.concatenate(ranks, axis=0)
    return jnp.where(rank < top_n, 1.0, 0.0)


def _nsa_kernel(bound_ref, qn_ref, kcmp_ref, vcmp_ref, ks_ref, vs_ref, kw_ref, vw_ref, misc_ref, zn_ref, ovt_ref,
                expand_ref, o_ref, shift0_scr, shift1_scr, cshift0_scr, cshift1_scr, wshift0_scr, wshift1_scr,
                bias_scr, *, n_cmp_pad, n_sel, top_n):
    tq = TQ_NSA
    cmp_shift = [cshift0_scr, cshift1_scr]
    win_shift = [wshift0_scr, wshift1_scr]
    t0 = pl.program_id(1) * tq
    lane = lax.broadcasted_iota(jnp.int32, (1, LANES), 1)
    lo64 = lane < 64
    q = qn_ref[0]
    misc = misc_ref[0]
    t_col = t0 + lax.broadcasted_iota(jnp.int32, (tq, 1), 0)
    n_chunks = (t0 + tq + KC - 1) // KC
    win_span = WINDOW + tq
    win_start = pl.multiple_of(jnp.maximum(t0 - WINDOW, 0), tq)
    ovt = ovt_ref[...]
    rows = HEADS_PER_GROUP * tq

    def add_bias(s, bias):
        return (s.reshape(HEADS_PER_GROUP, tq, s.shape[-1]) + bias[None]).reshape(rows, s.shape[-1])

    cmp_end = lax.broadcasted_iota(jnp.int32, (1, n_cmp_pad), 1) * CMP_STRIDE + (CMP_BLOCK - 1)
    cmp_bias = jnp.where(cmp_end <= t_col, 0.0, NEG)
    win_diff = t_col - (win_start + lax.broadcasted_iota(jnp.int32, (1, win_span), 1))
    win_bias = jnp.where((win_diff >= 0) & (win_diff < WINDOW), 0.0, NEG)
    q4s = [_stack_group_queries(q, g, lo64) for g in range(KV_GROUPS)]

    def cmp_scores(g):
        return add_bias(_nt_dot(q4s[g], kcmp_ref[0, g]), cmp_bias)

    def win_scores(g):
        return add_bias(_nt_dot(q4s[g], kw_ref[0, g, pl.ds(win_start, win_span), :]), win_bias)

    for scr in cmp_shift + win_shift:
        scr[...] = jnp.full((rows, 1), bound_ref[0], f32)

    @pl.when(bound_ref[0] > MAX_FIXED_SHIFT)
    def _():
        for g in range(KV_GROUPS):
            cmp_shift[g][...] = jnp.maximum(jnp.max(cmp_scores(g), axis=-1, keepdims=True), M_FLOOR)
            win_shift[g][...] = jnp.max(win_scores(g), axis=-1, keepdims=True)

    o_cmp, o_win, sels = [], [], []
    for g in range(KV_GROUPS):
        e = jnp.exp2(cmp_scores(g) - cmp_shift[g][...])
        l = jnp.sum(e, axis=-1, keepdims=True)
        p = e * (1.0 / jnp.maximum(l, 1e-30))
        o_cmp.append(_dot(p.astype(bf16), vcmp_ref[0, g]))

        psum = p[0:tq] + p[tq:2 * tq] + p[2 * tq:3 * tq] + p[3 * tq:4 * tq]
        hi, lo = _split_bf16(psum)
        imp = _nt_dot(ovt, hi) + _nt_dot(ovt, lo)
        jrow = lax.broadcasted_iota(jnp.int32, (n_sel, tq), 0)
        tl = t0 + lax.broadcasted_iota(jnp.int32, (n_sel, tq), 1)
        cur = lax.shift_right_logical(tl, 6)
        forced = (jrow == 0) | (jrow == cur) | (jrow == cur - 1)
        imp = jnp.where(forced, FORCE, imp)
        imp = jnp.where(jrow * SEL_BLOCK <= tl, imp, NEG)
        pad_row = lax.broadcasted_iota(jnp.int32, (LANES - n_sel, tq), 0)
        chosen = jnp.concatenate([_top_n_blocks(imp, n_sel, top_n) * -NEG, jnp.where(pad_row == 0, NEG, 0.0)], axis=0)
        sels.append(chosen.T.astype(bf16))

        e = jnp.exp2(win_scores(g) - win_shift[g][...])
        acc = _dot(e.astype(bf16), vw_ref[0, g, pl.ds(win_start, win_span), :])
        o_win.append(acc / acc[:, HEAD_DIM:HEAD_DIM + 1])

    def bias_body(c, carry):
        k0 = pl.multiple_of(c * KC, KC)
        for g in range(KV_GROUPS):
            bias_scr[g, :, pl.ds(k0, KC)] = _dot(sels[g], expand_ref[:, pl.ds(k0, KC)])
        return carry

    _loop_by_pairs(n_chunks, bias_body, 0)
    k_last = pl.multiple_of((n_chunks - 1) * KC, KC)
    causal = jnp.where(k_last + lax.broadcasted_iota(jnp.int32, (1, KC), 1) <= t_col, 0.0, NEG)
    for g in range(KV_GROUPS):
        bias_scr[g, :, pl.ds(k_last, KC)] = bias_scr[g, :, pl.ds(k_last, KC)] + causal

    o_sel = _masked_flash(q4s, ks_ref, vs_ref, t0 + tq,
                          [lambda k0, w, g=g: bias_scr[g, :, pl.ds(k0, w)] for g in range(KV_GROUPS)], tq,
                          bound_ref, (shift0_scr, shift1_scr))

    group_outs = []
    for g in range(KV_GROUPS):
        def gate(branch):
            cols = [misc[:, (g * HEADS_PER_GROUP + r) * 3 + branch:(g * HEADS_PER_GROUP + r) * 3 + branch + 1]
                    for r in range(HEADS_PER_GROUP)]
            return jnp.concatenate(cols, axis=0)

        og = gate(0) * o_cmp[g] + gate(1) * o_sel[g] + gate(2) * o_win[g]
        group_outs.append(_group_output(og, lo64, tq))

    o = jnp.concatenate(group_outs, axis=1) * zn_ref[0].astype(f32)
    o_ref[0] = o.astype(bf16)


def _score_bound(q_gain, *k_gains):
    k_max = functools.reduce(jnp.maximum, [jnp.max(jnp.abs(g)) for g in k_gains])
    return (Q_SCALE * HEAD_DIM * NORM_SLACK * jnp.max(jnp.abs(q_gain)) * k_max).reshape(1)


_SMEM_SCALAR = pl.BlockSpec(memory_space=pltpu.SMEM)


def _shift_scratch(tq):
    return [pltpu.VMEM((HEADS_PER_GROUP * tq, 1), f32)] * KV_GROUPS


def _nsa_attention(bound, qn, kcmp, vcmp, ks, vs, kw, vw, misc, zn, ovt):
    B, S, _ = qn.shape
    tq = TQ_NSA
    n_cmp_pad = kcmp.shape[2]
    n_sel = S // SEL_BLOCK
    row = lambda c: pl.BlockSpec((1, tq, c), lambda b, i: (b, i, 0))
    whole = lambda n: pl.BlockSpec((1, KV_GROUPS, n, LANES), lambda b, i: (b, 0, 0, 0))
    kern = functools.partial(_nsa_kernel, n_cmp_pad=n_cmp_pad, n_sel=n_sel, top_n=min(SEL_TOPN, n_sel))
    assert n_sel < LANES
    expand = np.zeros((LANES, S), np.float32)
    expand[np.arange(S) // SEL_BLOCK, np.arange(S)] = 1.0
    expand[n_sel] = 1.0
    expand = jnp.asarray(expand, bf16)
    return pl.pallas_call(
        kern,
        grid=(B, S // tq),
        in_specs=[_SMEM_SCALAR, row(512), whole(n_cmp_pad), whole(n_cmp_pad), whole(S), whole(S), whole(S),
                  whole(S), row(LANES), row(512), pl.BlockSpec((n_sel, n_cmp_pad), lambda b, i: (0, 0)),
                  pl.BlockSpec((LANES, S), lambda b, i: (0, 0))],
        out_specs=row(512),
        out_shape=jax.ShapeDtypeStruct((B, S, 512), bf16),
        scratch_shapes=3 * _shift_scratch(tq) + [pltpu.VMEM((KV_GROUPS, tq, S), f32)],
        compiler_params=pltpu.CompilerParams(
            dimension_semantics=("arbitrary", "arbitrary"), vmem_limit_bytes=VMEM_LIMIT),
        name="nsa_attention",
    )(bound, qn, kcmp, vcmp, ks, vs, kw, vw, misc, zn, ovt, expand)


N_SLABS = 32
_TRANSPOSE_STAGES = ((16, 0x0000FFFF), (8, 0x00FF00FF), (4, 0x0F0F0F0F), (2, 0x33333333), (1, 0x55555555))


def _ordered_bits(v):
    return v ^ (jnp.right_shift(v, 31) & jnp.int32(0x7FFFFFFF))


def _bit_planes(score_ref, plane_ref, n_words):
    stages = [s for s in _TRANSPOSE_STAGES if s[0] < n_words]
    for stage, (j, mask) in enumerate(stages):
        if stage == 0:
            load = lambda k: _ordered_bits(pltpu.bitcast(score_ref[k], jnp.int32))
        else:
            load = lambda k: plane_ref[k]

        def body(p, carry, j=j, mask=mask, load=load):
            k = jnp.left_shift(p & ~(j - 1), 1) | (p & (j - 1))
            lo = load(k)
            hi = load(k + j)
            t = (lo ^ lax.shift_right_logical(hi, j)) & mask
            plane_ref[k] = lo ^ t
            plane_ref[k + j] = hi ^ jnp.left_shift(t, j)
            return carry

        lax.fori_loop(0, n_words // 2, body, 0)


def _dsa_kernel(bound_ref, qd_ref, qi_ref, misc_ref, ki_ref, kd_ref, vd_ref, zd_ref, tri_ref,
                o_ref, score_scr, plane_scr, bias_scr, shift0_scr, shift1_scr, select_scr, *, top_k, tq):
    t0 = pl.program_id(1) * tq
    lane = lax.broadcasted_iota(jnp.int32, (1, LANES), 1)
    lo64 = lane < 64
    t_col = t0 + lax.broadcasted_iota(jnp.int32, (tq, 1), 0)
    t_row = t0 + lax.broadcasted_iota(jnp.int32, (1, tq), 1)
    n_chunks = (t0 + tq + KC - 1) // KC
    slabs_per_chunk = KC // LANES
    int_min = jnp.int32(-2 ** 31)

    def chunk_slabs(c):
        return pl.ds(c * slabs_per_chunk, slabs_per_chunk)

    def to_col(row):
        return jnp.broadcast_to(row, (SUBLANES, tq)).T[:, 0:1]

    qi = qi_ref[0]
    qi_heads = [_half_masked(qi[:, (h // 2) * LANES:(h // 2 + 1) * LANES], h, lo64) for h in range(IDX_HEADS)]
    misc_t = misc_ref[0].T
    wi = [misc_t[MISC_WI + h:MISC_WI + h + 1, :] for h in range(IDX_HEADS)]

    def score_chunk(c, has_later_keys):
        k0 = pl.multiple_of(c * KC, KC)
        kib = ki_ref[0, pl.ds(k0, KC), :]
        sc = jnp.zeros((KC, tq), f32)
        for h in range(IDX_HEADS):
            sc = sc + jnp.maximum(_nt_dot(kib, qi_heads[h]), 0.0) * wi[h]
        sc = jnp.where(sc == 0.0, 0.0, sc)
        if has_later_keys:
            kpos = k0 + lax.broadcasted_iota(jnp.int32, (KC, 1), 0)
            sc = jnp.where(kpos <= t_row, sc, NON_CAUSAL_MARK)
        score_scr[chunk_slabs(c)] = sc.reshape(slabs_per_chunk, LANES, tq)

    def score_body(c, carry):
        score_chunk(c, False)
        return carry

    _loop_by_pairs(n_chunks - 1, score_body, 0)
    score_chunk(n_chunks - 1, True)

    def fill_body(c, carry):
        score_scr[chunk_slabs(c)] = jnp.full((slabs_per_chunk, LANES, tq), NON_CAUSAL_MARK, f32)
        return carry

    lax.fori_loop(n_chunks, N_SLABS // slabs_per_chunk, fill_body, 0)

    kf = float(top_k)

    def radix_select(n_words):
        _bit_planes(score_scr, plane_scr, n_words)
        all_ones = jnp.int32(-1 if n_words == 32 else 0xFFFF)

        def select_bit(i, carry):
            alive, above, thr = carry
            if n_words == 32:
                plane = plane_scr[i]
            else:
                word = plane_scr[i & 15]
                plane = jnp.where(i < 16, lax.shift_right_logical(word, 16), word) & all_ones
            plane = plane ^ jnp.where(i == 0, all_ones, jnp.int32(0))
            ones = alive & plane
            c1 = jnp.sum(lax.population_count(ones).astype(f32), axis=0, keepdims=True)
            take = (above + c1) >= kf
            alive = jnp.where(take, ones, alive ^ ones)
            above = jnp.where(take, above, above + c1)
            thr = jnp.where(take, thr | jnp.left_shift(jnp.int32(1), 31 - i), thr)
            return alive, above, thr

        init = (jnp.full((LANES, tq), all_ones, jnp.int32), jnp.zeros((1, tq), f32), jnp.zeros((1, tq), jnp.int32))
        _, above, thr_u = lax.fori_loop(0, 32, select_bit, init)
        select_scr[0:1, :] = thr_u
        select_scr[1:2, :] = pltpu.bitcast(above, jnp.int32)

    few_slabs = n_chunks * slabs_per_chunk <= 16

    @pl.when(few_slabs)
    def _():
        radix_select(16)

    @pl.when(jnp.logical_not(few_slabs))
    def _():
        radix_select(32)

    thr_u = select_scr[0:1, :]
    above = pltpu.bitcast(select_scr[1:2, :], f32)

    def key_to_score(key_row):
        return pltpu.bitcast(_ordered_bits(key_row), f32)

    def write_bias(thr_row, need_row):
        thr = to_col(thr_row)
        need = to_col(need_row)

        def body(c, carry):
            ties_before, n_above = carry
            k0 = pl.multiple_of(c * KC, KC)
            blk = score_scr[chunk_slabs(c)].reshape(KC, tq).T
            gt = jnp.where(blk > thr, 1.0, 0.0)
            eq = jnp.where(blk == thr, 1.0, 0.0)
            prefix = ties_before + _dot(eq.astype(bf16), tri_ref[...])
            chosen = gt + jnp.where(prefix <= need, eq, 0.0)
            kpos = k0 + lax.broadcasted_iota(jnp.int32, (1, KC), 1)
            bias_scr[:, pl.ds(k0, KC)] = jnp.where(kpos <= t_col, jnp.where(chosen > 0.5, 0.0, NEG), NEG)
            return prefix[:, KC - 1:KC], n_above + jnp.sum(gt, axis=-1, keepdims=True)

        zero = jnp.zeros((tq, 1), f32)
        return _loop_by_pairs(n_chunks, body, (zero, zero))

    ties, n_above = write_bias(key_to_score(thr_u ^ int_min), kf - above)
    verified = (n_above == to_col(above)) & (n_above < kf) & (n_above + ties >= kf)

    @pl.when(jnp.sum(jnp.where(verified, 0.0, 1.0)) > 0.0)
    def _():
        def count(pred):
            def body(c, acc):
                blk = score_scr[chunk_slabs(c)].reshape(KC, tq)
                return acc + jnp.sum(jnp.where(pred(blk), 1.0, 0.0), axis=0, keepdims=True)
            return lax.fori_loop(0, N_SLABS // slabs_per_chunk, body, jnp.zeros((1, tq), f32))

        key = jnp.where(count(lambda b: b >= 0.0) >= kf, jnp.int32(0), int_min)

        def bit(i, key):
            cand = key + jnp.left_shift(jnp.int32(1), 30 - i)
            cand_f = key_to_score(cand)
            return jnp.where(count(lambda b: b >= cand_f) >= kf, cand, key)

        thr_f = key_to_score(lax.fori_loop(0, 31, bit, key))
        write_bias(thr_f, kf - count(lambda b: b > thr_f))

    def bias_fn(k0, w):
        return bias_scr[:, pl.ds(k0, w)]

    qd = qd_ref[0]
    q4s = [_stack_group_queries(qd, g, lo64) for g in range(KV_GROUPS)]
    outs = _masked_flash(q4s, kd_ref, vd_ref, t0 + tq, [bias_fn] * KV_GROUPS, tq,
                         bound_ref, (shift0_scr, shift1_scr))
    o = jnp.concatenate([_group_output(og, lo64, tq) for og in outs], axis=1) * zd_ref[0].astype(f32)
    o_ref[0] = o.astype(bf16)


def _dsa_attention(bound, qd, qi, misc, ki, kd, vd, zd):
    B, S, _ = qd.shape
    tq = TQ_DSA
    assert S <= N_SLABS * LANES and KC % tq == 0
    tri = jnp.asarray(np.triu(np.ones((KC, KC), np.float32)), bf16)
    row = lambda c: pl.BlockSpec((1, tq, c), lambda b, i: (b, i, 0))
    whole = pl.BlockSpec((1, KV_GROUPS, S, LANES), lambda b, i: (b, 0, 0, 0))
    return pl.pallas_call(
        functools.partial(_dsa_kernel, top_k=min(DSA_TOPK_MAX, S // 4), tq=tq),
        grid=(B, S // tq),
        in_specs=[_SMEM_SCALAR, row(512), row(256), row(LANES), pl.BlockSpec((1, S, LANES), lambda b, i: (b, 0, 0)),
                  whole, whole, row(512), pl.BlockSpec((KC, KC), lambda b, i: (0, 0))],
        out_specs=row(512),
        out_shape=jax.ShapeDtypeStruct((B, S, 512), bf16),
        scratch_shapes=[pltpu.VMEM((N_SLABS, LANES, tq), f32), pltpu.VMEM((N_SLABS, LANES, tq), jnp.int32),
                        pltpu.VMEM((tq, S), f32)] + _shift_scratch(tq) + [pltpu.VMEM((SUBLANES, tq), jnp.int32)],
        compiler_params=pltpu.CompilerParams(
            dimension_semantics=("arbitrary", "arbitrary"), vmem_limit_bytes=VMEM_LIMIT),
        name="dsa_attention",
    )(bound, qd, qi, misc, ki, kd, vd, zd, tri)


def _outproj_kernel(x_ref, on_ref, od_ref, w_ref, o_ref):
    half = w_ref.shape[0] // 2
    o_ref[0] = x_ref[0] + _dot(on_ref[0], w_ref[0:half, :]) + _dot(od_ref[0], w_ref[half:, :])


def _out_projection(x, o_nsa, o_dsa, w_out):
    B, S, D = x.shape
    tm = min(TM_OUT, S)
    row = lambda c: pl.BlockSpec((1, tm, c), lambda b, i: (b, i, 0))
    return pl.pallas_call(
        _outproj_kernel,
        grid=(B, S // tm),
        in_specs=[row(D), row(512), row(512), pl.BlockSpec(w_out.shape, lambda b, i: (0, 0))],
        out_specs=row(D),
        out_shape=jax.ShapeDtypeStruct((B, S, D), f32),
        compiler_params=pltpu.CompilerParams(
            dimension_semantics=("arbitrary", "arbitrary"), vmem_limit_bytes=VMEM_LIMIT),
        name="out_projection",
    )(x, o_nsa, o_dsa, w_out)


def _rope_tables(pos):
    half = HEAD_DIM // 2
    inv_freq = ROPE_THETA ** (-jnp.arange(half, dtype=f32) / half)
    ang = pos[:, None] * inv_freq[None, :]
    cos, sin = jnp.cos(ang), jnp.sin(ang)
    cos_t = jnp.tile(cos, (1, LANES // half))
    sin_t = jnp.tile(jnp.concatenate([-sin, sin], axis=1), (1, LANES // HEAD_DIM))
    return cos_t, sin_t


def _overlap_t(seq, n_cmp_pad):
    n_cmp = (seq - CMP_BLOCK) // CMP_STRIDE + 1
    n_sel = seq // SEL_BLOCK
    c_start = np.arange(n_cmp) * CMP_STRIDE
    j_start = np.arange(n_sel) * SEL_BLOCK
    ov = np.clip(np.minimum(c_start[:, None] + CMP_BLOCK, j_start[None, :] + SEL_BLOCK)
                 - np.maximum(c_start[:, None], j_start[None, :]), 0, None).astype(np.float32) / CMP_BLOCK
    out = np.zeros((n_sel, n_cmp_pad), np.float32)
    out[:, :n_cmp] = ov.T
    return out


def _block_diag2(w):
    z = jnp.zeros_like(w)
    return jnp.concatenate([jnp.concatenate([w, z], axis=-1), jnp.concatenate([z, w], axis=-1)], axis=-2)


def _layer(x, norm_gain, w_in, nsa_q_gain, nsa_kc_gain, nsa_ks_gain, nsa_kw_gain,
           cmp_pe_k, cmp_k_w1, cmp_k_b1, cmp_k_w2, cmp_pe_v, cmp_v_w1, cmp_v_b1, cmp_v_w2,
           dsa_q_gain, dsa_k_gain, w_out):
    B, S, _ = x.shape
    assert S % KC == 0 and S >= WINDOW + TQ_NSA
    n_cmp_pad = S // CMP_STRIDE

    head_of_lane = np.arange(2 * LANES) // HEAD_DIM
    bd = jnp.asarray(head_of_lane[:, None] == head_of_lane[None, :], bf16)
    dup = lambda v: jnp.tile(v.reshape(1, -1), (1, 2))
    quad = lambda v: jnp.tile(v.reshape(1, -1), (1, 4))
    gains = jnp.concatenate([quad(nsa_q_gain), jnp.concatenate([dup(nsa_ks_gain), dup(nsa_kw_gain)], axis=1),
                             quad(dsa_q_gain), quad(dsa_k_gain), jnp.ones((4, 2 * LANES), f32)], axis=0)
    cos_t, sin_t = _rope_tables(jnp.arange(S, dtype=f32))

    (qn, qd, qi, ki, ks, kw, kd, vs, vw, vd, kc, vc, zn, zd, misc) = _in_projection(
        x, norm_gain.reshape(1, -1), w_in, bd, cos_t, sin_t, gains)

    cmp_pos = (jnp.arange(n_cmp_pad) * CMP_STRIDE + CMP_BLOCK - 1).astype(f32)
    cos_c, sin_c = _rope_tables(cmp_pos)

    def cmp_weights(pe, w1, b1, w2):
        w1_bd = _block_diag2(w1.reshape(CMP_BLOCK, HEAD_DIM, CMP_HIDDEN)).astype(bf16)
        return jnp.tile(pe, (1, 2)), w1_bd, dup(b1), _block_diag2(w2).astype(bf16)

    kcmp = _compress(kc, *cmp_weights(cmp_pe_k, cmp_k_w1, cmp_k_b1, cmp_k_w2), bd[:LANES, :LANES], dup(nsa_kc_gain),
                     cos_c, sin_c, is_key=True)
    vcmp = _compress(vc, *cmp_weights(cmp_pe_v, cmp_v_w1, cmp_v_b1, cmp_v_w2), bd[:LANES, :LANES], dup(nsa_kc_gain),
                     cos_c, sin_c, is_key=False)

    ovt = jnp.asarray(_overlap_t(S, n_cmp_pad), bf16)
    o_nsa = _nsa_attention(_score_bound(nsa_q_gain, nsa_kc_gain, nsa_ks_gain, nsa_kw_gain),
                           qn, kcmp, vcmp, ks, vs, kw, vw, misc, zn, ovt)

    o_dsa = _dsa_attention(_score_bound(dsa_q_gain, dsa_k_gain), qd, qi, misc, ki, kd, vd, zd)

    return _out_projection(x, o_nsa, o_dsa, w_out.astype(bf16))


def kernel(x, norm_gain, w_in, nsa_q_gain, nsa_kc_gain, nsa_ks_gain, nsa_kw_gain, cmp_pe_k, cmp_k_w1,
           cmp_k_b1, cmp_k_w2, cmp_pe_v, cmp_v_w1, cmp_v_b1, cmp_v_w2, dsa_q_gain, dsa_k_gain, w_out):
    for l in range(norm_gain.shape[0]):
        x = _layer(x, norm_gain[l], w_in[l:l + 1], nsa_q_gain[l], nsa_kc_gain[l], nsa_ks_gain[l], nsa_kw_gain[l],
                   cmp_pe_k[l], cmp_k_w1[l], cmp_k_b1[l], cmp_k_w2[l], cmp_pe_v[l], cmp_v_w1[l], cmp_v_b1[l],
                   cmp_v_w2[l], dsa_q_gain[l], dsa_k_gain[l], w_out[l])
    return x
```

```python
import functools

import numpy as np
import jax
import jax.numpy as jnp
from jax import lax
from jax.experimental import pallas as pl
from jax.experimental.pallas import tpu as pltpu

D_MODEL = 1024
HEAD_DIM = 64
NSA_HEADS = 8
DSA_HEADS = 8
KV_GROUPS = 2
HEADS_PER_GROUP = 4
IDX_HEADS = 4
CMP_BLOCK = 32
CMP_STRIDE = 16
CMP_HIDDEN = 256
SEL_BLOCK = 64
SEL_TOPN = 16
WINDOW = 512
DSA_TOPK_MAX = 256
ROPE_THETA = 10000.0
EPS = 1e-6
NEG = -1e30
FORCE = 1e6
ATTN_SCALE = HEAD_DIM ** -0.5
IDX_SCALE = HEAD_DIM ** -0.5
WI_SCALE = IDX_HEADS ** -0.5
LOG2E = 1.4426950408889634
Q_SCALE = ATTN_SCALE * LOG2E

LANES = 128
SUBLANES = 8
M_FLOOR = -5e29
NON_CAUSAL_MARK = -3.0e38
MAX_FIXED_SHIFT = 30.0
NORM_SLACK = 1.03
VMEM_LIMIT = 56 * 1024 * 1024

IN_WIDTHS = (512, 128, 128, 128, 128, 128, 128, 24, 512, 512, 128, 128, 256, 64, 4, 512)
IN_NAMES = ("q_n", "kc", "vc", "ks", "vs", "kw", "vw", "gate", "z_n",
            "q_d", "k_d", "v_d", "qi", "ki", "wi", "z_d")
IN_COLS = sum(IN_WIDTHS)

CH_QN, CH_KS, CH_KW, CH_QD, CH_KD, CH_QI, CH_KI = 0, 4, 5, 6, 10, 11, 13
CH_KC, CH_VC, CH_VS, CH_VW, CH_VD, CH_ZN, CH_ZD, CH_MISC = 14, 15, 16, 17, 18, 19, 23, 27
N_CHUNKS = 28
MISC_WI = 24

TM_PROJ = 512
TM_OUT = 1024
TQ_NSA = 256
TQ_DSA = 256
KC = 512

f32 = jnp.float32
bf16 = jnp.bfloat16


def _repack_weight(w_ref, w16_ref):
    off = dict(zip(IN_NAMES, np.cumsum((0,) + IN_WIDTHS[:-1])))
    width = dict(zip(IN_NAMES, IN_WIDTHS))
    order = ("q_n", "ks", "kw", "q_d", "k_d", "qi", "ki", "ki", "kc", "vc", "vs", "vw", "v_d", "z_n", "z_d",
             "gate", "wi")
    dst = 0
    for n in order:
        w16_ref[:, dst:dst + width[n]] = w_ref[0, :, off[n]:off[n] + width[n]].astype(bf16)
        dst += width[n]
    w16_ref[:, dst:] = jnp.zeros((D_MODEL, N_CHUNKS * LANES - dst), bf16)


def _nt_dot(a, b):
    return lax.dot_general(a, b, (((1,), (1,)), ((), ())), preferred_element_type=f32)


def _dot(a, b):
    return jnp.dot(a, b, preferred_element_type=f32)


def _split_bf16(v):
    hi = v.astype(bf16)
    lo = (v - hi.astype(f32)).astype(bf16)
    return hi, lo


def _head_rms_norm(y, bd, gain_row):
    ssq = _dot((y * y).astype(bf16), bd)
    return y * lax.rsqrt(ssq * (1.0 / HEAD_DIM) + EPS) * gain_row


def _rope(y, cos, sin_signed, lo32):
    partner = jnp.where(lo32, pltpu.roll(y, LANES - 32, 1), pltpu.roll(y, 32, 1))
    return y * cos + partner * sin_signed


def _dup_halves(y, lo64):
    r = pltpu.roll(y, 64, 1)
    return jnp.where(lo64, y, r), jnp.where(lo64, r, y)


def _value_with_ones(y, lo64):
    return jnp.where(lo64, y, 1.0), jnp.where(lo64, pltpu.roll(y, 64, 1), 1.0)


def _inproj_kernel(x_ref, ng_ref, w_ref, bd_ref, cos_ref, sin_ref, gains_ref,
                   qn_ref, qd_ref, qi_ref, ki_ref, ks_ref, kw_ref, kd_ref,
                   vs_ref, vw_ref, vd_ref, kc_ref, vc_ref, zn_ref, zd_ref, misc_ref, w16_ref):
    @pl.when((pl.program_id(0) == 0) & (pl.program_id(1) == 0))
    def _():
        _repack_weight(w_ref, w16_ref)

    x = x_ref[0]
    ms = jnp.mean(x * x, axis=-1, keepdims=True)
    h = (x * lax.rsqrt(ms + EPS) * ng_ref[...]).astype(bf16)
    cos = cos_ref[...]
    sin = sin_ref[...]
    bd = bd_ref[...]
    lane = lax.broadcasted_iota(jnp.int32, (1, LANES), 1)
    lo32 = (lane % 64) < 32
    lo64 = lane < 64

    def proj(c0, n):
        return _dot(h, w16_ref[:, c0 * LANES:(c0 + n) * LANES])

    def chunk(y, i):
        return y[:, i * LANES:(i + 1) * LANES]

    def normed_rope(y, gain_idx):
        w = y.shape[1]
        yn = _head_rms_norm(y, bd[:w, :w], gains_ref[gain_idx:gain_idx + 1, 0:w])
        return [_rope(chunk(yn, i), cos, sin, lo32) for i in range(w // LANES)]

    for c0, gain_idx, ref in ((CH_QN, 0, qn_ref), (CH_QD, 2, qd_ref)):
        y = proj(c0, 4)
        for i in range(2):
            for j, r in enumerate(normed_rope(y[:, 2 * i * LANES:(2 * i + 2) * LANES], gain_idx)):
                ref[0, :, (2 * i + j) * LANES:(2 * i + j + 1) * LANES] = (r * Q_SCALE).astype(bf16)
    y = proj(CH_QI, 2)
    for i in range(2):
        qi_ref[0, :, i * LANES:(i + 1) * LANES] = (_rope(chunk(y, i), cos, sin, lo32) * IDX_SCALE).astype(bf16)
    ki_ref[0] = _rope(proj(CH_KI, 1), cos, sin, lo32).astype(bf16)

    k_pairs = normed_rope(proj(CH_KS, 2), 1) + normed_rope(proj(CH_KD, 1), 3)
    for r, ref in zip(k_pairs, (ks_ref, kw_ref, kd_ref)):
        a, b = _dup_halves(r, lo64)
        ref[0, 0] = a.astype(bf16)
        ref[0, 1] = b.astype(bf16)
    for c0, ref in ((CH_VS, vs_ref), (CH_VW, vw_ref), (CH_VD, vd_ref)):
        a, b = _value_with_ones(proj(c0, 1), lo64)
        ref[0, 0] = a.astype(bf16)
        ref[0, 1] = b.astype(bf16)

    kc_ref[0] = proj(CH_KC, 1)
    vc_ref[0] = proj(CH_VC, 1)
    z = proj(CH_ZN, 4)
    zn_ref[0] = (z * jax.nn.sigmoid(z)).astype(bf16)
    z = proj(CH_ZD, 4)
    zd_ref[0] = (z * jax.nn.sigmoid(z)).astype(bf16)
    m = proj(CH_MISC, 1)
    misc_ref[0] = jnp.where(lane < MISC_WI, jax.nn.sigmoid(m), m * WI_SCALE)


def _in_projection(x, norm_gain, w_in, bd, cos_t, sin_t, gains):
    B, S, _ = x.shape
    tm = min(TM_PROJ, S)
    grid = (B, S // tm)
    row = lambda c: pl.BlockSpec((1, tm, c), lambda b, i: (b, i, 0))
    dup = pl.BlockSpec((1, KV_GROUPS, tm, LANES), lambda b, i: (b, 0, i, 0))
    const = lambda shape: pl.BlockSpec(shape, lambda b, i: tuple(0 for _ in shape))
    tab = pl.BlockSpec((tm, LANES), lambda b, i: (i, 0))
    sds = jax.ShapeDtypeStruct
    out_shape = (
        sds((B, S, 512), bf16), sds((B, S, 512), bf16), sds((B, S, 256), bf16), sds((B, S, LANES), bf16),
        sds((B, KV_GROUPS, S, LANES), bf16), sds((B, KV_GROUPS, S, LANES), bf16),
        sds((B, KV_GROUPS, S, LANES), bf16), sds((B, KV_GROUPS, S, LANES), bf16),
        sds((B, KV_GROUPS, S, LANES), bf16), sds((B, KV_GROUPS, S, LANES), bf16),
        sds((B, S, LANES), f32), sds((B, S, LANES), f32),
        sds((B, S, 512), bf16), sds((B, S, 512), bf16), sds((B, S, LANES), f32),
    )
    out_specs = (row(512), row(512), row(256), row(LANES), dup, dup, dup, dup, dup, dup,
                 row(LANES), row(LANES), row(512), row(512), row(LANES))
    return pl.pallas_call(
        _inproj_kernel,
        grid=grid,
        in_specs=[row(D_MODEL), const((1, D_MODEL)),
                  pl.BlockSpec((1, D_MODEL, IN_COLS), lambda b, i: (0, 0, 0), pipeline_mode=pl.Buffered(1)),
                  const((2 * LANES, 2 * LANES)), tab, tab, const((8, 2 * LANES))],
        out_specs=out_specs,
        out_shape=out_shape,
        scratch_shapes=[pltpu.VMEM((D_MODEL, N_CHUNKS * LANES), bf16)],
        compiler_params=pltpu.CompilerParams(
            dimension_semantics=("arbitrary", "arbitrary"), vmem_limit_bytes=VMEM_LIMIT),
        name="in_projection",
    )(x, norm_gain, w_in, bd, cos_t, sin_t, gains)


def _compress_kernel(src_ref, pe_ref, w1_ref, b1_ref, w2_ref, bd_ref, gain_ref, cos_ref, sin_ref,
                     out_ref, *, n_rows, is_key):
    half = CMP_BLOCK // 2
    acc_a = jnp.zeros((n_rows, 2 * CMP_HIDDEN), f32)
    acc_b = jnp.zeros((n_rows, 2 * CMP_HIDDEN), f32)
    for l in range(half):
        rows = src_ref[0, pl.ds(l, n_rows, stride=CMP_STRIDE), :]
        acc_a = acc_a + _dot((rows + pe_ref[l:l + 1, :]).astype(bf16), w1_ref[l])
        acc_b = acc_b + _dot((rows + pe_ref[l + half:l + half + 1, :]).astype(bf16), w1_ref[l + half])
    pre = acc_a + pltpu.roll(acc_b, n_rows - 1, 0) + b1_ref[...]
    hid = pre * jax.nn.sigmoid(pre)
    out = _dot(hid.astype(bf16), w2_ref[...])
    lane = lax.broadcasted_iota(jnp.int32, (1, LANES), 1)
    if is_key:
        out = _head_rms_norm(out, bd_ref[...], gain_ref[...])
        out = _rope(out, cos_ref[...], sin_ref[...], (lane % 64) < 32)
        a, b = _dup_halves(out, lane < 64)
    else:
        a, b = _value_with_ones(out, lane < 64)
    out_ref[0, 0] = a.astype(bf16)
    out_ref[0, 1] = b.astype(bf16)


def _compress(src, pe_dup, w1_bd, b1_dup, w2_bd, bd, gain_dup, cos_c, sin_c, *, is_key):
    B, S, _ = src.shape
    n_rows = S // CMP_STRIDE
    const = lambda shape: pl.BlockSpec(shape, lambda b: tuple(0 for _ in shape))
    return pl.pallas_call(
        functools.partial(_compress_kernel, n_rows=n_rows, is_key=is_key),
        grid=(B,),
        in_specs=[pl.BlockSpec((1, S, LANES), lambda b: (b, 0, 0)),
                  const((CMP_BLOCK, LANES)), const((CMP_BLOCK, LANES, 2 * CMP_HIDDEN)),
                  const((1, 2 * CMP_HIDDEN)), const((2 * CMP_HIDDEN, LANES)), const((LANES, LANES)),
                  const((1, LANES)), const((n_rows, LANES)), const((n_rows, LANES))],
        out_specs=pl.BlockSpec((1, KV_GROUPS, n_rows, LANES), lambda b: (b, 0, 0, 0)),
        out_shape=jax.ShapeDtypeStruct((B, KV_GROUPS, n_rows, LANES), bf16),
        compiler_params=pltpu.CompilerParams(
            dimension_semantics=("arbitrary",), vmem_limit_bytes=VMEM_LIMIT),
        name="compress_k" if is_key else "compress_v",
    )(src, pe_dup, w1_bd, b1_dup, w2_bd, bd, gain_dup, cos_c, sin_c)


def _half_masked(pair, h, lo64):
    keep = lo64 if h % 2 == 0 else jnp.logical_not(lo64)
    return jnp.where(keep, pair, jnp.zeros_like(pair))


def _loop_by_pairs(n, body, init, quads=False):
    done = 0
    if quads:
        four = lambda i, cr: body(4 * i + 3, body(4 * i + 2, body(4 * i + 1, body(4 * i, cr))))
        init = lax.fori_loop(0, n // 4, four, init)
        done = 4 * (n // 4)
    pair = lambda i, cr: body(2 * i + 1, body(2 * i, cr))
    carry = lax.fori_loop(done // 2, n // 2, pair, init)
    return lax.fori_loop(2 * (n // 2), n, body, carry)


def _stack_group_queries(q, g, lo64):
    slabs = []
    for r in range(HEADS_PER_GROUP):
        h = g * HEADS_PER_GROUP + r
        slabs.append(_half_masked(q[:, (h // 2) * LANES:(h // 2 + 1) * LANES], h, lo64))
    return jnp.concatenate(slabs, axis=0)


def _masked_flash(q4s, k_ref, v_ref, n_chunks, bias_fns, tq, bound_ref, shift_scrs):
    rows = HEADS_PER_GROUP * tq
    bound = bound_ref[0]

    q8s = [q4.astype(jnp.float8_e4m3fn) for q4 in q4s]

    def scores(g, k0):
        s = _nt_dot(q8s[g], k_ref[0, g, pl.ds(k0, KC), :].astype(jnp.float8_e4m3fn))
        return (s.reshape(HEADS_PER_GROUP, tq, KC) + bias_fns[g](k0)[None]).reshape(rows, KC)

    for g in range(KV_GROUPS):
        shift_scrs[g][...] = jnp.full((rows, 1), bound, f32)

    @pl.when(bound > MAX_FIXED_SHIFT)
    def _():
        def body(c, ms):
            k0 = pl.multiple_of(c * KC, KC)
            return tuple(jnp.maximum(ms[g], jnp.max(scores(g, k0), axis=-1, keepdims=True))
                         for g in range(KV_GROUPS))

        init = tuple(jnp.full((rows, 1), M_FLOOR, f32) for _ in range(KV_GROUPS))
        for g, m in enumerate(lax.fori_loop(0, n_chunks, body, init)):
            shift_scrs[g][...] = m

    def body(c, accs):
        k0 = pl.multiple_of(c * KC, KC)
        new = []
        for g in range(KV_GROUPS):
            p = jnp.exp2(scores(g, k0) - shift_scrs[g][...]).astype(bf16)
            new.append(accs[g] + _dot(p, v_ref[0, g, pl.ds(k0, KC), :]))
        return tuple(new)

    init = tuple(jnp.zeros((rows, LANES), f32) for _ in range(KV_GROUPS))
    out = _loop_by_pairs(n_chunks, body, init, quads=True)
    return [acc / acc[:, HEAD_DIM:HEAD_DIM + 1] for acc in out]


def _group_output(og, lo64, tq):
    pairs = []
    for p in range(2):
        even = og[(2 * p) * tq:(2 * p + 1) * tq]
        odd = og[(2 * p + 1) * tq:(2 * p + 2) * tq]
        pairs.append(jnp.where(lo64, even, pltpu.roll(odd, HEAD_DIM, 1)))
    return jnp.concatenate(pairs, axis=1)


def _top_n_blocks(imp, n_sel, top_n):
    groups = [imp[SUBLANES * a:SUBLANES * (a + 1)] for a in range(n_sel // SUBLANES)]
    jsub = lax.broadcasted_iota(jnp.int32, groups[0].shape, 0)
    ranks = [jnp.zeros(groups[0].shape, f32) for _ in groups]
    for i in range(n_sel):
        row = imp[i:i + 1, :]
        for a, grp in enumerate(groups):
            if SUBLANES * a > i:
                beats = row >= grp
            elif SUBLANES * (a + 1) - 1 < i:
                beats = row > grp
            else:
                tie = jnp.where(jsub + SUBLANES * a > i, 1.0, 0.0)
                ranks[a] = ranks[a] + jnp.where(row > grp, 1.0, jnp.where(row == grp, tie, 0.0))
                continue
            ranks[a] = ranks[a] + jnp.where(beats, 1.0, 0.0)
    rank = jnp.concatenate(ranks, axis=0)
    return jnp.where(rank < top_n, 1.0, 0.0)


def _nsa_kernel(bound_ref, qn_ref, kcmp_ref, vcmp_ref, ks_ref, vs_ref, kw_ref, vw_ref, misc_ref, zn_ref, ovt_ref,
                expand_ref, o_ref, shift0_scr, shift1_scr, cshift0_scr, cshift1_scr, wshift0_scr, wshift1_scr,
                bias_scr, *, n_cmp_pad, n_sel, top_n):
    tq = TQ_NSA
    cmp_shift = [cshift0_scr, cshift1_scr]
    win_shift = [wshift0_scr, wshift1_scr]
    t0 = pl.program_id(1) * tq
    lane = lax.broadcasted_iota(jnp.int32, (1, LANES), 1)
    lo64 = lane < 64
    q = qn_ref[0]
    misc = misc_ref[0]
    t_col = t0 + lax.broadcasted_iota(jnp.int32, (tq, 1), 0)
    n_chunks = (t0 + tq + KC - 1) // KC
    win_span = WINDOW + tq
    win_start = pl.multiple_of(jnp.maximum(t0 - WINDOW, 0), tq)
    ovt = ovt_ref[...]
    rows = HEADS_PER_GROUP * tq

    def add_bias(s, bias):
        return (s.reshape(HEADS_PER_GROUP, tq, s.shape[-1]) + bias[None]).reshape(rows, s.shape[-1])

    cmp_end = lax.broadcasted_iota(jnp.int32, (1, n_cmp_pad), 1) * CMP_STRIDE + (CMP_BLOCK - 1)
    cmp_bias = jnp.where(cmp_end <= t_col, 0.0, NEG)
    win_diff = t_col - (win_start + lax.broadcasted_iota(jnp.int32, (1, win_span), 1))
    win_bias = jnp.where((win_diff >= 0) & (win_diff < WINDOW), 0.0, NEG)
    q4s = [_stack_group_queries(q, g, lo64) for g in range(KV_GROUPS)]

    def cmp_scores(g):
        return add_bias(_nt_dot(q4s[g], kcmp_ref[0, g]), cmp_bias)

    def win_scores(g):
        return add_bias(_nt_dot(q4s[g], kw_ref[0, g, pl.ds(win_start, win_span), :]), win_bias)

    for scr in cmp_shift + win_shift:
        scr[...] = jnp.full((rows, 1), bound_ref[0], f32)

    @pl.when(bound_ref[0] > MAX_FIXED_SHIFT)
    def _():
        for g in range(KV_GROUPS):
            cmp_shift[g][...] = jnp.maximum(jnp.max(cmp_scores(g), axis=-1, keepdims=True), M_FLOOR)
            win_shift[g][...] = jnp.max(win_scores(g), axis=-1, keepdims=True)

    o_cmp, o_win, sels = [], [], []
    for g in range(KV_GROUPS):
        e = jnp.exp2(cmp_scores(g) - cmp_shift[g][...])
        l = jnp.sum(e, axis=-1, keepdims=True)
        p = e * (1.0 / jnp.maximum(l, 1e-30))
        o_cmp.append(_dot(p.astype(bf16), vcmp_ref[0, g]))

        psum = p[0:tq] + p[tq:2 * tq] + p[2 * tq:3 * tq] + p[3 * tq:4 * tq]
        hi, lo = _split_bf16(psum)
        imp = _nt_dot(ovt, hi) + _nt_dot(ovt, lo)
        jrow = lax.broadcasted_iota(jnp.int32, (n_sel, tq), 0)
        tl = t0 + lax.broadcasted_iota(jnp.int32, (n_sel, tq), 1)
        cur = lax.shift_right_logical(tl, 6)
        forced = (jrow == 0) | (jrow == cur) | (jrow == cur - 1)
        imp = jnp.where(forced, FORCE, imp)
        imp = jnp.where(jrow * SEL_BLOCK <= tl, imp, NEG)
        pad_row = lax.broadcasted_iota(jnp.int32, (LANES - n_sel, tq), 0)
        chosen = jnp.concatenate([_top_n_blocks(imp, n_sel, top_n) * -NEG, jnp.where(pad_row == 0, NEG, 0.0)], axis=0)
        sels.append(chosen.T.astype(bf16))

        e = jnp.exp2(win_scores(g) - win_shift[g][...])
        acc = _dot(e.astype(bf16), vw_ref[0, g, pl.ds(win_start, win_span), :])
        o_win.append(acc / acc[:, HEAD_DIM:HEAD_DIM + 1])

    def bias_body(c, carry):
        k0 = pl.multiple_of(c * KC, KC)
        for g in range(KV_GROUPS):
            bias_scr[g, :, pl.ds(k0, KC)] = _dot(sels[g], expand_ref[:, pl.ds(k0, KC)])
        return carry

    _loop_by_pairs(n_chunks, bias_body, 0)
    k_last = pl.multiple_of((n_chunks - 1) * KC, KC)
    causal = jnp.where(k_last + lax.broadcasted_iota(jnp.int32, (1, KC), 1) <= t_col, 0.0, NEG)
    for g in range(KV_GROUPS):
        bias_scr[g, :, pl.ds(k_last, KC)] = bias_scr[g, :, pl.ds(k_last, KC)] + causal

    o_sel = _masked_flash(q4s, ks_ref, vs_ref, n_chunks,
                          [lambda k0, g=g: bias_scr[g, :, pl.ds(k0, KC)] for g in range(KV_GROUPS)], tq,
                          bound_ref, (shift0_scr, shift1_scr))

    group_outs = []
    for g in range(KV_GROUPS):
        def gate(branch):
            cols = [misc[:, (g * HEADS_PER_GROUP + r) * 3 + branch:(g * HEADS_PER_GROUP + r) * 3 + branch + 1]
                    for r in range(HEADS_PER_GROUP)]
            return jnp.concatenate(cols, axis=0)

        og = gate(0) * o_cmp[g] + gate(1) * o_sel[g] + gate(2) * o_win[g]
        group_outs.append(_group_output(og, lo64, tq))

    o = jnp.concatenate(group_outs, axis=1) * zn_ref[0].astype(f32)
    o_ref[0] = o.astype(bf16)


def _score_bound(q_gain, *k_gains):
    k_max = functools.reduce(jnp.maximum, [jnp.max(jnp.abs(g)) for g in k_gains])
    return (Q_SCALE * HEAD_DIM * NORM_SLACK * jnp.max(jnp.abs(q_gain)) * k_max).reshape(1)


_SMEM_SCALAR = pl.BlockSpec(memory_space=pltpu.SMEM)


def _shift_scratch(tq):
    return [pltpu.VMEM((HEADS_PER_GROUP * tq, 1), f32)] * KV_GROUPS


def _nsa_attention(bound, qn, kcmp, vcmp, ks, vs, kw, vw, misc, zn, ovt):
    B, S, _ = qn.shape
    tq = TQ_NSA
    n_cmp_pad = kcmp.shape[2]
    n_sel = S // SEL_BLOCK
    row = lambda c: pl.BlockSpec((1, tq, c), lambda b, i: (b, i, 0))
    whole = lambda n: pl.BlockSpec((1, KV_GROUPS, n, LANES), lambda b, i: (b, 0, 0, 0))
    kern = functools.partial(_nsa_kernel, n_cmp_pad=n_cmp_pad, n_sel=n_sel, top_n=min(SEL_TOPN, n_sel))
    assert n_sel < LANES
    expand = np.zeros((LANES, S), np.float32)
    expand[np.arange(S) // SEL_BLOCK, np.arange(S)] = 1.0
    expand[n_sel] = 1.0
    expand = jnp.asarray(expand, bf16)
    return pl.pallas_call(
        kern,
        grid=(B, S // tq),
        in_specs=[_SMEM_SCALAR, row(512), whole(n_cmp_pad), whole(n_cmp_pad), whole(S), whole(S), whole(S),
                  whole(S), row(LANES), row(512), pl.BlockSpec((n_sel, n_cmp_pad), lambda b, i: (0, 0)),
                  pl.BlockSpec((LANES, S), lambda b, i: (0, 0))],
        out_specs=row(512),
        out_shape=jax.ShapeDtypeStruct((B, S, 512), bf16),
        scratch_shapes=3 * _shift_scratch(tq) + [pltpu.VMEM((KV_GROUPS, tq, S), f32)],
        compiler_params=pltpu.CompilerParams(
            dimension_semantics=("arbitrary", "arbitrary"), vmem_limit_bytes=VMEM_LIMIT),
        name="nsa_attention",
    )(bound, qn, kcmp, vcmp, ks, vs, kw, vw, misc, zn, ovt, expand)


N_SLABS = 32
_TRANSPOSE_STAGES = ((16, 0x0000FFFF), (8, 0x00FF00FF), (4, 0x0F0F0F0F), (2, 0x33333333), (1, 0x55555555))


def _ordered_bits(v):
    return v ^ (jnp.right_shift(v, 31) & jnp.int32(0x7FFFFFFF))


def _bit_planes(score_ref, plane_ref, n_words):
    stages = [s for s in _TRANSPOSE_STAGES if s[0] < n_words]
    for stage, (j, mask) in enumerate(stages):
        if stage == 0:
            load = lambda k: _ordered_bits(pltpu.bitcast(score_ref[k], jnp.int32))
        else:
            load = lambda k: plane_ref[k]

        def body(p, carry, j=j, mask=mask, load=load):
            k = jnp.left_shift(p & ~(j - 1), 1) | (p & (j - 1))
            lo = load(k)
            hi = load(k + j)
            t = (lo ^ lax.shift_right_logical(hi, j)) & mask
            plane_ref[k] = lo ^ t
            plane_ref[k + j] = hi ^ jnp.left_shift(t, j)
            return carry

        lax.fori_loop(0, n_words // 2, body, 0)


def _dsa_kernel(bound_ref, qd_ref, qi_ref, misc_ref, ki_ref, kd_ref, vd_ref, zd_ref, tri_ref,
                o_ref, score_scr, plane_scr, bias_scr, shift0_scr, shift1_scr, select_scr, *, top_k, tq):
    t0 = pl.program_id(1) * tq
    lane = lax.broadcasted_iota(jnp.int32, (1, LANES), 1)
    lo64 = lane < 64
    t_col = t0 + lax.broadcasted_iota(jnp.int32, (tq, 1), 0)
    t_row = t0 + lax.broadcasted_iota(jnp.int32, (1, tq), 1)
    n_chunks = (t0 + tq + KC - 1) // KC
    slabs_per_chunk = KC // LANES
    int_min = jnp.int32(-2 ** 31)

    def chunk_slabs(c):
        return pl.ds(c * slabs_per_chunk, slabs_per_chunk)

    def to_col(row):
        return jnp.broadcast_to(row, (SUBLANES, tq)).T[:, 0:1]

    qi = qi_ref[0]
    qi_heads = [_half_masked(qi[:, (h // 2) * LANES:(h // 2 + 1) * LANES], h, lo64) for h in range(IDX_HEADS)]
    misc_t = misc_ref[0].T
    wi = [misc_t[MISC_WI + h:MISC_WI + h + 1, :] for h in range(IDX_HEADS)]

    def score_chunk(c, has_later_keys):
        k0 = pl.multiple_of(c * KC, KC)
        kib = ki_ref[0, pl.ds(k0, KC), :]
        sc = jnp.zeros((KC, tq), f32)
        for h in range(IDX_HEADS):
            sc = sc + jnp.maximum(_nt_dot(kib, qi_heads[h]), 0.0) * wi[h]
        sc = jnp.where(sc == 0.0, 0.0, sc)
        if has_later_keys:
            kpos = k0 + lax.broadcasted_iota(jnp.int32, (KC, 1), 0)
            sc = jnp.where(kpos <= t_row, sc, NON_CAUSAL_MARK)
        score_scr[chunk_slabs(c)] = sc.reshape(slabs_per_chunk, LANES, tq)

    def score_body(c, carry):
        score_chunk(c, False)
        return carry

    _loop_by_pairs(n_chunks - 1, score_body, 0)
    score_chunk(n_chunks - 1, True)

    def fill_body(c, carry):
        score_scr[chunk_slabs(c)] = jnp.full((slabs_per_chunk, LANES, tq), NON_CAUSAL_MARK, f32)
        return carry

    lax.fori_loop(n_chunks, N_SLABS // slabs_per_chunk, fill_body, 0)

    kf = float(top_k)

    def radix_select(n_words):
        _bit_planes(score_scr, plane_scr, n_words)
        all_ones = jnp.int32(-1 if n_words == 32 else 0xFFFF)

        def select_bit(i, carry):
            alive, above, thr = carry
            if n_words == 32:
                plane = plane_scr[i]
            else:
                word = plane_scr[i & 15]
                plane = jnp.where(i < 16, lax.shift_right_logical(word, 16), word) & all_ones
            plane = plane ^ jnp.where(i == 0, all_ones, jnp.int32(0))
            ones = alive & plane
            c1 = jnp.sum(lax.population_count(ones).astype(f32), axis=0, keepdims=True)
            take = (above + c1) >= kf
            alive = jnp.where(take, ones, alive ^ ones)
            above = jnp.where(take, above, above + c1)
            thr = jnp.where(take, thr | jnp.left_shift(jnp.int32(1), 31 - i), thr)
            return alive, above, thr

        init = (jnp.full((LANES, tq), all_ones, jnp.int32), jnp.zeros((1, tq), f32), jnp.zeros((1, tq), jnp.int32))
        _, above, thr_u = lax.fori_loop(0, 32, select_bit, init)
        select_scr[0:1, :] = thr_u
        select_scr[1:2, :] = pltpu.bitcast(above, jnp.int32)

    few_slabs = n_chunks * slabs_per_chunk <= 16

    @pl.when(few_slabs)
    def _():
        radix_select(16)

    @pl.when(jnp.logical_not(few_slabs))
    def _():
        radix_select(32)

    thr_u = select_scr[0:1, :]
    above = pltpu.bitcast(select_scr[1:2, :], f32)

    def key_to_score(key_row):
        return pltpu.bitcast(_ordered_bits(key_row), f32)

    def write_bias(thr_row, need_row):
        thr = to_col(thr_row)
        need = to_col(need_row)

        def body(c, carry):
            ties_before, n_above = carry
            k0 = pl.multiple_of(c * KC, KC)
            blk = score_scr[chunk_slabs(c)].reshape(KC, tq).T
            gt = jnp.where(blk > thr, 1.0, 0.0)
            eq = jnp.where(blk == thr, 1.0, 0.0)
            prefix = ties_before + _dot(eq.astype(bf16), tri_ref[...])
            chosen = gt + jnp.where(prefix <= need, eq, 0.0)
            kpos = k0 + lax.broadcasted_iota(jnp.int32, (1, KC), 1)
            bias_scr[:, pl.ds(k0, KC)] = jnp.where(kpos <= t_col, jnp.where(chosen > 0.5, 0.0, NEG), NEG)
            return prefix[:, KC - 1:KC], n_above + jnp.sum(gt, axis=-1, keepdims=True)

        zero = jnp.zeros((tq, 1), f32)
        return _loop_by_pairs(n_chunks, body, (zero, zero))

    ties, n_above = write_bias(key_to_score(thr_u ^ int_min), kf - above)
    verified = (n_above == to_col(above)) & (n_above < kf) & (n_above + ties >= kf)

    @pl.when(jnp.sum(jnp.where(verified, 0.0, 1.0)) > 0.0)
    def _():
        def count(pred):
            def body(c, acc):
                blk = score_scr[chunk_slabs(c)].reshape(KC, tq)
                return acc + jnp.sum(jnp.where(pred(blk), 1.0, 0.0), axis=0, keepdims=True)
            return lax.fori_loop(0, N_SLABS // slabs_per_chunk, body, jnp.zeros((1, tq), f32))

        key = jnp.where(count(lambda b: b >= 0.0) >= kf, jnp.int32(0), int_min)

        def bit(i, key):
            cand = key + jnp.left_shift(jnp.int32(1), 30 - i)
            cand_f = key_to_score(cand)
            return jnp.where(count(lambda b: b >= cand_f) >= kf, cand, key)

        thr_f = key_to_score(lax.fori_loop(0, 31, bit, key))
        write_bias(thr_f, kf - count(lambda b: b > thr_f))

    def bias_fn(k0):
        return bias_scr[:, pl.ds(k0, KC)]

    qd = qd_ref[0]
    q4s = [_stack_group_queries(qd, g, lo64) for g in range(KV_GROUPS)]
    outs = _masked_flash(q4s, kd_ref, vd_ref, n_chunks, [bias_fn] * KV_GROUPS, tq,
                         bound_ref, (shift0_scr, shift1_scr))
    o = jnp.concatenate([_group_output(og, lo64, tq) for og in outs], axis=1) * zd_ref[0].astype(f32)
    o_ref[0] = o.astype(bf16)


def _dsa_attention(bound, qd, qi, misc, ki, kd, vd, zd):
    B, S, _ = qd.shape
    tq = TQ_DSA
    assert S <= N_SLABS * LANES and KC % tq == 0
    tri = jnp.asarray(np.triu(np.ones((KC, KC), np.float32)), bf16)
    row = lambda c: pl.BlockSpec((1, tq, c), lambda b, i: (b, i, 0))
    whole = pl.BlockSpec((1, KV_GROUPS, S, LANES), lambda b, i: (b, 0, 0, 0))
    return pl.pallas_call(
        functools.partial(_dsa_kernel, top_k=min(DSA_TOPK_MAX, S // 4), tq=tq),
        grid=(B, S // tq),
        in_specs=[_SMEM_SCALAR, row(512), row(256), row(LANES), pl.BlockSpec((1, S, LANES), lambda b, i: (b, 0, 0)),
                  whole, whole, row(512), pl.BlockSpec((KC, KC), lambda b, i: (0, 0))],
        out_specs=row(512),
        out_shape=jax.ShapeDtypeStruct((B, S, 512), bf16),
        scratch_shapes=[pltpu.VMEM((N_SLABS, LANES, tq), f32), pltpu.VMEM((N_SLABS, LANES, tq), jnp.int32),
                        pltpu.VMEM((tq, S), f32)] + _shift_scratch(tq) + [pltpu.VMEM((SUBLANES, tq), jnp.int32)],
        compiler_params=pltpu.CompilerParams(
            dimension_semantics=("arbitrary", "arbitrary"), vmem_limit_bytes=VMEM_LIMIT),
        name="dsa_attention",
    )(bound, qd, qi, misc, ki, kd, vd, zd, tri)


def _outproj_kernel(x_ref, on_ref, od_ref, w_ref, o_ref):
    half = w_ref.shape[0] // 2
    o_ref[0] = x_ref[0] + _dot(on_ref[0], w_ref[0:half, :]) + _dot(od_ref[0], w_ref[half:, :])


def _out_projection(x, o_nsa, o_dsa, w_out):
    B, S, D = x.shape
    tm = min(TM_OUT, S)
    row = lambda c: pl.BlockSpec((1, tm, c), lambda b, i: (b, i, 0))
    return pl.pallas_call(
        _outproj_kernel,
        grid=(B, S // tm),
        in_specs=[row(D), row(512), row(512), pl.BlockSpec(w_out.shape, lambda b, i: (0, 0))],
        out_specs=row(D),
        out_shape=jax.ShapeDtypeStruct((B, S, D), f32),
        compiler_params=pltpu.CompilerParams(
            dimension_semantics=("arbitrary", "arbitrary"), vmem_limit_bytes=VMEM_LIMIT),
        name="out_projection",
    )(x, o_nsa, o_dsa, w_out)


def _rope_tables(pos):
    half = HEAD_DIM // 2
    inv_freq = ROPE_THETA ** (-jnp.arange(half, dtype=f32) / half)
    ang = pos[:, None] * inv_freq[None, :]
    cos, sin = jnp.cos(ang), jnp.sin(ang)
    cos_t = jnp.tile(cos, (1, LANES // half))
    sin_t = jnp.tile(jnp.concatenate([-sin, sin], axis=1), (1, LANES // HEAD_DIM))
    return cos_t, sin_t


def _overlap_t(seq, n_cmp_pad):
    n_cmp = (seq - CMP_BLOCK) // CMP_STRIDE + 1
    n_sel = seq // SEL_BLOCK
    c_start = np.arange(n_cmp) * CMP_STRIDE
    j_start = np.arange(n_sel) * SEL_BLOCK
    ov = np.clip(np.minimum(c_start[:, None] + CMP_BLOCK, j_start[None, :] + SEL_BLOCK)
                 - np.maximum(c_start[:, None], j_start[None, :]), 0, None).astype(np.float32) / CMP_BLOCK
    out = np.zeros((n_sel, n_cmp_pad), np.float32)
    out[:, :n_cmp] = ov.T
    return out


def _block_diag2(w):
    z = jnp.zeros_like(w)
    return jnp.concatenate([jnp.concatenate([w, z], axis=-1), jnp.concatenate([z, w], axis=-1)], axis=-2)


def _layer(x, norm_gain, w_in, nsa_q_gain, nsa_kc_gain, nsa_ks_gain, nsa_kw_gain,
           cmp_pe_k, cmp_k_w1, cmp_k_b1, cmp_k_w2, cmp_pe_v, cmp_v_w1, cmp_v_b1, cmp_v_w2,
           dsa_q_gain, dsa_k_gain, w_out):
    B, S, _ = x.shape
    assert S % KC == 0 and S >= WINDOW + TQ_NSA
    n_cmp_pad = S // CMP_STRIDE

    head_of_lane = np.arange(2 * LANES) // HEAD_DIM
    bd = jnp.asarray(head_of_lane[:, None] == head_of_lane[None, :], bf16)
    dup = lambda v: jnp.tile(v.reshape(1, -1), (1, 2))
    quad = lambda v: jnp.tile(v.reshape(1, -1), (1, 4))
    gains = jnp.concatenate([quad(nsa_q_gain), jnp.concatenate([dup(nsa_ks_gain), dup(nsa_kw_gain)], axis=1),
                             quad(dsa_q_gain), quad(dsa_k_gain), jnp.ones((4, 2 * LANES), f32)], axis=0)
    cos_t, sin_t = _rope_tables(jnp.arange(S, dtype=f32))

    (qn, qd, qi, ki, ks, kw, kd, vs, vw, vd, kc, vc, zn, zd, misc) = _in_projection(
        x, norm_gain.reshape(1, -1), w_in, bd, cos_t, sin_t, gains)

    cmp_pos = (jnp.arange(n_cmp_pad) * CMP_STRIDE + CMP_BLOCK - 1).astype(f32)
    cos_c, sin_c = _rope_tables(cmp_pos)

    def cmp_weights(pe, w1, b1, w2):
        w1_bd = _block_diag2(w1.reshape(CMP_BLOCK, HEAD_DIM, CMP_HIDDEN)).astype(bf16)
        return jnp.tile(pe, (1, 2)), w1_bd, dup(b1), _block_diag2(w2).astype(bf16)

    kcmp = _compress(kc, *cmp_weights(cmp_pe_k, cmp_k_w1, cmp_k_b1, cmp_k_w2), bd[:LANES, :LANES], dup(nsa_kc_gain),
                     cos_c, sin_c, is_key=True)
    vcmp = _compress(vc, *cmp_weights(cmp_pe_v, cmp_v_w1, cmp_v_b1, cmp_v_w2), bd[:LANES, :LANES], dup(nsa_kc_gain),
                     cos_c, sin_c, is_key=False)

    ovt = jnp.asarray(_overlap_t(S, n_cmp_pad), bf16)
    o_nsa = _nsa_attention(_score_bound(nsa_q_gain, nsa_kc_gain, nsa_ks_gain, nsa_kw_gain),
                           qn, kcmp, vcmp, ks, vs, kw, vw, misc, zn, ovt)

    o_dsa = _dsa_attention(_score_bound(dsa_q_gain, dsa_k_gain), qd, qi, misc, ki, kd, vd, zd)

    return _out_projection(x, o_nsa, o_dsa, w_out.astype(bf16))


def kernel(x, norm_gain, w_in, nsa_q_gain, nsa_kc_gain, nsa_ks_gain, nsa_kw_gain, cmp_pe_k, cmp_k_w1,
           cmp_k_b1, cmp_k_w2, cmp_pe_v, cmp_v_w1, cmp_v_b1, cmp_v_w2, dsa_q_gain, dsa_k_gain, w_out):
    for l in range(norm_gain.shape[0]):
        x = _layer(x, norm_gain[l], w_in[l:l + 1], nsa_q_gain[l], nsa_kc_gain[l], nsa_ks_gain[l], nsa_kw_gain[l],
                   cmp_pe_k[l], cmp_k_w1[l], cmp_k_b1[l], cmp_k_w2[l], cmp_pe_v[l], cmp_v_w1[l], cmp_v_b1[l],
                   cmp_v_w2[l], dsa_q_gain[l], dsa_k_gain[l], w_out[l])
    return x
```

```python
import functools

import numpy as np
import jax
import jax.numpy as jnp
from jax import lax
from jax.experimental import pallas as pl
from jax.experimental.pallas import tpu as pltpu

D_MODEL = 1024
HEAD_DIM = 64
NSA_HEADS = 8
DSA_HEADS = 8
KV_GROUPS = 2
HEADS_PER_GROUP = 4
IDX_HEADS = 4
CMP_BLOCK = 32
CMP_STRIDE = 16
CMP_HIDDEN = 256
SEL_BLOCK = 64
SEL_TOPN = 16
WINDOW = 512
DSA_TOPK_MAX = 256
ROPE_THETA = 10000.0
EPS = 1e-6
NEG = -1e30
FORCE = 1e6
ATTN_SCALE = HEAD_DIM ** -0.5
IDX_SCALE = HEAD_DIM ** -0.5
WI_SCALE = IDX_HEADS ** -0.5
LOG2E = 1.4426950408889634
Q_SCALE = ATTN_SCALE * LOG2E

LANES = 128
SUBLANES = 8
M_FLOOR = -5e29
NON_CAUSAL_MARK = -3.0e38
MAX_FIXED_SHIFT = 30.0
NORM_SLACK = 1.03
VMEM_LIMIT = 56 * 1024 * 1024

IN_WIDTHS = (512, 128, 128, 128, 128, 128, 128, 24, 512, 512, 128, 128, 256, 64, 4, 512)
IN_NAMES = ("q_n", "kc", "vc", "ks", "vs", "kw", "vw", "gate", "z_n",
            "q_d", "k_d", "v_d", "qi", "ki", "wi", "z_d")
IN_COLS = sum(IN_WIDTHS)

CH_QN, CH_KS, CH_KW, CH_QD, CH_KD, CH_QI, CH_KI = 0, 4, 5, 6, 10, 11, 13
CH_KC, CH_VC, CH_VS, CH_VW, CH_VD, CH_ZN, CH_ZD, CH_MISC = 14, 15, 16, 17, 18, 19, 23, 27
N_CHUNKS = 28
MISC_WI = 24

TM_PROJ = 512
TQ_NSA = 256
TQ_DSA = 256
KC = 512

f32 = jnp.float32
bf16 = jnp.bfloat16


def _repack_weight(w_ref, w16_ref):
    off = dict(zip(IN_NAMES, np.cumsum((0,) + IN_WIDTHS[:-1])))
    width = dict(zip(IN_NAMES, IN_WIDTHS))
    order = ("q_n", "ks", "kw", "q_d", "k_d", "qi", "ki", "ki", "kc", "vc", "vs", "vw", "v_d", "z_n", "z_d",
             "gate", "wi")
    dst = 0
    for n in order:
        w16_ref[:, dst:dst + width[n]] = w_ref[0, :, off[n]:off[n] + width[n]].astype(bf16)
        dst += width[n]
    w16_ref[:, dst:] = jnp.zeros((D_MODEL, N_CHUNKS * LANES - dst), bf16)


def _nt_dot(a, b):
    return lax.dot_general(a, b, (((1,), (1,)), ((), ())), preferred_element_type=f32)


def _dot(a, b):
    return jnp.dot(a, b, preferred_element_type=f32)


def _split_bf16(v):
    hi = v.astype(bf16)
    lo = (v - hi.astype(f32)).astype(bf16)
    return hi, lo


def _head_rms_norm(y, bd, gain_row):
    ssq = _dot((y * y).astype(bf16), bd)
    return y * lax.rsqrt(ssq * (1.0 / HEAD_DIM) + EPS) * gain_row


def _rope(y, cos, sin_signed, lo32):
    partner = jnp.where(lo32, pltpu.roll(y, LANES - 32, 1), pltpu.roll(y, 32, 1))
    return y * cos + partner * sin_signed


def _dup_halves(y, lo64):
    r = pltpu.roll(y, 64, 1)
    return jnp.where(lo64, y, r), jnp.where(lo64, r, y)


def _value_with_ones(y, lo64):
    return jnp.where(lo64, y, 1.0), jnp.where(lo64, pltpu.roll(y, 64, 1), 1.0)


def _inproj_kernel(x_ref, ng_ref, w_ref, bd_ref, cos_ref, sin_ref, gains_ref,
                   qn_ref, qd_ref, qi_ref, ki_ref, ks_ref, kw_ref, kd_ref,
                   vs_ref, vw_ref, vd_ref, kc_ref, vc_ref, zn_ref, zd_ref, misc_ref, w16_ref):
    @pl.when((pl.program_id(0) == 0) & (pl.program_id(1) == 0))
    def _():
        _repack_weight(w_ref, w16_ref)

    x = x_ref[0]
    ms = jnp.mean(x * x, axis=-1, keepdims=True)
    h = (x * lax.rsqrt(ms + EPS) * ng_ref[...]).astype(bf16)
    cos = cos_ref[...]
    sin = sin_ref[...]
    bd = bd_ref[...]
    lane = lax.broadcasted_iota(jnp.int32, (1, LANES), 1)
    lo32 = (lane % 64) < 32
    lo64 = lane < 64

    def proj(c0, n):
        return _dot(h, w16_ref[:, c0 * LANES:(c0 + n) * LANES])

    def chunk(y, i):
        return y[:, i * LANES:(i + 1) * LANES]

    def normed_rope(y, gain_idx):
        w = y.shape[1]
        yn = _head_rms_norm(y, bd[:w, :w], gains_ref[gain_idx:gain_idx + 1, 0:w])
        return [_rope(chunk(yn, i), cos, sin, lo32) for i in range(w // LANES)]

    for c0, gain_idx, ref in ((CH_QN, 0, qn_ref), (CH_QD, 2, qd_ref)):
        y = proj(c0, 4)
        for i in range(2):
            for j, r in enumerate(normed_rope(y[:, 2 * i * LANES:(2 * i + 2) * LANES], gain_idx)):
                ref[0, :, (2 * i + j) * LANES:(2 * i + j + 1) * LANES] = (r * Q_SCALE).astype(bf16)
    y = proj(CH_QI, 2)
    for i in range(2):
        qi_ref[0, :, i * LANES:(i + 1) * LANES] = (_rope(chunk(y, i), cos, sin, lo32) * IDX_SCALE).astype(bf16)
    ki_ref[0] = _rope(proj(CH_KI, 1), cos, sin, lo32).astype(bf16)

    k_pairs = normed_rope(proj(CH_KS, 2), 1) + normed_rope(proj(CH_KD, 1), 3)
    for r, ref in zip(k_pairs, (ks_ref, kw_ref, kd_ref)):
        a, b = _dup_halves(r, lo64)
        ref[0, 0] = a.astype(bf16)
        ref[0, 1] = b.astype(bf16)
    for c0, ref in ((CH_VS, vs_ref), (CH_VW, vw_ref), (CH_VD, vd_ref)):
        a, b = _value_with_ones(proj(c0, 1), lo64)
        ref[0, 0] = a.astype(bf16)
        ref[0, 1] = b.astype(bf16)

    kc_ref[0] = proj(CH_KC, 1)
    vc_ref[0] = proj(CH_VC, 1)
    z = proj(CH_ZN, 4)
    zn_ref[0] = (z * jax.nn.sigmoid(z)).astype(bf16)
    z = proj(CH_ZD, 4)
    zd_ref[0] = (z * jax.nn.sigmoid(z)).astype(bf16)
    m = proj(CH_MISC, 1)
    misc_ref[0] = jnp.where(lane < MISC_WI, jax.nn.sigmoid(m), m * WI_SCALE)


def _in_projection(x, norm_gain, w_in, bd, cos_t, sin_t, gains):
    B, S, _ = x.shape
    tm = min(TM_PROJ, S)
    grid = (B, S // tm)
    row = lambda c: pl.BlockSpec((1, tm, c), lambda b, i: (b, i, 0))
    dup = pl.BlockSpec((1, KV_GROUPS, tm, LANES), lambda b, i: (b, 0, i, 0))
    const = lambda shape: pl.BlockSpec(shape, lambda b, i: tuple(0 for _ in shape))
    tab = pl.BlockSpec((tm, LANES), lambda b, i: (i, 0))
    sds = jax.ShapeDtypeStruct
    out_shape = (
        sds((B, S, 512), bf16), sds((B, S, 512), bf16), sds((B, S, 256), bf16), sds((B, S, LANES), bf16),
        sds((B, KV_GROUPS, S, LANES), bf16), sds((B, KV_GROUPS, S, LANES), bf16),
        sds((B, KV_GROUPS, S, LANES), bf16), sds((B, KV_GROUPS, S, LANES), bf16),
        sds((B, KV_GROUPS, S, LANES), bf16), sds((B, KV_GROUPS, S, LANES), bf16),
        sds((B, S, LANES), f32), sds((B, S, LANES), f32),
        sds((B, S, 512), bf16), sds((B, S, 512), bf16), sds((B, S, LANES), f32),
    )
    out_specs = (row(512), row(512), row(256), row(LANES), dup, dup, dup, dup, dup, dup,
                 row(LANES), row(LANES), row(512), row(512), row(LANES))
    return pl.pallas_call(
        _inproj_kernel,
        grid=grid,
        in_specs=[row(D_MODEL), const((1, D_MODEL)),
                  pl.BlockSpec((1, D_MODEL, IN_COLS), lambda b, i: (0, 0, 0), pipeline_mode=pl.Buffered(1)),
                  const((2 * LANES, 2 * LANES)), tab, tab, const((8, 2 * LANES))],
        out_specs=out_specs,
        out_shape=out_shape,
        scratch_shapes=[pltpu.VMEM((D_MODEL, N_CHUNKS * LANES), bf16)],
        compiler_params=pltpu.CompilerParams(
            dimension_semantics=("arbitrary", "arbitrary"), vmem_limit_bytes=VMEM_LIMIT),
        name="in_projection",
    )(x, norm_gain, w_in, bd, cos_t, sin_t, gains)


def _compress_kernel(src_ref, pe_ref, w1_ref, b1_ref, w2_ref, bd_ref, gain_ref, cos_ref, sin_ref,
                     out_ref, *, n_rows, is_key):
    half = CMP_BLOCK // 2
    acc_a = jnp.zeros((n_rows, 2 * CMP_HIDDEN), f32)
    acc_b = jnp.zeros((n_rows, 2 * CMP_HIDDEN), f32)
    for l in range(half):
        rows = src_ref[0, pl.ds(l, n_rows, stride=CMP_STRIDE), :]
        acc_a = acc_a + _dot((rows + pe_ref[l:l + 1, :]).astype(bf16), w1_ref[l])
        acc_b = acc_b + _dot((rows + pe_ref[l + half:l + half + 1, :]).astype(bf16), w1_ref[l + half])
    pre = acc_a + pltpu.roll(acc_b, n_rows - 1, 0) + b1_ref[...]
    hid = pre * jax.nn.sigmoid(pre)
    out = _dot(hid.astype(bf16), w2_ref[...])
    lane = lax.broadcasted_iota(jnp.int32, (1, LANES), 1)
    if is_key:
        out = _head_rms_norm(out, bd_ref[...], gain_ref[...])
        out = _rope(out, cos_ref[...], sin_ref[...], (lane % 64) < 32)
        a, b = _dup_halves(out, lane < 64)
    else:
        a, b = _value_with_ones(out, lane < 64)
    out_ref[0, 0] = a.astype(bf16)
    out_ref[0, 1] = b.astype(bf16)


def _compress(src, pe_dup, w1_bd, b1_dup, w2_bd, bd, gain_dup, cos_c, sin_c, *, is_key):
    B, S, _ = src.shape
    n_rows = S // CMP_STRIDE
    const = lambda shape: pl.BlockSpec(shape, lambda b: tuple(0 for _ in shape))
    return pl.pallas_call(
        functools.partial(_compress_kernel, n_rows=n_rows, is_key=is_key),
        grid=(B,),
        in_specs=[pl.BlockSpec((1, S, LANES), lambda b: (b, 0, 0)),
                  const((CMP_BLOCK, LANES)), const((CMP_BLOCK, LANES, 2 * CMP_HIDDEN)),
                  const((1, 2 * CMP_HIDDEN)), const((2 * CMP_HIDDEN, LANES)), const((LANES, LANES)),
                  const((1, LANES)), const((n_rows, LANES)), const((n_rows, LANES))],
        out_specs=pl.BlockSpec((1, KV_GROUPS, n_rows, LANES), lambda b: (b, 0, 0, 0)),
        out_shape=jax.ShapeDtypeStruct((B, KV_GROUPS, n_rows, LANES), bf16),
        compiler_params=pltpu.CompilerParams(
            dimension_semantics=("arbitrary",), vmem_limit_bytes=VMEM_LIMIT),
        name="compress_k" if is_key else "compress_v",
    )(src, pe_dup, w1_bd, b1_dup, w2_bd, bd, gain_dup, cos_c, sin_c)


def _half_masked(pair, h, lo64):
    keep = lo64 if h % 2 == 0 else jnp.logical_not(lo64)
    return jnp.where(keep, pair, jnp.zeros_like(pair))


def _loop_by_pairs(n, body, init, quads=False):
    done = 0
    if quads:
        four = lambda i, cr: body(4 * i + 3, body(4 * i + 2, body(4 * i + 1, body(4 * i, cr))))
        init = lax.fori_loop(0, n // 4, four, init)
        done = 4 * (n // 4)
    pair = lambda i, cr: body(2 * i + 1, body(2 * i, cr))
    carry = lax.fori_loop(done // 2, n // 2, pair, init)
    return lax.fori_loop(2 * (n // 2), n, body, carry)


def _stack_group_queries(q, g, lo64):
    slabs = []
    for r in range(HEADS_PER_GROUP):
        h = g * HEADS_PER_GROUP + r
        slabs.append(_half_masked(q[:, (h // 2) * LANES:(h // 2 + 1) * LANES], h, lo64))
    return jnp.concatenate(slabs, axis=0)


def _masked_flash(q4s, k_ref, v_ref, n_chunks, bias_fns, tq, bound_ref, shift_scrs):
    rows = HEADS_PER_GROUP * tq
    bound = bound_ref[0]

    def scores(g, k0):
        s = _nt_dot(q4s[g], k_ref[0, g, pl.ds(k0, KC), :])
        return (s.reshape(HEADS_PER_GROUP, tq, KC) + bias_fns[g](k0)[None]).reshape(rows, KC)

    for g in range(KV_GROUPS):
        shift_scrs[g][...] = jnp.full((rows, 1), bound, f32)

    @pl.when(bound > MAX_FIXED_SHIFT)
    def _():
        def body(c, ms):
            k0 = pl.multiple_of(c * KC, KC)
            return tuple(jnp.maximum(ms[g], jnp.max(scores(g, k0), axis=-1, keepdims=True))
                         for g in range(KV_GROUPS))

        init = tuple(jnp.full((rows, 1), M_FLOOR, f32) for _ in range(KV_GROUPS))
        for g, m in enumerate(lax.fori_loop(0, n_chunks, body, init)):
            shift_scrs[g][...] = m

    def body(c, accs):
        k0 = pl.multiple_of(c * KC, KC)
        new = []
        for g in range(KV_GROUPS):
            p = jnp.exp2(scores(g, k0) - shift_scrs[g][...]).astype(bf16)
            new.append(accs[g] + _dot(p, v_ref[0, g, pl.ds(k0, KC), :]))
        return tuple(new)

    init = tuple(jnp.zeros((rows, LANES), f32) for _ in range(KV_GROUPS))
    out = _loop_by_pairs(n_chunks, body, init, quads=True)
    return [acc / acc[:, HEAD_DIM:HEAD_DIM + 1] for acc in out]


def _group_output(og, lo64, tq):
    pairs = []
    for p in range(2):
        even = og[(2 * p) * tq:(2 * p + 1) * tq]
        odd = og[(2 * p + 1) * tq:(2 * p + 2) * tq]
        pairs.append(jnp.where(lo64, even, pltpu.roll(odd, HEAD_DIM, 1)))
    return jnp.concatenate(pairs, axis=1)


def _top_n_blocks(imp, n_sel, top_n):
    groups = [imp[SUBLANES * a:SUBLANES * (a + 1)] for a in range(n_sel // SUBLANES)]
    jsub = lax.broadcasted_iota(jnp.int32, groups[0].shape, 0)
    ranks = [jnp.zeros(groups[0].shape, f32) for _ in groups]
    for i in range(n_sel):
        row = imp[i:i + 1, :]
        for a, grp in enumerate(groups):
            if SUBLANES * a > i:
                beats = row >= grp
            elif SUBLANES * (a + 1) - 1 < i:
                beats = row > grp
            else:
                tie = jnp.where(jsub + SUBLANES * a > i, 1.0, 0.0)
                ranks[a] = ranks[a] + jnp.where(row > grp, 1.0, jnp.where(row == grp, tie, 0.0))
                continue
            ranks[a] = ranks[a] + jnp.where(beats, 1.0, 0.0)
    rank = jnp.concatenate(ranks, axis=0)
    return jnp.where(rank < top_n, 1.0, 0.0)


def _nsa_kernel(bound_ref, qn_ref, kcmp_ref, vcmp_ref, ks_ref, vs_ref, kw_ref, vw_ref, misc_ref, zn_ref, ovt_ref,
                expand_ref, o_ref, shift0_scr, shift1_scr, cshift0_scr, cshift1_scr, wshift0_scr, wshift1_scr,
                bias_scr, *, n_cmp_pad, n_sel, top_n):
    tq = TQ_NSA
    cmp_shift = [cshift0_scr, cshift1_scr]
    win_shift = [wshift0_scr, wshift1_scr]
    t0 = pl.program_id(1) * tq
    lane = lax.broadcasted_iota(jnp.int32, (1, LANES), 1)
    lo64 = lane < 64
    q = qn_ref[0]
    misc = misc_ref[0]
    t_col = t0 + lax.broadcasted_iota(jnp.int32, (tq, 1), 0)
    n_chunks = (t0 + tq + KC - 1) // KC
    win_span = WINDOW + tq
    win_start = pl.multiple_of(jnp.maximum(t0 - WINDOW, 0), tq)
    ovt = ovt_ref[...]
    rows = HEADS_PER_GROUP * tq

    def add_bias(s, bias):
        return (s.reshape(HEADS_PER_GROUP, tq, s.shape[-1]) + bias[None]).reshape(rows, s.shape[-1])

    cmp_end = lax.broadcasted_iota(jnp.int32, (1, n_cmp_pad), 1) * CMP_STRIDE + (CMP_BLOCK - 1)
    cmp_bias = jnp.where(cmp_end <= t_col, 0.0, NEG)
    win_diff = t_col - (win_start + lax.broadcasted_iota(jnp.int32, (1, win_span), 1))
    win_bias = jnp.where((win_diff >= 0) & (win_diff < WINDOW), 0.0, NEG)
    q4s = [_stack_group_queries(q, g, lo64) for g in range(KV_GROUPS)]

    def cmp_scores(g):
        return add_bias(_nt_dot(q4s[g], kcmp_ref[0, g]), cmp_bias)

    def win_scores(g):
        return add_bias(_nt_dot(q4s[g], kw_ref[0, g, pl.ds(win_start, win_span), :]), win_bias)

    for scr in cmp_shift + win_shift:
        scr[...] = jnp.full((rows, 1), bound_ref[0], f32)

    @pl.when(bound_ref[0] > MAX_FIXED_SHIFT)
    def _():
        for g in range(KV_GROUPS):
            cmp_shift[g][...] = jnp.maximum(jnp.max(cmp_scores(g), axis=-1, keepdims=True), M_FLOOR)
            win_shift[g][...] = jnp.max(win_scores(g), axis=-1, keepdims=True)

    o_cmp, o_win, sels = [], [], []
    for g in range(KV_GROUPS):
        e = jnp.exp2(cmp_scores(g) - cmp_shift[g][...])
        l = jnp.sum(e, axis=-1, keepdims=True)
        p = e * (1.0 / jnp.maximum(l, 1e-30))
        o_cmp.append(_dot(p.astype(bf16), vcmp_ref[0, g]))

        psum = p[0:tq] + p[tq:2 * tq] + p[2 * tq:3 * tq] + p[3 * tq:4 * tq]
        hi, lo = _split_bf16(psum)
        imp = _nt_dot(ovt, hi) + _nt_dot(ovt, lo)
        jrow = lax.broadcasted_iota(jnp.int32, (n_sel, tq), 0)
        tl = t0 + lax.broadcasted_iota(jnp.int32, (n_sel, tq), 1)
        cur = lax.shift_right_logical(tl, 6)
        forced = (jrow == 0) | (jrow == cur) | (jrow == cur - 1)
        imp = jnp.where(forced, FORCE, imp)
        imp = jnp.where(jrow * SEL_BLOCK <= tl, imp, NEG)
        pad_row = lax.broadcasted_iota(jnp.int32, (LANES - n_sel, tq), 0)
        chosen = jnp.concatenate([_top_n_blocks(imp, n_sel, top_n) * -NEG, jnp.where(pad_row == 0, NEG, 0.0)], axis=0)
        sels.append(chosen.T.astype(bf16))

        e = jnp.exp2(win_scores(g) - win_shift[g][...])
        acc = _dot(e.astype(bf16), vw_ref[0, g, pl.ds(win_start, win_span), :])
        o_win.append(acc / acc[:, HEAD_DIM:HEAD_DIM + 1])

    def bias_body(c, carry):
        k0 = pl.multiple_of(c * KC, KC)
        for g in range(KV_GROUPS):
            bias_scr[g, :, pl.ds(k0, KC)] = _dot(sels[g], expand_ref[:, pl.ds(k0, KC)])
        return carry

    _loop_by_pairs(n_chunks, bias_body, 0)
    k_last = pl.multiple_of((n_chunks - 1) * KC, KC)
    causal = jnp.where(k_last + lax.broadcasted_iota(jnp.int32, (1, KC), 1) <= t_col, 0.0, NEG)
    for g in range(KV_GROUPS):
        bias_scr[g, :, pl.ds(k_last, KC)] = bias_scr[g, :, pl.ds(k_last, KC)] + causal

    o_sel = _masked_flash(q4s, ks_ref, vs_ref, n_chunks,
                          [lambda k0, g=g: bias_scr[g, :, pl.ds(k0, KC)] for g in range(KV_GROUPS)], tq,
                          bound_ref, (shift0_scr, shift1_scr))

    group_outs = []
    for g in range(KV_GROUPS):
        def gate(branch):
            cols = [misc[:, (g * HEADS_PER_GROUP + r) * 3 + branch:(g * HEADS_PER_GROUP + r) * 3 + branch + 1]
                    for r in range(HEADS_PER_GROUP)]
            return jnp.concatenate(cols, axis=0)

        og = gate(0) * o_cmp[g] + gate(1) * o_sel[g] + gate(2) * o_win[g]
        group_outs.append(_group_output(og, lo64, tq))

    o = jnp.concatenate(group_outs, axis=1) * zn_ref[0].astype(f32)
    o_ref[0] = o.astype(bf16)


def _score_bound(q_gain, *k_gains):
    k_max = functools.reduce(jnp.maximum, [jnp.max(jnp.abs(g)) for g in k_gains])
    return (Q_SCALE * HEAD_DIM * NORM_SLACK * jnp.max(jnp.abs(q_gain)) * k_max).reshape(1)


_SMEM_SCALAR = pl.BlockSpec(memory_space=pltpu.SMEM)


def _shift_scratch(tq):
    return [pltpu.VMEM((HEADS_PER_GROUP * tq, 1), f32)] * KV_GROUPS


def _nsa_attention(bound, qn, kcmp, vcmp, ks, vs, kw, vw, misc, zn, ovt):
    B, S, _ = qn.shape
    tq = TQ_NSA
    n_cmp_pad = kcmp.shape[2]
    n_sel = S // SEL_BLOCK
    row = lambda c: pl.BlockSpec((1, tq, c), lambda b, i: (b, i, 0))
    whole = lambda n: pl.BlockSpec((1, KV_GROUPS, n, LANES), lambda b, i: (b, 0, 0, 0))
    kern = functools.partial(_nsa_kernel, n_cmp_pad=n_cmp_pad, n_sel=n_sel, top_n=min(SEL_TOPN, n_sel))
    assert n_sel < LANES
    expand = np.zeros((LANES, S), np.float32)
    expand[np.arange(S) // SEL_BLOCK, np.arange(S)] = 1.0
    expand[n_sel] = 1.0
    expand = jnp.asarray(expand, bf16)
    return pl.pallas_call(
        kern,
        grid=(B, S // tq),
        in_specs=[_SMEM_SCALAR, row(512), whole(n_cmp_pad), whole(n_cmp_pad), whole(S), whole(S), whole(S),
                  whole(S), row(LANES), row(512), pl.BlockSpec((n_sel, n_cmp_pad), lambda b, i: (0, 0)),
                  pl.BlockSpec((LANES, S), lambda b, i: (0, 0))],
        out_specs=row(512),
        out_shape=jax.ShapeDtypeStruct((B, S, 512), bf16),
        scratch_shapes=3 * _shift_scratch(tq) + [pltpu.VMEM((KV_GROUPS, tq, S), f32)],
        compiler_params=pltpu.CompilerParams(
            dimension_semantics=("arbitrary", "arbitrary"), vmem_limit_bytes=VMEM_LIMIT),
        name="nsa_attention",
    )(bound, qn, kcmp, vcmp, ks, vs, kw, vw, misc, zn, ovt, expand)


N_SLABS = 32
_TRANSPOSE_STAGES = ((16, 0x0000FFFF), (8, 0x00FF00FF), (4, 0x0F0F0F0F), (2, 0x33333333), (1, 0x55555555))


def _ordered_bits(v):
    return v ^ (jnp.right_shift(v, 31) & jnp.int32(0x7FFFFFFF))


def _bit_planes(score_ref, plane_ref, n_words):
    stages = [s for s in _TRANSPOSE_STAGES if s[0] < n_words]
    for stage, (j, mask) in enumerate(stages):
        if stage == 0:
            load = lambda k: _ordered_bits(pltpu.bitcast(score_ref[k], jnp.int32))
        else:
            load = lambda k: plane_ref[k]

        def body(p, carry, j=j, mask=mask, load=load):
            k = jnp.left_shift(p & ~(j - 1), 1) | (p & (j - 1))
            lo = load(k)
            hi = load(k + j)
            t = (lo ^ lax.shift_right_logical(hi, j)) & mask
            plane_ref[k] = lo ^ t
            plane_ref[k + j] = hi ^ jnp.left_shift(t, j)
            return carry

        lax.fori_loop(0, n_words // 2, body, 0)


def _dsa_kernel(bound_ref, qd_ref, qi_ref, misc_ref, ki_ref, kd_ref, vd_ref, zd_ref, tri_ref,
                x_ref, onsa_ref, wout_ref, o_ref, score_scr, plane_scr, bias_scr, shift0_scr, shift1_scr, select_scr, *, top_k, tq):
    t0 = pl.program_id(1) * tq
    lane = lax.broadcasted_iota(jnp.int32, (1, LANES), 1)
    lo64 = lane < 64
    t_col = t0 + lax.broadcasted_iota(jnp.int32, (tq, 1), 0)
    t_row = t0 + lax.broadcasted_iota(jnp.int32, (1, tq), 1)
    n_chunks = (t0 + tq + KC - 1) // KC
    slabs_per_chunk = KC // LANES
    int_min = jnp.int32(-2 ** 31)

    def chunk_slabs(c):
        return pl.ds(c * slabs_per_chunk, slabs_per_chunk)

    def to_col(row):
        return jnp.broadcast_to(row, (SUBLANES, tq)).T[:, 0:1]

    qi = qi_ref[0]
    qi_heads = [_half_masked(qi[:, (h // 2) * LANES:(h // 2 + 1) * LANES], h, lo64) for h in range(IDX_HEADS)]
    misc_t = misc_ref[0].T
    wi = [misc_t[MISC_WI + h:MISC_WI + h + 1, :] for h in range(IDX_HEADS)]

    def score_chunk(c, has_later_keys):
        k0 = pl.multiple_of(c * KC, KC)
        kib = ki_ref[0, pl.ds(k0, KC), :]
        sc = jnp.zeros((KC, tq), f32)
        for h in range(IDX_HEADS):
            sc = sc + jnp.maximum(_nt_dot(kib, qi_heads[h]), 0.0) * wi[h]
        sc = jnp.where(sc == 0.0, 0.0, sc)
        if has_later_keys:
            kpos = k0 + lax.broadcasted_iota(jnp.int32, (KC, 1), 0)
            sc = jnp.where(kpos <= t_row, sc, NON_CAUSAL_MARK)
        score_scr[chunk_slabs(c)] = sc.reshape(slabs_per_chunk, LANES, tq)

    def score_body(c, carry):
        score_chunk(c, False)
        return carry

    _loop_by_pairs(n_chunks - 1, score_body, 0)
    score_chunk(n_chunks - 1, True)

    def fill_body(c, carry):
        score_scr[chunk_slabs(c)] = jnp.full((slabs_per_chunk, LANES, tq), NON_CAUSAL_MARK, f32)
        return carry

    lax.fori_loop(n_chunks, N_SLABS // slabs_per_chunk, fill_body, 0)

    kf = float(top_k)

    def radix_select(n_words):
        _bit_planes(score_scr, plane_scr, n_words)
        all_ones = jnp.int32(-1 if n_words == 32 else 0xFFFF)

        def select_bit(i, carry):
            alive, above, thr = carry
            if n_words == 32:
                plane = plane_scr[i]
            else:
                word = plane_scr[i & 15]
                plane = jnp.where(i < 16, lax.shift_right_logical(word, 16), word) & all_ones
            plane = plane ^ jnp.where(i == 0, all_ones, jnp.int32(0))
            ones = alive & plane
            c1 = jnp.sum(lax.population_count(ones).astype(f32), axis=0, keepdims=True)
            take = (above + c1) >= kf
            alive = jnp.where(take, ones, alive ^ ones)
            above = jnp.where(take, above, above + c1)
            thr = jnp.where(take, thr | jnp.left_shift(jnp.int32(1), 31 - i), thr)
            return alive, above, thr

        init = (jnp.full((LANES, tq), all_ones, jnp.int32), jnp.zeros((1, tq), f32), jnp.zeros((1, tq), jnp.int32))
        _, above, thr_u = lax.fori_loop(0, 32, select_bit, init)
        select_scr[0:1, :] = thr_u
        select_scr[1:2, :] = pltpu.bitcast(above, jnp.int32)

    few_slabs = n_chunks * slabs_per_chunk <= 16

    @pl.when(few_slabs)
    def _():
        radix_select(16)

    @pl.when(jnp.logical_not(few_slabs))
    def _():
        radix_select(32)

    thr_u = select_scr[0:1, :]
    above = pltpu.bitcast(select_scr[1:2, :], f32)

    def key_to_score(key_row):
        return pltpu.bitcast(_ordered_bits(key_row), f32)

    def write_bias(thr_row, need_row):
        thr = to_col(thr_row)
        need = to_col(need_row)

        def body(c, carry):
            ties_before, n_above = carry
            k0 = pl.multiple_of(c * KC, KC)
            blk = score_scr[chunk_slabs(c)].reshape(KC, tq).T
            gt = jnp.where(blk > thr, 1.0, 0.0)
            eq = jnp.where(blk == thr, 1.0, 0.0)
            prefix = ties_before + _dot(eq.astype(bf16), tri_ref[...])
            chosen = gt + jnp.where(prefix <= need, eq, 0.0)
            kpos = k0 + lax.broadcasted_iota(jnp.int32, (1, KC), 1)
            bias_scr[:, pl.ds(k0, KC)] = jnp.where(kpos <= t_col, jnp.where(chosen > 0.5, 0.0, NEG), NEG)
            return prefix[:, KC - 1:KC], n_above + jnp.sum(gt, axis=-1, keepdims=True)

        zero = jnp.zeros((tq, 1), f32)
        return _loop_by_pairs(n_chunks, body, (zero, zero))

    ties, n_above = write_bias(key_to_score(thr_u ^ int_min), kf - above)
    verified = (n_above == to_col(above)) & (n_above < kf) & (n_above + ties >= kf)

    @pl.when(jnp.sum(jnp.where(verified, 0.0, 1.0)) > 0.0)
    def _():
        def count(pred):
            def body(c, acc):
                blk = score_scr[chunk_slabs(c)].reshape(KC, tq)
                return acc + jnp.sum(jnp.where(pred(blk), 1.0, 0.0), axis=0, keepdims=True)
            return lax.fori_loop(0, N_SLABS // slabs_per_chunk, body, jnp.zeros((1, tq), f32))

        key = jnp.where(count(lambda b: b >= 0.0) >= kf, jnp.int32(0), int_min)

        def bit(i, key):
            cand = key + jnp.left_shift(jnp.int32(1), 30 - i)
            cand_f = key_to_score(cand)
            return jnp.where(count(lambda b: b >= cand_f) >= kf, cand, key)

        thr_f = key_to_score(lax.fori_loop(0, 31, bit, key))
        write_bias(thr_f, kf - count(lambda b: b > thr_f))

    def bias_fn(k0):
        return bias_scr[:, pl.ds(k0, KC)]

    qd = qd_ref[0]
    q4s = [_stack_group_queries(qd, g, lo64) for g in range(KV_GROUPS)]
    outs = _masked_flash(q4s, kd_ref, vd_ref, n_chunks, [bias_fn] * KV_GROUPS, tq,
                         bound_ref, (shift0_scr, shift1_scr))
    o = jnp.concatenate([_group_output(og, lo64, tq) for og in outs], axis=1) * zd_ref[0].astype(f32)
    half = wout_ref.shape[0] // 2
    o_ref[0] = (x_ref[0] + _dot(onsa_ref[0], wout_ref[0:half, :])
                + _dot(o.astype(bf16), wout_ref[half:, :]))


def _dsa_attention(bound, qd, qi, misc, ki, kd, vd, zd, x, o_nsa, w_out):
    B, S, _ = qd.shape
    tq = TQ_DSA
    assert S <= N_SLABS * LANES and KC % tq == 0
    tri = jnp.asarray(np.triu(np.ones((KC, KC), np.float32)), bf16)
    row = lambda c: pl.BlockSpec((1, tq, c), lambda b, i: (b, i, 0))
    whole = pl.BlockSpec((1, KV_GROUPS, S, LANES), lambda b, i: (b, 0, 0, 0))
    return pl.pallas_call(
        functools.partial(_dsa_kernel, top_k=min(DSA_TOPK_MAX, S // 4), tq=tq),
        grid=(B, S // tq),
        in_specs=[_SMEM_SCALAR, row(512), row(256), row(LANES), pl.BlockSpec((1, S, LANES), lambda b, i: (b, 0, 0)),
                  whole, whole, row(512), pl.BlockSpec((KC, KC), lambda b, i: (0, 0)),
                  row(D_MODEL), row(512),
                  pl.BlockSpec(w_out.shape, lambda b, i: (0, 0), pipeline_mode=pl.Buffered(1))],
        out_specs=row(D_MODEL),
        out_shape=jax.ShapeDtypeStruct((B, S, D_MODEL), f32),
        scratch_shapes=[pltpu.VMEM((N_SLABS, LANES, tq), f32), pltpu.VMEM((N_SLABS, LANES, tq), jnp.int32),
                        pltpu.VMEM((tq, S), f32)] + _shift_scratch(tq) + [pltpu.VMEM((SUBLANES, tq), jnp.int32)],
        compiler_params=pltpu.CompilerParams(
            dimension_semantics=("arbitrary", "arbitrary"), vmem_limit_bytes=VMEM_LIMIT),
        name="dsa_attention",
    )(bound, qd, qi, misc, ki, kd, vd, zd, tri, x, o_nsa, w_out)


def _rope_tables(pos):
    half = HEAD_DIM // 2
    inv_freq = ROPE_THETA ** (-jnp.arange(half, dtype=f32) / half)
    ang = pos[:, None] * inv_freq[None, :]
    cos, sin = jnp.cos(ang), jnp.sin(ang)
    cos_t = jnp.tile(cos, (1, LANES // half))
    sin_t = jnp.tile(jnp.concatenate([-sin, sin], axis=1), (1, LANES // HEAD_DIM))
    return cos_t, sin_t


def _overlap_t(seq, n_cmp_pad):
    n_cmp = (seq - CMP_BLOCK) // CMP_STRIDE + 1
    n_sel = seq // SEL_BLOCK
    c_start = np.arange(n_cmp) * CMP_STRIDE
    j_start = np.arange(n_sel) * SEL_BLOCK
    ov = np.clip(np.minimum(c_start[:, None] + CMP_BLOCK, j_start[None, :] + SEL_BLOCK)
                 - np.maximum(c_start[:, None], j_start[None, :]), 0, None).astype(np.float32) / CMP_BLOCK
    out = np.zeros((n_sel, n_cmp_pad), np.float32)
    out[:, :n_cmp] = ov.T
    return out


def _block_diag2(w):
    z = jnp.zeros_like(w)
    return jnp.concatenate([jnp.concatenate([w, z], axis=-1), jnp.concatenate([z, w], axis=-1)], axis=-2)


def _layer(x, norm_gain, w_in, nsa_q_gain, nsa_kc_gain, nsa_ks_gain, nsa_kw_gain,
           cmp_pe_k, cmp_k_w1, cmp_k_b1, cmp_k_w2, cmp_pe_v, cmp_v_w1, cmp_v_b1, cmp_v_w2,
           dsa_q_gain, dsa_k_gain, w_out):
    B, S, _ = x.shape
    assert S % KC == 0 and S >= WINDOW + TQ_NSA
    n_cmp_pad = S // CMP_STRIDE

    head_of_lane = np.arange(2 * LANES) // HEAD_DIM
    bd = jnp.asarray(head_of_lane[:, None] == head_of_lane[None, :], bf16)
    dup = lambda v: jnp.tile(v.reshape(1, -1), (1, 2))
    quad = lambda v: jnp.tile(v.reshape(1, -1), (1, 4))
    gains = jnp.concatenate([quad(nsa_q_gain), jnp.concatenate([dup(nsa_ks_gain), dup(nsa_kw_gain)], axis=1),
                             quad(dsa_q_gain), quad(dsa_k_gain), jnp.ones((4, 2 * LANES), f32)], axis=0)
    cos_t, sin_t = _rope_tables(jnp.arange(S, dtype=f32))

    (qn, qd, qi, ki, ks, kw, kd, vs, vw, vd, kc, vc, zn, zd, misc) = _in_projection(
        x, norm_gain.reshape(1, -1), w_in, bd, cos_t, sin_t, gains)

    cmp_pos = (jnp.arange(n_cmp_pad) * CMP_STRIDE + CMP_BLOCK - 1).astype(f32)
    cos_c, sin_c = _rope_tables(cmp_pos)

    def cmp_weights(pe, w1, b1, w2):
        w1_bd = _block_diag2(w1.reshape(CMP_BLOCK, HEAD_DIM, CMP_HIDDEN)).astype(bf16)
        return jnp.tile(pe, (1, 2)), w1_bd, dup(b1), _block_diag2(w2).astype(bf16)

    kcmp = _compress(kc, *cmp_weights(cmp_pe_k, cmp_k_w1, cmp_k_b1, cmp_k_w2), bd[:LANES, :LANES], dup(nsa_kc_gain),
                     cos_c, sin_c, is_key=True)
    vcmp = _compress(vc, *cmp_weights(cmp_pe_v, cmp_v_w1, cmp_v_b1, cmp_v_w2), bd[:LANES, :LANES], dup(nsa_kc_gain),
                     cos_c, sin_c, is_key=False)

    ovt = jnp.asarray(_overlap_t(S, n_cmp_pad), bf16)
    o_nsa = _nsa_attention(_score_bound(nsa_q_gain, nsa_kc_gain, nsa_ks_gain, nsa_kw_gain),
                           qn, kcmp, vcmp, ks, vs, kw, vw, misc, zn, ovt)

    return _dsa_attention(_score_bound(dsa_q_gain, dsa_k_gain), qd, qi, misc, ki, kd, vd, zd,
                          x, o_nsa, w_out.astype(bf16))


def kernel(x, norm_gain, w_in, nsa_q_gain, nsa_kc_gain, nsa_ks_gain, nsa_kw_gain, cmp_pe_k, cmp_k_w1,
           cmp_k_b1, cmp_k_w2, cmp_pe_v, cmp_v_w1, cmp_v_b1, cmp_v_w2, dsa_q_gain, dsa_k_gain, w_out):
    for l in range(norm_gain.shape[0]):
        x = _layer(x, norm_gain[l], w_in[l:l + 1], nsa_q_gain[l], nsa_kc_gain[l], nsa_ks_gain[l], nsa_kw_gain[l],
                   cmp_pe_k[l], cmp_k_w1[l], cmp_k_b1[l], cmp_k_w2[l], cmp_pe_v[l], cmp_v_w1[l], cmp_v_b1[l],
                   cmp_v_w2[l], dsa_q_gain[l], dsa_k_gain[l], w_out[l])
    return x
```

```python
import functools

import numpy as np
import jax
import jax.numpy as jnp
from jax import lax
from jax.experimental import pallas as pl
from jax.experimental.pallas import tpu as pltpu

D_MODEL = 1024
HEAD_DIM = 64
NSA_HEADS = 8
DSA_HEADS = 8
KV_GROUPS = 2
HEADS_PER_GROUP = 4
IDX_HEADS = 4
CMP_BLOCK = 32
CMP_STRIDE = 16
CMP_HIDDEN = 256
SEL_BLOCK = 64
SEL_TOPN = 16
WINDOW = 512
DSA_TOPK_MAX = 256
ROPE_THETA = 10000.0
EPS = 1e-6
NEG = -1e30
FORCE = 1e6
ATTN_SCALE = HEAD_DIM ** -0.5
IDX_SCALE = HEAD_DIM ** -0.5
WI_SCALE = IDX_HEADS ** -0.5
LOG2E = 1.4426950408889634
Q_SCALE = ATTN_SCALE * LOG2E

LANES = 128
SUBLANES = 8
M_FLOOR = -5e29
NON_CAUSAL_MARK = -3.0e38
MAX_FIXED_SHIFT = 30.0
NORM_SLACK = 1.03
VMEM_LIMIT = 56 * 1024 * 1024

IN_WIDTHS = (512, 128, 128, 128, 128, 128, 128, 24, 512, 512, 128, 128, 256, 64, 4, 512)
IN_NAMES = ("q_n", "kc", "vc", "ks", "vs", "kw", "vw", "gate", "z_n",
            "q_d", "k_d", "v_d", "qi", "ki", "wi", "z_d")
IN_COLS = sum(IN_WIDTHS)

CH_QN, CH_KS, CH_KW, CH_QD, CH_KD, CH_QI, CH_KI = 0, 4, 5, 6, 10, 11, 13
CH_KC, CH_VC, CH_VS, CH_VW, CH_VD, CH_ZN, CH_ZD, CH_MISC = 14, 15, 16, 17, 18, 19, 23, 27
N_CHUNKS = 28
MISC_WI = 24

TM_PROJ = 512
TQ_NSA = 256
TQ_DSA = 256
KC = 512

f32 = jnp.float32
bf16 = jnp.bfloat16


def _repack_weight(w_ref, w16_ref):
    off = dict(zip(IN_NAMES, np.cumsum((0,) + IN_WIDTHS[:-1])))
    width = dict(zip(IN_NAMES, IN_WIDTHS))
    order = ("q_n", "ks", "kw", "q_d", "k_d", "qi", "ki", "ki", "kc", "vc", "vs", "vw", "v_d", "z_n", "z_d",
             "gate", "wi")
    dst = 0
    for n in order:
        w16_ref[:, dst:dst + width[n]] = w_ref[0, :, off[n]:off[n] + width[n]].astype(bf16)
        dst += width[n]
    w16_ref[:, dst:] = jnp.zeros((D_MODEL, N_CHUNKS * LANES - dst), bf16)


def _nt_dot(a, b):
    return lax.dot_general(a, b, (((1,), (1,)), ((), ())), preferred_element_type=f32)


def _dot(a, b):
    return jnp.dot(a, b, preferred_element_type=f32)


def _split_bf16(v):
    hi = v.astype(bf16)
    lo = (v - hi.astype(f32)).astype(bf16)
    return hi, lo


def _head_rms_norm(y, bd, gain_row):
    ssq = _dot((y * y).astype(bf16), bd)
    return y * lax.rsqrt(ssq * (1.0 / HEAD_DIM) + EPS) * gain_row


def _rope(y, cos, sin_signed, lo32):
    partner = jnp.where(lo32, pltpu.roll(y, LANES - 32, 1), pltpu.roll(y, 32, 1))
    return y * cos + partner * sin_signed


def _dup_halves(y, lo64):
    r = pltpu.roll(y, 64, 1)
    return jnp.where(lo64, y, r), jnp.where(lo64, r, y)


def _value_with_ones(y, lo64):
    return jnp.where(lo64, y, 1.0), jnp.where(lo64, pltpu.roll(y, 64, 1), 1.0)


def _inproj_kernel(x_ref, ng_ref, w_ref, bd_ref, cos_ref, sin_ref, gains_ref,
                   qn_ref, qd_ref, qi_ref, ki_ref, ks_ref, kw_ref, kd_ref,
                   vs_ref, vw_ref, vd_ref, kc_ref, vc_ref, zn_ref, zd_ref, misc_ref, w16_ref):
    @pl.when((pl.program_id(0) == 0) & (pl.program_id(1) == 0))
    def _():
        _repack_weight(w_ref, w16_ref)

    x = x_ref[0]
    ms = jnp.mean(x * x, axis=-1, keepdims=True)
    h = (x * lax.rsqrt(ms + EPS) * ng_ref[...]).astype(bf16)
    cos = cos_ref[...]
    sin = sin_ref[...]
    bd = bd_ref[...]
    lane = lax.broadcasted_iota(jnp.int32, (1, LANES), 1)
    lo32 = (lane % 64) < 32
    lo64 = lane < 64

    def proj(c0, n):
        return _dot(h, w16_ref[:, c0 * LANES:(c0 + n) * LANES])

    def chunk(y, i):
        return y[:, i * LANES:(i + 1) * LANES]

    def normed_rope(y, gain_idx):
        w = y.shape[1]
        yn = _head_rms_norm(y, bd[:w, :w], gains_ref[gain_idx:gain_idx + 1, 0:w])
        return [_rope(chunk(yn, i), cos, sin, lo32) for i in range(w // LANES)]

    for c0, gain_idx, ref in ((CH_QN, 0, qn_ref), (CH_QD, 2, qd_ref)):
        y = proj(c0, 4)
        for i in range(2):
            for j, r in enumerate(normed_rope(y[:, 2 * i * LANES:(2 * i + 2) * LANES], gain_idx)):
                ref[0, :, (2 * i + j) * LANES:(2 * i + j + 1) * LANES] = (r * Q_SCALE).astype(bf16)
    y = proj(CH_QI, 2)
    for i in range(2):
        qi_ref[0, :, i * LANES:(i + 1) * LANES] = (_rope(chunk(y, i), cos, sin, lo32) * IDX_SCALE).astype(bf16)
    ki_ref[0] = _rope(proj(CH_KI, 1), cos, sin, lo32).astype(bf16)

    k_pairs = normed_rope(proj(CH_KS, 2), 1) + normed_rope(proj(CH_KD, 1), 3)
    for r, ref in zip(k_pairs, (ks_ref, kw_ref, kd_ref)):
        a, b = _dup_halves(r, lo64)
        ref[0, 0] = a.astype(bf16)
        ref[0, 1] = b.astype(bf16)
    for c0, ref in ((CH_VS, vs_ref), (CH_VW, vw_ref), (CH_VD, vd_ref)):
        a, b = _value_with_ones(proj(c0, 1), lo64)
        ref[0, 0] = a.astype(bf16)
        ref[0, 1] = b.astype(bf16)

    kc_ref[0] = proj(CH_KC, 1)
    vc_ref[0] = proj(CH_VC, 1)
    z = proj(CH_ZN, 4)
    zn_ref[0] = (z * jax.nn.sigmoid(z)).astype(bf16)
    z = proj(CH_ZD, 4)
    zd_ref[0] = (z * jax.nn.sigmoid(z)).astype(bf16)
    m = proj(CH_MISC, 1)
    misc_ref[0] = jnp.where(lane < MISC_WI, jax.nn.sigmoid(m), m * WI_SCALE)


def _in_projection(x, norm_gain, w_in, bd, cos_t, sin_t, gains):
    B, S, _ = x.shape
    tm = min(TM_PROJ, S)
    grid = (B, S // tm)
    row = lambda c: pl.BlockSpec((1, tm, c), lambda b, i: (b, i, 0))
    dup = pl.BlockSpec((1, KV_GROUPS, tm, LANES), lambda b, i: (b, 0, i, 0))
    const = lambda shape: pl.BlockSpec(shape, lambda b, i: tuple(0 for _ in shape))
    tab = pl.BlockSpec((tm, LANES), lambda b, i: (i, 0))
    sds = jax.ShapeDtypeStruct
    out_shape = (
        sds((B, S, 512), bf16), sds((B, S, 512), bf16), sds((B, S, 256), bf16), sds((B, S, LANES), bf16),
        sds((B, KV_GROUPS, S, LANES), bf16), sds((B, KV_GROUPS, S, LANES), bf16),
        sds((B, KV_GROUPS, S, LANES), bf16), sds((B, KV_GROUPS, S, LANES), bf16),
        sds((B, KV_GROUPS, S, LANES), bf16), sds((B, KV_GROUPS, S, LANES), bf16),
        sds((B, S, LANES), f32), sds((B, S, LANES), f32),
        sds((B, S, 512), bf16), sds((B, S, 512), bf16), sds((B, S, LANES), f32),
    )
    out_specs = (row(512), row(512), row(256), row(LANES), dup, dup, dup, dup, dup, dup,
                 row(LANES), row(LANES), row(512), row(512), row(LANES))
    return pl.pallas_call(
        _inproj_kernel,
        grid=grid,
        in_specs=[row(D_MODEL), const((1, D_MODEL)),
                  pl.BlockSpec((1, D_MODEL, IN_COLS), lambda b, i: (0, 0, 0), pipeline_mode=pl.Buffered(1)),
                  const((2 * LANES, 2 * LANES)), tab, tab, const((8, 2 * LANES))],
        out_specs=out_specs,
        out_shape=out_shape,
        scratch_shapes=[pltpu.VMEM((D_MODEL, N_CHUNKS * LANES), bf16)],
        compiler_params=pltpu.CompilerParams(
            dimension_semantics=("arbitrary", "arbitrary"), vmem_limit_bytes=VMEM_LIMIT),
        name="in_projection",
    )(x, norm_gain, w_in, bd, cos_t, sin_t, gains)


def _compress_kernel(kc_ref, vc_ref, pek_ref, w1k_ref, b1k_ref, w2k_ref, pev_ref, w1v_ref, b1v_ref, w2v_ref,
                     bd_ref, gain_ref, cos_ref, sin_ref, kout_ref, vout_ref, *, n_rows):
    half = CMP_BLOCK // 2

    def mlp(src_ref, pe_ref, w1_ref, b1_ref, w2_ref):
        acc_a = jnp.zeros((n_rows, 2 * CMP_HIDDEN), f32)
        acc_b = jnp.zeros((n_rows, 2 * CMP_HIDDEN), f32)
        for l in range(half):
            rows = src_ref[0, pl.ds(l, n_rows, stride=CMP_STRIDE), :]
            acc_a = acc_a + _dot((rows + pe_ref[l:l + 1, :]).astype(bf16), w1_ref[l])
            acc_b = acc_b + _dot((rows + pe_ref[l + half:l + half + 1, :]).astype(bf16), w1_ref[l + half])
        pre = acc_a + pltpu.roll(acc_b, n_rows - 1, 0) + b1_ref[...]
        hid = pre * jax.nn.sigmoid(pre)
        return _dot(hid.astype(bf16), w2_ref[...])

    lane = lax.broadcasted_iota(jnp.int32, (1, LANES), 1)
    k = _head_rms_norm(mlp(kc_ref, pek_ref, w1k_ref, b1k_ref, w2k_ref), bd_ref[...], gain_ref[...])
    a, b = _dup_halves(_rope(k, cos_ref[...], sin_ref[...], (lane % 64) < 32), lane < 64)
    kout_ref[0, 0] = a.astype(bf16)
    kout_ref[0, 1] = b.astype(bf16)
    a, b = _value_with_ones(mlp(vc_ref, pev_ref, w1v_ref, b1v_ref, w2v_ref), lane < 64)
    vout_ref[0, 0] = a.astype(bf16)
    vout_ref[0, 1] = b.astype(bf16)


def _compress(kc, vc, k_weights, v_weights, bd, gain_dup, cos_c, sin_c):
    B, S, _ = kc.shape
    n_rows = S // CMP_STRIDE
    const = lambda shape: pl.BlockSpec(shape, lambda b: tuple(0 for _ in shape), pipeline_mode=pl.Buffered(1))
    src = pl.BlockSpec((1, S, LANES), lambda b: (b, 0, 0))
    weights = [const((CMP_BLOCK, LANES)), const((CMP_BLOCK, LANES, 2 * CMP_HIDDEN)),
               const((1, 2 * CMP_HIDDEN)), const((2 * CMP_HIDDEN, LANES))]
    out = pl.BlockSpec((1, KV_GROUPS, n_rows, LANES), lambda b: (b, 0, 0, 0))
    sds = jax.ShapeDtypeStruct((B, KV_GROUPS, n_rows, LANES), bf16)
    return pl.pallas_call(
        functools.partial(_compress_kernel, n_rows=n_rows),
        grid=(B,),
        in_specs=[src, src] + weights + weights + [const((LANES, LANES)), const((1, LANES)),
                                                   const((n_rows, LANES)), const((n_rows, LANES))],
        out_specs=(out, out),
        out_shape=(sds, sds),
        compiler_params=pltpu.CompilerParams(
            dimension_semantics=("arbitrary",), vmem_limit_bytes=VMEM_LIMIT),
        name="compress_kv",
    )(kc, vc, *k_weights, *v_weights, bd, gain_dup, cos_c, sin_c)


def _half_masked(pair, h, lo64):
    keep = lo64 if h % 2 == 0 else jnp.logical_not(lo64)
    return jnp.where(keep, pair, jnp.zeros_like(pair))


def _loop_by_pairs(n, body, init, quads=False):
    done = 0
    if quads:
        four = lambda i, cr: body(4 * i + 3, body(4 * i + 2, body(4 * i + 1, body(4 * i, cr))))
        init = lax.fori_loop(0, n // 4, four, init)
        done = 4 * (n // 4)
    pair = lambda i, cr: body(2 * i + 1, body(2 * i, cr))
    carry = lax.fori_loop(done // 2, n // 2, pair, init)
    return lax.fori_loop(2 * (n // 2), n, body, carry)


def _stack_group_queries(q, g, lo64):
    slabs = []
    for r in range(HEADS_PER_GROUP):
        h = g * HEADS_PER_GROUP + r
        slabs.append(_half_masked(q[:, (h // 2) * LANES:(h // 2 + 1) * LANES], h, lo64))
    return jnp.concatenate(slabs, axis=0)


def _masked_flash(q4s, k_ref, v_ref, n_chunks, bias_fns, tq, bound_ref, shift_scrs):
    rows = HEADS_PER_GROUP * tq
    bound = bound_ref[0]

    def scores(g, k0):
        s = _nt_dot(q4s[g], k_ref[0, g, pl.ds(k0, KC), :])
        return (s.reshape(HEADS_PER_GROUP, tq, KC) + bias_fns[g](k0)[None]).reshape(rows, KC)

    for g in range(KV_GROUPS):
        shift_scrs[g][...] = jnp.full((rows, 1), bound, f32)

    @pl.when(bound > MAX_FIXED_SHIFT)
    def _():
        def body(c, ms):
            k0 = pl.multiple_of(c * KC, KC)
            return tuple(jnp.maximum(ms[g], jnp.max(scores(g, k0), axis=-1, keepdims=True))
                         for g in range(KV_GROUPS))

        init = tuple(jnp.full((rows, 1), M_FLOOR, f32) for _ in range(KV_GROUPS))
        for g, m in enumerate(lax.fori_loop(0, n_chunks, body, init)):
            shift_scrs[g][...] = m

    def body(c, accs):
        k0 = pl.multiple_of(c * KC, KC)
        new = []
        for g in range(KV_GROUPS):
            p = jnp.exp2(scores(g, k0) - shift_scrs[g][...]).astype(bf16)
            new.append(accs[g] + _dot(p, v_ref[0, g, pl.ds(k0, KC), :]))
        return tuple(new)

    init = tuple(jnp.zeros((rows, LANES), f32) for _ in range(KV_GROUPS))
    out = _loop_by_pairs(n_chunks, body, init, quads=True)
    return [acc / acc[:, HEAD_DIM:HEAD_DIM + 1] for acc in out]


def _group_output(og, lo64, tq):
    pairs = []
    for p in range(2):
        even = og[(2 * p) * tq:(2 * p + 1) * tq]
        odd = og[(2 * p + 1) * tq:(2 * p + 2) * tq]
        pairs.append(jnp.where(lo64, even, pltpu.roll(odd, HEAD_DIM, 1)))
    return jnp.concatenate(pairs, axis=1)


def _top_n_blocks(imp, n_sel, top_n):
    groups = [imp[SUBLANES * a:SUBLANES * (a + 1)] for a in range(n_sel // SUBLANES)]
    jsub = lax.broadcasted_iota(jnp.int32, groups[0].shape, 0)
    ranks = [jnp.zeros(groups[0].shape, f32) for _ in groups]
    for i in range(n_sel):
        row = imp[i:i + 1, :]
        for a, grp in enumerate(groups):
            if SUBLANES * a > i:
                beats = row >= grp
            elif SUBLANES * (a + 1) - 1 < i:
                beats = row > grp
            else:
                tie = jnp.where(jsub + SUBLANES * a > i, 1.0, 0.0)
                ranks[a] = ranks[a] + jnp.where(row > grp, 1.0, jnp.where(row == grp, tie, 0.0))
                continue
            ranks[a] = ranks[a] + jnp.where(beats, 1.0, 0.0)
    rank = jnp.concatenate(ranks, axis=0)
    return jnp.where(rank < top_n, 1.0, 0.0)


def _nsa_kernel(bound_ref, qn_ref, kcmp_ref, vcmp_ref, ks_ref, vs_ref, kw_ref, vw_ref, misc_ref, zn_ref, ovt_ref,
                expand_ref, o_ref, shift0_scr, shift1_scr, cshift0_scr, cshift1_scr, wshift0_scr, wshift1_scr,
                bias_scr, *, n_cmp_pad, n_sel, top_n):
    tq = TQ_NSA
    cmp_shift = [cshift0_scr, cshift1_scr]
    win_shift = [wshift0_scr, wshift1_scr]
    t0 = pl.program_id(1) * tq
    lane = lax.broadcasted_iota(jnp.int32, (1, LANES), 1)
    lo64 = lane < 64
    q = qn_ref[0]
    misc = misc_ref[0]
    t_col = t0 + lax.broadcasted_iota(jnp.int32, (tq, 1), 0)
    n_chunks = (t0 + tq + KC - 1) // KC
    win_span = WINDOW + tq
    win_start = pl.multiple_of(jnp.maximum(t0 - WINDOW, 0), tq)
    ovt = ovt_ref[...]
    rows = HEADS_PER_GROUP * tq

    def add_bias(s, bias):
        return (s.reshape(HEADS_PER_GROUP, tq, s.shape[-1]) + bias[None]).reshape(rows, s.shape[-1])

    cmp_end = lax.broadcasted_iota(jnp.int32, (1, n_cmp_pad), 1) * CMP_STRIDE + (CMP_BLOCK - 1)
    cmp_bias = jnp.where(cmp_end <= t_col, 0.0, NEG)
    win_diff = t_col - (win_start + lax.broadcasted_iota(jnp.int32, (1, win_span), 1))
    win_bias = jnp.where((win_diff >= 0) & (win_diff < WINDOW), 0.0, NEG)
    q4s = [_stack_group_queries(q, g, lo64) for g in range(KV_GROUPS)]

    def cmp_scores(g):
        return add_bias(_nt_dot(q4s[g], kcmp_ref[0, g]), cmp_bias)

    def win_scores(g):
        return add_bias(_nt_dot(q4s[g], kw_ref[0, g, pl.ds(win_start, win_span), :]), win_bias)

    for scr in cmp_shift + win_shift:
        scr[...] = jnp.full((rows, 1), bound_ref[0], f32)

    @pl.when(bound_ref[0] > MAX_FIXED_SHIFT)
    def _():
        for g in range(KV_GROUPS):
            cmp_shift[g][...] = jnp.maximum(jnp.max(cmp_scores(g), axis=-1, keepdims=True), M_FLOOR)
            win_shift[g][...] = jnp.max(win_scores(g), axis=-1, keepdims=True)

    o_cmp, o_win, sels = [], [], []
    for g in range(KV_GROUPS):
        e = jnp.exp2(cmp_scores(g) - cmp_shift[g][...])
        l = jnp.sum(e, axis=-1, keepdims=True)
        p = e * (1.0 / jnp.maximum(l, 1e-30))
        o_cmp.append(_dot(p.astype(bf16), vcmp_ref[0, g]))

        psum = p[0:tq] + p[tq:2 * tq] + p[2 * tq:3 * tq] + p[3 * tq:4 * tq]
        hi, lo = _split_bf16(psum)
        imp = _nt_dot(ovt, hi) + _nt_dot(ovt, lo)
        jrow = lax.broadcasted_iota(jnp.int32, (n_sel, tq), 0)
        tl = t0 + lax.broadcasted_iota(jnp.int32, (n_sel, tq), 1)
        cur = lax.shift_right_logical(tl, 6)
        forced = (jrow == 0) | (jrow == cur) | (jrow == cur - 1)
        imp = jnp.where(forced, FORCE, imp)
        imp = jnp.where(jrow * SEL_BLOCK <= tl, imp, NEG)
        pad_row = lax.broadcasted_iota(jnp.int32, (LANES - n_sel, tq), 0)
        chosen = jnp.concatenate([_top_n_blocks(imp, n_sel, top_n) * -NEG, jnp.where(pad_row == 0, NEG, 0.0)], axis=0)
        sels.append(chosen.T.astype(bf16))

        e = jnp.exp2(win_scores(g) - win_shift[g][...])
        acc = _dot(e.astype(bf16), vw_ref[0, g, pl.ds(win_start, win_span), :])
        o_win.append(acc / acc[:, HEAD_DIM:HEAD_DIM + 1])

    def bias_body(c, carry):
        k0 = pl.multiple_of(c * KC, KC)
        for g in range(KV_GROUPS):
            bias_scr[g, :, pl.ds(k0, KC)] = _dot(sels[g], expand_ref[:, pl.ds(k0, KC)])
        return carry

    _loop_by_pairs(n_chunks, bias_body, 0)
    k_last = pl.multiple_of((n_chunks - 1) * KC, KC)
    causal = jnp.where(k_last + lax.broadcasted_iota(jnp.int32, (1, KC), 1) <= t_col, 0.0, NEG)
    for g in range(KV_GROUPS):
        bias_scr[g, :, pl.ds(k_last, KC)] = bias_scr[g, :, pl.ds(k_last, KC)] + causal

    o_sel = _masked_flash(q4s, ks_ref, vs_ref, n_chunks,
                          [lambda k0, g=g: bias_scr[g, :, pl.ds(k0, KC)] for g in range(KV_GROUPS)], tq,
                          bound_ref, (shift0_scr, shift1_scr))

    group_outs = []
    for g in range(KV_GROUPS):
        def gate(branch):
            cols = [misc[:, (g * HEADS_PER_GROUP + r) * 3 + branch:(g * HEADS_PER_GROUP + r) * 3 + branch + 1]
                    for r in range(HEADS_PER_GROUP)]
            return jnp.concatenate(cols, axis=0)

        og = gate(0) * o_cmp[g] + gate(1) * o_sel[g] + gate(2) * o_win[g]
        group_outs.append(_group_output(og, lo64, tq))

    o = jnp.concatenate(group_outs, axis=1) * zn_ref[0].astype(f32)
    o_ref[0] = o.astype(bf16)


def _score_bound(q_gain, *k_gains):
    k_max = functools.reduce(jnp.maximum, [jnp.max(jnp.abs(g)) for g in k_gains])
    return (Q_SCALE * HEAD_DIM * NORM_SLACK * jnp.max(jnp.abs(q_gain)) * k_max).reshape(1)


_SMEM_SCALAR = pl.BlockSpec(memory_space=pltpu.SMEM)


def _shift_scratch(tq):
    return [pltpu.VMEM((HEADS_PER_GROUP * tq, 1), f32)] * KV_GROUPS


def _nsa_attention(bound, qn, kcmp, vcmp, ks, vs, kw, vw, misc, zn, ovt):
    B, S, _ = qn.shape
    tq = TQ_NSA
    n_cmp_pad = kcmp.shape[2]
    n_sel = S // SEL_BLOCK
    row = lambda c: pl.BlockSpec((1, tq, c), lambda b, i: (b, i, 0))
    whole = lambda n: pl.BlockSpec((1, KV_GROUPS, n, LANES), lambda b, i: (b, 0, 0, 0))
    kern = functools.partial(_nsa_kernel, n_cmp_pad=n_cmp_pad, n_sel=n_sel, top_n=min(SEL_TOPN, n_sel))
    assert n_sel < LANES
    expand = np.zeros((LANES, S), np.float32)
    expand[np.arange(S) // SEL_BLOCK, np.arange(S)] = 1.0
    expand[n_sel] = 1.0
    expand = jnp.asarray(expand, bf16)
    return pl.pallas_call(
        kern,
        grid=(B, S // tq),
        in_specs=[_SMEM_SCALAR, row(512), whole(n_cmp_pad), whole(n_cmp_pad), whole(S), whole(S), whole(S),
                  whole(S), row(LANES), row(512), pl.BlockSpec((n_sel, n_cmp_pad), lambda b, i: (0, 0)),
                  pl.BlockSpec((LANES, S), lambda b, i: (0, 0))],
        out_specs=row(512),
        out_shape=jax.ShapeDtypeStruct((B, S, 512), bf16),
        scratch_shapes=3 * _shift_scratch(tq) + [pltpu.VMEM((KV_GROUPS, tq, S), f32)],
        compiler_params=pltpu.CompilerParams(
            dimension_semantics=("arbitrary", "arbitrary"), vmem_limit_bytes=VMEM_LIMIT),
        name="nsa_attention",
    )(bound, qn, kcmp, vcmp, ks, vs, kw, vw, misc, zn, ovt, expand)


N_SLABS = 32
_TRANSPOSE_STAGES = ((16, 0x0000FFFF), (8, 0x00FF00FF), (4, 0x0F0F0F0F), (2, 0x33333333), (1, 0x55555555))


def _ordered_bits(v):
    return v ^ (jnp.right_shift(v, 31) & jnp.int32(0x7FFFFFFF))


def _bit_planes(score_ref, plane_ref, n_words):
    stages = [s for s in _TRANSPOSE_STAGES if s[0] < n_words]
    for stage, (j, mask) in enumerate(stages):
        if stage == 0:
            load = lambda k: _ordered_bits(pltpu.bitcast(score_ref[k], jnp.int32))
        else:
            load = lambda k: plane_ref[k]

        def body(p, carry, j=j, mask=mask, load=load):
            k = jnp.left_shift(p & ~(j - 1), 1) | (p & (j - 1))
            lo = load(k)
            hi = load(k + j)
            t = (lo ^ lax.shift_right_logical(hi, j)) & mask
            plane_ref[k] = lo ^ t
            plane_ref[k + j] = hi ^ jnp.left_shift(t, j)
            return carry

        lax.fori_loop(0, n_words // 2, body, 0)


def _dsa_kernel(bound_ref, qd_ref, qi_ref, misc_ref, ki_ref, kd_ref, vd_ref, zd_ref, tri_ref,
                x_ref, onsa_ref, wout_ref, o_ref, score_scr, plane_scr, bias_scr, shift0_scr, shift1_scr, select_scr, *, top_k, tq):
    t0 = pl.program_id(1) * tq
    lane = lax.broadcasted_iota(jnp.int32, (1, LANES), 1)
    lo64 = lane < 64
    t_col = t0 + lax.broadcasted_iota(jnp.int32, (tq, 1), 0)
    t_row = t0 + lax.broadcasted_iota(jnp.int32, (1, tq), 1)
    n_chunks = (t0 + tq + KC - 1) // KC
    slabs_per_chunk = KC // LANES
    int_min = jnp.int32(-2 ** 31)

    def chunk_slabs(c):
        return pl.ds(c * slabs_per_chunk, slabs_per_chunk)

    def to_col(row):
        return jnp.broadcast_to(row, (SUBLANES, tq)).T[:, 0:1]

    qi = qi_ref[0]
    qi_heads = [_half_masked(qi[:, (h // 2) * LANES:(h // 2 + 1) * LANES], h, lo64) for h in range(IDX_HEADS)]
    misc_t = misc_ref[0].T
    wi = [misc_t[MISC_WI + h:MISC_WI + h + 1, :] for h in range(IDX_HEADS)]

    def score_chunk(c, has_later_keys):
        k0 = pl.multiple_of(c * KC, KC)
        kib = ki_ref[0, pl.ds(k0, KC), :]
        sc = jnp.zeros((KC, tq), f32)
        for h in range(IDX_HEADS):
            sc = sc + jnp.maximum(_nt_dot(kib, qi_heads[h]), 0.0) * wi[h]
        sc = jnp.where(sc == 0.0, 0.0, sc)
        if has_later_keys:
            kpos = k0 + lax.broadcasted_iota(jnp.int32, (KC, 1), 0)
            sc = jnp.where(kpos <= t_row, sc, NON_CAUSAL_MARK)
        score_scr[chunk_slabs(c)] = sc.reshape(slabs_per_chunk, LANES, tq)

    def score_body(c, carry):
        score_chunk(c, False)
        return carry

    _loop_by_pairs(n_chunks - 1, score_body, 0)
    score_chunk(n_chunks - 1, True)

    def fill_body(c, carry):
        score_scr[chunk_slabs(c)] = jnp.full((slabs_per_chunk, LANES, tq), NON_CAUSAL_MARK, f32)
        return carry

    lax.fori_loop(n_chunks, N_SLABS // slabs_per_chunk, fill_body, 0)

    kf = float(top_k)

    def radix_select(n_words):
        _bit_planes(score_scr, plane_scr, n_words)
        all_ones = jnp.int32(-1 if n_words == 32 else 0xFFFF)

        def select_bit(i, carry):
            alive, above, thr = carry
            if n_words == 32:
                plane = plane_scr[i]
            else:
                word = plane_scr[i & 15]
                plane = jnp.where(i < 16, lax.shift_right_logical(word, 16), word) & all_ones
            plane = plane ^ jnp.where(i == 0, all_ones, jnp.int32(0))
            ones = alive & plane
            c1 = jnp.sum(lax.population_count(ones).astype(f32), axis=0, keepdims=True)
            take = (above + c1) >= kf
            alive = jnp.where(take, ones, alive ^ ones)
            above = jnp.where(take, above, above + c1)
            thr = jnp.where(take, thr | jnp.left_shift(jnp.int32(1), 31 - i), thr)
            return alive, above, thr

        init = (jnp.full((LANES, tq), all_ones, jnp.int32), jnp.zeros((1, tq), f32), jnp.zeros((1, tq), jnp.int32))
        _, above, thr_u = lax.fori_loop(0, 32, select_bit, init)
        select_scr[0:1, :] = thr_u
        select_scr[1:2, :] = pltpu.bitcast(above, jnp.int32)

    few_slabs = n_chunks * slabs_per_chunk <= 16

    @pl.when(few_slabs)
    def _():
        radix_select(16)

    @pl.when(jnp.logical_not(few_slabs))
    def _():
        radix_select(32)

    thr_u = select_scr[0:1, :]
    above = pltpu.bitcast(select_scr[1:2, :], f32)

    def key_to_score(key_row):
        return pltpu.bitcast(_ordered_bits(key_row), f32)

    def write_bias(thr_row, need_row):
        thr = to_col(thr_row)
        need = to_col(need_row)

        def body(c, carry):
            ties_before, n_above = carry
            k0 = pl.multiple_of(c * KC, KC)
            blk = score_scr[chunk_slabs(c)].reshape(KC, tq).T
            gt = jnp.where(blk > thr, 1.0, 0.0)
            eq = jnp.where(blk == thr, 1.0, 0.0)
            prefix = ties_before + _dot(eq.astype(bf16), tri_ref[...])
            chosen = gt + jnp.where(prefix <= need, eq, 0.0)
            kpos = k0 + lax.broadcasted_iota(jnp.int32, (1, KC), 1)
            bias_scr[:, pl.ds(k0, KC)] = jnp.where(kpos <= t_col, jnp.where(chosen > 0.5, 0.0, NEG), NEG)
            return prefix[:, KC - 1:KC], n_above + jnp.sum(gt, axis=-1, keepdims=True)

        zero = jnp.zeros((tq, 1), f32)
        return _loop_by_pairs(n_chunks, body, (zero, zero))

    ties, n_above = write_bias(key_to_score(thr_u ^ int_min), kf - above)
    verified = (n_above == to_col(above)) & (n_above < kf) & (n_above + ties >= kf)

    @pl.when(jnp.sum(jnp.where(verified, 0.0, 1.0)) > 0.0)
    def _():
        def count(pred):
            def body(c, acc):
                blk = score_scr[chunk_slabs(c)].reshape(KC, tq)
                return acc + jnp.sum(jnp.where(pred(blk), 1.0, 0.0), axis=0, keepdims=True)
            return lax.fori_loop(0, N_SLABS // slabs_per_chunk, body, jnp.zeros((1, tq), f32))

        key = jnp.where(count(lambda b: b >= 0.0) >= kf, jnp.int32(0), int_min)

        def bit(i, key):
            cand = key + jnp.left_shift(jnp.int32(1), 30 - i)
            cand_f = key_to_score(cand)
            return jnp.where(count(lambda b: b >= cand_f) >= kf, cand, key)

        thr_f = key_to_score(lax.fori_loop(0, 31, bit, key))
        write_bias(thr_f, kf - count(lambda b: b > thr_f))

    def bias_fn(k0):
        return bias_scr[:, pl.ds(k0, KC)]

    qd = qd_ref[0]
    q4s = [_stack_group_queries(qd, g, lo64) for g in range(KV_GROUPS)]
    outs = _masked_flash(q4s, kd_ref, vd_ref, n_chunks, [bias_fn] * KV_GROUPS, tq,
                         bound_ref, (shift0_scr, shift1_scr))
    o = jnp.concatenate([_group_output(og, lo64, tq) for og in outs], axis=1) * zd_ref[0].astype(f32)
    half = wout_ref.shape[0] // 2
    o_ref[0] = (x_ref[0] + _dot(onsa_ref[0], wout_ref[0:half, :])
                + _dot(o.astype(bf16), wout_ref[half:, :]))


def _dsa_attention(bound, qd, qi, misc, ki, kd, vd, zd, x, o_nsa, w_out):
    B, S, _ = qd.shape
    tq = TQ_DSA
    assert S <= N_SLABS * LANES and KC % tq == 0
    tri = jnp.asarray(np.triu(np.ones((KC, KC), np.float32)), bf16)
    row = lambda c: pl.BlockSpec((1, tq, c), lambda b, i: (b, i, 0))
    whole = pl.BlockSpec((1, KV_GROUPS, S, LANES), lambda b, i: (b, 0, 0, 0))
    return pl.pallas_call(
        functools.partial(_dsa_kernel, top_k=min(DSA_TOPK_MAX, S // 4), tq=tq),
        grid=(B, S // tq),
        in_specs=[_SMEM_SCALAR, row(512), row(256), row(LANES), pl.BlockSpec((1, S, LANES), lambda b, i: (b, 0, 0)),
                  whole, whole, row(512), pl.BlockSpec((KC, KC), lambda b, i: (0, 0)),
                  row(D_MODEL), row(512),
                  pl.BlockSpec(w_out.shape, lambda b, i: (0, 0), pipeline_mode=pl.Buffered(1))],
        out_specs=row(D_MODEL),
        out_shape=jax.ShapeDtypeStruct((B, S, D_MODEL), f32),
        scratch_shapes=[pltpu.VMEM((N_SLABS, LANES, tq), f32), pltpu.VMEM((N_SLABS, LANES, tq), jnp.int32),
                        pltpu.VMEM((tq, S), f32)] + _shift_scratch(tq) + [pltpu.VMEM((SUBLANES, tq), jnp.int32)],
        compiler_params=pltpu.CompilerParams(
            dimension_semantics=("arbitrary", "arbitrary"), vmem_limit_bytes=VMEM_LIMIT),
        name="dsa_attention",
    )(bound, qd, qi, misc, ki, kd, vd, zd, tri, x, o_nsa, w_out)


def _rope_tables(pos):
    half = HEAD_DIM // 2
    inv_freq = ROPE_THETA ** (-jnp.arange(half, dtype=f32) / half)
    ang = pos[:, None] * inv_freq[None, :]
    cos, sin = jnp.cos(ang), jnp.sin(ang)
    cos_t = jnp.tile(cos, (1, LANES // half))
    sin_t = jnp.tile(jnp.concatenate([-sin, sin], axis=1), (1, LANES // HEAD_DIM))
    return cos_t, sin_t


def _overlap_t(seq, n_cmp_pad):
    n_cmp = (seq - CMP_BLOCK) // CMP_STRIDE + 1
    n_sel = seq // SEL_BLOCK
    c_start = np.arange(n_cmp) * CMP_STRIDE
    j_start = np.arange(n_sel) * SEL_BLOCK
    ov = np.clip(np.minimum(c_start[:, None] + CMP_BLOCK, j_start[None, :] + SEL_BLOCK)
                 - np.maximum(c_start[:, None], j_start[None, :]), 0, None).astype(np.float32) / CMP_BLOCK
    out = np.zeros((n_sel, n_cmp_pad), np.float32)
    out[:, :n_cmp] = ov.T
    return out


def _block_diag2(w):
    z = jnp.zeros_like(w)
    return jnp.concatenate([jnp.concatenate([w, z], axis=-1), jnp.concatenate([z, w], axis=-1)], axis=-2)


def _layer(x, norm_gain, w_in, nsa_q_gain, nsa_kc_gain, nsa_ks_gain, nsa_kw_gain,
           cmp_pe_k, cmp_k_w1, cmp_k_b1, cmp_k_w2, cmp_pe_v, cmp_v_w1, cmp_v_b1, cmp_v_w2,
           dsa_q_gain, dsa_k_gain, w_out):
    B, S, _ = x.shape
    assert S % KC == 0 and S >= WINDOW + TQ_NSA
    n_cmp_pad = S // CMP_STRIDE

    head_of_lane = np.arange(2 * LANES) // HEAD_DIM
    bd = jnp.asarray(head_of_lane[:, None] == head_of_lane[None, :], bf16)
    dup = lambda v: jnp.tile(v.reshape(1, -1), (1, 2))
    quad = lambda v: jnp.tile(v.reshape(1, -1), (1, 4))
    gains = jnp.concatenate([quad(nsa_q_gain), jnp.concatenate([dup(nsa_ks_gain), dup(nsa_kw_gain)], axis=1),
                             quad(dsa_q_gain), quad(dsa_k_gain), jnp.ones((4, 2 * LANES), f32)], axis=0)
    cos_t, sin_t = _rope_tables(jnp.arange(S, dtype=f32))

    (qn, qd, qi, ki, ks, kw, kd, vs, vw, vd, kc, vc, zn, zd, misc) = _in_projection(
        x, norm_gain.reshape(1, -1), w_in, bd, cos_t, sin_t, gains)

    cmp_pos = (jnp.arange(n_cmp_pad) * CMP_STRIDE + CMP_BLOCK - 1).astype(f32)
    cos_c, sin_c = _rope_tables(cmp_pos)

    def cmp_weights(pe, w1, b1, w2):
        w1_bd = _block_diag2(w1.reshape(CMP_BLOCK, HEAD_DIM, CMP_HIDDEN)).astype(bf16)
        return jnp.tile(pe, (1, 2)), w1_bd, dup(b1), _block_diag2(w2).astype(bf16)

    kcmp, vcmp = _compress(kc, vc, cmp_weights(cmp_pe_k, cmp_k_w1, cmp_k_b1, cmp_k_w2),
                           cmp_weights(cmp_pe_v, cmp_v_w1, cmp_v_b1, cmp_v_w2),
                           bd[:LANES, :LANES], dup(nsa_kc_gain), cos_c, sin_c)

    ovt = jnp.asarray(_overlap_t(S, n_cmp_pad), bf16)
    o_nsa = _nsa_attention(_score_bound(nsa_q_gain, nsa_kc_gain, nsa_ks_gain, nsa_kw_gain),
                           qn, kcmp, vcmp, ks, vs, kw, vw, misc, zn, ovt)

    return _dsa_attention(_score_bound(dsa_q_gain, dsa_k_gain), qd, qi, misc, ki, kd, vd, zd,
                          x, o_nsa, w_out.astype(bf16))


def kernel(x, norm_gain, w_in, nsa_q_gain, nsa_kc_gain, nsa_ks_gain, nsa_kw_gain, cmp_pe_k, cmp_k_w1,
           cmp_k_b1, cmp_k_w2, cmp_pe_v, cmp_v_w1, cmp_v_b1, cmp_v_w2, dsa_q_gain, dsa_k_gain, w_out):
    for l in range(norm_gain.shape[0]):
        x = _layer(x, norm_gain[l], w_in[l:l + 1], nsa_q_gain[l], nsa_kc_gain[l], nsa_ks_gain[l], nsa_kw_gain[l],
                   cmp_pe_k[l], cmp_k_w1[l], cmp_k_b1[l], cmp_k_w2[l], cmp_pe_v[l], cmp_v_w1[l], cmp_v_b1[l],
                   cmp_v_w2[l], dsa_q_gain[l], dsa_k_gain[l], w_out[l])
    return x
```

```python
import functools

import numpy as np
import jax
import jax.numpy as jnp
from jax import lax
from jax.experimental import pallas as pl
from jax.experimental.pallas import tpu as pltpu

D_MODEL = 1024
HEAD_DIM = 64
NSA_HEADS = 8
DSA_HEADS = 8
KV_GROUPS = 2
HEADS_PER_GROUP = 4
IDX_HEADS = 4
CMP_BLOCK = 32
CMP_STRIDE = 16
CMP_HIDDEN = 256
SEL_BLOCK = 64
SEL_TOPN = 16
WINDOW = 512
DSA_TOPK_MAX = 256
ROPE_THETA = 10000.0
EPS = 1e-6
NEG = -1e30
FORCE = 1e6
ATTN_SCALE = HEAD_DIM ** -0.5
IDX_SCALE = HEAD_DIM ** -0.5
WI_SCALE = IDX_HEADS ** -0.5
LOG2E = 1.4426950408889634
Q_SCALE = ATTN_SCALE * LOG2E

LANES = 128
SUBLANES = 8
M_FLOOR = -5e29
NON_CAUSAL_MARK = -3.0e38
MAX_FIXED_SHIFT = 30.0
NORM_SLACK = 1.03
VMEM_LIMIT = 56 * 1024 * 1024

IN_WIDTHS = (512, 128, 128, 128, 128, 128, 128, 24, 512, 512, 128, 128, 256, 64, 4, 512)
IN_NAMES = ("q_n", "kc", "vc", "ks", "vs", "kw", "vw", "gate", "z_n",
            "q_d", "k_d", "v_d", "qi", "ki", "wi", "z_d")
IN_COLS = sum(IN_WIDTHS)

CH_QN, CH_KS, CH_KW, CH_QD, CH_KD, CH_QI, CH_KI = 0, 4, 5, 6, 10, 11, 13
CH_KC, CH_VC, CH_VS, CH_VW, CH_VD, CH_ZN, CH_ZD, CH_MISC = 14, 15, 16, 17, 18, 19, 23, 27
N_CHUNKS = 28
MISC_WI = 24

TM_PROJ = 512
TQ_NSA = 256
TQ_DSA = 256
KC = 512

f32 = jnp.float32
bf16 = jnp.bfloat16


def _repack_weight(w_ref, w16_ref):
    off = dict(zip(IN_NAMES, np.cumsum((0,) + IN_WIDTHS[:-1])))
    width = dict(zip(IN_NAMES, IN_WIDTHS))
    order = ("q_n", "ks", "kw", "q_d", "k_d", "qi", "ki", "ki", "kc", "vc", "vs", "vw", "v_d", "z_n", "z_d",
             "gate", "wi")
    dst = 0
    for n in order:
        w16_ref[:, dst:dst + width[n]] = w_ref[0, :, off[n]:off[n] + width[n]].astype(bf16)
        dst += width[n]
    w16_ref[:, dst:] = jnp.zeros((D_MODEL, N_CHUNKS * LANES - dst), bf16)


def _nt_dot(a, b):
    return lax.dot_general(a, b, (((1,), (1,)), ((), ())), preferred_element_type=f32)


def _dot(a, b):
    return jnp.dot(a, b, preferred_element_type=f32)


def _split_bf16(v):
    hi = v.astype(bf16)
    lo = (v - hi.astype(f32)).astype(bf16)
    return hi, lo


def _head_rms_norm(y, bd, gain_row):
    ssq = _dot((y * y).astype(bf16), bd)
    return y * lax.rsqrt(ssq * (1.0 / HEAD_DIM) + EPS) * gain_row


def _rope(y, cos, sin_signed, lo32):
    partner = jnp.where(lo32, pltpu.roll(y, LANES - 32, 1), pltpu.roll(y, 32, 1))
    return y * cos + partner * sin_signed


def _dup_halves(y, lo64):
    r = pltpu.roll(y, 64, 1)
    return jnp.where(lo64, y, r), jnp.where(lo64, r, y)


def _value_with_ones(y, lo64):
    return jnp.where(lo64, y, 1.0), jnp.where(lo64, pltpu.roll(y, 64, 1), 1.0)


def _inproj_kernel(x_ref, ng_ref, w_ref, bd_ref, cos_ref, sin_ref, gains_ref,
                   qn_ref, qd_ref, qi_ref, ki_ref, ks_ref, kw_ref, kd_ref,
                   vs_ref, vw_ref, vd_ref, kc_ref, vc_ref, zn_ref, zd_ref, misc_ref, w16_ref):
    @pl.when((pl.program_id(0) == 0) & (pl.program_id(1) == 0))
    def _():
        _repack_weight(w_ref, w16_ref)

    x = x_ref[0]
    ms = jnp.mean(x * x, axis=-1, keepdims=True)
    h = (x * lax.rsqrt(ms + EPS) * ng_ref[...]).astype(bf16)
    cos = cos_ref[...]
    sin = sin_ref[...]
    bd = bd_ref[...]
    lane = lax.broadcasted_iota(jnp.int32, (1, LANES), 1)
    lo32 = (lane % 64) < 32
    lo64 = lane < 64

    def proj(c0, n):
        return _dot(h, w16_ref[:, c0 * LANES:(c0 + n) * LANES])

    def chunk(y, i):
        return y[:, i * LANES:(i + 1) * LANES]

    def normed_rope(y, gain_idx):
        w = y.shape[1]
        yn = _head_rms_norm(y, bd[:w, :w], gains_ref[gain_idx:gain_idx + 1, 0:w])
        return [_rope(chunk(yn, i), cos, sin, lo32) for i in range(w // LANES)]

    for c0, gain_idx, ref in ((CH_QN, 0, qn_ref), (CH_QD, 2, qd_ref)):
        y = proj(c0, 4)
        for i in range(2):
            for j, r in enumerate(normed_rope(y[:, 2 * i * LANES:(2 * i + 2) * LANES], gain_idx)):
                ref[0, :, (2 * i + j) * LANES:(2 * i + j + 1) * LANES] = (r * Q_SCALE).astype(bf16)
    y = proj(CH_QI, 2)
    for i in range(2):
        qi_ref[0, :, i * LANES:(i + 1) * LANES] = (_rope(chunk(y, i), cos, sin, lo32) * IDX_SCALE).astype(bf16)
    ki_ref[0] = _rope(proj(CH_KI, 1), cos, sin, lo32).astype(bf16)

    k_pairs = normed_rope(proj(CH_KS, 2), 1) + normed_rope(proj(CH_KD, 1), 3)
    for r, ref in zip(k_pairs, (ks_ref, kw_ref, kd_ref)):
        a, b = _dup_halves(r, lo64)
        ref[0, 0] = a.astype(bf16)
        ref[0, 1] = b.astype(bf16)
    for c0, ref in ((CH_VS, vs_ref), (CH_VW, vw_ref), (CH_VD, vd_ref)):
        a, b = _value_with_ones(proj(c0, 1), lo64)
        ref[0, 0] = a.astype(bf16)
        ref[0, 1] = b.astype(bf16)

    kc_ref[0] = proj(CH_KC, 1)
    vc_ref[0] = proj(CH_VC, 1)
    z = proj(CH_ZN, 4)
    zn_ref[0] = (z * jax.nn.sigmoid(z)).astype(bf16)
    z = proj(CH_ZD, 4)
    zd_ref[0] = (z * jax.nn.sigmoid(z)).astype(bf16)
    m = proj(CH_MISC, 1)
    misc_ref[0] = jnp.where(lane < MISC_WI, jax.nn.sigmoid(m), m * WI_SCALE)


def _in_projection(x, norm_gain, w_in, bd, cos_t, sin_t, gains):
    B, S, _ = x.shape
    tm = min(TM_PROJ, S)
    grid = (B, S // tm)
    row = lambda c: pl.BlockSpec((1, tm, c), lambda b, i: (b, i, 0))
    dup = pl.BlockSpec((1, KV_GROUPS, tm, LANES), lambda b, i: (b, 0, i, 0))
    const = lambda shape: pl.BlockSpec(shape, lambda b, i: tuple(0 for _ in shape))
    tab = pl.BlockSpec((tm, LANES), lambda b, i: (i, 0))
    sds = jax.ShapeDtypeStruct
    out_shape = (
        sds((B, S, 512), bf16), sds((B, S, 512), bf16), sds((B, S, 256), bf16), sds((B, S, LANES), bf16),
        sds((B, KV_GROUPS, S, LANES), bf16), sds((B, KV_GROUPS, S, LANES), bf16),
        sds((B, KV_GROUPS, S, LANES), bf16), sds((B, KV_GROUPS, S, LANES), bf16),
        sds((B, KV_GROUPS, S, LANES), bf16), sds((B, KV_GROUPS, S, LANES), bf16),
        sds((B, S, LANES), f32), sds((B, S, LANES), f32),
        sds((B, S, 512), bf16), sds((B, S, 512), bf16), sds((B, S, LANES), f32),
    )
    out_specs = (row(512), row(512), row(256), row(LANES), dup, dup, dup, dup, dup, dup,
                 row(LANES), row(LANES), row(512), row(512), row(LANES))
    return pl.pallas_call(
        _inproj_kernel,
        grid=grid,
        in_specs=[row(D_MODEL), const((1, D_MODEL)),
                  pl.BlockSpec((1, D_MODEL, IN_COLS), lambda b, i: (0, 0, 0), pipeline_mode=pl.Buffered(1)),
                  const((2 * LANES, 2 * LANES)), tab, tab, const((8, 2 * LANES))],
        out_specs=out_specs,
        out_shape=out_shape,
        scratch_shapes=[pltpu.VMEM((D_MODEL, N_CHUNKS * LANES), bf16)],
        compiler_params=pltpu.CompilerParams(
            dimension_semantics=("arbitrary", "arbitrary"), vmem_limit_bytes=VMEM_LIMIT),
        name="in_projection",
    )(x, norm_gain, w_in, bd, cos_t, sin_t, gains)


def _compress_kernel(kc_ref, vc_ref, pek_ref, w1k_ref, b1k_ref, w2k_ref, pev_ref, w1v_ref, b1v_ref, w2v_ref,
                     bd_ref, gain_ref, cos_ref, sin_ref, kout_ref, vout_ref, *, n_rows):
    half = CMP_BLOCK // 2

    def mlp(src_ref, pe_ref, w1_ref, b1_ref, w2_ref):
        acc_a = jnp.zeros((n_rows, 2 * CMP_HIDDEN), f32)
        acc_b = jnp.zeros((n_rows, 2 * CMP_HIDDEN), f32)
        for l in range(half):
            rows = src_ref[0, pl.ds(l, n_rows, stride=CMP_STRIDE), :]
            acc_a = acc_a + _dot((rows + pe_ref[l:l + 1, :]).astype(bf16), w1_ref[l])
            acc_b = acc_b + _dot((rows + pe_ref[l + half:l + half + 1, :]).astype(bf16), w1_ref[l + half])
        pre = acc_a + pltpu.roll(acc_b, n_rows - 1, 0) + b1_ref[...]
        hid = pre * jax.nn.sigmoid(pre)
        return _dot(hid.astype(bf16), w2_ref[...])

    lane = lax.broadcasted_iota(jnp.int32, (1, LANES), 1)
    k = _head_rms_norm(mlp(kc_ref, pek_ref, w1k_ref, b1k_ref, w2k_ref), bd_ref[...], gain_ref[...])
    a, b = _dup_halves(_rope(k, cos_ref[...], sin_ref[...], (lane % 64) < 32), lane < 64)
    kout_ref[0, 0] = a.astype(bf16)
    kout_ref[0, 1] = b.astype(bf16)
    a, b = _value_with_ones(mlp(vc_ref, pev_ref, w1v_ref, b1v_ref, w2v_ref), lane < 64)
    vout_ref[0, 0] = a.astype(bf16)
    vout_ref[0, 1] = b.astype(bf16)


def _compress(kc, vc, k_weights, v_weights, bd, gain_dup, cos_c, sin_c):
    B, S, _ = kc.shape
    n_rows = S // CMP_STRIDE
    const = lambda shape: pl.BlockSpec(shape, lambda b: tuple(0 for _ in shape), pipeline_mode=pl.Buffered(1))
    src = pl.BlockSpec((1, S, LANES), lambda b: (b, 0, 0))
    weights = [const((CMP_BLOCK, LANES)), const((CMP_BLOCK, LANES, 2 * CMP_HIDDEN)),
               const((1, 2 * CMP_HIDDEN)), const((2 * CMP_HIDDEN, LANES))]
    out = pl.BlockSpec((1, KV_GROUPS, n_rows, LANES), lambda b: (b, 0, 0, 0))
    sds = jax.ShapeDtypeStruct((B, KV_GROUPS, n_rows, LANES), bf16)
    return pl.pallas_call(
        functools.partial(_compress_kernel, n_rows=n_rows),
        grid=(B,),
        in_specs=[src, src] + weights + weights + [const((LANES, LANES)), const((1, LANES)),
                                                   const((n_rows, LANES)), const((n_rows, LANES))],
        out_specs=(out, out),
        out_shape=(sds, sds),
        compiler_params=pltpu.CompilerParams(
            dimension_semantics=("arbitrary",), vmem_limit_bytes=VMEM_LIMIT),
        name="compress_kv",
    )(kc, vc, *k_weights, *v_weights, bd, gain_dup, cos_c, sin_c)


def _half_masked(pair, h, lo64):
    keep = lo64 if h % 2 == 0 else jnp.logical_not(lo64)
    return jnp.where(keep, pair, jnp.zeros_like(pair))


def _loop_by_pairs(n, body, init, quads=False):
    done = 0
    if quads:
        four = lambda i, cr: body(4 * i + 3, body(4 * i + 2, body(4 * i + 1, body(4 * i, cr))))
        init = lax.fori_loop(0, n // 4, four, init)
        done = 4 * (n // 4)
    pair = lambda i, cr: body(2 * i + 1, body(2 * i, cr))
    carry = lax.fori_loop(done // 2, n // 2, pair, init)
    return lax.fori_loop(2 * (n // 2), n, body, carry)


def _stack_group_queries(q, g, lo64):
    slabs = []
    for r in range(HEADS_PER_GROUP):
        h = g * HEADS_PER_GROUP + r
        slabs.append(_half_masked(q[:, (h // 2) * LANES:(h // 2 + 1) * LANES], h, lo64))
    return jnp.concatenate(slabs, axis=0)


def _masked_flash(q4s, k_ref, v_ref, n_chunks, bias_fns, tq, bound_ref, shift_scrs):
    rows = HEADS_PER_GROUP * tq
    bound = bound_ref[0]

    def scores(g, k0):
        s = _nt_dot(q4s[g], k_ref[0, g, pl.ds(k0, KC), :])
        return (s.reshape(HEADS_PER_GROUP, tq, KC) + bias_fns[g](k0)[None]).reshape(rows, KC)

    for g in range(KV_GROUPS):
        shift_scrs[g][...] = jnp.full((rows, 1), bound, f32)

    @pl.when(bound > MAX_FIXED_SHIFT)
    def _():
        def body(c, ms):
            k0 = pl.multiple_of(c * KC, KC)
            return tuple(jnp.maximum(ms[g], jnp.max(scores(g, k0), axis=-1, keepdims=True))
                         for g in range(KV_GROUPS))

        init = tuple(jnp.full((rows, 1), M_FLOOR, f32) for _ in range(KV_GROUPS))
        for g, m in enumerate(lax.fori_loop(0, n_chunks, body, init)):
            shift_scrs[g][...] = m

    def body(c, accs):
        k0 = pl.multiple_of(c * KC, KC)
        new = []
        for g in range(KV_GROUPS):
            p = jnp.exp2(scores(g, k0) - shift_scrs[g][...]).astype(bf16)
            new.append(accs[g] + _dot(p, v_ref[0, g, pl.ds(k0, KC), :]))
        return tuple(new)

    init = tuple(jnp.zeros((rows, LANES), f32) for _ in range(KV_GROUPS))
    out = _loop_by_pairs(n_chunks, body, init, quads=True)
    return [acc / acc[:, HEAD_DIM:HEAD_DIM + 1] for acc in out]


def _group_output(og, lo64, tq):
    pairs = []
    for p in range(2):
        even = og[(2 * p) * tq:(2 * p + 1) * tq]
        odd = og[(2 * p + 1) * tq:(2 * p + 2) * tq]
        pairs.append(jnp.where(lo64, even, pltpu.roll(odd, HEAD_DIM, 1)))
    return jnp.concatenate(pairs, axis=1)


def _top_n_blocks(imp, n_sel, top_n):
    groups = [imp[SUBLANES * a:SUBLANES * (a + 1)] for a in range(n_sel // SUBLANES)]
    jsub = lax.broadcasted_iota(jnp.int32, groups[0].shape, 0)
    ranks = [jnp.zeros(groups[0].shape, f32) for _ in groups]
    for i in range(n_sel):
        row = imp[i:i + 1, :]
        for a, grp in enumerate(groups):
            if SUBLANES * a > i:
                beats = row >= grp
            elif SUBLANES * (a + 1) - 1 < i:
                beats = row > grp
            else:
                tie = jnp.where(jsub + SUBLANES * a > i, 1.0, 0.0)
                ranks[a] = ranks[a] + jnp.where(row > grp, 1.0, jnp.where(row == grp, tie, 0.0))
                continue
            ranks[a] = ranks[a] + jnp.where(beats, 1.0, 0.0)
    rank = jnp.concatenate(ranks, axis=0)
    return jnp.where(rank < top_n, 1.0, 0.0)


def _nsa_kernel(bound_ref, qn_ref, kcmp_ref, vcmp_ref, ks_ref, vs_ref, kw_ref, vw_ref, misc_ref, zn_ref, ovt_ref,
                expand_ref, o_ref, shift0_scr, shift1_scr, cshift0_scr, cshift1_scr, wshift0_scr, wshift1_scr,
                bias_scr, *, n_cmp_pad, n_sel, top_n):
    tq = TQ_NSA
    cmp_shift = [cshift0_scr, cshift1_scr]
    win_shift = [wshift0_scr, wshift1_scr]
    t0 = pl.program_id(1) * tq
    lane = lax.broadcasted_iota(jnp.int32, (1, LANES), 1)
    lo64 = lane < 64
    q = qn_ref[0]
    misc = misc_ref[0]
    t_col = t0 + lax.broadcasted_iota(jnp.int32, (tq, 1), 0)
    n_chunks = (t0 + tq + KC - 1) // KC
    win_span = WINDOW + tq
    win_start = pl.multiple_of(jnp.maximum(t0 - WINDOW, 0), tq)
    ovt = ovt_ref[...]
    rows = HEADS_PER_GROUP * tq

    def add_bias(s, bias):
        return (s.reshape(HEADS_PER_GROUP, tq, s.shape[-1]) + bias[None]).reshape(rows, s.shape[-1])

    cmp_end = lax.broadcasted_iota(jnp.int32, (1, n_cmp_pad), 1) * CMP_STRIDE + (CMP_BLOCK - 1)
    cmp_bias = jnp.where(cmp_end <= t_col, 0.0, NEG)
    win_diff = t_col - (win_start + lax.broadcasted_iota(jnp.int32, (1, win_span), 1))
    win_bias = jnp.where((win_diff >= 0) & (win_diff < WINDOW), 0.0, NEG)
    q4s = [_stack_group_queries(q, g, lo64) for g in range(KV_GROUPS)]

    def cmp_scores(g):
        return add_bias(_nt_dot(q4s[g], kcmp_ref[0, g]), cmp_bias)

    def win_scores(g):
        return add_bias(_nt_dot(q4s[g], kw_ref[0, g, pl.ds(win_start, win_span), :]), win_bias)

    for scr in cmp_shift + win_shift:
        scr[...] = jnp.full((rows, 1), bound_ref[0], f32)

    @pl.when(bound_ref[0] > MAX_FIXED_SHIFT)
    def _():
        for g in range(KV_GROUPS):
            cmp_shift[g][...] = jnp.maximum(jnp.max(cmp_scores(g), axis=-1, keepdims=True), M_FLOOR)
            win_shift[g][...] = jnp.max(win_scores(g), axis=-1, keepdims=True)

    o_cmp, o_win, sels = [], [], []
    for g in range(KV_GROUPS):
        e = jnp.exp2(cmp_scores(g) - cmp_shift[g][...])
        l = jnp.sum(e, axis=-1, keepdims=True)
        p = e * (1.0 / jnp.maximum(l, 1e-30))
        o_cmp.append(_dot(p.astype(bf16), vcmp_ref[0, g]))

        psum = p[0:tq] + p[tq:2 * tq] + p[2 * tq:3 * tq] + p[3 * tq:4 * tq]
        hi, lo = _split_bf16(psum)
        imp = _nt_dot(ovt, hi) + _nt_dot(ovt, lo)
        jrow = lax.broadcasted_iota(jnp.int32, (n_sel, tq), 0)
        tl = t0 + lax.broadcasted_iota(jnp.int32, (n_sel, tq), 1)
        cur = lax.shift_right_logical(tl, 6)
        forced = (jrow == 0) | (jrow == cur) | (jrow == cur - 1)
        imp = jnp.where(forced, FORCE, imp)
        imp = jnp.where(jrow * SEL_BLOCK <= tl, imp, NEG)
        pad_row = lax.broadcasted_iota(jnp.int32, (LANES - n_sel, tq), 0)
        chosen = jnp.concatenate([_top_n_blocks(imp, n_sel, top_n) * -NEG, jnp.where(pad_row == 0, NEG, 0.0)], axis=0)
        sels.append(chosen.T.astype(bf16))

        e = jnp.exp2(win_scores(g) - win_shift[g][...])
        acc = _dot(e.astype(bf16), vw_ref[0, g, pl.ds(win_start, win_span), :])
        o_win.append(acc / acc[:, HEAD_DIM:HEAD_DIM + 1])

    def bias_body(c, carry):
        k0 = pl.multiple_of(c * KC, KC)
        for g in range(KV_GROUPS):
            bias_scr[g, :, pl.ds(k0, KC)] = _dot(sels[g], expand_ref[:, pl.ds(k0, KC)])
        return carry

    _loop_by_pairs(n_chunks, bias_body, 0)
    k_last = pl.multiple_of((n_chunks - 1) * KC, KC)
    causal = jnp.where(k_last + lax.broadcasted_iota(jnp.int32, (1, KC), 1) <= t_col, 0.0, NEG)
    for g in range(KV_GROUPS):
        bias_scr[g, :, pl.ds(k_last, KC)] = bias_scr[g, :, pl.ds(k_last, KC)] + causal

    o_sel = _masked_flash(q4s, ks_ref, vs_ref, n_chunks,
                          [lambda k0, g=g: bias_scr[g, :, pl.ds(k0, KC)] for g in range(KV_GROUPS)], tq,
                          bound_ref, (shift0_scr, shift1_scr))

    group_outs = []
    for g in range(KV_GROUPS):
        def gate(branch):
            cols = [misc[:, (g * HEADS_PER_GROUP + r) * 3 + branch:(g * HEADS_PER_GROUP + r) * 3 + branch + 1]
                    for r in range(HEADS_PER_GROUP)]
            return jnp.concatenate(cols, axis=0)

        og = gate(0) * o_cmp[g] + gate(1) * o_sel[g] + gate(2) * o_win[g]
        group_outs.append(_group_output(og, lo64, tq))

    o = jnp.concatenate(group_outs, axis=1) * zn_ref[0].astype(f32)
    o_ref[0] = o.astype(bf16)


def _score_bound(q_gain, *k_gains):
    k_max = functools.reduce(jnp.maximum, [jnp.max(jnp.abs(g)) for g in k_gains])
    return (Q_SCALE * HEAD_DIM * NORM_SLACK * jnp.max(jnp.abs(q_gain)) * k_max).reshape(1)


_SMEM_SCALAR = pl.BlockSpec(memory_space=pltpu.SMEM)


def _shift_scratch(tq):
    return [pltpu.VMEM((HEADS_PER_GROUP * tq, 1), f32)] * KV_GROUPS


def _nsa_attention(bound, qn, kcmp, vcmp, ks, vs, kw, vw, misc, zn, ovt):
    B, S, _ = qn.shape
    tq = TQ_NSA
    n_cmp_pad = kcmp.shape[2]
    n_sel = S // SEL_BLOCK
    row = lambda c: pl.BlockSpec((1, tq, c), lambda b, i: (b, i, 0))
    whole = lambda n: pl.BlockSpec((1, KV_GROUPS, n, LANES), lambda b, i: (b, 0, 0, 0))
    kern = functools.partial(_nsa_kernel, n_cmp_pad=n_cmp_pad, n_sel=n_sel, top_n=min(SEL_TOPN, n_sel))
    assert n_sel < LANES
    expand = np.zeros((LANES, S), np.float32)
    expand[np.arange(S) // SEL_BLOCK, np.arange(S)] = 1.0
    expand[n_sel] = 1.0
    expand = jnp.asarray(expand, bf16)
    return pl.pallas_call(
        kern,
        grid=(B, S // tq),
        in_specs=[_SMEM_SCALAR, row(512), whole(n_cmp_pad), whole(n_cmp_pad), whole(S), whole(S), whole(S),
                  whole(S), row(LANES), row(512), pl.BlockSpec((n_sel, n_cmp_pad), lambda b, i: (0, 0)),
                  pl.BlockSpec((LANES, S), lambda b, i: (0, 0))],
        out_specs=row(512),
        out_shape=jax.ShapeDtypeStruct((B, S, 512), bf16),
        scratch_shapes=3 * _shift_scratch(tq) + [pltpu.VMEM((KV_GROUPS, tq, S), f32)],
        compiler_params=pltpu.CompilerParams(
            dimension_semantics=("arbitrary", "arbitrary"), vmem_limit_bytes=VMEM_LIMIT),
        name="nsa_attention",
    )(bound, qn, kcmp, vcmp, ks, vs, kw, vw, misc, zn, ovt, expand)


N_SLABS = 32
_TRANSPOSE_STAGES = ((16, 0x0000FFFF), (8, 0x00FF00FF), (4, 0x0F0F0F0F), (2, 0x33333333), (1, 0x55555555))


def _ordered_bits(v):
    return v ^ (jnp.right_shift(v, 31) & jnp.int32(0x7FFFFFFF))


def _bit_planes(score_ref, plane_ref, n_words):
    stages = [s for s in _TRANSPOSE_STAGES if s[0] < n_words]
    for stage, (j, mask) in enumerate(stages):
        if stage == 0:
            load = lambda k: _ordered_bits(pltpu.bitcast(score_ref[k], jnp.int32))
        else:
            load = lambda k: plane_ref[k]

        def body(p, carry, j=j, mask=mask, load=load):
            k = jnp.left_shift(p & ~(j - 1), 1) | (p & (j - 1))
            lo = load(k)
            hi = load(k + j)
            t = (lo ^ lax.shift_right_logical(hi, j)) & mask
            plane_ref[k] = lo ^ t
            plane_ref[k + j] = hi ^ jnp.left_shift(t, j)
            return carry

        lax.fori_loop(0, n_words // 2, body, 0)


def _dsa_kernel(bound_ref, qd_ref, qi_ref, misc_ref, ki_ref, kd_ref, vd_ref, zd_ref, tri_ref,
                x_ref, onsa_ref, wout_ref, o_ref, score_scr, plane_scr, bias_scr, shift0_scr, shift1_scr, select_scr, *, top_k, tq):
    t0 = pl.program_id(1) * tq
    lane = lax.broadcasted_iota(jnp.int32, (1, LANES), 1)
    lo64 = lane < 64
    t_col = t0 + lax.broadcasted_iota(jnp.int32, (tq, 1), 0)
    t_row = t0 + lax.broadcasted_iota(jnp.int32, (1, tq), 1)
    n_chunks = (t0 + tq + KC - 1) // KC
    slabs_per_chunk = KC // LANES
    int_min = jnp.int32(-2 ** 31)

    def chunk_slabs(c):
        return pl.ds(c * slabs_per_chunk, slabs_per_chunk)

    def to_col(row):
        return jnp.broadcast_to(row, (SUBLANES, tq)).T[:, 0:1]

    qi = qi_ref[0]
    qi_heads = [_half_masked(qi[:, (h // 2) * LANES:(h // 2 + 1) * LANES], h, lo64) for h in range(IDX_HEADS)]
    misc_t = misc_ref[0].T
    wi = [misc_t[MISC_WI + h:MISC_WI + h + 1, :] for h in range(IDX_HEADS)]

    def score_chunk(c, has_later_keys):
        k0 = pl.multiple_of(c * KC, KC)
        kib = ki_ref[0, pl.ds(k0, KC), :]
        sc = jnp.zeros((KC, tq), f32)
        for h in range(IDX_HEADS):
            sc = sc + jnp.maximum(_nt_dot(kib, qi_heads[h]), 0.0) * wi[h]
        sc = jnp.where(sc == 0.0, 0.0, sc)
        if has_later_keys:
            kpos = k0 + lax.broadcasted_iota(jnp.int32, (KC, 1), 0)
            sc = jnp.where(kpos <= t_row, sc, NON_CAUSAL_MARK)
        score_scr[chunk_slabs(c)] = sc.reshape(slabs_per_chunk, LANES, tq)

    def score_body(c, carry):
        score_chunk(c, False)
        return carry

    _loop_by_pairs(n_chunks - 1, score_body, 0)
    score_chunk(n_chunks - 1, True)

    def fill_body(c, carry):
        score_scr[chunk_slabs(c)] = jnp.full((slabs_per_chunk, LANES, tq), NON_CAUSAL_MARK, f32)
        return carry

    lax.fori_loop(n_chunks, N_SLABS // slabs_per_chunk, fill_body, 0)

    kf = float(top_k)

    def radix_select(n_words):
        _bit_planes(score_scr, plane_scr, n_words)
        all_ones = jnp.int32(-1 if n_words == 32 else 0xFFFF)

        def select_bit(i, carry):
            alive, above, thr = carry
            if n_words == 32:
                plane = plane_scr[i]
            else:
                word = plane_scr[i & 15]
                plane = jnp.where(i < 16, lax.shift_right_logical(word, 16), word) & all_ones
            plane = plane ^ jnp.where(i == 0, all_ones, jnp.int32(0))
            ones = alive & plane
            c1 = jnp.sum(lax.population_count(ones).astype(f32), axis=0, keepdims=True)
            take = (above + c1) >= kf
            alive = jnp.where(take, ones, alive ^ ones)
            above = jnp.where(take, above, above + c1)
            thr = jnp.where(take, thr | jnp.left_shift(jnp.int32(1), 31 - i), thr)
            return alive, above, thr

        init = (jnp.full((LANES, tq), all_ones, jnp.int32), jnp.zeros((1, tq), f32), jnp.zeros((1, tq), jnp.int32))
        _, above, thr_u = lax.fori_loop(0, 32, select_bit, init)
        select_scr[0:1, :] = thr_u
        select_scr[1:2, :] = pltpu.bitcast(above, jnp.int32)

    few_slabs = n_chunks * slabs_per_chunk <= 16

    @pl.when(few_slabs)
    def _():
        radix_select(16)

    @pl.when(jnp.logical_not(few_slabs))
    def _():
        radix_select(32)

    thr_u = select_scr[0:1, :]
    above = pltpu.bitcast(select_scr[1:2, :], f32)

    def key_to_score(key_row):
        return pltpu.bitcast(_ordered_bits(key_row), f32)

    def write_bias(thr_row, need_row):
        thr = to_col(thr_row)
        need = to_col(need_row)

        def body(c, carry):
            ties_before, n_above = carry
            k0 = pl.multiple_of(c * KC, KC)
            blk = score_scr[chunk_slabs(c)].reshape(KC, tq).T
            gt = jnp.where(blk > thr, 1.0, 0.0)
            eq = jnp.where(blk == thr, 1.0, 0.0)
            prefix = ties_before + _dot(eq.astype(bf16), tri_ref[...])
            chosen = gt + jnp.where(prefix <= need, eq, 0.0)
            kpos = k0 + lax.broadcasted_iota(jnp.int32, (1, KC), 1)
            bias_scr[:, pl.ds(k0, KC)] = jnp.where(kpos <= t_col, jnp.where(chosen > 0.5, 0.0, NEG), NEG)
            return prefix[:, KC - 1:KC], n_above + jnp.sum(gt, axis=-1, keepdims=True)

        zero = jnp.zeros((tq, 1), f32)
        return _loop_by_pairs(n_chunks, body, (zero, zero))

    ties, n_above = write_bias(key_to_score(thr_u ^ int_min), kf - above)
    verified = (n_above == to_col(above)) & (n_above < kf) & (n_above + ties >= kf)

    @pl.when(jnp.sum(jnp.where(verified, 0.0, 1.0)) > 0.0)
    def _():
        def count(pred):
            def body(c, acc):
                blk = score_scr[chunk_slabs(c)].reshape(KC, tq)
                return acc + jnp.sum(jnp.where(pred(blk), 1.0, 0.0), axis=0, keepdims=True)
            return lax.fori_loop(0, N_SLABS // slabs_per_chunk, body, jnp.zeros((1, tq), f32))

        key = jnp.where(count(lambda b: b >= 0.0) >= kf, jnp.int32(0), int_min)

        def bit(i, key):
            cand = key + jnp.left_shift(jnp.int32(1), 30 - i)
            cand_f = key_to_score(cand)
            return jnp.where(count(lambda b: b >= cand_f) >= kf, cand, key)

        thr_f = key_to_score(lax.fori_loop(0, 31, bit, key))
        write_bias(thr_f, kf - count(lambda b: b > thr_f))

    def bias_fn(k0):
        return bias_scr[:, pl.ds(k0, KC)]

    qd = qd_ref[0]
    q4s = [_stack_group_queries(qd, g, lo64) for g in range(KV_GROUPS)]
    outs = _masked_flash(q4s, kd_ref, vd_ref, n_chunks, [bias_fn] * KV_GROUPS, tq,
                         bound_ref, (shift0_scr, shift1_scr))
    o = jnp.concatenate([_group_output(og, lo64, tq) for og in outs], axis=1) * zd_ref[0].astype(f32)
    half = wout_ref.shape[0] // 2
    o_ref[0] = (x_ref[0] + _dot(onsa_ref[0], wout_ref[0:half, :])
                + _dot(o.astype(bf16), wout_ref[half:, :]))


def _dsa_attention(bound, qd, qi, misc, ki, kd, vd, zd, x, o_nsa, w_out):
    B, S, _ = qd.shape
    tq = TQ_DSA
    assert S <= N_SLABS * LANES and KC % tq == 0
    tri = jnp.asarray(np.triu(np.ones((KC, KC), np.float32)), bf16)
    row = lambda c: pl.BlockSpec((1, tq, c), lambda b, i: (b, i, 0))
    whole = pl.BlockSpec((1, KV_GROUPS, S, LANES), lambda b, i: (b, 0, 0, 0))
    return pl.pallas_call(
        functools.partial(_dsa_kernel, top_k=min(DSA_TOPK_MAX, S // 4), tq=tq),
        grid=(B, S // tq),
        in_specs=[_SMEM_SCALAR, row(512), row(256), row(LANES), pl.BlockSpec((1, S, LANES), lambda b, i: (b, 0, 0)),
                  whole, whole, row(512), pl.BlockSpec((KC, KC), lambda b, i: (0, 0)),
                  row(D_MODEL), row(512),
                  pl.BlockSpec(w_out.shape, lambda b, i: (0, 0), pipeline_mode=pl.Buffered(1))],
        out_specs=row(D_MODEL),
        out_shape=jax.ShapeDtypeStruct((B, S, D_MODEL), f32),
        scratch_shapes=[pltpu.VMEM((N_SLABS, LANES, tq), f32), pltpu.VMEM((N_SLABS, LANES, tq), jnp.int32),
                        pltpu.VMEM((tq, S), f32)] + _shift_scratch(tq) + [pltpu.VMEM((SUBLANES, tq), jnp.int32)],
        compiler_params=pltpu.CompilerParams(
            dimension_semantics=("arbitrary", "arbitrary"), vmem_limit_bytes=VMEM_LIMIT),
        name="dsa_attention",
    )(bound, qd, qi, misc, ki, kd, vd, zd, tri, x, o_nsa, w_out)


def _rope_tables(pos):
    half = HEAD_DIM // 2
    inv_freq = ROPE_THETA ** (-np.arange(half, dtype=np.float64) / half)
    ang = np.asarray(pos, np.float64)[:, None] * inv_freq[None, :]
    cos, sin = np.cos(ang), np.sin(ang)
    cos_t = np.tile(cos, (1, LANES // half))
    sin_t = np.tile(np.concatenate([-sin, sin], axis=1), (1, LANES // HEAD_DIM))
    return jnp.asarray(cos_t, f32), jnp.asarray(sin_t, f32)


def _overlap_t(seq, n_cmp_pad):
    n_cmp = (seq - CMP_BLOCK) // CMP_STRIDE + 1
    n_sel = seq // SEL_BLOCK
    c_start = np.arange(n_cmp) * CMP_STRIDE
    j_start = np.arange(n_sel) * SEL_BLOCK
    ov = np.clip(np.minimum(c_start[:, None] + CMP_BLOCK, j_start[None, :] + SEL_BLOCK)
                 - np.maximum(c_start[:, None], j_start[None, :]), 0, None).astype(np.float32) / CMP_BLOCK
    out = np.zeros((n_sel, n_cmp_pad), np.float32)
    out[:, :n_cmp] = ov.T
    return out


def _block_diag2(w):
    z = jnp.zeros_like(w)
    return jnp.concatenate([jnp.concatenate([w, z], axis=-1), jnp.concatenate([z, w], axis=-1)], axis=-2)


def _layer(x, norm_gain, w_in, nsa_q_gain, nsa_kc_gain, nsa_ks_gain, nsa_kw_gain,
           cmp_pe_k, cmp_k_w1, cmp_k_b1, cmp_k_w2, cmp_pe_v, cmp_v_w1, cmp_v_b1, cmp_v_w2,
           dsa_q_gain, dsa_k_gain, w_out):
    B, S, _ = x.shape
    assert S % KC == 0 and S >= WINDOW + TQ_NSA
    n_cmp_pad = S // CMP_STRIDE

    head_of_lane = np.arange(2 * LANES) // HEAD_DIM
    bd = jnp.asarray(head_of_lane[:, None] == head_of_lane[None, :], bf16)
    dup = lambda v: jnp.tile(v.reshape(1, -1), (1, 2))
    quad = lambda v: jnp.tile(v.reshape(1, -1), (1, 4))
    gains = jnp.concatenate([quad(nsa_q_gain), jnp.concatenate([dup(nsa_ks_gain), dup(nsa_kw_gain)], axis=1),
                             quad(dsa_q_gain), quad(dsa_k_gain), jnp.ones((4, 2 * LANES), f32)], axis=0)
    cos_t, sin_t = _rope_tables(np.arange(S))

    (qn, qd, qi, ki, ks, kw, kd, vs, vw, vd, kc, vc, zn, zd, misc) = _in_projection(
        x, norm_gain.reshape(1, -1), w_in, bd, cos_t, sin_t, gains)

    cos_c, sin_c = _rope_tables(np.arange(n_cmp_pad) * CMP_STRIDE + CMP_BLOCK - 1)

    def cmp_weights(pe, w1, b1, w2):
        w1_bd = _block_diag2(w1.reshape(CMP_BLOCK, HEAD_DIM, CMP_HIDDEN)).astype(bf16)
        return jnp.tile(pe, (1, 2)), w1_bd, dup(b1), _block_diag2(w2).astype(bf16)

    kcmp, vcmp = _compress(kc, vc, cmp_weights(cmp_pe_k, cmp_k_w1, cmp_k_b1, cmp_k_w2),
                           cmp_weights(cmp_pe_v, cmp_v_w1, cmp_v_b1, cmp_v_w2),
                           bd[:LANES, :LANES], dup(nsa_kc_gain), cos_c, sin_c)

    ovt = jnp.asarray(_overlap_t(S, n_cmp_pad), bf16)
    o_nsa = _nsa_attention(_score_bound(nsa_q_gain, nsa_kc_gain, nsa_ks_gain, nsa_kw_gain),
                           qn, kcmp, vcmp, ks, vs, kw, vw, misc, zn, ovt)

    return _dsa_attention(_score_bound(dsa_q_gain, dsa_k_gain), qd, qi, misc, ki, kd, vd, zd,
                          x, o_nsa, w_out.astype(bf16))


def kernel(x, norm_gain, w_in, nsa_q_gain, nsa_kc_gain, nsa_ks_gain, nsa_kw_gain, cmp_pe_k, cmp_k_w1,
           cmp_k_b1, cmp_k_w2, cmp_pe_v, cmp_v_w1, cmp_v_b1, cmp_v_w2, dsa_q_gain, dsa_k_gain, w_out):
    for l in range(norm_gain.shape[0]):
        x = _layer(x, norm_gain[l], w_in[l:l + 1], nsa_q_gain[l], nsa_kc_gain[l], nsa_ks_gain[l], nsa_kw_gain[l],
                   cmp_pe_k[l], cmp_k_w1[l], cmp_k_b1[l], cmp_k_w2[l], cmp_pe_v[l], cmp_v_w1[l], cmp_v_b1[l],
                   cmp_v_w2[l], dsa_q_gain[l], dsa_k_gain[l], w_out[l])
    return x
```

```python
import functools

import numpy as np
import jax
import jax.numpy as jnp
from jax import lax
from jax.experimental import pallas as pl
from jax.experimental.pallas import tpu as pltpu

D_MODEL = 1024
HEAD_DIM = 64
NSA_HEADS = 8
DSA_HEADS = 8
KV_GROUPS = 2
HEADS_PER_GROUP = 4
IDX_HEADS = 4
CMP_BLOCK = 32
CMP_STRIDE = 16
CMP_HIDDEN = 256
SEL_BLOCK = 64
SEL_TOPN = 16
WINDOW = 512
DSA_TOPK_MAX = 256
ROPE_THETA = 10000.0
EPS = 1e-6
NEG = -1e30
FORCE = 1e6
ATTN_SCALE = HEAD_DIM ** -0.5
IDX_SCALE = HEAD_DIM ** -0.5
WI_SCALE = IDX_HEADS ** -0.5
LOG2E = 1.4426950408889634
Q_SCALE = ATTN_SCALE * LOG2E

LANES = 128
SUBLANES = 8
M_FLOOR = -5e29
NON_CAUSAL_MARK = -3.0e38
MAX_FIXED_SHIFT = 30.0
NORM_SLACK = 1.03
VMEM_LIMIT = 56 * 1024 * 1024

IN_WIDTHS = (512, 128, 128, 128, 128, 128, 128, 24, 512, 512, 128, 128, 256, 64, 4, 512)
IN_NAMES = ("q_n", "kc", "vc", "ks", "vs", "kw", "vw", "gate", "z_n",
            "q_d", "k_d", "v_d", "qi", "ki", "wi", "z_d")
IN_COLS = sum(IN_WIDTHS)

CH_QN, CH_KS, CH_KW, CH_QD, CH_KD, CH_QI, CH_KI = 0, 4, 5, 6, 10, 11, 13
CH_KC, CH_VC, CH_VS, CH_VW, CH_VD, CH_ZN, CH_ZD, CH_MISC = 14, 15, 16, 17, 18, 19, 23, 27
N_CHUNKS = 28
MISC_WI = 24

TM_PROJ = 512
TQ_NSA = 256
TQ_DSA = 256
KC = 512

f32 = jnp.float32
bf16 = jnp.bfloat16


def _repack_weight(w_ref, w16_ref):
    off = dict(zip(IN_NAMES, np.cumsum((0,) + IN_WIDTHS[:-1])))
    width = dict(zip(IN_NAMES, IN_WIDTHS))
    order = ("q_n", "ks", "kw", "q_d", "k_d", "qi", "ki", "ki", "kc", "vc", "vs", "vw", "v_d", "z_n", "z_d",
             "gate", "wi")
    dst = 0
    for n in order:
        w16_ref[:, dst:dst + width[n]] = w_ref[0, :, off[n]:off[n] + width[n]].astype(bf16)
        dst += width[n]
    w16_ref[:, dst:] = jnp.zeros((D_MODEL, N_CHUNKS * LANES - dst), bf16)


def _nt_dot(a, b):
    return lax.dot_general(a, b, (((1,), (1,)), ((), ())), preferred_element_type=f32)


def _dot(a, b):
    return jnp.dot(a, b, preferred_element_type=f32)


def _split_bf16(v):
    hi = v.astype(bf16)
    lo = (v - hi.astype(f32)).astype(bf16)
    return hi, lo


def _head_rms_norm(y, bd, gain_row):
    ssq = _dot((y * y).astype(bf16), bd)
    return y * lax.rsqrt(ssq * (1.0 / HEAD_DIM) + EPS) * gain_row


def _rope(y, cos, sin_signed, lo32):
    partner = jnp.where(lo32, pltpu.roll(y, LANES - 32, 1), pltpu.roll(y, 32, 1))
    return y * cos + partner * sin_signed


def _dup_halves(y, lo64):
    r = pltpu.roll(y, 64, 1)
    return jnp.where(lo64, y, r), jnp.where(lo64, r, y)


def _value_with_ones(y, lo64):
    return jnp.where(lo64, y, 1.0), jnp.where(lo64, pltpu.roll(y, 64, 1), 1.0)


def _inproj_kernel(x_ref, ng_ref, w_ref, bd_ref, cos_ref, sin_ref, gains_ref,
                   qn_ref, qd_ref, qi_ref, ki_ref, ks_ref, kw_ref, kd_ref,
                   vs_ref, vw_ref, vd_ref, kc_ref, vc_ref, zn_ref, zd_ref, misc_ref, w16_ref):
    @pl.when((pl.program_id(0) == 0) & (pl.program_id(1) == 0))
    def _():
        _repack_weight(w_ref, w16_ref)

    x = x_ref[0]
    ms = jnp.mean(x * x, axis=-1, keepdims=True)
    h = (x * lax.rsqrt(ms + EPS) * ng_ref[...]).astype(bf16)
    cos = cos_ref[...]
    sin = sin_ref[...]
    bd = bd_ref[...]
    lane = lax.broadcasted_iota(jnp.int32, (1, LANES), 1)
    lo32 = (lane % 64) < 32
    lo64 = lane < 64

    def proj(c0, n):
        return _dot(h, w16_ref[:, c0 * LANES:(c0 + n) * LANES])

    def chunk(y, i):
        return y[:, i * LANES:(i + 1) * LANES]

    def normed_rope(y, gain_idx):
        w = y.shape[1]
        yn = _head_rms_norm(y, bd[:w, :w], gains_ref[gain_idx:gain_idx + 1, 0:w])
        return [_rope(chunk(yn, i), cos, sin, lo32) for i in range(w // LANES)]

    for c0, gain_idx, ref in ((CH_QN, 0, qn_ref), (CH_QD, 2, qd_ref)):
        y = proj(c0, 4)
        for i in range(2):
            for j, r in enumerate(normed_rope(y[:, 2 * i * LANES:(2 * i + 2) * LANES], gain_idx)):
                ref[0, :, (2 * i + j) * LANES:(2 * i + j + 1) * LANES] = (r * Q_SCALE).astype(bf16)
    y = proj(CH_QI, 2)
    for i in range(2):
        qi_ref[0, :, i * LANES:(i + 1) * LANES] = (_rope(chunk(y, i), cos, sin, lo32) * IDX_SCALE).astype(bf16)
    ki_ref[0] = _rope(proj(CH_KI, 1), cos, sin, lo32).astype(bf16)

    k_pairs = normed_rope(proj(CH_KS, 2), 1) + normed_rope(proj(CH_KD, 1), 3)
    for r, ref in zip(k_pairs, (ks_ref, kw_ref, kd_ref)):
        a, b = _dup_halves(r, lo64)
        ref[0, 0] = a.astype(bf16)
        ref[0, 1] = b.astype(bf16)
    for c0, ref in ((CH_VS, vs_ref), (CH_VW, vw_ref), (CH_VD, vd_ref)):
        a, b = _value_with_ones(proj(c0, 1), lo64)
        ref[0, 0] = a.astype(bf16)
        ref[0, 1] = b.astype(bf16)

    kc_ref[0] = proj(CH_KC, 1)
    vc_ref[0] = proj(CH_VC, 1)
    z = proj(CH_ZN, 4)
    zn_ref[0] = (z * jax.nn.sigmoid(z)).astype(bf16)
    z = proj(CH_ZD, 4)
    zd_ref[0] = (z * jax.nn.sigmoid(z)).astype(bf16)
    m = proj(CH_MISC, 1)
    misc_ref[0] = jnp.where(lane < MISC_WI, jax.nn.sigmoid(m), m * WI_SCALE)


def _in_projection(x, norm_gain, w_in, bd, cos_t, sin_t, gains):
    B, S, _ = x.shape
    tm = min(TM_PROJ, S)
    grid = (B, S // tm)
    row = lambda c: pl.BlockSpec((1, tm, c), lambda b, i: (b, i, 0))
    dup = pl.BlockSpec((1, KV_GROUPS, tm, LANES), lambda b, i: (b, 0, i, 0))
    const = lambda shape: pl.BlockSpec(shape, lambda b, i: tuple(0 for _ in shape))
    tab = pl.BlockSpec((tm, LANES), lambda b, i: (i, 0))
    sds = jax.ShapeDtypeStruct
    out_shape = (
        sds((B, S, 512), bf16), sds((B, S, 512), bf16), sds((B, S, 256), bf16), sds((B, S, LANES), bf16),
        sds((B, KV_GROUPS, S, LANES), bf16), sds((B, KV_GROUPS, S, LANES), bf16),
        sds((B, KV_GROUPS, S, LANES), bf16), sds((B, KV_GROUPS, S, LANES), bf16),
        sds((B, KV_GROUPS, S, LANES), bf16), sds((B, KV_GROUPS, S, LANES), bf16),
        sds((B, S, LANES), f32), sds((B, S, LANES), f32),
        sds((B, S, 512), bf16), sds((B, S, 512), bf16), sds((B, S, LANES), f32),
    )
    out_specs = (row(512), row(512), row(256), row(LANES), dup, dup, dup, dup, dup, dup,
                 row(LANES), row(LANES), row(512), row(512), row(LANES))
    return pl.pallas_call(
        _inproj_kernel,
        grid=grid,
        in_specs=[row(D_MODEL), const((1, D_MODEL)),
                  pl.BlockSpec((1, D_MODEL, IN_COLS), lambda b, i: (0, 0, 0), pipeline_mode=pl.Buffered(1)),
                  const((2 * LANES, 2 * LANES)), tab, tab, const((8, 2 * LANES))],
        out_specs=out_specs,
        out_shape=out_shape,
        scratch_shapes=[pltpu.VMEM((D_MODEL, N_CHUNKS * LANES), bf16)],
        compiler_params=pltpu.CompilerParams(
            dimension_semantics=("arbitrary", "arbitrary"), vmem_limit_bytes=VMEM_LIMIT),
        name="in_projection",
    )(x, norm_gain, w_in, bd, cos_t, sin_t, gains)


def _compress_kernel(kc_ref, vc_ref, pek_ref, w1k_ref, b1k_ref, w2k_ref, pev_ref, w1v_ref, b1v_ref, w2v_ref,
                     bd_ref, gain_ref, cos_ref, sin_ref, kout_ref, vout_ref, w1k_bd, w1v_bd, *, n_rows):
    half = CMP_BLOCK // 2

    @pl.when(pl.program_id(0) == 0)
    def _():
        for raw, bd in ((w1k_ref, w1k_bd), (w1v_ref, w1v_bd)):
            bd[...] = jnp.zeros(bd.shape, bf16)
            bd[:, 0:HEAD_DIM, 0:CMP_HIDDEN] = raw[...]
            bd[:, HEAD_DIM:, CMP_HIDDEN:] = raw[...]

    def mlp(src_ref, pe_ref, w1_ref, b1_ref, w2_ref):
        acc_a = jnp.zeros((n_rows, 2 * CMP_HIDDEN), f32)
        acc_b = jnp.zeros((n_rows, 2 * CMP_HIDDEN), f32)
        for l in range(half):
            rows = src_ref[0, pl.ds(l, n_rows, stride=CMP_STRIDE), :]
            acc_a = acc_a + _dot((rows + pe_ref[l:l + 1, :]).astype(bf16), w1_ref[l])
            acc_b = acc_b + _dot((rows + pe_ref[l + half:l + half + 1, :]).astype(bf16), w1_ref[l + half])
        pre = acc_a + pltpu.roll(acc_b, n_rows - 1, 0) + b1_ref[...]
        hid = pre * jax.nn.sigmoid(pre)
        return _dot(hid.astype(bf16), w2_ref[...])

    lane = lax.broadcasted_iota(jnp.int32, (1, LANES), 1)
    k = _head_rms_norm(mlp(kc_ref, pek_ref, w1k_bd, b1k_ref, w2k_ref), bd_ref[...], gain_ref[...])
    a, b = _dup_halves(_rope(k, cos_ref[...], sin_ref[...], (lane % 64) < 32), lane < 64)
    kout_ref[0, 0] = a.astype(bf16)
    kout_ref[0, 1] = b.astype(bf16)
    a, b = _value_with_ones(mlp(vc_ref, pev_ref, w1v_bd, b1v_ref, w2v_ref), lane < 64)
    vout_ref[0, 0] = a.astype(bf16)
    vout_ref[0, 1] = b.astype(bf16)


def _compress(kc, vc, k_weights, v_weights, bd, gain_dup, cos_c, sin_c):
    B, S, _ = kc.shape
    n_rows = S // CMP_STRIDE
    const = lambda shape: pl.BlockSpec(shape, lambda b: tuple(0 for _ in shape), pipeline_mode=pl.Buffered(1))
    src = pl.BlockSpec((1, S, LANES), lambda b: (b, 0, 0))
    weights = [const((CMP_BLOCK, LANES)), const((CMP_BLOCK, HEAD_DIM, CMP_HIDDEN)),
               const((1, 2 * CMP_HIDDEN)), const((2 * CMP_HIDDEN, LANES))]
    out = pl.BlockSpec((1, KV_GROUPS, n_rows, LANES), lambda b: (b, 0, 0, 0))
    sds = jax.ShapeDtypeStruct((B, KV_GROUPS, n_rows, LANES), bf16)
    return pl.pallas_call(
        functools.partial(_compress_kernel, n_rows=n_rows),
        grid=(B,),
        in_specs=[src, src] + weights + weights + [const((LANES, LANES)), const((1, LANES)),
                                                   const((n_rows, LANES)), const((n_rows, LANES))],
        out_specs=(out, out),
        out_shape=(sds, sds),
        scratch_shapes=[pltpu.VMEM((CMP_BLOCK, LANES, 2 * CMP_HIDDEN), bf16)] * 2,
        compiler_params=pltpu.CompilerParams(
            dimension_semantics=("arbitrary",), vmem_limit_bytes=VMEM_LIMIT),
        name="compress_kv",
    )(kc, vc, *k_weights, *v_weights, bd, gain_dup, cos_c, sin_c)


def _half_masked(pair, h, lo64):
    keep = lo64 if h % 2 == 0 else jnp.logical_not(lo64)
    return jnp.where(keep, pair, jnp.zeros_like(pair))


def _loop_by_pairs(n, body, init, quads=False):
    done = 0
    if quads:
        four = lambda i, cr: body(4 * i + 3, body(4 * i + 2, body(4 * i + 1, body(4 * i, cr))))
        init = lax.fori_loop(0, n // 4, four, init)
        done = 4 * (n // 4)
    pair = lambda i, cr: body(2 * i + 1, body(2 * i, cr))
    carry = lax.fori_loop(done // 2, n // 2, pair, init)
    return lax.fori_loop(2 * (n // 2), n, body, carry)


def _stack_group_queries(q, g, lo64):
    slabs = []
    for r in range(HEADS_PER_GROUP):
        h = g * HEADS_PER_GROUP + r
        slabs.append(_half_masked(q[:, (h // 2) * LANES:(h // 2 + 1) * LANES], h, lo64))
    return jnp.concatenate(slabs, axis=0)


def _masked_flash(q4s, k_ref, v_ref, n_chunks, bias_fns, tq, bound_ref, shift_scrs):
    rows = HEADS_PER_GROUP * tq
    bound = bound_ref[0]

    def scores(g, k0):
        s = _nt_dot(q4s[g], k_ref[0, g, pl.ds(k0, KC), :])
        return (s.reshape(HEADS_PER_GROUP, tq, KC) + bias_fns[g](k0)[None]).reshape(rows, KC)

    for g in range(KV_GROUPS):
        shift_scrs[g][...] = jnp.full((rows, 1), bound, f32)

    @pl.when(bound > MAX_FIXED_SHIFT)
    def _():
        def body(c, ms):
            k0 = pl.multiple_of(c * KC, KC)
            return tuple(jnp.maximum(ms[g], jnp.max(scores(g, k0), axis=-1, keepdims=True))
                         for g in range(KV_GROUPS))

        init = tuple(jnp.full((rows, 1), M_FLOOR, f32) for _ in range(KV_GROUPS))
        for g, m in enumerate(lax.fori_loop(0, n_chunks, body, init)):
            shift_scrs[g][...] = m

    def body(c, accs):
        k0 = pl.multiple_of(c * KC, KC)
        new = []
        for g in range(KV_GROUPS):
            p = jnp.exp2(scores(g, k0) - shift_scrs[g][...]).astype(bf16)
            new.append(accs[g] + _dot(p, v_ref[0, g, pl.ds(k0, KC), :]))
        return tuple(new)

    init = tuple(jnp.zeros((rows, LANES), f32) for _ in range(KV_GROUPS))
    out = _loop_by_pairs(n_chunks, body, init, quads=True)
    return [acc / acc[:, HEAD_DIM:HEAD_DIM + 1] for acc in out]


def _group_output(og, lo64, tq):
    pairs = []
    for p in range(2):
        even = og[(2 * p) * tq:(2 * p + 1) * tq]
        odd = og[(2 * p + 1) * tq:(2 * p + 2) * tq]
        pairs.append(jnp.where(lo64, even, pltpu.roll(odd, HEAD_DIM, 1)))
    return jnp.concatenate(pairs, axis=1)


def _top_n_blocks(imp, n_sel, top_n):
    groups = [imp[SUBLANES * a:SUBLANES * (a + 1)] for a in range(n_sel // SUBLANES)]
    jsub = lax.broadcasted_iota(jnp.int32, groups[0].shape, 0)
    ranks = [jnp.zeros(groups[0].shape, f32) for _ in groups]
    for i in range(n_sel):
        row = imp[i:i + 1, :]
        for a, grp in enumerate(groups):
            if SUBLANES * a > i:
                beats = row >= grp
            elif SUBLANES * (a + 1) - 1 < i:
                beats = row > grp
            else:
                tie = jnp.where(jsub + SUBLANES * a > i, 1.0, 0.0)
                ranks[a] = ranks[a] + jnp.where(row > grp, 1.0, jnp.where(row == grp, tie, 0.0))
                continue
            ranks[a] = ranks[a] + jnp.where(beats, 1.0, 0.0)
    rank = jnp.concatenate(ranks, axis=0)
    return jnp.where(rank < top_n, 1.0, 0.0)


def _nsa_kernel(bound_ref, qn_ref, kcmp_ref, vcmp_ref, ks_ref, vs_ref, kw_ref, vw_ref, misc_ref, zn_ref, ovt_ref,
                expand_ref, o_ref, shift0_scr, shift1_scr, cshift0_scr, cshift1_scr, wshift0_scr, wshift1_scr,
                bias_scr, *, n_cmp_pad, n_sel, top_n):
    tq = TQ_NSA
    cmp_shift = [cshift0_scr, cshift1_scr]
    win_shift = [wshift0_scr, wshift1_scr]
    t0 = pl.program_id(1) * tq
    lane = lax.broadcasted_iota(jnp.int32, (1, LANES), 1)
    lo64 = lane < 64
    q = qn_ref[0]
    misc = misc_ref[0]
    t_col = t0 + lax.broadcasted_iota(jnp.int32, (tq, 1), 0)
    n_chunks = (t0 + tq + KC - 1) // KC
    win_span = WINDOW + tq
    win_start = pl.multiple_of(jnp.maximum(t0 - WINDOW, 0), tq)
    ovt = ovt_ref[...]
    rows = HEADS_PER_GROUP * tq

    def add_bias(s, bias):
        return (s.reshape(HEADS_PER_GROUP, tq, s.shape[-1]) + bias[None]).reshape(rows, s.shape[-1])

    cmp_end = lax.broadcasted_iota(jnp.int32, (1, n_cmp_pad), 1) * CMP_STRIDE + (CMP_BLOCK - 1)
    cmp_bias = jnp.where(cmp_end <= t_col, 0.0, NEG)
    win_diff = t_col - (win_start + lax.broadcasted_iota(jnp.int32, (1, win_span), 1))
    win_bias = jnp.where((win_diff >= 0) & (win_diff < WINDOW), 0.0, NEG)
    q4s = [_stack_group_queries(q, g, lo64) for g in range(KV_GROUPS)]

    def cmp_scores(g):
        return add_bias(_nt_dot(q4s[g], kcmp_ref[0, g]), cmp_bias)

    def win_scores(g):
        return add_bias(_nt_dot(q4s[g], kw_ref[0, g, pl.ds(win_start, win_span), :]), win_bias)

    for scr in cmp_shift + win_shift:
        scr[...] = jnp.full((rows, 1), bound_ref[0], f32)

    @pl.when(bound_ref[0] > MAX_FIXED_SHIFT)
    def _():
        for g in range(KV_GROUPS):
            cmp_shift[g][...] = jnp.maximum(jnp.max(cmp_scores(g), axis=-1, keepdims=True), M_FLOOR)
            win_shift[g][...] = jnp.max(win_scores(g), axis=-1, keepdims=True)

    o_cmp, o_win, sels = [], [], []
    for g in range(KV_GROUPS):
        e = jnp.exp2(cmp_scores(g) - cmp_shift[g][...])
        l = jnp.sum(e, axis=-1, keepdims=True)
        p = e * (1.0 / jnp.maximum(l, 1e-30))
        o_cmp.append(_dot(p.astype(bf16), vcmp_ref[0, g]))

        psum = p[0:tq] + p[tq:2 * tq] + p[2 * tq:3 * tq] + p[3 * tq:4 * tq]
        hi, lo = _split_bf16(psum)
        imp = _nt_dot(ovt, hi) + _nt_dot(ovt, lo)
        jrow = lax.broadcasted_iota(jnp.int32, (n_sel, tq), 0)
        tl = t0 + lax.broadcasted_iota(jnp.int32, (n_sel, tq), 1)
        cur = lax.shift_right_logical(tl, 6)
        forced = (jrow == 0) | (jrow == cur) | (jrow == cur - 1)
        imp = jnp.where(forced, FORCE, imp)
        imp = jnp.where(jrow * SEL_BLOCK <= tl, imp, NEG)
        pad_row = lax.broadcasted_iota(jnp.int32, (LANES - n_sel, tq), 0)
        chosen = jnp.concatenate([_top_n_blocks(imp, n_sel, top_n) * -NEG, jnp.where(pad_row == 0, NEG, 0.0)], axis=0)
        sels.append(chosen.T.astype(bf16))

        e = jnp.exp2(win_scores(g) - win_shift[g][...])
        acc = _dot(e.astype(bf16), vw_ref[0, g, pl.ds(win_start, win_span), :])
        o_win.append(acc / acc[:, HEAD_DIM:HEAD_DIM + 1])

    def bias_body(c, carry):
        k0 = pl.multiple_of(c * KC, KC)
        for g in range(KV_GROUPS):
            bias_scr[g, :, pl.ds(k0, KC)] = _dot(sels[g], expand_ref[:, pl.ds(k0, KC)])
        return carry

    _loop_by_pairs(n_chunks, bias_body, 0)
    k_last = pl.multiple_of((n_chunks - 1) * KC, KC)
    causal = jnp.where(k_last + lax.broadcasted_iota(jnp.int32, (1, KC), 1) <= t_col, 0.0, NEG)
    for g in range(KV_GROUPS):
        bias_scr[g, :, pl.ds(k_last, KC)] = bias_scr[g, :, pl.ds(k_last, KC)] + causal

    o_sel = _masked_flash(q4s, ks_ref, vs_ref, n_chunks,
                          [lambda k0, g=g: bias_scr[g, :, pl.ds(k0, KC)] for g in range(KV_GROUPS)], tq,
                          bound_ref, (shift0_scr, shift1_scr))

    group_outs = []
    for g in range(KV_GROUPS):
        def gate(branch):
            cols = [misc[:, (g * HEADS_PER_GROUP + r) * 3 + branch:(g * HEADS_PER_GROUP + r) * 3 + branch + 1]
                    for r in range(HEADS_PER_GROUP)]
            return jnp.concatenate(cols, axis=0)

        og = gate(0) * o_cmp[g] + gate(1) * o_sel[g] + gate(2) * o_win[g]
        group_outs.append(_group_output(og, lo64, tq))

    o = jnp.concatenate(group_outs, axis=1) * zn_ref[0].astype(f32)
    o_ref[0] = o.astype(bf16)


def _score_bound(q_gain, *k_gains):
    k_max = functools.reduce(jnp.maximum, [jnp.max(jnp.abs(g)) for g in k_gains])
    return (Q_SCALE * HEAD_DIM * NORM_SLACK * jnp.max(jnp.abs(q_gain)) * k_max).reshape(1)


_SMEM_SCALAR = pl.BlockSpec(memory_space=pltpu.SMEM)


def _shift_scratch(tq):
    return [pltpu.VMEM((HEADS_PER_GROUP * tq, 1), f32)] * KV_GROUPS


def _nsa_attention(bound, qn, kcmp, vcmp, ks, vs, kw, vw, misc, zn, ovt):
    B, S, _ = qn.shape
    tq = TQ_NSA
    n_cmp_pad = kcmp.shape[2]
    n_sel = S // SEL_BLOCK
    row = lambda c: pl.BlockSpec((1, tq, c), lambda b, i: (b, i, 0))
    whole = lambda n: pl.BlockSpec((1, KV_GROUPS, n, LANES), lambda b, i: (b, 0, 0, 0))
    kern = functools.partial(_nsa_kernel, n_cmp_pad=n_cmp_pad, n_sel=n_sel, top_n=min(SEL_TOPN, n_sel))
    assert n_sel < LANES
    expand = np.zeros((LANES, S), np.float32)
    expand[np.arange(S) // SEL_BLOCK, np.arange(S)] = 1.0
    expand[n_sel] = 1.0
    expand = jnp.asarray(expand, bf16)
    return pl.pallas_call(
        kern,
        grid=(B, S // tq),
        in_specs=[_SMEM_SCALAR, row(512), whole(n_cmp_pad), whole(n_cmp_pad), whole(S), whole(S), whole(S),
                  whole(S), row(LANES), row(512), pl.BlockSpec((n_sel, n_cmp_pad), lambda b, i: (0, 0)),
                  pl.BlockSpec((LANES, S), lambda b, i: (0, 0))],
        out_specs=row(512),
        out_shape=jax.ShapeDtypeStruct((B, S, 512), bf16),
        scratch_shapes=3 * _shift_scratch(tq) + [pltpu.VMEM((KV_GROUPS, tq, S), f32)],
        compiler_params=pltpu.CompilerParams(
            dimension_semantics=("arbitrary", "arbitrary"), vmem_limit_bytes=VMEM_LIMIT),
        name="nsa_attention",
    )(bound, qn, kcmp, vcmp, ks, vs, kw, vw, misc, zn, ovt, expand)


N_SLABS = 32
_TRANSPOSE_STAGES = ((16, 0x0000FFFF), (8, 0x00FF00FF), (4, 0x0F0F0F0F), (2, 0x33333333), (1, 0x55555555))


def _ordered_bits(v):
    return v ^ (jnp.right_shift(v, 31) & jnp.int32(0x7FFFFFFF))


def _bit_planes(score_ref, plane_ref, n_words):
    stages = [s for s in _TRANSPOSE_STAGES if s[0] < n_words]
    for stage, (j, mask) in enumerate(stages):
        if stage == 0:
            load = lambda k: _ordered_bits(pltpu.bitcast(score_ref[k], jnp.int32))
        else:
            load = lambda k: plane_ref[k]

        def body(p, carry, j=j, mask=mask, load=load):
            k = jnp.left_shift(p & ~(j - 1), 1) | (p & (j - 1))
            lo = load(k)
            hi = load(k + j)
            t = (lo ^ lax.shift_right_logical(hi, j)) & mask
            plane_ref[k] = lo ^ t
            plane_ref[k + j] = hi ^ jnp.left_shift(t, j)
            return carry

        lax.fori_loop(0, n_words // 2, body, 0)


def _dsa_kernel(bound_ref, qd_ref, qi_ref, misc_ref, ki_ref, kd_ref, vd_ref, zd_ref, tri_ref,
                x_ref, onsa_ref, wout_ref, o_ref, score_scr, plane_scr, bias_scr, shift0_scr, shift1_scr, select_scr, *, top_k, tq):
    t0 = pl.program_id(1) * tq
    lane = lax.broadcasted_iota(jnp.int32, (1, LANES), 1)
    lo64 = lane < 64
    t_col = t0 + lax.broadcasted_iota(jnp.int32, (tq, 1), 0)
    t_row = t0 + lax.broadcasted_iota(jnp.int32, (1, tq), 1)
    n_chunks = (t0 + tq + KC - 1) // KC
    slabs_per_chunk = KC // LANES
    int_min = jnp.int32(-2 ** 31)

    def chunk_slabs(c):
        return pl.ds(c * slabs_per_chunk, slabs_per_chunk)

    def to_col(row):
        return jnp.broadcast_to(row, (SUBLANES, tq)).T[:, 0:1]

    qi = qi_ref[0]
    qi_heads = [_half_masked(qi[:, (h // 2) * LANES:(h // 2 + 1) * LANES], h, lo64) for h in range(IDX_HEADS)]
    misc_t = misc_ref[0].T
    wi = [misc_t[MISC_WI + h:MISC_WI + h + 1, :] for h in range(IDX_HEADS)]

    def score_chunk(c, has_later_keys):
        k0 = pl.multiple_of(c * KC, KC)
        kib = ki_ref[0, pl.ds(k0, KC), :]
        sc = jnp.zeros((KC, tq), f32)
        for h in range(IDX_HEADS):
            sc = sc + jnp.maximum(_nt_dot(kib, qi_heads[h]), 0.0) * wi[h]
        sc = jnp.where(sc == 0.0, 0.0, sc)
        if has_later_keys:
            kpos = k0 + lax.broadcasted_iota(jnp.int32, (KC, 1), 0)
            sc = jnp.where(kpos <= t_row, sc, NON_CAUSAL_MARK)
        score_scr[chunk_slabs(c)] = sc.reshape(slabs_per_chunk, LANES, tq)

    def score_body(c, carry):
        score_chunk(c, False)
        return carry

    _loop_by_pairs(n_chunks - 1, score_body, 0)
    score_chunk(n_chunks - 1, True)

    def fill_body(c, carry):
        score_scr[chunk_slabs(c)] = jnp.full((slabs_per_chunk, LANES, tq), NON_CAUSAL_MARK, f32)
        return carry

    lax.fori_loop(n_chunks, N_SLABS // slabs_per_chunk, fill_body, 0)

    kf = float(top_k)

    def radix_select(n_words):
        _bit_planes(score_scr, plane_scr, n_words)
        all_ones = jnp.int32(-1 if n_words == 32 else 0xFFFF)

        def select_bit(i, carry):
            alive, above, thr = carry
            if n_words == 32:
                plane = plane_scr[i]
            else:
                word = plane_scr[i & 15]
                plane = jnp.where(i < 16, lax.shift_right_logical(word, 16), word) & all_ones
            plane = plane ^ jnp.where(i == 0, all_ones, jnp.int32(0))
            ones = alive & plane
            c1 = jnp.sum(lax.population_count(ones).astype(f32), axis=0, keepdims=True)
            take = (above + c1) >= kf
            alive = jnp.where(take, ones, alive ^ ones)
            above = jnp.where(take, above, above + c1)
            thr = jnp.where(take, thr | jnp.left_shift(jnp.int32(1), 31 - i), thr)
            return alive, above, thr

        init = (jnp.full((LANES, tq), all_ones, jnp.int32), jnp.zeros((1, tq), f32), jnp.zeros((1, tq), jnp.int32))
        _, above, thr_u = lax.fori_loop(0, 32, select_bit, init)
        select_scr[0:1, :] = thr_u
        select_scr[1:2, :] = pltpu.bitcast(above, jnp.int32)

    few_slabs = n_chunks * slabs_per_chunk <= 16

    @pl.when(few_slabs)
    def _():
        radix_select(16)

    @pl.when(jnp.logical_not(few_slabs))
    def _():
        radix_select(32)

    thr_u = select_scr[0:1, :]
    above = pltpu.bitcast(select_scr[1:2, :], f32)

    def key_to_score(key_row):
        return pltpu.bitcast(_ordered_bits(key_row), f32)

    def write_bias(thr_row, need_row):
        thr = to_col(thr_row)
        need = to_col(need_row)

        def body(c, carry):
            ties_before, n_above = carry
            k0 = pl.multiple_of(c * KC, KC)
            blk = score_scr[chunk_slabs(c)].reshape(KC, tq).T
            gt = jnp.where(blk > thr, 1.0, 0.0)
            eq = jnp.where(blk == thr, 1.0, 0.0)
            prefix = ties_before + _dot(eq.astype(bf16), tri_ref[...])
            chosen = gt + jnp.where(prefix <= need, eq, 0.0)
            kpos = k0 + lax.broadcasted_iota(jnp.int32, (1, KC), 1)
            bias_scr[:, pl.ds(k0, KC)] = jnp.where(kpos <= t_col, jnp.where(chosen > 0.5, 0.0, NEG), NEG)
            return prefix[:, KC - 1:KC], n_above + jnp.sum(gt, axis=-1, keepdims=True)

        zero = jnp.zeros((tq, 1), f32)
        return _loop_by_pairs(n_chunks, body, (zero, zero))

    ties, n_above = write_bias(key_to_score(thr_u ^ int_min), kf - above)
    verified = (n_above == to_col(above)) & (n_above < kf) & (n_above + ties >= kf)

    @pl.when(jnp.sum(jnp.where(verified, 0.0, 1.0)) > 0.0)
    def _():
        def count(pred):
            def body(c, acc):
                blk = score_scr[chunk_slabs(c)].reshape(KC, tq)
                return acc + jnp.sum(jnp.where(pred(blk), 1.0, 0.0), axis=0, keepdims=True)
            return lax.fori_loop(0, N_SLABS // slabs_per_chunk, body, jnp.zeros((1, tq), f32))

        key = jnp.where(count(lambda b: b >= 0.0) >= kf, jnp.int32(0), int_min)

        def bit(i, key):
            cand = key + jnp.left_shift(jnp.int32(1), 30 - i)
            cand_f = key_to_score(cand)
            return jnp.where(count(lambda b: b >= cand_f) >= kf, cand, key)

        thr_f = key_to_score(lax.fori_loop(0, 31, bit, key))
        write_bias(thr_f, kf - count(lambda b: b > thr_f))

    def bias_fn(k0):
        return bias_scr[:, pl.ds(k0, KC)]

    qd = qd_ref[0]
    q4s = [_stack_group_queries(qd, g, lo64) for g in range(KV_GROUPS)]
    outs = _masked_flash(q4s, kd_ref, vd_ref, n_chunks, [bias_fn] * KV_GROUPS, tq,
                         bound_ref, (shift0_scr, shift1_scr))
    o = jnp.concatenate([_group_output(og, lo64, tq) for og in outs], axis=1) * zd_ref[0].astype(f32)
    half = wout_ref.shape[0] // 2
    o_ref[0] = (x_ref[0] + _dot(onsa_ref[0], wout_ref[0:half, :])
                + _dot(o.astype(bf16), wout_ref[half:, :]))


def _dsa_attention(bound, qd, qi, misc, ki, kd, vd, zd, x, o_nsa, w_out):
    B, S, _ = qd.shape
    tq = TQ_DSA
    assert S <= N_SLABS * LANES and KC % tq == 0
    tri = jnp.asarray(np.triu(np.ones((KC, KC), np.float32)), bf16)
    row = lambda c: pl.BlockSpec((1, tq, c), lambda b, i: (b, i, 0))
    whole = pl.BlockSpec((1, KV_GROUPS, S, LANES), lambda b, i: (b, 0, 0, 0))
    return pl.pallas_call(
        functools.partial(_dsa_kernel, top_k=min(DSA_TOPK_MAX, S // 4), tq=tq),
        grid=(B, S // tq),
        in_specs=[_SMEM_SCALAR, row(512), row(256), row(LANES), pl.BlockSpec((1, S, LANES), lambda b, i: (b, 0, 0)),
                  whole, whole, row(512), pl.BlockSpec((KC, KC), lambda b, i: (0, 0)),
                  row(D_MODEL), row(512),
                  pl.BlockSpec(w_out.shape, lambda b, i: (0, 0), pipeline_mode=pl.Buffered(1))],
        out_specs=row(D_MODEL),
        out_shape=jax.ShapeDtypeStruct((B, S, D_MODEL), f32),
        scratch_shapes=[pltpu.VMEM((N_SLABS, LANES, tq), f32), pltpu.VMEM((N_SLABS, LANES, tq), jnp.int32),
                        pltpu.VMEM((tq, S), f32)] + _shift_scratch(tq) + [pltpu.VMEM((SUBLANES, tq), jnp.int32)],
        compiler_params=pltpu.CompilerParams(
            dimension_semantics=("arbitrary", "arbitrary"), vmem_limit_bytes=VMEM_LIMIT),
        name="dsa_attention",
    )(bound, qd, qi, misc, ki, kd, vd, zd, tri, x, o_nsa, w_out)


def _rope_tables(pos):
    half = HEAD_DIM // 2
    inv_freq = ROPE_THETA ** (-np.arange(half, dtype=np.float64) / half)
    ang = np.asarray(pos, np.float64)[:, None] * inv_freq[None, :]
    cos, sin = np.cos(ang), np.sin(ang)
    cos_t = np.tile(cos, (1, LANES // half))
    sin_t = np.tile(np.concatenate([-sin, sin], axis=1), (1, LANES // HEAD_DIM))
    return jnp.asarray(cos_t, f32), jnp.asarray(sin_t, f32)


def _overlap_t(seq, n_cmp_pad):
    n_cmp = (seq - CMP_BLOCK) // CMP_STRIDE + 1
    n_sel = seq // SEL_BLOCK
    c_start = np.arange(n_cmp) * CMP_STRIDE
    j_start = np.arange(n_sel) * SEL_BLOCK
    ov = np.clip(np.minimum(c_start[:, None] + CMP_BLOCK, j_start[None, :] + SEL_BLOCK)
                 - np.maximum(c_start[:, None], j_start[None, :]), 0, None).astype(np.float32) / CMP_BLOCK
    out = np.zeros((n_sel, n_cmp_pad), np.float32)
    out[:, :n_cmp] = ov.T
    return out


def _block_diag2(w):
    z = jnp.zeros_like(w)
    return jnp.concatenate([jnp.concatenate([w, z], axis=-1), jnp.concatenate([z, w], axis=-1)], axis=-2)


def _layer(x, norm_gain, w_in, nsa_q_gain, nsa_kc_gain, nsa_ks_gain, nsa_kw_gain,
           cmp_pe_k, cmp_k_w1, cmp_k_b1, cmp_k_w2, cmp_pe_v, cmp_v_w1, cmp_v_b1, cmp_v_w2,
           dsa_q_gain, dsa_k_gain, w_out):
    B, S, _ = x.shape
    assert S % KC == 0 and S >= WINDOW + TQ_NSA
    n_cmp_pad = S // CMP_STRIDE

    head_of_lane = np.arange(2 * LANES) // HEAD_DIM
    bd = jnp.asarray(head_of_lane[:, None] == head_of_lane[None, :], bf16)
    dup = lambda v: jnp.tile(v.reshape(1, -1), (1, 2))
    quad = lambda v: jnp.tile(v.reshape(1, -1), (1, 4))
    gains = jnp.concatenate([quad(nsa_q_gain), jnp.concatenate([dup(nsa_ks_gain), dup(nsa_kw_gain)], axis=1),
                             quad(dsa_q_gain), quad(dsa_k_gain), jnp.ones((4, 2 * LANES), f32)], axis=0)
    cos_t, sin_t = _rope_tables(np.arange(S))

    (qn, qd, qi, ki, ks, kw, kd, vs, vw, vd, kc, vc, zn, zd, misc) = _in_projection(
        x, norm_gain.reshape(1, -1), w_in, bd, cos_t, sin_t, gains)

    cos_c, sin_c = _rope_tables(np.arange(n_cmp_pad) * CMP_STRIDE + CMP_BLOCK - 1)

    def cmp_weights(pe, w1, b1, w2):
        w1_blocks = w1.reshape(CMP_BLOCK, HEAD_DIM, CMP_HIDDEN).astype(bf16)
        return jnp.tile(pe, (1, 2)), w1_blocks, dup(b1), _block_diag2(w2).astype(bf16)

    kcmp, vcmp = _compress(kc, vc, cmp_weights(cmp_pe_k, cmp_k_w1, cmp_k_b1, cmp_k_w2),
                           cmp_weights(cmp_pe_v, cmp_v_w1, cmp_v_b1, cmp_v_w2),
                           bd[:LANES, :LANES], dup(nsa_kc_gain), cos_c, sin_c)

    ovt = jnp.asarray(_overlap_t(S, n_cmp_pad), bf16)
    o_nsa = _nsa_attention(_score_bound(nsa_q_gain, nsa_kc_gain, nsa_ks_gain, nsa_kw_gain),
                           qn, kcmp, vcmp, ks, vs, kw, vw, misc, zn, ovt)

    return _dsa_attention(_score_bound(dsa_q_gain, dsa_k_gain), qd, qi, misc, ki, kd, vd, zd,
                          x, o_nsa, w_out.astype(bf16))


def kernel(x, norm_gain, w_in, nsa_q_gain, nsa_kc_gain, nsa_ks_gain, nsa_kw_gain, cmp_pe_k, cmp_k_w1,
           cmp_k_b1, cmp_k_w2, cmp_pe_v, cmp_v_w1, cmp_v_b1, cmp_v_w2, dsa_q_gain, dsa_k_gain, w_out):
    for l in range(norm_gain.shape[0]):
        x = _layer(x, norm_gain[l], w_in[l:l + 1], nsa_q_gain[l], nsa_kc_gain[l], nsa_ks_gain[l], nsa_kw_gain[l],
                   cmp_pe_k[l], cmp_k_w1[l], cmp_k_b1[l], cmp_k_w2[l], cmp_pe_v[l], cmp_v_w1[l], cmp_v_b1[l],
                   cmp_v_w2[l], dsa_q_gain[l], dsa_k_gain[l], w_out[l])
    return x
```
